```python
import math
import jax, jax.numpy as jnp
from jax import lax
import numpy as np

D_MODEL = 2048
BATCH = 8
SEQ = 4096
DEPTH = 2

EPS = 1e-6
N_Q_HEADS_A = 12
N_KV_HEADS_A = 4
HEAD_DIM_A = 64
WINDOW = 128
BLOCK_A = 128
N_HEADS_B = 4
Q_LORA_B = 448
KV_LORA_B = 128
D_NOPE_B = 128
D_ROPE_B = 64
D_V_B = 128
ROPE_THETA = 10000.0
Q_BLOCK_B = 128
N_HEADS_C = 4
DQK_C = 96
DV_C = 192
CHUNK_C = 64
W_A = N_Q_HEADS_A * HEAD_DIM_A
W_B = N_HEADS_B * D_V_B
W_C = N_HEADS_C * DV_C
MIX_WIDTH = W_A + W_B + W_C
IN_SIZES = (N_Q_HEADS_A * HEAD_DIM_A, N_KV_HEADS_A * HEAD_DIM_A, N_KV_HEADS_A * HEAD_DIM_A,
            Q_LORA_B, KV_LORA_B, D_ROPE_B,
            N_HEADS_C * DQK_C, N_HEADS_C * DQK_C, N_HEADS_C * DV_C, 4 * N_HEADS_C, N_HEADS_C * DV_C)
IN_WIDTH = sum(IN_SIZES)
D_FF = -(-8 * D_MODEL // (3 * 256)) * 256

kernel_name = "hybrid_swa_mla_mlstm_encoder"


def _split_points(sizes):
    pts, acc = [], 0
    for s in sizes[:-1]:
        acc += s
        pts.append(acc)
    return pts


def rms_norm(x, g):
    xf = x.astype(jnp.float32)
    y = xf * lax.rsqrt(jnp.mean(xf * xf, axis=-1, keepdims=True) + EPS)
    return (y * g.astype(jnp.float32)).astype(x.dtype)


def alibi_slopes(n):
    def pow2(m):
        start = 2.0 ** (-8.0 / m)
        return [start ** (i + 1) for i in range(m)]
    if math.log2(n).is_integer():
        s = pow2(n)
    else:
        p = 2 ** math.floor(math.log2(n))
        s = pow2(p) + pow2(2 * p)[0::2][: n - p]
    return np.array(s, dtype=np.float32)


def rope_tables(positions):
    inv = 1.0 / (ROPE_THETA ** (jnp.arange(0, D_ROPE_B, 2, dtype=jnp.float32) / D_ROPE_B))
    ang = positions.astype(jnp.float32)[..., None] * inv
    return jnp.cos(ang), jnp.sin(ang)


def apply_rope(x, cos, sin):
    xf = x.astype(jnp.float32)
    x1, x2 = jnp.split(xf, 2, axis=-1)
    return jnp.concatenate([x1 * cos - x2 * sin, x1 * sin + x2 * cos], axis=-1).astype(x.dtype)


def window_attention(q, k, v, sinks):
    B, S, _ = q.shape
    nb = S // BLOCK_A
    G = N_Q_HEADS_A // N_KV_HEADS_A
    qb = q.reshape(B, nb, BLOCK_A, N_KV_HEADS_A, G, HEAD_DIM_A)
    pad = ((0, 0), (BLOCK_A, BLOCK_A), (0, 0))
    kp = jnp.pad(k, pad).reshape(B, nb + 2, BLOCK_A, N_KV_HEADS_A, HEAD_DIM_A)
    vp = jnp.pad(v, pad).reshape(B, nb + 2, BLOCK_A, N_KV_HEADS_A, HEAD_DIM_A)
    kb = jnp.concatenate([kp[:, :-2], kp[:, 1:-1], kp[:, 2:]], axis=2)
    vb = jnp.concatenate([vp[:, :-2], vp[:, 1:-1], vp[:, 2:]], axis=2)
    qpos = jnp.arange(S).reshape(nb, BLOCK_A)
    kpos = jnp.arange(-BLOCK_A, S + BLOCK_A).reshape(nb + 2, BLOCK_A)
    kpos = jnp.concatenate([kpos[:-2], kpos[1:-1], kpos[2:]], axis=1)
    dist = jnp.abs(qpos[:, :, None] - kpos[:, None, :])
    valid = (dist <= WINDOW) & (kpos[:, None, :] >= 0) & (kpos[:, None, :] < S)
    slopes = jnp.asarray(alibi_slopes(N_Q_HEADS_A)).reshape(N_KV_HEADS_A, G)
    bias = -slopes[None, :, :, None, None] * dist.astype(jnp.float32)[:, None, None, :, :]
    s = jnp.einsum('bnqhgd,bnkhd->bnhgqk', qb, kb, preferred_element_type=jnp.float32)
    s = jnp.where(valid[None, :, None, None], s * (HEAD_DIM_A ** -0.5) + bias[None], -jnp.inf)
    sink = sinks.astype(jnp.float32).reshape(N_KV_HEADS_A, G)[None, None, :, :, None]
    lse = jnp.logaddexp(jax.nn.logsumexp(s, axis=-1), sink)
    p = jnp.exp(s - lse[..., None])
    o = jnp.einsum('bnhgqk,bnkhd->bnqhgd', p.astype(v.dtype), vb)
    return o.reshape(B, S, W_A)


def mla_attention(c_q, c_kv, k_rope_in, q_norm_g, w_uq, kv_norm_g, w_ukv, cos, sin):
    B, S, _ = c_q.shape
    q = (rms_norm(c_q, q_norm_g) @ w_uq).reshape(B, S, N_HEADS_B, D_NOPE_B + D_ROPE_B)
    q_nope = q[..., :D_NOPE_B]
    q_rope = apply_rope(q[..., D_NOPE_B:], cos[:, :, None, :], sin[:, :, None, :])
    kv = (rms_norm(c_kv, kv_norm_g) @ w_ukv).reshape(B, S, N_HEADS_B, D_NOPE_B + D_V_B)
    k_nope, v = kv[..., :D_NOPE_B], kv[..., D_NOPE_B:]
    k_rope = apply_rope(k_rope_in, cos, sin)
    nb = S // Q_BLOCK_B
    qn = q_nope.reshape(B, nb, Q_BLOCK_B, N_HEADS_B, D_NOPE_B).transpose(1, 0, 2, 3, 4)
    qr = q_rope.reshape(B, nb, Q_BLOCK_B, N_HEADS_B, D_ROPE_B).transpose(1, 0, 2, 3, 4)
    scale = (D_NOPE_B + D_ROPE_B) ** -0.5

    def block(args):
        qn_b, qr_b = args
        s = (jnp.einsum('bqhd,bkhd->bhqk', qn_b, k_nope, preferred_element_type=jnp.float32)
             + jnp.einsum('bqhd,bkd->bhqk', qr_b, k_rope, preferred_element_type=jnp.float32)) * scale
        p = jax.nn.softmax(s, axis=-1)
        return jnp.einsum('bhqk,bkhd->bqhd', p.astype(v.dtype), v)

    o = lax.map(block, (qn, qr))
    return o.transpose(1, 0, 2, 3, 4).reshape(B, S, W_B)


def mlstm_direction(q, k, v, log_i, log_f):
    B, H, S, _ = q.shape
    nc, L = S // CHUNK_C, CHUNK_C
    qc = q.reshape(B, H, nc, L, DQK_C)
    kc = k.reshape(B, H, nc, L, DQK_C)
    vc = v.reshape(B, H, nc, L, DV_C)
    ic = log_i.reshape(B, H, nc, L)
    b = jnp.cumsum(log_f.reshape(B, H, nc, L), axis=-1)
    b_last = b[..., -1]
    tri = jnp.tril(jnp.ones((L, L), dtype=bool))
    dmat = jnp.where(tri, b[..., :, None] - b[..., None, :] + ic[..., None, :], -jnp.inf)
    a = b_last[..., None] - b + ic
    m_loc = jnp.max(a, axis=-1)
    w = jnp.exp(a - m_loc[..., None])
    C_loc = jnp.einsum('bhcl,bhcld,bhcle->bhcde', w, kc, vc)
    n_loc = jnp.einsum('bhcl,bhcld->bhcd', w, kc)

    def step(carry, inp):
        C, n, m = carry
        Cl, nl, ml, bl = inp
        m_new = jnp.maximum(bl + m, ml)
        sp = jnp.exp(bl + m - m_new)
        sl = jnp.exp(ml - m_new)
        C_new = sp[..., None, None] * C + sl[..., None, None] * Cl
        n_new = sp[..., None] * n + sl[..., None] * nl
        return (C_new, n_new, m_new), (C, n, m)

    init = (jnp.zeros((B, H, DQK_C, DV_C), jnp.float32), jnp.zeros((B, H, DQK_C), jnp.float32),
            jnp.full((B, H), -jnp.inf, jnp.float32))
    xs = (jnp.moveaxis(C_loc, 2, 0), jnp.moveaxis(n_loc, 2, 0),
          jnp.moveaxis(m_loc, 2, 0), jnp.moveaxis(b_last, 2, 0))
    _, (C_prev, n_prev, m_prev) = lax.scan(step, init, xs)
    C_prev = jnp.moveaxis(C_prev, 0, 2)
    n_prev = jnp.moveaxis(n_prev, 0, 2)
    m_prev = jnp.moveaxis(m_prev, 0, 2)
    inter_log = b + m_prev[..., None]
    m_t = jnp.maximum(inter_log, jnp.max(dmat, axis=-1))
    inter_w = jnp.exp(inter_log - m_t)
    qk = jnp.einsum('bhcld,bhcsd->bhcls', qc, kc) * jnp.exp(dmat - m_t[..., None])
    num = (inter_w[..., None] * jnp.einsum('bhcld,bhcde->bhcle', qc, C_prev)
           + jnp.einsum('bhcls,bhcse->bhcle', qk, vc))
    den = inter_w * jnp.einsum('bhcld,bhcd->bhcl', qc, n_prev) + jnp.sum(qk, axis=-1)
    h = num / jnp.maximum(jnp.abs(den), jnp.exp(-m_t))[..., None]
    return h.reshape(B, H, S, DV_C)


def mlstm_mixer(q, k, v, gates, o_pre, gate_b, head_g):
    B, S, _ = q.shape
    f32 = jnp.float32

    def heads(t, d):
        return t.astype(f32).reshape(B, S, N_HEADS_C, d).transpose(0, 2, 1, 3)

    qh = heads(q, DQK_C) * (DQK_C ** -0.5)
    kh = heads(k, DQK_C)
    vh = heads(v, DV_C)
    g = (gates.astype(f32) + gate_b.astype(f32)).reshape(B, S, 4, N_HEADS_C).transpose(2, 0, 3, 1)
    log_i_f, log_i_b = g[0], g[1]
    log_f_f, log_f_b = jax.nn.log_sigmoid(g[2]), jax.nn.log_sigmoid(g[3])
    h_f = mlstm_direction(qh, kh, vh, log_i_f, log_f_f)
    fl = lambda t: jnp.flip(t, axis=2)
    h_b = fl(mlstm_direction(fl(qh), fl(kh), fl(vh), fl(log_i_b), fl(log_f_b)))
    h = (h_f + h_b).transpose(0, 2, 1, 3)
    h = h * lax.rsqrt(jnp.mean(h * h, axis=-1, keepdims=True) + EPS)
    h = h * head_g.astype(f32).reshape(N_HEADS_C, DV_C)
    out = jax.nn.sigmoid(o_pre.astype(f32)) * h.reshape(B, S, W_C)
    return out.astype(q.dtype)


def setup_inputs(seed: int = 0) -> dict:
    key = jax.random.key(seed)
    ks = jax.random.split(key, 24)
    f32 = jnp.float32

    def dense(k, shape, fan_in, scale=1.0):
        return jax.random.normal(k, shape, f32) * (scale * fan_in ** -0.5)

    def gain(k, shape):
        return 1.0 + 0.05 * jax.random.normal(k, shape, f32)

    x = jax.random.normal(ks[0], (BATCH, SEQ, D_MODEL), f32)
    c = jax.random.normal(ks[1], (BATCH, D_MODEL), f32)
    offs = jax.random.randint(ks[2], (BATCH, 1), 0, 1024, dtype=jnp.int32)
    positions = offs + jnp.arange(SEQ, dtype=jnp.int32)[None, :]
    gate_b = jnp.concatenate([
        0.5 * jax.random.normal(ks[3], (DEPTH, 2 * N_HEADS_C), f32),
        3.0 + 3.0 * jax.random.uniform(ks[4], (DEPTH, 2 * N_HEADS_C), f32)], axis=-1)
    return {
        "x": x,
        "c": c,
        "positions": positions,
        "mod_w": dense(ks[5], (DEPTH, D_MODEL, 6 * D_MODEL), D_MODEL, 0.5),
        "mod_b": 0.02 * jax.random.normal(ks[6], (DEPTH, 6 * D_MODEL), f32),
        "pre_mix_g": gain(ks[7], (DEPTH, D_MODEL)),
        "post_mix_g": gain(ks[8], (DEPTH, D_MODEL)),
        "pre_ffn_g": gain(ks[9], (DEPTH, D_MODEL)),
        "post_ffn_g": gain(ks[10], (DEPTH, D_MODEL)),
        "w_in": dense(ks[11], (DEPTH, D_MODEL, IN_WIDTH), D_MODEL),
        "attn_sink": jax.random.normal(ks[12], (DEPTH, N_Q_HEADS_A), f32),
        "mla_q_norm_g": gain(ks[13], (DEPTH, Q_LORA_B)),
        "mla_w_uq": dense(ks[14], (DEPTH, Q_LORA_B, N_HEADS_B * (D_NOPE_B + D_ROPE_B)), Q_LORA_B),
        "mla_kv_norm_g": gain(ks[15], (DEPTH, KV_LORA_B)),
        "mla_w_ukv": dense(ks[16], (DEPTH, KV_LORA_B, N_HEADS_B * (D_NOPE_B + D_V_B)), KV_LORA_B),
        "mlstm_gate_b": gate_b,
        "mlstm_head_g": gain(ks[17], (DEPTH, W_C)),
        "w_out": dense(ks[18], (DEPTH, MIX_WIDTH, D_MODEL), MIX_WIDTH),
        "ffn_w_gate": dense(ks[19], (DEPTH, D_MODEL, D_FF), D_MODEL),
        "ffn_w_up": dense(ks[20], (DEPTH, D_MODEL, D_FF), D_MODEL),
        "ffn_w_down": dense(ks[21], (DEPTH, D_FF, D_MODEL), D_FF),
    }


def reference(x, c, positions, mod_w, mod_b, pre_mix_g, post_mix_g, pre_ffn_g, post_ffn_g,
              w_in, attn_sink, mla_q_norm_g, mla_w_uq, mla_kv_norm_g, mla_w_ukv,
              mlstm_gate_b, mlstm_head_g, w_out, ffn_w_gate, ffn_w_up, ffn_w_down):
    cos, sin = rope_tables(positions)
    split_pts = _split_points(IN_SIZES)
    for l in range(DEPTH):
        mod = jax.nn.silu(c) @ mod_w[l] + mod_b[l]
        shift1, scale1, gate1, shift2, scale2, gate2 = jnp.split(mod, 6, axis=-1)
        h = rms_norm(x, pre_mix_g[l]) * (1.0 + scale1[:, None, :]) + shift1[:, None, :]
        proj = h @ w_in[l]
        (aq, ak, av, bcq, bckv, bkr, cq, ck, cv, cg, co) = jnp.split(proj, split_pts, axis=-1)
        y_a = window_attention(aq, ak, av, attn_sink[l])
        y_b = mla_attention(bcq, bckv, bkr, mla_q_norm_g[l], mla_w_uq[l],
                            mla_kv_norm_g[l], mla_w_ukv[l], cos, sin)
        y_c = mlstm_mixer(cq, ck, cv, cg, co, mlstm_gate_b[l], mlstm_head_g[l])
        y = jnp.concatenate([y_a, y_b.astype(y_a.dtype), y_c.astype(y_a.dtype)], axis=-1) @ w_out[l]
        x = x + gate1[:, None, :] * rms_norm(y, post_mix_g[l])
        h = rms_norm(x, pre_ffn_g[l]) * (1.0 + scale2[:, None, :]) + shift2[:, None, :]
        f = (jax.nn.silu(h @ ffn_w_gate[l]) * (h @ ffn_w_up[l])) @ ffn_w_down[l]
        x = x + gate2[:, None, :] * rms_norm(f, post_ffn_g[l])
    return x
```

```python
import functools
import math

import numpy as np
import jax
import jax.numpy as jnp
from jax import lax
from jax.experimental import pallas as pl
from jax.experimental.pallas import tpu as pltpu

F32 = jnp.float32
BF16 = jnp.bfloat16

D_MODEL = 2048
EPS = 1e-6
N_Q_A, N_KV_A, DH_A, GROUP_A = 12, 4, 64, 3
WINDOW = 128
BLOCK_A = 128
N_H_B, Q_LORA, KV_LORA, D_NOPE, D_ROPE, D_V_B = 4, 448, 128, 128, 64, 128
ROPE_THETA = 10000.0
N_H_C, DQK_C, DV_C = 4, 96, 192
W_A, W_B, W_C = N_Q_A * DH_A, N_H_B * D_V_B, N_H_C * DV_C
D_FF = 5632

LANES = 128
DQK_PAD = 128
DV_PAD = 256
ONES_COL = DV_C
VMEM_LIMIT = 56 * 1024 * 1024

OFF_AQ, OFF_AK, OFF_AV = 0, 768, 1024
OFF_CKV, OFF_CG, OFF_CQKR = 1280, 1408, 1536
OFF_MQ, OFF_MK, OFF_MV, OFF_MO = 2048, 2560, 3072, 4096
IN_WIDTH_PAD = 5120
W_C_PAD = N_H_C * DV_PAD
MIX_PAD = W_A + W_B + W_C_PAD

CHUNK = 128


def _alibi_slopes(n):
    def pow2(m):
        start = 2.0 ** (-8.0 / m)
        return [start ** (i + 1) for i in range(m)]
    if math.log2(n).is_integer():
        s = pow2(n)
    else:
        p = 2 ** math.floor(math.log2(n))
        s = pow2(p) + pow2(2 * p)[0::2][: n - p]
    return [float(np.float32(v)) for v in s]


SLOPES_A = _alibi_slopes(N_Q_A)


def _cparams(sem):
    return pltpu.CompilerParams(dimension_semantics=sem, vmem_limit_bytes=VMEM_LIMIT)


def _dot(a, b):
    return jnp.dot(a, b, preferred_element_type=F32)


def _dot_nt(a, b):
    return lax.dot_general(a, b, (((1,), (1,)), ((), ())), preferred_element_type=F32)


def _dot_tn(a, b):
    return lax.dot_general(a, b, (((0,), (0,)), ((), ())), preferred_element_type=F32)


def _mod_kernel(c_ref, w_ref, b_ref, o_ref):
    c = c_ref[...]
    s = c * jax.nn.sigmoid(c)
    o_ref[...] = _dot(s.astype(BF16), w_ref[...].astype(BF16)) + b_ref[...]


def _modulation(c, mod_w, mod_b):
    depth, d, n = mod_w.shape
    bsz = c.shape[0]
    tn = 1024
    return pl.pallas_call(
        _mod_kernel,
        grid=(depth, n // tn),
        in_specs=[
            pl.BlockSpec((bsz, d), lambda l, j: (0, 0)),
            pl.BlockSpec((None, d, tn), lambda l, j: (l, 0, j)),
            pl.BlockSpec((None, 1, tn), lambda l, j: (l, 0, j)),
        ],
        out_specs=pl.BlockSpec((None, bsz, tn), lambda l, j: (l, 0, j)),
        out_shape=jax.ShapeDtypeStruct((depth, bsz, n), F32),
        compiler_params=_cparams(("parallel", "parallel")),
        name="adaln_mod",
    )(c, mod_w, mod_b.reshape(depth, 1, n))


def _mod_norm(x, g, scale, shift):
    ms = jnp.mean(x * x, axis=-1, keepdims=True)
    y = x * lax.rsqrt(ms + EPS) * g
    return y * (1.0 + scale) + shift


IN_TN = 512
GATE_TILE = OFF_CG // IN_TN
GATE_OFF = OFF_CG % IN_TN


def _in_kernel(x_ref, g_ref, sc_ref, sh_ref, w_ref, o_ref, gate_ref, h_scr):
    j = pl.program_id(2)

    @pl.when(j == 0)
    def _():
        h_scr[...] = _mod_norm(x_ref[...], g_ref[...], sc_ref[...], sh_ref[...]).astype(BF16)

    acc = _dot(h_scr[...], w_ref[...])
    o_ref[...] = acc.astype(BF16)

    @pl.when(j == GATE_TILE)
    def _():
        gate_ref[...] = acc[:, GATE_OFF:GATE_OFF + LANES]


def _in_proj(x, g, modl, w):
    bsz, s, d = x.shape
    n = w.shape[1]
    tm = min(1024, s)
    return pl.pallas_call(
        _in_kernel,
        grid=(bsz, s // tm, n // IN_TN),
        in_specs=[
            pl.BlockSpec((None, tm, d), lambda b, i, j: (b, i, 0)),
            pl.BlockSpec((1, d), lambda b, i, j: (0, 0)),
            pl.BlockSpec((None, 1, d), lambda b, i, j: (b, 0, 1)),
            pl.BlockSpec((None, 1, d), lambda b, i, j: (b, 0, 0)),
            pl.BlockSpec((d, IN_TN), lambda b, i, j: (0, j)),
        ],
        out_specs=[
            pl.BlockSpec((None, tm, IN_TN), lambda b, i, j: (b, i, j)),
            pl.BlockSpec((None, tm, LANES), lambda b, i, j: (b, i, 0)),
        ],
        out_shape=[
            jax.ShapeDtypeStruct((bsz, s, n), BF16),
            jax.ShapeDtypeStruct((bsz, s, LANES), F32),
        ],
        scratch_shapes=[pltpu.VMEM((tm, d), BF16)],
        compiler_params=_cparams(("parallel", "parallel", "arbitrary")),
        name="in_proj",
    )(x, g, modl, modl, w)


def _win_kernel(sink_ref, q_ref, kp_ref, kc_ref, kn_ref, vp_ref, vc_ref, vn_ref, o_ref, *, nb):
    n = pl.program_id(1)
    blk = BLOCK_A
    qi = lax.broadcasted_iota(jnp.int32, (blk, 3 * blk), 0)
    kj = lax.broadcasted_iota(jnp.int32, (blk, 3 * blk), 1)
    dist = jnp.abs(qi - kj + blk)
    in_seq = ((kj >= blk) | (n > 0)) & ((kj < 2 * blk) | (n < nb - 1))
    valid = (dist <= WINDOW) & in_seq
    distf = dist.astype(F32)
    k_all = jnp.concatenate([kp_ref[...], kc_ref[...], kn_ref[...]], axis=0)
    v_all = jnp.concatenate([vp_ref[...], vc_ref[...], vn_ref[...]], axis=0)
    low = lax.broadcasted_iota(jnp.int32, (blk, LANES), 1) < DH_A
    zero = jnp.zeros((blk, LANES), BF16)
    for p in range(2):
        kt = k_all[:, p * LANES:(p + 1) * LANES]
        vt = v_all[:, p * LANES:(p + 1) * LANES]
        for a in range(GROUP_A):
            t = p * GROUP_A + a
            q = q_ref[:, t * LANES:(t + 1) * LANES]
            outs = []
            for half in range(2):
                head = (2 * p + half) * GROUP_A + a
                qm = jnp.where(low, q, zero) if half == 0 else jnp.where(low, zero, q)
                s = _dot_nt(qm, kt) * (DH_A ** -0.5) - SLOPES_A[head] * distf
                s = jnp.where(valid, s, -jnp.inf)
                sink = sink_ref[head]
                m = jnp.maximum(jnp.max(s, axis=-1, keepdims=True), sink)
                e = jnp.exp(s - m)
                l = jnp.sum(e, axis=-1, keepdims=True) + jnp.exp(sink - m)
                outs.append(_dot(e.astype(BF16), vt) / l)
            o_ref[:, t * LANES:(t + 1) * LANES] = jnp.where(low, outs[0], outs[1]).astype(BF16)


def _window_attention(proj, sinks):
    bsz, s, _ = proj.shape
    blk = BLOCK_A
    nb = s // blk
    kb, vb = OFF_AK // 256, OFF_AV // 256
    prev = lambda n: jnp.maximum(n - 1, 0)
    nxt = lambda n: jnp.minimum(n + 1, nb - 1)
    return pl.pallas_call(
        functools.partial(_win_kernel, nb=nb),
        grid=(bsz, nb),
        in_specs=[
            pl.BlockSpec(memory_space=pltpu.SMEM),
            pl.BlockSpec((None, blk, W_A), lambda b, n: (b, n, 0)),
            pl.BlockSpec((None, blk, 256), lambda b, n: (b, prev(n), kb)),
            pl.BlockSpec((None, blk, 256), lambda b, n: (b, n, kb)),
            pl.BlockSpec((None, blk, 256), lambda b, n: (b, nxt(n), kb)),
            pl.BlockSpec((None, blk, 256), lambda b, n: (b, prev(n), vb)),
            pl.BlockSpec((None, blk, 256), lambda b, n: (b, n, vb)),
            pl.BlockSpec((None, blk, 256), lambda b, n: (b, nxt(n), vb)),
        ],
        out_specs=pl.BlockSpec((None, blk, W_A), lambda b, n: (b, n, 0)),
        out_shape=jax.ShapeDtypeStruct((bsz, s, W_A), BF16),
        compiler_params=_cparams(("parallel", "parallel")),
        name="win_attn",
    )(sinks, proj, proj, proj, proj, proj, proj, proj)


def _mla_pre_kernel(cqkr_ref, ckv_ref, pos_ref, qg_ref, wq_ref, kvg_ref, wkv_ref, inv_ref,
                    q_ref, k_ref, v_ref):
    tm = cqkr_ref.shape[0]
    t = cqkr_ref[...].astype(F32)
    lane = lax.broadcasted_iota(jnp.int32, (tm, 4 * LANES), 1)
    cq = jnp.where(lane < Q_LORA, t, 0.0)
    ms = jnp.sum(cq * cq, axis=-1, keepdims=True) * (1.0 / Q_LORA)
    qn = cq * lax.rsqrt(ms + EPS) * qg_ref[...]
    q = _dot(qn.astype(BF16), wq_ref[...])
    c = ckv_ref[...].astype(F32)
    ms = jnp.mean(c * c, axis=-1, keepdims=True)
    kvn = c * lax.rsqrt(ms + EPS) * kvg_ref[...]
    kv = _dot(kvn.astype(BF16), wkv_ref[...])

    ang = pos_ref[...].astype(F32) * inv_ref[...]
    cs, sn = jnp.cos(ang), jnp.sin(ang)
    l128 = lax.broadcasted_iota(jnp.int32, (tm, LANES), 1)
    first = (l128 % D_ROPE) < (D_ROPE // 2)
    sgn_sn = jnp.where(first, -sn, sn)

    def rope(x):
        swapped = jnp.where(first, pltpu.roll(x, LANES - D_ROPE // 2, 1), pltpu.roll(x, D_ROPE // 2, 1))
        return x * cs + swapped * sgn_sn

    kr = rope(t[:, 3 * LANES:4 * LANES])
    kr_hi = jnp.where(l128 >= D_ROPE, kr, 0.0)
    kr_lo = pltpu.roll(kr_hi, D_ROPE, 1)
    kr_tiles = (kr_lo.astype(BF16), kr_hi.astype(BF16))
    for pair in range(2):
        qr = rope(q[:, (4 + pair) * LANES:(5 + pair) * LANES]).astype(BF16)
        for half in range(2):
            h = 2 * pair + half
            q_ref[:, (2 * h) * LANES:(2 * h + 1) * LANES] = q[:, h * LANES:(h + 1) * LANES].astype(BF16)
            q_ref[:, (2 * h + 1) * LANES:(2 * h + 2) * LANES] = qr
            k_ref[:, (2 * h) * LANES:(2 * h + 1) * LANES] = kv[:, h * LANES:(h + 1) * LANES].astype(BF16)
            k_ref[:, (2 * h + 1) * LANES:(2 * h + 2) * LANES] = kr_tiles[half]
    v_ref[...] = kv[:, 4 * LANES:].astype(BF16)


def _mla_pre(proj, pos3, qg, wq, kvg, wkv, inv):
    bsz, s, _ = proj.shape
    tm = min(512, s)
    const = lambda shape: pl.BlockSpec(shape, lambda b, i: (0,) * len(shape))
    return pl.pallas_call(
        _mla_pre_kernel,
        grid=(bsz, s // tm),
        in_specs=[
            pl.BlockSpec((None, tm, 512), lambda b, i: (b, i, OFF_CQKR // 512)),
            pl.BlockSpec((None, tm, LANES), lambda b, i: (b, i, OFF_CKV // LANES)),
            pl.BlockSpec((None, tm, 1), lambda b, i: (b, i, 0)),
            const((1, 512)), const((512, 768)), const((1, LANES)), const((LANES, 1024)), const((1, LANES)),
        ],
        out_specs=[
            pl.BlockSpec((None, tm, 1024), lambda b, i: (b, i, 0)),
            pl.BlockSpec((None, tm, 1024), lambda b, i: (b, i, 0)),
            pl.BlockSpec((None, tm, 512), lambda b, i: (b, i, 0)),
        ],
        out_shape=[
            jax.ShapeDtypeStruct((bsz, s, 1024), BF16),
            jax.ShapeDtypeStruct((bsz, s, 1024), BF16),
            jax.ShapeDtypeStruct((bsz, s, 512), BF16),
        ],
        compiler_params=_cparams(("parallel", "parallel")),
        name="mla_pre",
    )(proj, proj, pos3, qg, wq, kvg, wkv, inv)


def _mla_attn_kernel(q_ref, k_ref, v_ref, o_ref, *, tk):
    tq = q_ref.shape[0]
    s_len = k_ref.shape[0]
    scale = (D_NOPE + D_ROPE) ** -0.5
    q = q_ref[...]
    m = jnp.full((tq, 1), -jnp.inf, F32)
    l = jnp.zeros((tq, 1), F32)
    acc = jnp.zeros((tq, D_V_B), F32)
    for c in range(s_len // tk):
        k = k_ref[c * tk:(c + 1) * tk, :]
        v = v_ref[c * tk:(c + 1) * tk, :]
        s = _dot_nt(q, k) * scale
        m_new = jnp.maximum(m, jnp.max(s, axis=-1, keepdims=True))
        alpha = jnp.exp(m - m_new)
        p = jnp.exp(s - m_new)
        l = alpha * l + jnp.sum(p, axis=-1, keepdims=True)
        acc = alpha * acc + _dot(p.astype(BF16), v)
        m = m_new
    o_ref[...] = (acc / l).astype(BF16)


def _mla_attention(q, k, v):
    bsz, s, _ = q.shape
    tq = min(256, s)
    tk = min(1024, s)
    return pl.pallas_call(
        functools.partial(_mla_attn_kernel, tk=tk),
        grid=(bsz, N_H_B, s // tq),
        in_specs=[
            pl.BlockSpec((None, tq, 256), lambda b, h, i: (b, i, h)),
            pl.BlockSpec((None, s, 256), lambda b, h, i: (b, 0, h)),
            pl.BlockSpec((None, s, D_V_B), lambda b, h, i: (b, 0, h)),
        ],
        out_specs=pl.BlockSpec((None, tq, D_V_B), lambda b, h, i: (b, i, h)),
        out_shape=jax.ShapeDtypeStruct((bsz, s, W_B), BF16),
        compiler_params=_cparams(("parallel", "parallel", "arbitrary")),
        name="mla_attn",
    )(q, k, v)


def _log_sigmoid(x):
    return jnp.minimum(x, 0.0) - jnp.log1p(jnp.exp(-jnp.abs(x)))


def _lane_cumsum(x, reverse):
    n = x.shape[-1]
    lane = lax.broadcasted_iota(jnp.int32, x.shape, 1)
    step = 1
    while step < n:
        if reverse:
            x = x + jnp.where(lane < n - step, pltpu.roll(x, n - step, 1), 0.0)
        else:
            x = x + jnp.where(lane >= step, pltpu.roll(x, step, 1), 0.0)
        step *= 2
    return x


def _mlstm_chunk(q, k, v_ext, li_row, gf_row, c_state, m_state, reverse):
    L = q.shape[0]
    scale = DQK_C ** -0.5
    lf_row = _log_sigmoid(gf_row)
    b_row = _lane_cumsum(lf_row, reverse)
    total = b_row[0:1, 0:1] if reverse else b_row[0:1, L - 1:L]
    r_row = (li_row - b_row)[0:1, :]
    m_loc = total + jnp.max(r_row, axis=-1, keepdims=True)
    ii = lax.broadcasted_iota(jnp.int32, (L, L), 0)
    jj = lax.broadcasted_iota(jnp.int32, (L, L), 1)
    causal = (jj >= ii) if reverse else (jj <= ii)
    b_col = jnp.sum(jnp.where(causal, lf_row[0:1, :], 0.0), axis=-1, keepdims=True)
    li_col = jnp.sum(jnp.where(ii == jj, li_row[0:1, :], 0.0), axis=-1, keepdims=True)
    r_col = li_col - b_col
    dmat = jnp.where(causal, b_col + r_row, -jnp.inf)
    m_intra = jnp.max(dmat, axis=-1, keepdims=True)
    pmat = jnp.exp(dmat - m_intra)
    s = _dot_nt(q, k)
    intra = _dot((s * pmat).astype(BF16), v_ext)
    qc = _dot(q, c_state.astype(BF16))
    inter_log = b_col + m_state
    m_t = jnp.maximum(inter_log, m_intra)
    num = (jnp.exp(inter_log - m_t) * scale) * qc + (jnp.exp(m_intra - m_t) * scale) * intra
    den = num[:, ONES_COL:ONES_COL + 1]
    h_ext = num / jnp.maximum(jnp.abs(den), jnp.exp(-m_t))
    w_col = jnp.exp(total + r_col - m_loc)
    c_loc = _dot_tn((k.astype(F32) * w_col).astype(BF16), v_ext)
    m_new = jnp.maximum(total + m_state, m_loc)
    c_new = jnp.exp(total + m_state - m_new) * c_state + jnp.exp(m_loc - m_new) * c_loc
    return h_ext, c_new, m_new


def _mlstm_kernel(q_ref, k_ref, v_ref, o_ref, gate_ref, gb_ref, hg_ref, y_ref,
                  hf_scr, hb_scr, cf_scr, cb_scr, rf_scr, rb_scr):
    s_len = q_ref.shape[0]
    L = CHUNK
    nc = s_len // L
    head = pl.program_id(1)
    cf_scr[...] = jnp.zeros_like(cf_scr)
    cb_scr[...] = jnp.zeros_like(cb_scr)
    ones_lane = lax.broadcasted_iota(jnp.int32, (L, DV_PAD), 1) == ONES_COL

    def one_dir(c, m_state, reverse, c_scr, r_scr, h_scr):
        r0 = pl.multiple_of(c * L, L)
        g = gate_ref[pl.ds(r0, L), :] + gb_ref[...]
        r_scr[...] = g.T[0:16, :]
        d = 1 if reverse else 0
        li_row = jnp.broadcast_to(r_scr[pl.ds(d * N_H_C + head, 1), :], (8, L))
        gf_row = jnp.broadcast_to(r_scr[pl.ds((2 + d) * N_H_C + head, 1), :], (8, L))
        q = q_ref[pl.ds(r0, L), :]
        k = k_ref[pl.ds(r0, L), :]
        v_ext = jnp.where(ones_lane, jnp.ones((L, DV_PAD), BF16), v_ref[pl.ds(r0, L), :])
        h_ext, c_new, m_new = _mlstm_chunk(q, k, v_ext, li_row, gf_row, c_scr[...], m_state, reverse)
        h_scr[pl.ds(r0, L), :] = h_ext
        c_scr[...] = c_new
        return m_new

    def body(c, carry):
        mf, mb = carry
        mf = one_dir(c, mf, False, cf_scr, rf_scr, hf_scr)
        mb = one_dir(nc - 1 - c, mb, True, cb_scr, rb_scr, hb_scr)
        return mf, mb

    neg = jnp.full((1, 1), -jnp.inf, F32)
    lax.fori_loop(0, nc, body, (neg, neg))

    real = lax.broadcasted_iota(jnp.int32, (L, DV_PAD), 1) < DV_C

    def epilogue(c, _):
        r0 = pl.multiple_of(c * L, L)
        h = jnp.where(real, hf_scr[pl.ds(r0, L), :] + hb_scr[pl.ds(r0, L), :], 0.0)
        ms = jnp.sum(h * h, axis=-1, keepdims=True) * (1.0 / DV_C)
        hn = h * lax.rsqrt(ms + EPS) * hg_ref[...]
        y = jax.nn.sigmoid(o_ref[pl.ds(r0, L), :].astype(F32)) * hn
        y_ref[pl.ds(r0, L), :] = y.astype(BF16)
        return 0

    lax.fori_loop(0, nc, epilogue, 0)


def _mlstm(proj, gates, gate_b, head_g):
    bsz, s, _ = proj.shape
    return pl.pallas_call(
        _mlstm_kernel,
        grid=(bsz, N_H_C),
        in_specs=[
            pl.BlockSpec((None, s, DQK_PAD), lambda b, h: (b, 0, OFF_MQ // DQK_PAD + h)),
            pl.BlockSpec((None, s, DQK_PAD), lambda b, h: (b, 0, OFF_MK // DQK_PAD + h)),
            pl.BlockSpec((None, s, DV_PAD), lambda b, h: (b, 0, OFF_MV // DV_PAD + h)),
            pl.BlockSpec((None, s, DV_PAD), lambda b, h: (b, 0, OFF_MO // DV_PAD + h)),
            pl.BlockSpec((None, s, LANES), lambda b, h: (b, 0, 0)),
            pl.BlockSpec((1, LANES), lambda b, h: (0, 0)),
            pl.BlockSpec((None, 1, DV_PAD), lambda b, h: (h, 0, 0)),
        ],
        out_specs=pl.BlockSpec((None, s, DV_PAD), lambda b, h: (b, 0, h)),
        out_shape=jax.ShapeDtypeStruct((bsz, s, W_C_PAD), BF16),
        scratch_shapes=[
            pltpu.VMEM((s, DV_PAD), F32), pltpu.VMEM((s, DV_PAD), F32),
            pltpu.VMEM((DQK_PAD, DV_PAD), F32), pltpu.VMEM((DQK_PAD, DV_PAD), F32),
            pltpu.VMEM((16, CHUNK), F32), pltpu.VMEM((16, CHUNK), F32),
        ],
        compiler_params=_cparams(("parallel", "arbitrary")),
        name="mlstm",
    )(proj, proj, proj, proj, gates, gate_b, head_g)


def _out_kernel(ya_ref, yb_ref, yc_ref, wa_ref, wb_ref, wc_ref, pg_ref, gate_ref, x_ref, o_ref):
    y = _dot(ya_ref[...], wa_ref[...]) + _dot(yb_ref[...], wb_ref[...]) + _dot(yc_ref[...], wc_ref[...])
    ms = jnp.mean(y * y, axis=-1, keepdims=True)
    yn = y * lax.rsqrt(ms + EPS) * pg_ref[...]
    o_ref[...] = x_ref[...] + gate_ref[...] * yn


def _out_proj(ya, yb, yc, wa, wb, wc, pg, modl, x):
    bsz, s, d = x.shape
    tm = min(256, s)
    row = lambda w: pl.BlockSpec((None, tm, w), lambda b, i: (b, i, 0))
    const = lambda shape: pl.BlockSpec(shape, lambda b, i: (0,) * len(shape))
    return pl.pallas_call(
        _out_kernel,
        grid=(bsz, s // tm),
        in_specs=[
            row(W_A), row(W_B), row(W_C_PAD),
            const((W_A, d)), const((W_B, d)), const((W_C_PAD, d)), const((1, d)),
            pl.BlockSpec((None, 1, d), lambda b, i: (b, 0, 2)),
            row(d),
        ],
        out_specs=row(d),
        out_shape=jax.ShapeDtypeStruct((bsz, s, d), F32),
        compiler_params=_cparams(("parallel", "parallel")),
        name="out_proj",
    )(ya, yb, yc, wa, wb, wc, pg, modl, x)


FFN_TF = 512


def _ffn_kernel(x_ref, g_ref, sc_ref, sh_ref, wg_ref, wu_ref, wd_ref, pg_ref, gate_ref, o_ref,
                h_scr, acc_scr):
    j = pl.program_id(2)

    @pl.when(j == 0)
    def _():
        h_scr[...] = _mod_norm(x_ref[...], g_ref[...], sc_ref[...], sh_ref[...]).astype(BF16)
        acc_scr[...] = jnp.zeros_like(acc_scr)

    h = h_scr[...]
    a = _dot(h, wg_ref[...])
    u = _dot(h, wu_ref[...])
    hid = (a * jax.nn.sigmoid(a)) * u
    acc_scr[...] += _dot(hid.astype(BF16), wd_ref[...])

    @pl.when(j == pl.num_programs(2) - 1)
    def _():
        f = acc_scr[...]
        ms = jnp.mean(f * f, axis=-1, keepdims=True)
        fn = f * lax.rsqrt(ms + EPS) * pg_ref[...]
        o_ref[...] = x_ref[...] + gate_ref[...] * fn


def _ffn(x, g, modl, wg, wu, wd, pg):
    bsz, s, d = x.shape
    f = wg.shape[1]
    tm = min(512, s)
    row = pl.BlockSpec((None, tm, d), lambda b, i, j: (b, i, 0))
    vec = pl.BlockSpec((1, d), lambda b, i, j: (0, 0))
    modv = lambda k: pl.BlockSpec((None, 1, d), lambda b, i, j: (b, 0, k))
    return pl.pallas_call(
        _ffn_kernel,
        grid=(bsz, s // tm, f // FFN_TF),
        in_specs=[
            row, vec, modv(4), modv(3),
            pl.BlockSpec((d, FFN_TF), lambda b, i, j: (0, j)),
            pl.BlockSpec((d, FFN_TF), lambda b, i, j: (0, j)),
            pl.BlockSpec((FFN_TF, d), lambda b, i, j: (j, 0)),
            vec, modv(5),
        ],
        out_specs=row,
        out_shape=jax.ShapeDtypeStruct((bsz, s, d), F32),
        scratch_shapes=[pltpu.VMEM((tm, d), BF16), pltpu.VMEM((tm, d), F32)],
        compiler_params=_cparams(("parallel", "parallel", "arbitrary")),
        name="ffn",
    )(x, g, modl, modl, wg, wu, wd, pg, modl)


def _pair_heads_a(w, axis):
    shape = w.shape
    w = jnp.moveaxis(w, axis, -1)
    lead = w.shape[:-1]
    w = w.reshape(lead + (2, 2, GROUP_A, DH_A))
    w = jnp.swapaxes(w, -3, -2)
    w = w.reshape(lead + (W_A,))
    return jnp.moveaxis(w, -1, axis).reshape(shape)


def _pad_heads(w, n_heads, width, padded, axis):
    w = jnp.moveaxis(w, axis, -1)
    lead = w.shape[:-1]
    w = w.reshape(lead + (n_heads, width))
    w = jnp.pad(w, [(0, 0)] * len(lead) + [(0, 0), (0, padded - width)])
    w = w.reshape(lead + (n_heads * padded,))
    return jnp.moveaxis(w, -1, axis)


def _layout_w_in(w):
    d = w.shape[0]
    sizes = (W_A, N_KV_A * DH_A, N_KV_A * DH_A, Q_LORA, KV_LORA, D_ROPE,
             N_H_C * DQK_C, N_H_C * DQK_C, W_C, 4 * N_H_C, W_C)
    pts = np.cumsum(sizes)[:-1].tolist()
    aq, ak, av, bcq, bckv, bkr, cq, ck, cv, cg, co = jnp.split(w, pts, axis=1)
    parts = [
        _pair_heads_a(aq, 1), ak, av, bckv,
        jnp.pad(cg, ((0, 0), (0, LANES - 4 * N_H_C))),
        bcq, bkr,
        _pad_heads(cq, N_H_C, DQK_C, DQK_PAD, 1), _pad_heads(ck, N_H_C, DQK_C, DQK_PAD, 1),
        _pad_heads(cv, N_H_C, DV_C, DV_PAD, 1), _pad_heads(co, N_H_C, DV_C, DV_PAD, 1),
    ]
    out = jnp.concatenate(parts, axis=1).astype(BF16)
    assert out.shape == (d, IN_WIDTH_PAD)
    return out


def _layout_w_uq(w):
    w = w.reshape(Q_LORA, N_H_B, D_NOPE + D_ROPE)
    nope = w[:, :, :D_NOPE].reshape(Q_LORA, N_H_B * D_NOPE)
    rope = w[:, :, D_NOPE:].reshape(Q_LORA, N_H_B * D_ROPE)
    w = jnp.concatenate([nope, rope], axis=1)
    return jnp.pad(w, ((0, 512 - Q_LORA), (0, 0))).astype(BF16)


def _layout_w_ukv(w):
    w = w.reshape(KV_LORA, N_H_B, D_NOPE + D_V_B)
    kn = w[:, :, :D_NOPE].reshape(KV_LORA, N_H_B * D_NOPE)
    v = w[:, :, D_NOPE:].reshape(KV_LORA, N_H_B * D_V_B)
    return jnp.concatenate([kn, v], axis=1).astype(BF16)


def kernel(x, c, positions, mod_w, mod_b, pre_mix_g, post_mix_g, pre_ffn_g, post_ffn_g, w_in, attn_sink, mla_q_norm_g, mla_w_uq, mla_kv_norm_g, mla_w_ukv, mlstm_gate_b, mlstm_head_g, w_out, ffn_w_gate, ffn_w_up, ffn_w_down):
    depth = mod_w.shape[0]
    bsz, s, d = x.shape
    mod = _modulation(c, mod_w, mod_b)
    pos3 = positions.reshape(bsz, s, 1)
    inv = 1.0 / (ROPE_THETA ** (jnp.arange(0, D_ROPE, 2, dtype=F32) / D_ROPE))
    inv128 = jnp.tile(inv, LANES // (D_ROPE // 2)).reshape(1, LANES)
    for l in range(depth):
        modl = mod[l].reshape(bsz, 1, 6 * d)
        proj, gates = _in_proj(x, pre_mix_g[l].reshape(1, d), modl, _layout_w_in(w_in[l]))
        ya = _window_attention(proj, attn_sink[l])
        qg = jnp.pad(mla_q_norm_g[l], (0, 512 - Q_LORA)).reshape(1, 512)
        qb, kb, vb = _mla_pre(proj, pos3, qg, _layout_w_uq(mla_w_uq[l]),
                              mla_kv_norm_g[l].reshape(1, KV_LORA), _layout_w_ukv(mla_w_ukv[l]), inv128)
        yb = _mla_attention(qb, kb, vb)
        gate_b = jnp.pad(mlstm_gate_b[l], (0, LANES - 4 * N_H_C)).reshape(1, LANES)
        head_g = jnp.pad(mlstm_head_g[l].reshape(N_H_C, 1, DV_C), ((0, 0), (0, 0), (0, DV_PAD - DV_C)))
        yc = _mlstm(proj, gates, gate_b, head_g)
        wo = w_out[l]
        wa = _pair_heads_a(wo[:W_A], 0).astype(BF16)
        wb = wo[W_A:W_A + W_B].astype(BF16)
        wc = _pad_heads(wo[W_A + W_B:], N_H_C, DV_C, DV_PAD, 0).astype(BF16)
        x = _out_proj(ya, yb, yc, wa, wb, wc, post_mix_g[l].reshape(1, d), modl, x)
        x = _ffn(x, pre_ffn_g[l].reshape(1, d), modl, ffn_w_gate[l].astype(BF16), ffn_w_up[l].astype(BF16),
                 ffn_w_down[l].astype(BF16), post_ffn_g[l].reshape(1, d))
    return x
```

```python
import functools
import math

import numpy as np
import jax
import jax.numpy as jnp
from jax import lax
from jax.experimental import pallas as pl
from jax.experimental.pallas import tpu as pltpu

F32 = jnp.float32
BF16 = jnp.bfloat16

D_MODEL = 2048
EPS = 1e-6
N_Q_A, N_KV_A, DH_A, GROUP_A = 12, 4, 64, 3
WINDOW = 128
BLOCK_A = 128
N_H_B, Q_LORA, KV_LORA, D_NOPE, D_ROPE, D_V_B = 4, 448, 128, 128, 64, 128
ROPE_THETA = 10000.0
N_H_C, DQK_C, DV_C = 4, 96, 192
W_A, W_B, W_C = N_Q_A * DH_A, N_H_B * D_V_B, N_H_C * DV_C
D_FF = 5632

LANES = 128
DQK_PAD = 128
DV_PAD = 256
VMEM_LIMIT = 56 * 1024 * 1024

OFF_AQ, OFF_AK, OFF_AV = 0, 768, 1024
OFF_CKV, OFF_CG, OFF_CQKR = 1280, 1408, 1536
OFF_MQ, OFF_MK, OFF_MV, OFF_MO = 2048, 2560, 3072, 4096
IN_WIDTH_PAD = 5120
W_C_PAD = N_H_C * DV_PAD
MIX_PAD = W_A + W_B + W_C_PAD

CHUNK = 128


def _alibi_slopes(n):
    def pow2(m):
        start = 2.0 ** (-8.0 / m)
        return [start ** (i + 1) for i in range(m)]
    if math.log2(n).is_integer():
        s = pow2(n)
    else:
        p = 2 ** math.floor(math.log2(n))
        s = pow2(p) + pow2(2 * p)[0::2][: n - p]
    return [float(np.float32(v)) for v in s]


SLOPES_A = _alibi_slopes(N_Q_A)


def _cparams(sem):
    return pltpu.CompilerParams(dimension_semantics=sem, vmem_limit_bytes=VMEM_LIMIT)


def _dot(a, b):
    return jnp.dot(a, b, preferred_element_type=F32)


def _dot_nt(a, b):
    return lax.dot_general(a, b, (((1,), (1,)), ((), ())), preferred_element_type=F32)


def _dot_tn(a, b):
    return lax.dot_general(a, b, (((0,), (0,)), ((), ())), preferred_element_type=F32)


def _mod_kernel(c_ref, w_ref, b_ref, o_ref):
    c = c_ref[...]
    s = c * jax.nn.sigmoid(c)
    o_ref[...] = _dot(s.astype(BF16), w_ref[...].astype(BF16)) + b_ref[...]


def _modulation(c, mod_w, mod_b):
    depth, d, n = mod_w.shape
    bsz = c.shape[0]
    tn = 1024
    return pl.pallas_call(
        _mod_kernel,
        grid=(depth, n // tn),
        in_specs=[
            pl.BlockSpec((bsz, d), lambda l, j: (0, 0)),
            pl.BlockSpec((None, d, tn), lambda l, j: (l, 0, j)),
            pl.BlockSpec((None, 1, tn), lambda l, j: (l, 0, j)),
        ],
        out_specs=pl.BlockSpec((None, bsz, tn), lambda l, j: (l, 0, j)),
        out_shape=jax.ShapeDtypeStruct((depth, bsz, n), F32),
        compiler_params=_cparams(("parallel", "parallel")),
        name="adaln_mod",
    )(c, mod_w, mod_b.reshape(depth, 1, n))


def _mod_norm(x, g, scale, shift):
    ms = jnp.mean(x * x, axis=-1, keepdims=True)
    y = x * lax.rsqrt(ms + EPS) * g
    return y * (1.0 + scale) + shift


IN_TN = 512
GATE_TILE = OFF_CG // IN_TN
GATE_OFF = OFF_CG % IN_TN


def _in_kernel(x_ref, g_ref, sc_ref, sh_ref, w_ref, o_ref, gate_ref, h_scr):
    j = pl.program_id(2)

    @pl.when(j == 0)
    def _():
        h_scr[...] = _mod_norm(x_ref[...], g_ref[...], sc_ref[...], sh_ref[...]).astype(BF16)

    acc = _dot(h_scr[...], w_ref[...])
    o_ref[...] = acc.astype(BF16)

    @pl.when(j == GATE_TILE)
    def _():
        gate_ref[...] = acc[:, GATE_OFF:GATE_OFF + LANES]


def _in_proj(x, g, modl, w):
    bsz, s, d = x.shape
    n = w.shape[1]
    tm = min(1024, s)
    return pl.pallas_call(
        _in_kernel,
        grid=(bsz, s // tm, n // IN_TN),
        in_specs=[
            pl.BlockSpec((None, tm, d), lambda b, i, j: (b, i, 0)),
            pl.BlockSpec((1, d), lambda b, i, j: (0, 0)),
            pl.BlockSpec((None, 1, d), lambda b, i, j: (b, 0, 1)),
            pl.BlockSpec((None, 1, d), lambda b, i, j: (b, 0, 0)),
            pl.BlockSpec((d, IN_TN), lambda b, i, j: (0, j)),
        ],
        out_specs=[
            pl.BlockSpec((None, tm, IN_TN), lambda b, i, j: (b, i, j)),
            pl.BlockSpec((None, tm, LANES), lambda b, i, j: (b, i, 0)),
        ],
        out_shape=[
            jax.ShapeDtypeStruct((bsz, s, n), BF16),
            jax.ShapeDtypeStruct((bsz, s, LANES), F32),
        ],
        scratch_shapes=[pltpu.VMEM((tm, d), BF16)],
        compiler_params=_cparams(("parallel", "parallel", "arbitrary")),
        name="in_proj",
    )(x, g, modl, modl, w)


def _win_kernel(sink_ref, q_ref, kp_ref, kc_ref, kn_ref, vp_ref, vc_ref, vn_ref, o_ref, *, nb):
    n = pl.program_id(1)
    blk = BLOCK_A
    qi = lax.broadcasted_iota(jnp.int32, (blk, 3 * blk), 0)
    kj = lax.broadcasted_iota(jnp.int32, (blk, 3 * blk), 1)
    dist = jnp.abs(qi - kj + blk)
    in_seq = ((kj >= blk) | (n > 0)) & ((kj < 2 * blk) | (n < nb - 1))
    valid = (dist <= WINDOW) & in_seq
    distf = dist.astype(F32)
    k_all = jnp.concatenate([kp_ref[...], kc_ref[...], kn_ref[...]], axis=0)
    v_all = jnp.concatenate([vp_ref[...], vc_ref[...], vn_ref[...]], axis=0)
    low = lax.broadcasted_iota(jnp.int32, (blk, LANES), 1) < DH_A
    zero = jnp.zeros((blk, LANES), BF16)
    for p in range(2):
        kt = k_all[:, p * LANES:(p + 1) * LANES]
        vt = v_all[:, p * LANES:(p + 1) * LANES]
        for a in range(GROUP_A):
            t = p * GROUP_A + a
            q = q_ref[:, t * LANES:(t + 1) * LANES]
            outs = []
            for half in range(2):
                head = (2 * p + half) * GROUP_A + a
                qm = jnp.where(low, q, zero) if half == 0 else jnp.where(low, zero, q)
                s = _dot_nt(qm, kt) * (DH_A ** -0.5) - SLOPES_A[head] * distf
                s = jnp.where(valid, s, -jnp.inf)
                sink = sink_ref[head]
                m = jnp.maximum(jnp.max(s, axis=-1, keepdims=True), sink)
                e = jnp.exp(s - m)
                l = jnp.sum(e, axis=-1, keepdims=True) + jnp.exp(sink - m)
                outs.append(_dot(e.astype(BF16), vt) / l)
            o_ref[:, t * LANES:(t + 1) * LANES] = jnp.where(low, outs[0], outs[1]).astype(BF16)


def _window_attention(proj, sinks):
    bsz, s, _ = proj.shape
    blk = BLOCK_A
    nb = s // blk
    kb, vb = OFF_AK // 256, OFF_AV // 256
    prev = lambda n: jnp.maximum(n - 1, 0)
    nxt = lambda n: jnp.minimum(n + 1, nb - 1)
    return pl.pallas_call(
        functools.partial(_win_kernel, nb=nb),
        grid=(bsz, nb),
        in_specs=[
            pl.BlockSpec(memory_space=pltpu.SMEM),
            pl.BlockSpec((None, blk, W_A), lambda b, n: (b, n, 0)),
            pl.BlockSpec((None, blk, 256), lambda b, n: (b, prev(n), kb)),
            pl.BlockSpec((None, blk, 256), lambda b, n: (b, n, kb)),
            pl.BlockSpec((None, blk, 256), lambda b, n: (b, nxt(n), kb)),
            pl.BlockSpec((None, blk, 256), lambda b, n: (b, prev(n), vb)),
            pl.BlockSpec((None, blk, 256), lambda b, n: (b, n, vb)),
            pl.BlockSpec((None, blk, 256), lambda b, n: (b, nxt(n), vb)),
        ],
        out_specs=pl.BlockSpec((None, blk, W_A), lambda b, n: (b, n, 0)),
        out_shape=jax.ShapeDtypeStruct((bsz, s, W_A), BF16),
        compiler_params=_cparams(("parallel", "parallel")),
        name="win_attn",
    )(sinks, proj, proj, proj, proj, proj, proj, proj)


MLA_TK = 512
MLA_QSCALE = (D_NOPE + D_ROPE) ** -0.5 * math.log2(math.e)


def _mla_pre_kernel(cqkr_ref, ckv_ref, pos_ref, qg_ref, wq_ref, kvg_ref, wk_ref, wvt_ref, inv_ref,
                    q_ref, k_ref, vt_ref):
    tm = cqkr_ref.shape[0]
    t = cqkr_ref[...].astype(F32)
    lane = lax.broadcasted_iota(jnp.int32, (tm, 4 * LANES), 1)
    cq = jnp.where(lane < Q_LORA, t, 0.0)
    ms = jnp.sum(cq * cq, axis=-1, keepdims=True) * (1.0 / Q_LORA)
    qn = cq * lax.rsqrt(ms + EPS) * qg_ref[...]
    q = _dot(qn.astype(BF16), wq_ref[...]) * MLA_QSCALE
    c = ckv_ref[...].astype(F32)
    ms = jnp.mean(c * c, axis=-1, keepdims=True)
    kvn = (c * lax.rsqrt(ms + EPS) * kvg_ref[...]).astype(BF16)
    kn = _dot(kvn, wk_ref[...])
    vt_ref[...] = _dot_nt(wvt_ref[...], kvn).astype(BF16)

    ang = pos_ref[...].astype(F32) * inv_ref[...]
    cs, sn = jnp.cos(ang), jnp.sin(ang)
    l128 = lax.broadcasted_iota(jnp.int32, (tm, LANES), 1)
    first = (l128 % D_ROPE) < (D_ROPE // 2)
    sgn_sn = jnp.where(first, -sn, sn)

    def rope(x):
        swapped = jnp.where(first, pltpu.roll(x, LANES - D_ROPE // 2, 1), pltpu.roll(x, D_ROPE // 2, 1))
        return x * cs + swapped * sgn_sn

    kr = rope(t[:, 3 * LANES:4 * LANES])
    kr_hi = jnp.where(l128 >= D_ROPE, kr, 0.0)
    kr_lo = pltpu.roll(kr_hi, D_ROPE, 1)
    kr_tiles = (kr_lo.astype(BF16), kr_hi.astype(BF16))
    for pair in range(2):
        qr = rope(q[:, (4 + pair) * LANES:(5 + pair) * LANES]).astype(BF16)
        for half in range(2):
            h = 2 * pair + half
            q_ref[:, (2 * h) * LANES:(2 * h + 1) * LANES] = q[:, h * LANES:(h + 1) * LANES].astype(BF16)
            q_ref[:, (2 * h + 1) * LANES:(2 * h + 2) * LANES] = qr
            k_ref[:, (2 * h) * LANES:(2 * h + 1) * LANES] = kn[:, h * LANES:(h + 1) * LANES].astype(BF16)
            k_ref[:, (2 * h + 1) * LANES:(2 * h + 2) * LANES] = kr_tiles[half]


def _mla_pre(proj, pos3, qg, wq, kvg, wk, wvt, inv):
    bsz, s, _ = proj.shape
    tm = MLA_TK
    const = lambda shape: pl.BlockSpec(shape, lambda b, i: (0,) * len(shape))
    return pl.pallas_call(
        _mla_pre_kernel,
        grid=(bsz, s // tm),
        in_specs=[
            pl.BlockSpec((None, tm, 512), lambda b, i: (b, i, OFF_CQKR // 512)),
            pl.BlockSpec((None, tm, LANES), lambda b, i: (b, i, OFF_CKV // LANES)),
            pl.BlockSpec((None, tm, 1), lambda b, i: (b, i, 0)),
            const((1, 512)), const((512, 768)), const((1, LANES)), const((LANES, W_B)), const((W_B, LANES)),
            const((1, LANES)),
        ],
        out_specs=[
            pl.BlockSpec((None, tm, 1024), lambda b, i: (b, i, 0)),
            pl.BlockSpec((None, tm, 1024), lambda b, i: (b, i, 0)),
            pl.BlockSpec((None, None, W_B, tm), lambda b, i: (b, i, 0, 0)),
        ],
        out_shape=[
            jax.ShapeDtypeStruct((bsz, s, 1024), BF16),
            jax.ShapeDtypeStruct((bsz, s, 1024), BF16),
            jax.ShapeDtypeStruct((bsz, s // tm, W_B, tm), BF16),
        ],
        compiler_params=_cparams(("parallel", "parallel")),
        name="mla_pre",
    )(proj, proj, pos3, qg, wq, kvg, wk, wvt, inv)


def _mla_attn_kernel(q_ref, k_ref, vt_ref, o_ref, st_scr):
    tq = q_ref.shape[0]
    n_chunks, _, tk = vt_ref.shape
    q = q_ref[...]
    m8 = jnp.full((8, tq), -jnp.inf, F32)
    for c in range(n_chunks):
        st = _dot_nt(k_ref[c * tk:(c + 1) * tk, :], q)
        st_scr[c * tk:(c + 1) * tk, :] = st
        m8 = jnp.maximum(m8, jnp.max(st.reshape(tk // 8, 8, tq), axis=0))
    m = jnp.max(m8, axis=0, keepdims=True)
    l8 = jnp.zeros((8, tq), F32)
    acc = jnp.zeros((D_V_B, tq), F32)
    for c in range(n_chunks):
        p = jnp.exp2(st_scr[c * tk:(c + 1) * tk, :] - m)
        l8 = l8 + jnp.sum(p.reshape(tk // 8, 8, tq), axis=0)
        acc = acc + _dot(vt_ref[c], p.astype(BF16))
    l = jnp.sum(l8, axis=0, keepdims=True)
    o_ref[...] = (acc / l).T.astype(BF16)


def _mla_attention(q, k, vt):
    bsz, s, _ = q.shape
    n_chunks, tk = vt.shape[1], vt.shape[3]
    tq = min(512, s)
    return pl.pallas_call(
        _mla_attn_kernel,
        grid=(bsz, N_H_B, s // tq),
        in_specs=[
            pl.BlockSpec((None, tq, 256), lambda b, h, i: (b, i, h)),
            pl.BlockSpec((None, s, 256), lambda b, h, i: (b, 0, h)),
            pl.BlockSpec((None, n_chunks, D_V_B, tk), lambda b, h, i: (b, 0, h, 0)),
        ],
        out_specs=pl.BlockSpec((None, tq, D_V_B), lambda b, h, i: (b, i, h)),
        out_shape=jax.ShapeDtypeStruct((bsz, s, W_B), BF16),
        scratch_shapes=[pltpu.VMEM((s, tq), F32)],
        compiler_params=_cparams(("parallel", "parallel", "arbitrary")),
        name="mla_attn",
    )(q, k, vt)


def _log_sigmoid(x):
    return jnp.minimum(x, 0.0) - jnp.log1p(jnp.exp(-jnp.abs(x)))


def _split3(x):
    hi = x.astype(BF16).astype(F32)
    r1 = x - hi
    mid = r1.astype(BF16).astype(F32)
    return jnp.concatenate([hi, mid, r1 - mid], axis=-1)


def _mlstm_kernel(q_ref, k_ref, v_ref, o_ref, gate_ref, gb_ref, hg_ref, y_ref,
                  hf_scr, hb_scr, tile_scr, kw_scr, rowq_scr, bp_scr, cf_scr, cb_scr, gt_scr):
    s_len = q_ref.shape[0]
    L = CHUNK
    nc = s_len // L
    head = pl.program_id(1)
    ii = lax.broadcasted_iota(jnp.int32, (L, L), 0)
    jj = lax.broadcasted_iota(jnp.int32, (L, L), 1)
    ones = jnp.ones((L, L), BF16)
    eye3 = jnp.concatenate([(ii == jj).astype(BF16)] * 3, axis=1)
    tri_ones = [jnp.concatenate([jnp.concatenate([m.astype(BF16), ones], axis=1)] * 3, axis=0)
                for m in (ii <= jj, ii >= jj)]
    masks = (jj <= ii, jj >= ii)
    h_scrs = (hf_scr, hb_scr)
    c_scrs = (cf_scr, cb_scr)

    def load_v_ext(r0):
        return jnp.concatenate([v_ref[pl.ds(r0, L), :], ones], axis=1)

    def transpose_gates(c, _):
        g = gate_ref[pl.ds(pl.multiple_of(c * L, L), L), :] + gb_ref[...]
        gt_scr[pl.ds(pl.multiple_of(c * 16, 16), 16), :] = g.T[0:16, :]
        return 0

    lax.fori_loop(0, nc, transpose_gates, 0, unroll=4)

    for d in range(2):
        li = gt_scr[pl.ds(d * N_H_C + head, nc, stride=16), :]
        lf = _log_sigmoid(gt_scr[pl.ds((2 + d) * N_H_C + head, nc, stride=16), :])
        bt = _dot(_split3(lf).astype(BF16), tri_ones[d])
        b, tot = bt[:, :L], bt[:, L:]
        r = li - b
        m_loc = tot + jnp.broadcast_to(jnp.max(r, axis=-1, keepdims=True), r.shape)
        rowq_scr[d, 0] = r
        rowq_scr[d, 1] = jnp.exp(tot + r - m_loc)
        rowq_scr[d, 2] = tot
        rowq_scr[d, 3] = m_loc
        bp_scr[d] = _split3(b + math.log(DQK_C ** -0.5))

    def prep(c, _):
        r0 = pl.multiple_of(c * L, L)
        q, k = q_ref[pl.ds(r0, L), :], k_ref[pl.ds(r0, L), :]
        s = _dot_nt(q, k)
        kt = k.astype(F32).T
        p = []
        for d in range(2):
            u = jnp.where(masks[d], rowq_scr[d, 0, pl.ds(c, 1), :], -jnp.inf)
            ct = jnp.broadcast_to(jnp.max(u, axis=-1, keepdims=True), (L, L))
            p.append((s * jnp.exp(u - ct)).astype(BF16))
            rhs = jnp.broadcast_to(bp_scr[d, pl.ds(c, 1), :], (L, 3 * L)).astype(BF16)
            tile_scr[pl.ds(r0, L), (2 * d) * L:(2 * d + 1) * L] = _dot_nt(eye3, rhs)
            tile_scr[pl.ds(r0, L), (2 * d + 1) * L:(2 * d + 2) * L] = ct
            kw_scr[d, c] = (kt * rowq_scr[d, 1, pl.ds(c, 1), :]).astype(BF16)
        intra = _dot(jnp.concatenate(p, axis=0), load_v_ext(r0))
        hf_scr[pl.ds(r0, L), :] = intra[:L]
        hb_scr[pl.ds(r0, L), :] = intra[L:]
        return 0

    lax.fori_loop(0, nc, prep, 0, unroll=2)

    cf_scr[...] = jnp.zeros_like(cf_scr)
    cb_scr[...] = jnp.zeros_like(cb_scr)

    def one_dir(c, m_state, d):
        h_scr, c_scr = h_scrs[d], c_scrs[d]
        r0 = pl.multiple_of(c * L, L)
        tot, m_loc = rowq_scr[d, 2, pl.ds(c, 1), :], rowq_scr[d, 3, pl.ds(c, 1), :]
        bt = tile_scr[pl.ds(r0, L), (2 * d) * L:(2 * d + 1) * L]
        ct = tile_scr[pl.ds(r0, L), (2 * d + 1) * L:(2 * d + 2) * L]
        v_ext = load_v_ext(r0)
        c_state = c_scr[...]
        qc = _dot(q_ref[pl.ds(r0, L), :], c_state.astype(BF16))
        mx = jnp.maximum(m_state, ct)
        iw = jnp.exp(m_state - mx)
        xw = jnp.exp(ct - mx)
        num = [iw * qc[:, j * L:(j + 1) * L] + xw * h_scr[pl.ds(r0, L), j * L:(j + 1) * L] for j in range(3)]
        inv = 1.0 / jnp.maximum(jnp.abs(num[2]), jnp.exp(-(bt + mx)))
        h_scr[pl.ds(r0, L), 0:L] = num[0] * inv
        h_scr[pl.ds(r0, L), L:2 * L] = num[1] * inv
        c_loc = _dot(kw_scr[d, c], v_ext)
        m_new = jnp.maximum(tot + m_state, m_loc)
        sp, sl = jnp.exp(tot + m_state - m_new), jnp.exp(m_loc - m_new)
        c_scr[...] = jnp.concatenate(
            [sp * c_state[:, j * L:(j + 1) * L] + sl * c_loc[:, j * L:(j + 1) * L] for j in range(3)], axis=1)
        return m_new

    def scan(c, carry):
        return one_dir(c, carry[0], 0), one_dir(nc - 1 - c, carry[1], 1)

    neg = jnp.full((1, LANES), -jnp.inf, F32)
    lax.fori_loop(0, nc, scan, (neg, neg), unroll=2)

    real = lax.broadcasted_iota(jnp.int32, (L, DV_PAD), 1) < DV_C

    def epilogue(c, _):
        r0 = pl.multiple_of(c * L, L)
        h = jnp.where(real, hf_scr[pl.ds(r0, L), 0:DV_PAD] + hb_scr[pl.ds(r0, L), 0:DV_PAD], 0.0)
        ms = jnp.sum(h * h, axis=-1, keepdims=True) * (1.0 / DV_C)
        hn = h * lax.rsqrt(ms + EPS) * hg_ref[...]
        y = jax.nn.sigmoid(o_ref[pl.ds(r0, L), :].astype(F32)) * hn
        y_ref[pl.ds(r0, L), :] = y.astype(BF16)
        return 0

    lax.fori_loop(0, nc, epilogue, 0, unroll=2)


def _mlstm(proj, gates, gate_b, head_g):
    bsz, s, _ = proj.shape
    nc = s // CHUNK
    return pl.pallas_call(
        _mlstm_kernel,
        grid=(bsz, N_H_C),
        in_specs=[
            pl.BlockSpec((None, s, DQK_PAD), lambda b, h: (b, 0, OFF_MQ // DQK_PAD + h)),
            pl.BlockSpec((None, s, DQK_PAD), lambda b, h: (b, 0, OFF_MK // DQK_PAD + h)),
            pl.BlockSpec((None, s, DV_PAD), lambda b, h: (b, 0, OFF_MV // DV_PAD + h)),
            pl.BlockSpec((None, s, DV_PAD), lambda b, h: (b, 0, OFF_MO // DV_PAD + h)),
            pl.BlockSpec((None, s, LANES), lambda b, h: (b, 0, 0)),
            pl.BlockSpec((1, LANES), lambda b, h: (0, 0)),
            pl.BlockSpec((None, 1, DV_PAD), lambda b, h: (h, 0, 0)),
        ],
        out_specs=pl.BlockSpec((None, s, DV_PAD), lambda b, h: (b, 0, h)),
        out_shape=jax.ShapeDtypeStruct((bsz, s, W_C_PAD), BF16),
        scratch_shapes=[
            pltpu.VMEM((s, DV_PAD + LANES), F32), pltpu.VMEM((s, DV_PAD + LANES), F32),
            pltpu.VMEM((s, 4 * CHUNK), F32),
            pltpu.VMEM((2, nc, DQK_PAD, CHUNK), BF16),
            pltpu.VMEM((2, 4, nc, CHUNK), F32),
            pltpu.VMEM((2, nc, 3 * CHUNK), F32),
            pltpu.VMEM((DQK_PAD, DV_PAD + LANES), F32), pltpu.VMEM((DQK_PAD, DV_PAD + LANES), F32),
            pltpu.VMEM((nc * 16, CHUNK), F32),
        ],
        compiler_params=_cparams(("parallel", "arbitrary")),
        name="mlstm",
    )(proj, proj, proj, proj, gates, gate_b, head_g)


def _out_kernel(ya_ref, yb_ref, yc_ref, wa_ref, wb_ref, wc_ref, pg_ref, gate_ref, x_ref, o_ref):
    y = _dot(ya_ref[...], wa_ref[...]) + _dot(yb_ref[...], wb_ref[...]) + _dot(yc_ref[...], wc_ref[...])
    ms = jnp.mean(y * y, axis=-1, keepdims=True)
    yn = y * lax.rsqrt(ms + EPS) * pg_ref[...]
    o_ref[...] = x_ref[...] + gate_ref[...] * yn


def _out_proj(ya, yb, yc, wa, wb, wc, pg, modl, x):
    bsz, s, d = x.shape
    tm = min(256, s)
    row = lambda w: pl.BlockSpec((None, tm, w), lambda b, i: (b, i, 0))
    const = lambda shape: pl.BlockSpec(shape, lambda b, i: (0,) * len(shape))
    return pl.pallas_call(
        _out_kernel,
        grid=(bsz, s // tm),
        in_specs=[
            row(W_A), row(W_B), row(W_C_PAD),
            const((W_A, d)), const((W_B, d)), const((W_C_PAD, d)), const((1, d)),
            pl.BlockSpec((None, 1, d), lambda b, i: (b, 0, 2)),
            row(d),
        ],
        out_specs=row(d),
        out_shape=jax.ShapeDtypeStruct((bsz, s, d), F32),
        compiler_params=_cparams(("parallel", "parallel")),
        name="out_proj",
    )(ya, yb, yc, wa, wb, wc, pg, modl, x)


FFN_TF = 512


def _ffn_kernel(x_ref, g_ref, sc_ref, sh_ref, wg_ref, wu_ref, wd_ref, pg_ref, gate_ref, o_ref,
                h_scr, acc_scr):
    j = pl.program_id(2)

    @pl.when(j == 0)
    def _():
        h_scr[...] = _mod_norm(x_ref[...], g_ref[...], sc_ref[...], sh_ref[...]).astype(BF16)
        acc_scr[...] = jnp.zeros_like(acc_scr)

    h = h_scr[...]
    a = _dot(h, wg_ref[...])
    u = _dot(h, wu_ref[...])
    hid = (a * jax.nn.sigmoid(a)) * u
    acc_scr[...] += _dot(hid.astype(BF16), wd_ref[...])

    @pl.when(j == pl.num_programs(2) - 1)
    def _():
        f = acc_scr[...]
        ms = jnp.mean(f * f, axis=-1, keepdims=True)
        fn = f * lax.rsqrt(ms + EPS) * pg_ref[...]
        o_ref[...] = x_ref[...] + gate_ref[...] * fn


def _ffn(x, g, modl, wg, wu, wd, pg):
    bsz, s, d = x.shape
    f = wg.shape[1]
    tm = min(512, s)
    row = pl.BlockSpec((None, tm, d), lambda b, i, j: (b, i, 0))
    vec = pl.BlockSpec((1, d), lambda b, i, j: (0, 0))
    modv = lambda k: pl.BlockSpec((None, 1, d), lambda b, i, j: (b, 0, k))
    return pl.pallas_call(
        _ffn_kernel,
        grid=(bsz, s // tm, f // FFN_TF),
        in_specs=[
            row, vec, modv(4), modv(3),
            pl.BlockSpec((d, FFN_TF), lambda b, i, j: (0, j)),
            pl.BlockSpec((d, FFN_TF), lambda b, i, j: (0, j)),
            pl.BlockSpec((FFN_TF, d), lambda b, i, j: (j, 0)),
            vec, modv(5),
        ],
        out_specs=row,
        out_shape=jax.ShapeDtypeStruct((bsz, s, d), F32),
        scratch_shapes=[pltpu.VMEM((tm, d), BF16), pltpu.VMEM((tm, d), F32)],
        compiler_params=_cparams(("parallel", "parallel", "arbitrary")),
        name="ffn",
    )(x, g, modl, modl, wg, wu, wd, pg, modl)


def _pair_heads_a(w, axis):
    shape = w.shape
    w = jnp.moveaxis(w, axis, -1)
    lead = w.shape[:-1]
    w = w.reshape(lead + (2, 2, GROUP_A, DH_A))
    w = jnp.swapaxes(w, -3, -2)
    w = w.reshape(lead + (W_A,))
    return jnp.moveaxis(w, -1, axis).reshape(shape)


def _pad_heads(w, n_heads, width, padded, axis):
    w = jnp.moveaxis(w, axis, -1)
    lead = w.shape[:-1]
    w = w.reshape(lead + (n_heads, width))
    w = jnp.pad(w, [(0, 0)] * len(lead) + [(0, 0), (0, padded - width)])
    w = w.reshape(lead + (n_heads * padded,))
    return jnp.moveaxis(w, -1, axis)


def _layout_w_in(w):
    d = w.shape[0]
    sizes = (W_A, N_KV_A * DH_A, N_KV_A * DH_A, Q_LORA, KV_LORA, D_ROPE,
             N_H_C * DQK_C, N_H_C * DQK_C, W_C, 4 * N_H_C, W_C)
    pts = np.cumsum(sizes)[:-1].tolist()
    aq, ak, av, bcq, bckv, bkr, cq, ck, cv, cg, co = jnp.split(w, pts, axis=1)
    parts = [
        _pair_heads_a(aq, 1), ak, av, bckv,
        jnp.pad(cg, ((0, 0), (0, LANES - 4 * N_H_C))),
        bcq, bkr,
        _pad_heads(cq, N_H_C, DQK_C, DQK_PAD, 1), _pad_heads(ck, N_H_C, DQK_C, DQK_PAD, 1),
        _pad_heads(cv, N_H_C, DV_C, DV_PAD, 1), _pad_heads(co, N_H_C, DV_C, DV_PAD, 1),
    ]
    out = jnp.concatenate(parts, axis=1).astype(BF16)
    assert out.shape == (d, IN_WIDTH_PAD)
    return out


def _layout_w_uq(w):
    w = w.reshape(Q_LORA, N_H_B, D_NOPE + D_ROPE)
    nope = w[:, :, :D_NOPE].reshape(Q_LORA, N_H_B * D_NOPE)
    rope = w[:, :, D_NOPE:].reshape(Q_LORA, N_H_B * D_ROPE)
    w = jnp.concatenate([nope, rope], axis=1)
    return jnp.pad(w, ((0, 512 - Q_LORA), (0, 0))).astype(BF16)


def _layout_w_ukv(w):
    w = w.reshape(KV_LORA, N_H_B, D_NOPE + D_V_B)
    kn = w[:, :, :D_NOPE].reshape(KV_LORA, N_H_B * D_NOPE)
    v = w[:, :, D_NOPE:].reshape(KV_LORA, N_H_B * D_V_B)
    return kn.astype(BF16), v.T.astype(BF16)


def kernel(x, c, positions, mod_w, mod_b, pre_mix_g, post_mix_g, pre_ffn_g, post_ffn_g, w_in, attn_sink, mla_q_norm_g, mla_w_uq, mla_kv_norm_g, mla_w_ukv, mlstm_gate_b, mlstm_head_g, w_out, ffn_w_gate, ffn_w_up, ffn_w_down):
    depth = mod_w.shape[0]
    bsz, s, d = x.shape
    mod = _modulation(c, mod_w, mod_b)
    pos3 = positions.reshape(bsz, s, 1)
    inv = 1.0 / (ROPE_THETA ** (jnp.arange(0, D_ROPE, 2, dtype=F32) / D_ROPE))
    inv128 = jnp.tile(inv, LANES // (D_ROPE // 2)).reshape(1, LANES)
    for l in range(depth):
        modl = mod[l].reshape(bsz, 1, 6 * d)
        proj, gates = _in_proj(x, pre_mix_g[l].reshape(1, d), modl, _layout_w_in(w_in[l]))
        ya = _window_attention(proj, attn_sink[l])
        qg = jnp.pad(mla_q_norm_g[l], (0, 512 - Q_LORA)).reshape(1, 512)
        wk, wvt = _layout_w_ukv(mla_w_ukv[l])
        qb, kb, vtb = _mla_pre(proj, pos3, qg, _layout_w_uq(mla_w_uq[l]),
                               mla_kv_norm_g[l].reshape(1, KV_LORA), wk, wvt, inv128)
        yb = _mla_attention(qb, kb, vtb)
        gate_b = jnp.pad(mlstm_gate_b[l], (0, LANES - 4 * N_H_C)).reshape(1, LANES)
        head_g = jnp.pad(mlstm_head_g[l].reshape(N_H_C, 1, DV_C), ((0, 0), (0, 0), (0, DV_PAD - DV_C)))
        yc = _mlstm(proj, gates, gate_b, head_g)
        wo = w_out[l]
        wa = _pair_heads_a(wo[:W_A], 0).astype(BF16)
        wb = wo[W_A:W_A + W_B].astype(BF16)
        wc = _pad_heads(wo[W_A + W_B:], N_H_C, DV_C, DV_PAD, 0).astype(BF16)
        x = _out_proj(ya, yb, yc, wa, wb, wc, post_mix_g[l].reshape(1, d), modl, x)
        x = _ffn(x, pre_ffn_g[l].reshape(1, d), modl, ffn_w_gate[l].astype(BF16), ffn_w_up[l].astype(BF16),
                 ffn_w_down[l].astype(BF16), post_ffn_g[l].reshape(1, d))
    return x
```

```python
import functools
import math

import numpy as np
import jax
import jax.numpy as jnp
from jax import lax
from jax.experimental import pallas as pl
from jax.experimental.pallas import tpu as pltpu

F32 = jnp.float32
BF16 = jnp.bfloat16

D_MODEL = 2048
EPS = 1e-6
N_Q_A, N_KV_A, DH_A, GROUP_A = 12, 4, 64, 3
WINDOW = 128
BLOCK_A = 128
N_H_B, Q_LORA, KV_LORA, D_NOPE, D_ROPE, D_V_B = 4, 448, 128, 128, 64, 128
ROPE_THETA = 10000.0
N_H_C, DQK_C, DV_C = 4, 96, 192
W_A, W_B, W_C = N_Q_A * DH_A, N_H_B * D_V_B, N_H_C * DV_C
D_FF = 5632

LANES = 128
DQK_PAD = 128
DV_PAD = 256
VMEM_LIMIT = 56 * 1024 * 1024

OFF_AQ, OFF_AK, OFF_AV = 0, 768, 1024
OFF_CKV, OFF_CG, OFF_CQKR = 1280, 1408, 1536
OFF_MQ, OFF_MK, OFF_MV, OFF_MO = 2048, 2560, 3072, 4096
IN_WIDTH_PAD = 5120
W_C_PAD = N_H_C * DV_PAD
MIX_PAD = W_A + W_B + W_C_PAD

CHUNK = 128


def _alibi_slopes(n):
    def pow2(m):
        start = 2.0 ** (-8.0 / m)
        return [start ** (i + 1) for i in range(m)]
    if math.log2(n).is_integer():
        s = pow2(n)
    else:
        p = 2 ** math.floor(math.log2(n))
        s = pow2(p) + pow2(2 * p)[0::2][: n - p]
    return [float(np.float32(v)) for v in s]


SLOPES_A = _alibi_slopes(N_Q_A)


def _cparams(sem, vmem_limit=VMEM_LIMIT):
    return pltpu.CompilerParams(dimension_semantics=sem, vmem_limit_bytes=vmem_limit)


def _dot(a, b):
    return jnp.dot(a, b, preferred_element_type=F32)


def _dot_nt(a, b):
    return lax.dot_general(a, b, (((1,), (1,)), ((), ())), preferred_element_type=F32)


def _dot_tn(a, b):
    return lax.dot_general(a, b, (((0,), (0,)), ((), ())), preferred_element_type=F32)


def _mod_kernel(c_ref, w_ref, b_ref, o_ref):
    c = c_ref[...]
    s = c * jax.nn.sigmoid(c)
    o_ref[...] = _dot(s.astype(BF16), w_ref[...].astype(BF16)) + b_ref[...]


def _modulation(c, mod_w, mod_b):
    depth, d, n = mod_w.shape
    bsz = c.shape[0]
    tn = 1024
    return pl.pallas_call(
        _mod_kernel,
        grid=(depth, n // tn),
        in_specs=[
            pl.BlockSpec((bsz, d), lambda l, j: (0, 0)),
            pl.BlockSpec((None, d, tn), lambda l, j: (l, 0, j)),
            pl.BlockSpec((None, 1, tn), lambda l, j: (l, 0, j)),
        ],
        out_specs=pl.BlockSpec((None, bsz, tn), lambda l, j: (l, 0, j)),
        out_shape=jax.ShapeDtypeStruct((depth, bsz, n), F32),
        compiler_params=_cparams(("parallel", "parallel")),
        name="adaln_mod",
    )(c, mod_w, mod_b.reshape(depth, 1, n))


def _mod_norm(x, g, scale, shift):
    ms = jnp.mean(x * x, axis=-1, keepdims=True)
    y = x * lax.rsqrt(ms + EPS) * g
    return y * (1.0 + scale) + shift


IN_TN = 512
GATE_TILE = OFF_CG // IN_TN
GATE_OFF = OFF_CG % IN_TN


def _in_kernel(x_ref, g_ref, sc_ref, sh_ref, w_ref, cs_ref, o_ref, gate_ref, h_scr):
    j = pl.program_id(2)

    @pl.when(j == 0)
    def _():
        h_scr[...] = _mod_norm(x_ref[...], g_ref[...], sc_ref[...], sh_ref[...]).astype(BF16)

    acc = _dot(h_scr[...], w_ref[...])
    o_ref[...] = (acc * cs_ref[...]).astype(BF16)

    @pl.when(j == GATE_TILE)
    def _():
        gate_ref[...] = acc[:, GATE_OFF:GATE_OFF + LANES]


def _in_proj(x, g, modl, w):
    bsz, s, d = x.shape
    n = w.shape[1]
    tm = min(1024, s)
    col_scale = jnp.where(jnp.arange(n) < W_A, WIN_QSCALE, 1.0).astype(F32).reshape(1, n)
    return pl.pallas_call(
        _in_kernel,
        grid=(bsz, s // tm, n // IN_TN),
        in_specs=[
            pl.BlockSpec((None, tm, d), lambda b, i, j: (b, i, 0)),
            pl.BlockSpec((1, d), lambda b, i, j: (0, 0)),
            pl.BlockSpec((None, 1, d), lambda b, i, j: (b, 0, 1)),
            pl.BlockSpec((None, 1, d), lambda b, i, j: (b, 0, 0)),
            pl.BlockSpec((d, IN_TN), lambda b, i, j: (0, j)),
            pl.BlockSpec((1, IN_TN), lambda b, i, j: (0, j)),
        ],
        out_specs=[
            pl.BlockSpec((None, tm, IN_TN), lambda b, i, j: (b, i, j)),
            pl.BlockSpec((None, tm, LANES), lambda b, i, j: (b, i, 0)),
        ],
        out_shape=[
            jax.ShapeDtypeStruct((bsz, s, n), BF16),
            jax.ShapeDtypeStruct((bsz, s, LANES), F32),
        ],
        scratch_shapes=[pltpu.VMEM((tm, d), BF16)],
        compiler_params=_cparams(("parallel", "parallel", "arbitrary")),
        name="in_proj",
    )(x, g, modl, modl, w, col_scale)


LOG2E = math.log2(math.e)
WIN_QSCALE = DH_A ** -0.5 * LOG2E


def _win_kernel(sink_ref, q_ref, kp_ref, kc_ref, kn_ref, vp_ref, vc_ref, vn_ref, o_ref, *, nb):
    n = pl.program_id(1)
    blk = BLOCK_A
    nk = 3 * blk
    qi = lax.broadcasted_iota(jnp.int32, (blk, nk), 0)
    kj = lax.broadcasted_iota(jnp.int32, (blk, nk), 1)
    dist = jnp.abs(qi - kj + blk)
    in_seq = ((kj >= blk) | (n > 0)) & ((kj < 2 * blk) | (n < nb - 1))
    dist_masked = jnp.where((dist <= WINDOW) & in_seq, dist.astype(F32), jnp.inf)
    k_all = jnp.concatenate([kp_ref[...], kc_ref[...], kn_ref[...]], axis=0)
    v_all = jnp.concatenate([vp_ref[...], vc_ref[...], vn_ref[...]], axis=0)
    low_k = lax.broadcasted_iota(jnp.int32, (nk, LANES), 1) < DH_A
    low_q = lax.broadcasted_iota(jnp.int32, (blk, LANES), 1) < DH_A
    zero = jnp.zeros((nk, LANES), BF16)
    for p in range(2):
        kt = k_all[:, p * LANES:(p + 1) * LANES]
        vt = v_all[:, p * LANES:(p + 1) * LANES]
        k_big = jnp.concatenate([jnp.where(low_k, kt, zero), jnp.where(low_k, zero, kt)], axis=0)
        v_big = jnp.concatenate([jnp.where(low_k, vt, zero), jnp.where(low_k, zero, vt)], axis=0)
        q3 = q_ref[:, p * GROUP_A * LANES:(p + 1) * GROUP_A * LANES]
        q3 = jnp.concatenate([q3[:, a * LANES:(a + 1) * LANES] for a in range(GROUP_A)], axis=0)
        s3 = _dot_nt(q3, k_big)
        p_rows, inv_rows = [], []
        for a in range(GROUP_A):
            p_halves, inv_halves = [], []
            for half in range(2):
                head = (2 * p + half) * GROUP_A + a
                s = s3[a * blk:(a + 1) * blk, half * nk:(half + 1) * nk] - (SLOPES_A[head] * LOG2E) * dist_masked
                sink = sink_ref[head] * LOG2E
                m = jnp.maximum(jnp.max(s, axis=-1, keepdims=True), sink)
                e = jnp.exp2(s - m)
                l = jnp.sum(e, axis=-1, keepdims=True) + jnp.exp2(sink - m)
                p_halves.append(e.astype(BF16))
                inv_halves.append(1.0 / l)
            p_rows.append(jnp.concatenate(p_halves, axis=1))
            inv_rows.append(jnp.where(low_q, inv_halves[0], inv_halves[1]))
        o3 = _dot(jnp.concatenate(p_rows, axis=0), v_big)
        for a in range(GROUP_A):
            t = p * GROUP_A + a
            o_ref[:, t * LANES:(t + 1) * LANES] = (o3[a * blk:(a + 1) * blk] * inv_rows[a]).astype(BF16)


def _window_attention(proj, sinks):
    bsz, s, _ = proj.shape
    blk = BLOCK_A
    nb = s // blk
    kb, vb = OFF_AK // 256, OFF_AV // 256
    prev = lambda n: jnp.maximum(n - 1, 0)
    nxt = lambda n: jnp.minimum(n + 1, nb - 1)
    return pl.pallas_call(
        functools.partial(_win_kernel, nb=nb),
        grid=(bsz, nb),
        in_specs=[
            pl.BlockSpec(memory_space=pltpu.SMEM),
            pl.BlockSpec((None, blk, W_A), lambda b, n: (b, n, 0)),
            pl.BlockSpec((None, blk, 256), lambda b, n: (b, prev(n), kb)),
            pl.BlockSpec((None, blk, 256), lambda b, n: (b, n, kb)),
            pl.BlockSpec((None, blk, 256), lambda b, n: (b, nxt(n), kb)),
            pl.BlockSpec((None, blk, 256), lambda b, n: (b, prev(n), vb)),
            pl.BlockSpec((None, blk, 256), lambda b, n: (b, n, vb)),
            pl.BlockSpec((None, blk, 256), lambda b, n: (b, nxt(n), vb)),
        ],
        out_specs=pl.BlockSpec((None, blk, W_A), lambda b, n: (b, n, 0)),
        out_shape=jax.ShapeDtypeStruct((bsz, s, W_A), BF16),
        compiler_params=_cparams(("parallel", "parallel")),
        name="win_attn",
    )(sinks, proj, proj, proj, proj, proj, proj, proj)


MLA_TK = 512
MLA_QSCALE = (D_NOPE + D_ROPE) ** -0.5 * math.log2(math.e)


def _mla_pre_kernel(cqkr_ref, ckv_ref, pos_ref, qg_ref, wq_ref, kvg_ref, wk_ref, wvt_ref, inv_ref,
                    q_ref, k_ref, vt_ref):
    tm = cqkr_ref.shape[0]
    t = cqkr_ref[...].astype(F32)
    lane = lax.broadcasted_iota(jnp.int32, (tm, 4 * LANES), 1)
    cq = jnp.where(lane < Q_LORA, t, 0.0)
    ms = jnp.sum(cq * cq, axis=-1, keepdims=True) * (1.0 / Q_LORA)
    qn = cq * lax.rsqrt(ms + EPS) * qg_ref[...]
    q = _dot(qn.astype(BF16), wq_ref[...]) * MLA_QSCALE
    c = ckv_ref[...].astype(F32)
    ms = jnp.mean(c * c, axis=-1, keepdims=True)
    kvn = (c * lax.rsqrt(ms + EPS) * kvg_ref[...]).astype(BF16)
    kn = _dot(kvn, wk_ref[...])
    vt_ref[...] = _dot_nt(wvt_ref[...], kvn).astype(BF16)

    ang = pos_ref[...].astype(F32) * inv_ref[...]
    cs, sn = jnp.cos(ang), jnp.sin(ang)
    l128 = lax.broadcasted_iota(jnp.int32, (tm, LANES), 1)
    first = (l128 % D_ROPE) < (D_ROPE // 2)
    sgn_sn = jnp.where(first, -sn, sn)

    def rope(x):
        swapped = jnp.where(first, pltpu.roll(x, LANES - D_ROPE // 2, 1), pltpu.roll(x, D_ROPE // 2, 1))
        return x * cs + swapped * sgn_sn

    kr = rope(t[:, 3 * LANES:4 * LANES])
    kr_hi = jnp.where(l128 >= D_ROPE, kr, 0.0)
    kr_lo = pltpu.roll(kr_hi, D_ROPE, 1)
    kr_tiles = (kr_lo.astype(BF16), kr_hi.astype(BF16))
    for pair in range(2):
        qr = rope(q[:, (4 + pair) * LANES:(5 + pair) * LANES]).astype(BF16)
        for half in range(2):
            h = 2 * pair + half
            q_ref[:, (2 * h) * LANES:(2 * h + 1) * LANES] = q[:, h * LANES:(h + 1) * LANES].astype(BF16)
            q_ref[:, (2 * h + 1) * LANES:(2 * h + 2) * LANES] = qr
            k_ref[:, (2 * h) * LANES:(2 * h + 1) * LANES] = kn[:, h * LANES:(h + 1) * LANES].astype(BF16)
            k_ref[:, (2 * h + 1) * LANES:(2 * h + 2) * LANES] = kr_tiles[half]


def _mla_pre(proj, pos3, qg, wq, kvg, wk, wvt, inv):
    bsz, s, _ = proj.shape
    tm = MLA_TK
    const = lambda shape: pl.BlockSpec(shape, lambda b, i: (0,) * len(shape))
    return pl.pallas_call(
        _mla_pre_kernel,
        grid=(bsz, s // tm),
        in_specs=[
            pl.BlockSpec((None, tm, 512), lambda b, i: (b, i, OFF_CQKR // 512)),
            pl.BlockSpec((None, tm, LANES), lambda b, i: (b, i, OFF_CKV // LANES)),
            pl.BlockSpec((None, tm, 1), lambda b, i: (b, i, 0)),
            const((1, 512)), const((512, 768)), const((1, LANES)), const((LANES, W_B)), const((W_B, LANES)),
            const((1, LANES)),
        ],
        out_specs=[
            pl.BlockSpec((None, tm, 1024), lambda b, i: (b, i, 0)),
            pl.BlockSpec((None, tm, 1024), lambda b, i: (b, i, 0)),
            pl.BlockSpec((None, None, W_B, tm), lambda b, i: (b, i, 0, 0)),
        ],
        out_shape=[
            jax.ShapeDtypeStruct((bsz, s, 1024), BF16),
            jax.ShapeDtypeStruct((bsz, s, 1024), BF16),
            jax.ShapeDtypeStruct((bsz, s // tm, W_B, tm), BF16),
        ],
        compiler_params=_cparams(("parallel", "parallel")),
        name="mla_pre",
    )(proj, proj, pos3, qg, wq, kvg, wk, wvt, inv)


def _mla_attn_kernel(q_ref, k_ref, vt_ref, o_ref, st_scr):
    tq = q_ref.shape[0]
    n_chunks, _, tk = vt_ref.shape
    q = q_ref[...]
    m8 = jnp.full((8, tq), -jnp.inf, F32)
    for c in range(n_chunks):
        st = _dot_nt(k_ref[c * tk:(c + 1) * tk, :], q)
        st_scr[c * tk:(c + 1) * tk, :] = st
        m8 = jnp.maximum(m8, jnp.max(st.reshape(tk // 8, 8, tq), axis=0))
    m = jnp.max(m8, axis=0, keepdims=True)
    l8 = jnp.zeros((8, tq), F32)
    acc = jnp.zeros((D_V_B, tq), F32)
    for c in range(n_chunks):
        p = jnp.exp2(st_scr[c * tk:(c + 1) * tk, :] - m)
        l8 = l8 + jnp.sum(p.reshape(tk // 8, 8, tq), axis=0)
        acc = acc + _dot(vt_ref[c], p.astype(BF16))
    l = jnp.sum(l8, axis=0, keepdims=True)
    o_ref[...] = (acc / l).T.astype(BF16)


def _mla_attention(q, k, vt):
    bsz, s, _ = q.shape
    n_chunks, tk = vt.shape[1], vt.shape[3]
    tq = min(512, s)
    return pl.pallas_call(
        _mla_attn_kernel,
        grid=(bsz, N_H_B, s // tq),
        in_specs=[
            pl.BlockSpec((None, tq, 256), lambda b, h, i: (b, i, h)),
            pl.BlockSpec((None, s, 256), lambda b, h, i: (b, 0, h)),
            pl.BlockSpec((None, n_chunks, D_V_B, tk), lambda b, h, i: (b, 0, h, 0)),
        ],
        out_specs=pl.BlockSpec((None, tq, D_V_B), lambda b, h, i: (b, i, h)),
        out_shape=jax.ShapeDtypeStruct((bsz, s, W_B), BF16),
        scratch_shapes=[pltpu.VMEM((s, tq), F32)],
        compiler_params=_cparams(("parallel", "parallel", "arbitrary")),
        name="mla_attn",
    )(q, k, vt)


def _log_sigmoid(x):
    return jnp.minimum(x, 0.0) - jnp.log1p(jnp.exp(-jnp.abs(x)))


def _split3(x):
    hi = x.astype(BF16).astype(F32)
    r1 = x - hi
    mid = r1.astype(BF16).astype(F32)
    return jnp.concatenate([hi, mid, r1 - mid], axis=-1)


def _mlstm_kernel(q_ref, k_ref, v_ref, o_ref, gate_ref, gb_ref, hg_ref, y_ref,
                  hf_scr, hb_scr, tile_scr, kw_scr, rowq_scr, bp_scr, cf_scr, cb_scr, gt_scr):
    s_len = q_ref.shape[0]
    L = CHUNK
    nc = s_len // L
    head = pl.program_id(1)
    ii = lax.broadcasted_iota(jnp.int32, (L, L), 0)
    jj = lax.broadcasted_iota(jnp.int32, (L, L), 1)
    ones = jnp.ones((L, L), BF16)
    eye3 = jnp.concatenate([(ii == jj).astype(BF16)] * 3, axis=1)
    tri_ones = [jnp.concatenate([jnp.concatenate([m.astype(BF16), ones], axis=1)] * 3, axis=0)
                for m in (ii <= jj, ii >= jj)]
    masks = (jj <= ii, jj >= ii)
    h_scrs = (hf_scr, hb_scr)
    c_scrs = (cf_scr, cb_scr)

    def load_v_ext(r0):
        return jnp.concatenate([v_ref[pl.ds(r0, L), :], ones], axis=1)

    def transpose_gates(c, _):
        g = gate_ref[pl.ds(pl.multiple_of(c * L, L), L), :] + gb_ref[...]
        gt_scr[pl.ds(pl.multiple_of(c * 16, 16), 16), :] = g.T[0:16, :]
        return 0

    lax.fori_loop(0, nc, transpose_gates, 0, unroll=4)

    for d in range(2):
        li = gt_scr[pl.ds(d * N_H_C + head, nc, stride=16), :]
        lf = _log_sigmoid(gt_scr[pl.ds((2 + d) * N_H_C + head, nc, stride=16), :])
        bt = _dot(_split3(lf).astype(BF16), tri_ones[d])
        b, tot = bt[:, :L], bt[:, L:]
        r = li - b
        m_loc = tot + jnp.broadcast_to(jnp.max(r, axis=-1, keepdims=True), r.shape)
        rowq_scr[d, 0] = r
        rowq_scr[d, 1] = jnp.exp(tot + r - m_loc)
        rowq_scr[d, 2] = tot
        rowq_scr[d, 3] = m_loc
        bp_scr[d] = _split3(b + math.log(DQK_C ** -0.5))

    def prep(c, _):
        r0 = pl.multiple_of(c * L, L)
        q, k = q_ref[pl.ds(r0, L), :], k_ref[pl.ds(r0, L), :]
        s = _dot_nt(q, k)
        kt = k.astype(F32).T
        p = []
        for d in range(2):
            u = jnp.where(masks[d], rowq_scr[d, 0, pl.ds(c, 1), :], -jnp.inf)
            ct = jnp.broadcast_to(jnp.max(u, axis=-1, keepdims=True), (L, L))
            p.append((s * jnp.exp(u - ct)).astype(BF16))
            rhs = jnp.broadcast_to(bp_scr[d, pl.ds(c, 1), :], (L, 3 * L)).astype(BF16)
            tile_scr[pl.ds(r0, L), (2 * d) * L:(2 * d + 1) * L] = _dot_nt(eye3, rhs)
            tile_scr[pl.ds(r0, L), (2 * d + 1) * L:(2 * d + 2) * L] = ct
            kw_scr[d, c] = (kt * rowq_scr[d, 1, pl.ds(c, 1), :]).astype(BF16)
        intra = _dot(jnp.concatenate(p, axis=0), load_v_ext(r0))
        hf_scr[pl.ds(r0, L), :] = intra[:L]
        hb_scr[pl.ds(r0, L), :] = intra[L:]
        return 0

    lax.fori_loop(0, nc, prep, 0, unroll=2)

    cf_scr[...] = jnp.zeros_like(cf_scr)
    cb_scr[...] = jnp.zeros_like(cb_scr)

    def one_dir(c, m_state, d):
        h_scr, c_scr = h_scrs[d], c_scrs[d]
        r0 = pl.multiple_of(c * L, L)
        tot, m_loc = rowq_scr[d, 2, pl.ds(c, 1), :], rowq_scr[d, 3, pl.ds(c, 1), :]
        bt = tile_scr[pl.ds(r0, L), (2 * d) * L:(2 * d + 1) * L]
        ct = tile_scr[pl.ds(r0, L), (2 * d + 1) * L:(2 * d + 2) * L]
        v_ext = load_v_ext(r0)
        c_state = c_scr[...]
        qc = _dot(q_ref[pl.ds(r0, L), :], c_state.astype(BF16))
        mx = jnp.maximum(m_state, ct)
        iw = jnp.exp(m_state - mx)
        xw = jnp.exp(ct - mx)
        num = [iw * qc[:, j * L:(j + 1) * L] + xw * h_scr[pl.ds(r0, L), j * L:(j + 1) * L] for j in range(3)]
        inv = 1.0 / jnp.maximum(jnp.abs(num[2]), jnp.exp(-(bt + mx)))
        h_scr[pl.ds(r0, L), 0:L] = num[0] * inv
        h_scr[pl.ds(r0, L), L:2 * L] = num[1] * inv
        c_loc = _dot(kw_scr[d, c], v_ext)
        m_new = jnp.maximum(tot + m_state, m_loc)
        sp, sl = jnp.exp(tot + m_state - m_new), jnp.exp(m_loc - m_new)
        c_scr[...] = jnp.concatenate(
            [sp * c_state[:, j * L:(j + 1) * L] + sl * c_loc[:, j * L:(j + 1) * L] for j in range(3)], axis=1)
        return m_new

    def scan(c, carry):
        return one_dir(c, carry[0], 0), one_dir(nc - 1 - c, carry[1], 1)

    neg = jnp.full((1, LANES), -jnp.inf, F32)
    lax.fori_loop(0, nc, scan, (neg, neg), unroll=2)

    real = lax.broadcasted_iota(jnp.int32, (L, DV_PAD), 1) < DV_C

    def epilogue(c, _):
        r0 = pl.multiple_of(c * L, L)
        h = jnp.where(real, hf_scr[pl.ds(r0, L), 0:DV_PAD] + hb_scr[pl.ds(r0, L), 0:DV_PAD], 0.0)
        ms = jnp.sum(h * h, axis=-1, keepdims=True) * (1.0 / DV_C)
        hn = h * lax.rsqrt(ms + EPS) * hg_ref[...]
        y = jax.nn.sigmoid(o_ref[pl.ds(r0, L), :].astype(F32)) * hn
        y_ref[pl.ds(r0, L), :] = y.astype(BF16)
        return 0

    lax.fori_loop(0, nc, epilogue, 0, unroll=2)


def _mlstm(proj, gates, gate_b, head_g):
    bsz, s, _ = proj.shape
    nc = s // CHUNK
    return pl.pallas_call(
        _mlstm_kernel,
        grid=(bsz, N_H_C),
        in_specs=[
            pl.BlockSpec((None, s, DQK_PAD), lambda b, h: (b, 0, OFF_MQ // DQK_PAD + h)),
            pl.BlockSpec((None, s, DQK_PAD), lambda b, h: (b, 0, OFF_MK // DQK_PAD + h)),
            pl.BlockSpec((None, s, DV_PAD), lambda b, h: (b, 0, OFF_MV // DV_PAD + h)),
            pl.BlockSpec((None, s, DV_PAD), lambda b, h: (b, 0, OFF_MO // DV_PAD + h)),
            pl.BlockSpec((None, s, LANES), lambda b, h: (b, 0, 0)),
            pl.BlockSpec((1, LANES), lambda b, h: (0, 0)),
            pl.BlockSpec((None, 1, DV_PAD), lambda b, h: (h, 0, 0)),
        ],
        out_specs=pl.BlockSpec((None, s, DV_PAD), lambda b, h: (b, 0, h)),
        out_shape=jax.ShapeDtypeStruct((bsz, s, W_C_PAD), BF16),
        scratch_shapes=[
            pltpu.VMEM((s, DV_PAD + LANES), F32), pltpu.VMEM((s, DV_PAD + LANES), F32),
            pltpu.VMEM((s, 4 * CHUNK), F32),
            pltpu.VMEM((2, nc, DQK_PAD, CHUNK), BF16),
            pltpu.VMEM((2, 4, nc, CHUNK), F32),
            pltpu.VMEM((2, nc, 3 * CHUNK), F32),
            pltpu.VMEM((DQK_PAD, DV_PAD + LANES), F32), pltpu.VMEM((DQK_PAD, DV_PAD + LANES), F32),
            pltpu.VMEM((nc * 16, CHUNK), F32),
        ],
        compiler_params=_cparams(("parallel", "arbitrary")),
        name="mlstm",
    )(proj, proj, proj, proj, gates, gate_b, head_g)


def _out_kernel(ya_ref, yb_ref, yc_ref, wa_ref, wb_ref, wc_ref, pg_ref, gate_ref, x_ref, o_ref):
    y = _dot(ya_ref[...], wa_ref[...]) + _dot(yb_ref[...], wb_ref[...]) + _dot(yc_ref[...], wc_ref[...])
    ms = jnp.mean(y * y, axis=-1, keepdims=True)
    yn = y * lax.rsqrt(ms + EPS) * pg_ref[...]
    o_ref[...] = x_ref[...] + gate_ref[...] * yn


def _out_proj(ya, yb, yc, wa, wb, wc, pg, modl, x):
    bsz, s, d = x.shape
    tm = min(256, s)
    row = lambda w: pl.BlockSpec((None, tm, w), lambda b, i: (b, i, 0))
    const = lambda shape: pl.BlockSpec(shape, lambda b, i: (0,) * len(shape))
    return pl.pallas_call(
        _out_kernel,
        grid=(bsz, s // tm),
        in_specs=[
            row(W_A), row(W_B), row(W_C_PAD),
            const((W_A, d)), const((W_B, d)), const((W_C_PAD, d)), const((1, d)),
            pl.BlockSpec((None, 1, d), lambda b, i: (b, 0, 2)),
            row(d),
        ],
        out_specs=row(d),
        out_shape=jax.ShapeDtypeStruct((bsz, s, d), F32),
        compiler_params=_cparams(("parallel", "parallel")),
        name="out_proj",
    )(ya, yb, yc, wa, wb, wc, pg, modl, x)


FFN_TF = 512
FFN_VMEM_LIMIT = 62 * 1024 * 1024


def _ffn_kernel(x_ref, g_ref, sc_ref, sh_ref, wg_ref, wu_ref, wd_ref, pg_ref, gate_ref, o_ref, h_scr):
    j = pl.program_id(2)

    @pl.when(j == 0)
    def _():
        h_scr[...] = _mod_norm(x_ref[...], g_ref[...], sc_ref[...], sh_ref[...]).astype(BF16)
        o_ref[...] = jnp.zeros_like(o_ref)

    h = h_scr[...]
    a = _dot(h, wg_ref[...])
    u = _dot(h, wu_ref[...])
    hid = (a * jax.nn.sigmoid(a)) * u
    o_ref[...] += _dot(hid.astype(BF16), wd_ref[...])

    @pl.when(j == pl.num_programs(2) - 1)
    def _():
        f = o_ref[...]
        ms = jnp.mean(f * f, axis=-1, keepdims=True)
        fn = f * lax.rsqrt(ms + EPS) * pg_ref[...]
        o_ref[...] = x_ref[...] + gate_ref[...] * fn


def _ffn(x, g, modl, wg, wu, wd, pg):
    bsz, s, d = x.shape
    f = wg.shape[1]
    tm = min(1024, s)
    row = pl.BlockSpec((None, tm, d), lambda b, i, j: (b, i, 0))
    vec = pl.BlockSpec((1, d), lambda b, i, j: (0, 0))
    modv = lambda k: pl.BlockSpec((None, 1, d), lambda b, i, j: (b, 0, k))
    return pl.pallas_call(
        _ffn_kernel,
        grid=(bsz, s // tm, f // FFN_TF),
        in_specs=[
            row, vec, modv(4), modv(3),
            pl.BlockSpec((d, FFN_TF), lambda b, i, j: (0, j)),
            pl.BlockSpec((d, FFN_TF), lambda b, i, j: (0, j)),
            pl.BlockSpec((FFN_TF, d), lambda b, i, j: (j, 0)),
            vec, modv(5),
        ],
        out_specs=row,
        out_shape=jax.ShapeDtypeStruct((bsz, s, d), F32),
        scratch_shapes=[pltpu.VMEM((tm, d), BF16)],
        compiler_params=_cparams(("parallel", "parallel", "arbitrary"), FFN_VMEM_LIMIT),
        name="ffn",
    )(x, g, modl, modl, wg, wu, wd, pg, modl)


def _pair_heads_a(w, axis):
    shape = w.shape
    w = jnp.moveaxis(w, axis, -1)
    lead = w.shape[:-1]
    w = w.reshape(lead + (2, 2, GROUP_A, DH_A))
    w = jnp.swapaxes(w, -3, -2)
    w = w.reshape(lead + (W_A,))
    return jnp.moveaxis(w, -1, axis).reshape(shape)


def _pad_heads(w, n_heads, width, padded, axis):
    w = jnp.moveaxis(w, axis, -1)
    lead = w.shape[:-1]
    w = w.reshape(lead + (n_heads, width))
    w = jnp.pad(w, [(0, 0)] * len(lead) + [(0, 0), (0, padded - width)])
    w = w.reshape(lead + (n_heads * padded,))
    return jnp.moveaxis(w, -1, axis)


def _layout_w_in(w):
    d = w.shape[0]
    sizes = (W_A, N_KV_A * DH_A, N_KV_A * DH_A, Q_LORA, KV_LORA, D_ROPE,
             N_H_C * DQK_C, N_H_C * DQK_C, W_C, 4 * N_H_C, W_C)
    pts = np.cumsum(sizes)[:-1].tolist()
    aq, ak, av, bcq, bckv, bkr, cq, ck, cv, cg, co = jnp.split(w, pts, axis=1)
    parts = [
        _pair_heads_a(aq, 1), ak, av, bckv,
        jnp.pad(cg, ((0, 0), (0, LANES - 4 * N_H_C))),
        bcq, bkr,
        _pad_heads(cq, N_H_C, DQK_C, DQK_PAD, 1), _pad_heads(ck, N_H_C, DQK_C, DQK_PAD, 1),
        _pad_heads(cv, N_H_C, DV_C, DV_PAD, 1), _pad_heads(co, N_H_C, DV_C, DV_PAD, 1),
    ]
    out = jnp.concatenate(parts, axis=1).astype(BF16)
    assert out.shape == (d, IN_WIDTH_PAD)
    return out


def _layout_w_uq(w):
    w = w.reshape(Q_LORA, N_H_B, D_NOPE + D_ROPE)
    nope = w[:, :, :D_NOPE].reshape(Q_LORA, N_H_B * D_NOPE)
    rope = w[:, :, D_NOPE:].reshape(Q_LORA, N_H_B * D_ROPE)
    w = jnp.concatenate([nope, rope], axis=1)
    return jnp.pad(w, ((0, 512 - Q_LORA), (0, 0))).astype(BF16)


def _layout_w_ukv(w):
    w = w.reshape(KV_LORA, N_H_B, D_NOPE + D_V_B)
    kn = w[:, :, :D_NOPE].reshape(KV_LORA, N_H_B * D_NOPE)
    v = w[:, :, D_NOPE:].reshape(KV_LORA, N_H_B * D_V_B)
    return kn.astype(BF16), v.T.astype(BF16)


def kernel(x, c, positions, mod_w, mod_b, pre_mix_g, post_mix_g, pre_ffn_g, post_ffn_g, w_in, attn_sink, mla_q_norm_g, mla_w_uq, mla_kv_norm_g, mla_w_ukv, mlstm_gate_b, mlstm_head_g, w_out, ffn_w_gate, ffn_w_up, ffn_w_down):
    depth = mod_w.shape[0]
    bsz, s, d = x.shape
    mod = _modulation(c, mod_w, mod_b)
    pos3 = positions.reshape(bsz, s, 1)
    inv = 1.0 / (ROPE_THETA ** (jnp.arange(0, D_ROPE, 2, dtype=F32) / D_ROPE))
    inv128 = jnp.tile(inv, LANES // (D_ROPE // 2)).reshape(1, LANES)
    for l in range(depth):
        modl = mod[l].reshape(bsz, 1, 6 * d)
        proj, gates = _in_proj(x, pre_mix_g[l].reshape(1, d), modl, _layout_w_in(w_in[l]))
        ya = _window_attention(proj, attn_sink[l])
        qg = jnp.pad(mla_q_norm_g[l], (0, 512 - Q_LORA)).reshape(1, 512)
        wk, wvt = _layout_w_ukv(mla_w_ukv[l])
        qb, kb, vtb = _mla_pre(proj, pos3, qg, _layout_w_uq(mla_w_uq[l]),
                               mla_kv_norm_g[l].reshape(1, KV_LORA), wk, wvt, inv128)
        yb = _mla_attention(qb, kb, vtb)
        gate_b = jnp.pad(mlstm_gate_b[l], (0, LANES - 4 * N_H_C)).reshape(1, LANES)
        head_g = jnp.pad(mlstm_head_g[l].reshape(N_H_C, 1, DV_C), ((0, 0), (0, 0), (0, DV_PAD - DV_C)))
        yc = _mlstm(proj, gates, gate_b, head_g)
        wo = w_out[l]
        wa = _pair_heads_a(wo[:W_A], 0).astype(BF16)
        wb = wo[W_A:W_A + W_B].astype(BF16)
        wc = _pad_heads(wo[W_A + W_B:], N_H_C, DV_C, DV_PAD, 0).astype(BF16)
        x = _out_proj(ya, yb, yc, wa, wb, wc, post_mix_g[l].reshape(1, d), modl, x)
        x = _ffn(x, pre_ffn_g[l].reshape(1, d), modl, ffn_w_gate[l].astype(BF16), ffn_w_up[l].astype(BF16),
                 ffn_w_down[l].astype(BF16), post_ffn_g[l].reshape(1, d))
    return x
```

```python
import functools
import math

import numpy as np
import jax
import jax.numpy as jnp
from jax import lax
from jax.experimental import pallas as pl
from jax.experimental.pallas import tpu as pltpu

F32 = jnp.float32
BF16 = jnp.bfloat16

D_MODEL = 2048
EPS = 1e-6
N_Q_A, N_KV_A, DH_A, GROUP_A = 12, 4, 64, 3
WINDOW = 128
BLOCK_A = 128
N_H_B, Q_LORA, KV_LORA, D_NOPE, D_ROPE, D_V_B = 4, 448, 128, 128, 64, 128
ROPE_THETA = 10000.0
N_H_C, DQK_C, DV_C = 4, 96, 192
W_A, W_B, W_C = N_Q_A * DH_A, N_H_B * D_V_B, N_H_C * DV_C
D_FF = 5632

LANES = 128
DQK_PAD = 128
DV_PAD = 256
VMEM_LIMIT = 56 * 1024 * 1024

OFF_AQ, OFF_AK, OFF_AV = 0, 768, 1024
OFF_CKV, OFF_CG, OFF_CQKR = 1280, 1408, 1536
OFF_MQ, OFF_MK, OFF_MV, OFF_MO = 2048, 2560, 3072, 4096
IN_WIDTH_PAD = 5120
W_C_PAD = N_H_C * DV_PAD
MIX_PAD = W_A + W_B + W_C_PAD

CHUNK = 128


def _alibi_slopes(n):
    def pow2(m):
        start = 2.0 ** (-8.0 / m)
        return [start ** (i + 1) for i in range(m)]
    if math.log2(n).is_integer():
        s = pow2(n)
    else:
        p = 2 ** math.floor(math.log2(n))
        s = pow2(p) + pow2(2 * p)[0::2][: n - p]
    return [float(np.float32(v)) for v in s]


SLOPES_A = _alibi_slopes(N_Q_A)


def _cparams(sem, vmem_limit=VMEM_LIMIT):
    return pltpu.CompilerParams(dimension_semantics=sem, vmem_limit_bytes=vmem_limit)


def _dot(a, b):
    return jnp.dot(a, b, preferred_element_type=F32)


def _dot_nt(a, b):
    return lax.dot_general(a, b, (((1,), (1,)), ((), ())), preferred_element_type=F32)


def _dot_tn(a, b):
    return lax.dot_general(a, b, (((0,), (0,)), ((), ())), preferred_element_type=F32)


def _mod_kernel(c_ref, w_ref, b_ref, o_ref):
    c = c_ref[...]
    s = c * jax.nn.sigmoid(c)
    o_ref[...] = _dot(s.astype(BF16), w_ref[...].astype(BF16)) + b_ref[...]


def _modulation(c, mod_w, mod_b):
    depth, d, n = mod_w.shape
    bsz = c.shape[0]
    tn = 1024
    return pl.pallas_call(
        _mod_kernel,
        grid=(depth, n // tn),
        in_specs=[
            pl.BlockSpec((bsz, d), lambda l, j: (0, 0)),
            pl.BlockSpec((None, d, tn), lambda l, j: (l, 0, j)),
            pl.BlockSpec((None, 1, tn), lambda l, j: (l, 0, j)),
        ],
        out_specs=pl.BlockSpec((None, bsz, tn), lambda l, j: (l, 0, j)),
        out_shape=jax.ShapeDtypeStruct((depth, bsz, n), F32),
        compiler_params=_cparams(("parallel", "parallel")),
        name="adaln_mod",
    )(c, mod_w, mod_b.reshape(depth, 1, n))


def _mod_norm(x, g, scale, shift):
    ms = jnp.mean(x * x, axis=-1, keepdims=True)
    y = x * lax.rsqrt(ms + EPS) * g
    return y * (1.0 + scale) + shift


IN_TN = 1024
GATE_TILE = OFF_CG // IN_TN
GATE_OFF = OFF_CG % IN_TN


def _in_kernel(x_ref, g_ref, sc_ref, sh_ref, w_ref, cs_ref, o_ref, gate_ref, h_scr):
    j = pl.program_id(2)

    @pl.when(j == 0)
    def _():
        h_scr[...] = _mod_norm(x_ref[...], g_ref[...], sc_ref[...], sh_ref[...]).astype(BF16)

    acc = _dot(h_scr[...], w_ref[...])
    o_ref[...] = (acc * cs_ref[...]).astype(BF16)

    @pl.when(j == GATE_TILE)
    def _():
        gate_ref[...] = acc[:, GATE_OFF:GATE_OFF + LANES]


def _in_proj(x, g, modl, w):
    bsz, s, d = x.shape
    n = w.shape[1]
    tm = min(1024, s)
    col_scale = jnp.where(jnp.arange(n) < W_A, WIN_QSCALE, 1.0).astype(F32).reshape(1, n)
    return pl.pallas_call(
        _in_kernel,
        grid=(bsz, s // tm, n // IN_TN),
        in_specs=[
            pl.BlockSpec((None, tm, d), lambda b, i, j: (b, i, 0)),
            pl.BlockSpec((1, d), lambda b, i, j: (0, 0)),
            pl.BlockSpec((None, 1, d), lambda b, i, j: (b, 0, 1)),
            pl.BlockSpec((None, 1, d), lambda b, i, j: (b, 0, 0)),
            pl.BlockSpec((d, IN_TN), lambda b, i, j: (0, j)),
            pl.BlockSpec((1, IN_TN), lambda b, i, j: (0, j)),
        ],
        out_specs=[
            pl.BlockSpec((None, tm, IN_TN), lambda b, i, j: (b, i, j)),
            pl.BlockSpec((None, tm, LANES), lambda b, i, j: (b, i, 0)),
        ],
        out_shape=[
            jax.ShapeDtypeStruct((bsz, s, n), BF16),
            jax.ShapeDtypeStruct((bsz, s, LANES), F32),
        ],
        scratch_shapes=[pltpu.VMEM((tm, d), BF16)],
        compiler_params=_cparams(("parallel", "parallel", "arbitrary")),
        name="in_proj",
    )(x, g, modl, modl, w, col_scale)


LOG2E = math.log2(math.e)
WIN_QSCALE = DH_A ** -0.5 * LOG2E


def _win_kernel(sink_ref, q_ref, kp_ref, kc_ref, kn_ref, vp_ref, vc_ref, vn_ref, o_ref, *, nb):
    n = pl.program_id(1)
    blk = BLOCK_A
    nk = 3 * blk
    qi = lax.broadcasted_iota(jnp.int32, (blk, nk), 0)
    kj = lax.broadcasted_iota(jnp.int32, (blk, nk), 1)
    dist = jnp.abs(qi - kj + blk)
    in_seq = ((kj >= blk) | (n > 0)) & ((kj < 2 * blk) | (n < nb - 1))
    dist_masked = jnp.where((dist <= WINDOW) & in_seq, dist.astype(F32), jnp.inf)
    k_all = jnp.concatenate([kp_ref[...], kc_ref[...], kn_ref[...]], axis=0)
    v_all = jnp.concatenate([vp_ref[...], vc_ref[...], vn_ref[...]], axis=0)
    low_k = lax.broadcasted_iota(jnp.int32, (nk, LANES), 1) < DH_A
    low_q = lax.broadcasted_iota(jnp.int32, (blk, LANES), 1) < DH_A
    zero = jnp.zeros((nk, LANES), BF16)
    for p in range(2):
        kt = k_all[:, p * LANES:(p + 1) * LANES]
        vt = v_all[:, p * LANES:(p + 1) * LANES]
        k_big = jnp.concatenate([jnp.where(low_k, kt, zero), jnp.where(low_k, zero, kt)], axis=0)
        v_big = jnp.concatenate([jnp.where(low_k, vt, zero), jnp.where(low_k, zero, vt)], axis=0)
        q3 = q_ref[:, p * GROUP_A * LANES:(p + 1) * GROUP_A * LANES]
        q3 = jnp.concatenate([q3[:, a * LANES:(a + 1) * LANES] for a in range(GROUP_A)], axis=0)
        s3 = _dot_nt(q3, k_big)
        p_rows, inv_rows = [], []
        for a in range(GROUP_A):
            p_halves, inv_halves = [], []
            for half in range(2):
                head = (2 * p + half) * GROUP_A + a
                s = s3[a * blk:(a + 1) * blk, half * nk:(half + 1) * nk] - (SLOPES_A[head] * LOG2E) * dist_masked
                sink = sink_ref[head] * LOG2E
                m = jnp.maximum(jnp.max(s, axis=-1, keepdims=True), sink)
                e = jnp.exp2(s - m)
                l = jnp.sum(e, axis=-1, keepdims=True) + jnp.exp2(sink - m)
                p_halves.append(e.astype(BF16))
                inv_halves.append(1.0 / l)
            p_rows.append(jnp.concatenate(p_halves, axis=1))
            inv_rows.append(jnp.where(low_q, inv_halves[0], inv_halves[1]))
        o3 = _dot(jnp.concatenate(p_rows, axis=0), v_big)
        for a in range(GROUP_A):
            t = p * GROUP_A + a
            o_ref[:, t * LANES:(t + 1) * LANES] = (o3[a * blk:(a + 1) * blk] * inv_rows[a]).astype(BF16)


def _window_attention(proj, sinks):
    bsz, s, _ = proj.shape
    blk = BLOCK_A
    nb = s // blk
    kb, vb = OFF_AK // 256, OFF_AV // 256
    prev = lambda n: jnp.maximum(n - 1, 0)
    nxt = lambda n: jnp.minimum(n + 1, nb - 1)
    return pl.pallas_call(
        functools.partial(_win_kernel, nb=nb),
        grid=(bsz, nb),
        in_specs=[
            pl.BlockSpec(memory_space=pltpu.SMEM),
            pl.BlockSpec((None, blk, W_A), lambda b, n: (b, n, 0)),
            pl.BlockSpec((None, blk, 256), lambda b, n: (b, prev(n), kb)),
            pl.BlockSpec((None, blk, 256), lambda b, n: (b, n, kb)),
            pl.BlockSpec((None, blk, 256), lambda b, n: (b, nxt(n), kb)),
            pl.BlockSpec((None, blk, 256), lambda b, n: (b, prev(n), vb)),
            pl.BlockSpec((None, blk, 256), lambda b, n: (b, n, vb)),
            pl.BlockSpec((None, blk, 256), lambda b, n: (b, nxt(n), vb)),
        ],
        out_specs=pl.BlockSpec((None, blk, W_A), lambda b, n: (b, n, 0)),
        out_shape=jax.ShapeDtypeStruct((bsz, s, W_A), BF16),
        compiler_params=_cparams(("parallel", "parallel")),
        name="win_attn",
    )(sinks, proj, proj, proj, proj, proj, proj, proj)


MLA_TK = 512
MLA_QSCALE = (D_NOPE + D_ROPE) ** -0.5 * math.log2(math.e)


def _mla_pre_kernel(cqkr_ref, ckv_ref, pos_ref, qg_ref, wq_ref, kvg_ref, wk_ref, wvt_ref, inv_ref,
                    q_ref, k_ref, vt_ref):
    tm = cqkr_ref.shape[0]
    t = cqkr_ref[...].astype(F32)
    lane = lax.broadcasted_iota(jnp.int32, (tm, 4 * LANES), 1)
    cq = jnp.where(lane < Q_LORA, t, 0.0)
    ms = jnp.sum(cq * cq, axis=-1, keepdims=True) * (1.0 / Q_LORA)
    qn = cq * lax.rsqrt(ms + EPS) * qg_ref[...]
    q = _dot(qn.astype(BF16), wq_ref[...]) * MLA_QSCALE
    c = ckv_ref[...].astype(F32)
    ms = jnp.mean(c * c, axis=-1, keepdims=True)
    kvn = (c * lax.rsqrt(ms + EPS) * kvg_ref[...]).astype(BF16)
    kn = _dot(kvn, wk_ref[...])
    vt_ref[...] = _dot_nt(wvt_ref[...], kvn).astype(BF16)

    ang = pos_ref[...].astype(F32) * inv_ref[...]
    cs, sn = jnp.cos(ang), jnp.sin(ang)
    l128 = lax.broadcasted_iota(jnp.int32, (tm, LANES), 1)
    first = (l128 % D_ROPE) < (D_ROPE // 2)
    sgn_sn = jnp.where(first, -sn, sn)

    def rope(x):
        swapped = jnp.where(first, pltpu.roll(x, LANES - D_ROPE // 2, 1), pltpu.roll(x, D_ROPE // 2, 1))
        return x * cs + swapped * sgn_sn

    kr = rope(t[:, 3 * LANES:4 * LANES])
    kr_hi = jnp.where(l128 >= D_ROPE, kr, 0.0)
    kr_lo = pltpu.roll(kr_hi, D_ROPE, 1)
    kr_tiles = (kr_lo.astype(BF16), kr_hi.astype(BF16))
    for pair in range(2):
        qr = rope(q[:, (4 + pair) * LANES:(5 + pair) * LANES]).astype(BF16)
        for half in range(2):
            h = 2 * pair + half
            q_ref[:, (2 * h) * LANES:(2 * h + 1) * LANES] = q[:, h * LANES:(h + 1) * LANES].astype(BF16)
            q_ref[:, (2 * h + 1) * LANES:(2 * h + 2) * LANES] = qr
            k_ref[:, (2 * h) * LANES:(2 * h + 1) * LANES] = kn[:, h * LANES:(h + 1) * LANES].astype(BF16)
            k_ref[:, (2 * h + 1) * LANES:(2 * h + 2) * LANES] = kr_tiles[half]


def _mla_pre(proj, pos3, qg, wq, kvg, wk, wvt, inv):
    bsz, s, _ = proj.shape
    tm = MLA_TK
    const = lambda shape: pl.BlockSpec(shape, lambda b, i: (0,) * len(shape))
    return pl.pallas_call(
        _mla_pre_kernel,
        grid=(bsz, s // tm),
        in_specs=[
            pl.BlockSpec((None, tm, 512), lambda b, i: (b, i, OFF_CQKR // 512)),
            pl.BlockSpec((None, tm, LANES), lambda b, i: (b, i, OFF_CKV // LANES)),
            pl.BlockSpec((None, tm, 1), lambda b, i: (b, i, 0)),
            const((1, 512)), const((512, 768)), const((1, LANES)), const((LANES, W_B)), const((W_B, LANES)),
            const((1, LANES)),
        ],
        out_specs=[
            pl.BlockSpec((None, tm, 1024), lambda b, i: (b, i, 0)),
            pl.BlockSpec((None, tm, 1024), lambda b, i: (b, i, 0)),
            pl.BlockSpec((None, None, W_B, tm), lambda b, i: (b, i, 0, 0)),
        ],
        out_shape=[
            jax.ShapeDtypeStruct((bsz, s, 1024), BF16),
            jax.ShapeDtypeStruct((bsz, s, 1024), BF16),
            jax.ShapeDtypeStruct((bsz, s // tm, W_B, tm), BF16),
        ],
        compiler_params=_cparams(("parallel", "parallel")),
        name="mla_pre",
    )(proj, proj, pos3, qg, wq, kvg, wk, wvt, inv)


def _mla_attn_kernel(q_ref, k_ref, vt_ref, o_ref, st_a, st_b, *, tq):
    n_chunks, _, tk = vt_ref.shape
    nt = q_ref.shape[0] // tq

    def scores(t, st_scr):
        q = q_ref[pl.ds(pl.multiple_of(t * tq, tq), tq), :]
        m8 = jnp.full((8, tq), -jnp.inf, F32)
        for c in range(n_chunks):
            st = _dot_nt(k_ref[c * tk:(c + 1) * tk, :], q)
            st_scr[c * tk:(c + 1) * tk, :] = st
            m8 = jnp.maximum(m8, jnp.max(st.reshape(tk // 8, 8, tq), axis=0))
        return jnp.max(m8, axis=0, keepdims=True)

    def output(t, st_scr, m):
        l8 = jnp.zeros((8, tq), F32)
        acc = jnp.zeros((D_V_B, tq), F32)
        for c in range(n_chunks):
            p = jnp.exp2(st_scr[c * tk:(c + 1) * tk, :] - m)
            l8 = l8 + jnp.sum(p.reshape(tk // 8, 8, tq), axis=0)
            acc = acc + _dot(vt_ref[c], p.astype(BF16))
        l = jnp.sum(l8, axis=0, keepdims=True)
        o_ref[pl.ds(pl.multiple_of(t * tq, tq), tq), :] = (acc / l).T.astype(BF16)

    m_first = scores(0, st_a)
    if nt == 1:
        output(0, st_a, m_first)
        return

    def pair(u, m_a):
        t = 2 * u
        m_b = scores(t + 1, st_b)
        output(t, st_a, m_a)
        m_a = scores(t + 2, st_a)
        output(t + 1, st_b, m_b)
        return m_a

    m_a = lax.fori_loop(0, nt // 2 - 1, pair, m_first)
    m_b = scores(nt - 1, st_b)
    output(nt - 2, st_a, m_a)
    output(nt - 1, st_b, m_b)


def _mla_attention(q, k, vt):
    bsz, s, _ = q.shape
    n_chunks, tk = vt.shape[1], vt.shape[3]
    tq = min(512, s)
    assert s // tq == 1 or (s // tq) % 2 == 0
    return pl.pallas_call(
        functools.partial(_mla_attn_kernel, tq=tq),
        grid=(bsz, N_H_B),
        in_specs=[
            pl.BlockSpec((None, s, 256), lambda b, h: (b, 0, h)),
            pl.BlockSpec((None, s, 256), lambda b, h: (b, 0, h)),
            pl.BlockSpec((None, n_chunks, D_V_B, tk), lambda b, h: (b, 0, h, 0)),
        ],
        out_specs=pl.BlockSpec((None, s, D_V_B), lambda b, h: (b, 0, h)),
        out_shape=jax.ShapeDtypeStruct((bsz, s, W_B), BF16),
        scratch_shapes=[pltpu.VMEM((s, tq), F32), pltpu.VMEM((s, tq), F32)],
        compiler_params=_cparams(("parallel", "parallel")),
        name="mla_attn",
    )(q, k, vt)


def _log_sigmoid(x):
    return jnp.minimum(x, 0.0) - jnp.log1p(jnp.exp(-jnp.abs(x)))


def _split3(x):
    hi = x.astype(BF16).astype(F32)
    r1 = x - hi
    mid = r1.astype(BF16).astype(F32)
    return jnp.concatenate([hi, mid, r1 - mid], axis=-1)


def _mlstm_kernel(q_ref, k_ref, v_ref, o_ref, gate_ref, gb_ref, hg_ref, y_ref,
                  hf_scr, hb_scr, tile_scr, kw_scr, rowq_scr, bp_scr, cf_scr, cb_scr, gt_scr):
    s_len = q_ref.shape[0]
    L = CHUNK
    nc = s_len // L
    head = pl.program_id(1)
    ii = lax.broadcasted_iota(jnp.int32, (L, L), 0)
    jj = lax.broadcasted_iota(jnp.int32, (L, L), 1)
    ones = jnp.ones((L, L), BF16)
    eye3 = jnp.concatenate([(ii == jj).astype(BF16)] * 3, axis=1)
    tri_ones = [jnp.concatenate([jnp.concatenate([m.astype(BF16), ones], axis=1)] * 3, axis=0)
                for m in (ii <= jj, ii >= jj)]
    masks = (jj <= ii, jj >= ii)
    h_scrs = (hf_scr, hb_scr)
    c_scrs = (cf_scr, cb_scr)

    def load_v_ext(r0):
        return jnp.concatenate([v_ref[pl.ds(r0, L), :], ones], axis=1)

    def transpose_gates(c, _):
        g = gate_ref[pl.ds(pl.multiple_of(c * L, L), L), :] + gb_ref[...]
        gt_scr[pl.ds(pl.multiple_of(c * 16, 16), 16), :] = g.T[0:16, :]
        return 0

    lax.fori_loop(0, nc, transpose_gates, 0, unroll=4)

    for d in range(2):
        li = gt_scr[pl.ds(d * N_H_C + head, nc, stride=16), :]
        lf = _log_sigmoid(gt_scr[pl.ds((2 + d) * N_H_C + head, nc, stride=16), :])
        bt = _dot(_split3(lf).astype(BF16), tri_ones[d])
        b, tot = bt[:, :L], bt[:, L:]
        r = li - b
        m_loc = tot + jnp.broadcast_to(jnp.max(r, axis=-1, keepdims=True), r.shape)
        rowq_scr[d, 0] = r
        rowq_scr[d, 1] = jnp.exp(tot + r - m_loc)
        rowq_scr[d, 2] = tot
        rowq_scr[d, 3] = m_loc
        bp_scr[d] = _split3(b + math.log(DQK_C ** -0.5))

    def prep(c, _):
        r0 = pl.multiple_of(c * L, L)
        q, k = q_ref[pl.ds(r0, L), :], k_ref[pl.ds(r0, L), :]
        s = _dot_nt(q, k)
        kt = k.astype(F32).T
        p = []
        for d in range(2):
            u = jnp.where(masks[d], rowq_scr[d, 0, pl.ds(c, 1), :], -jnp.inf)
            ct = jnp.broadcast_to(jnp.max(u, axis=-1, keepdims=True), (L, L))
            p.append((s * jnp.exp(u - ct)).astype(BF16))
            rhs = jnp.broadcast_to(bp_scr[d, pl.ds(c, 1), :], (L, 3 * L)).astype(BF16)
            tile_scr[pl.ds(r0, L), (2 * d) * L:(2 * d + 1) * L] = _dot_nt(eye3, rhs)
            tile_scr[pl.ds(r0, L), (2 * d + 1) * L:(2 * d + 2) * L] = ct
            kw_scr[d, c] = (kt * rowq_scr[d, 1, pl.ds(c, 1), :]).astype(BF16)
        intra = _dot(jnp.concatenate(p, axis=0), load_v_ext(r0))
        hf_scr[pl.ds(r0, L), :] = intra[:L]
        hb_scr[pl.ds(r0, L), :] = intra[L:]
        return 0

    lax.fori_loop(0, nc, prep, 0, unroll=2)

    cf_scr[...] = jnp.zeros_like(cf_scr)
    cb_scr[...] = jnp.zeros_like(cb_scr)

    def one_dir(c, m_state, d):
        h_scr, c_scr = h_scrs[d], c_scrs[d]
        r0 = pl.multiple_of(c * L, L)
        tot, m_loc = rowq_scr[d, 2, pl.ds(c, 1), :], rowq_scr[d, 3, pl.ds(c, 1), :]
        bt = tile_scr[pl.ds(r0, L), (2 * d) * L:(2 * d + 1) * L]
        ct = tile_scr[pl.ds(r0, L), (2 * d + 1) * L:(2 * d + 2) * L]
        v_ext = load_v_ext(r0)
        c_state = c_scr[...]
        qc = _dot(q_ref[pl.ds(r0, L), :], c_state.astype(BF16))
        mx = jnp.maximum(m_state, ct)
        iw = jnp.exp(m_state - mx)
        xw = jnp.exp(ct - mx)
        num = [iw * qc[:, j * L:(j + 1) * L] + xw * h_scr[pl.ds(r0, L), j * L:(j + 1) * L] for j in range(3)]
        inv = 1.0 / jnp.maximum(jnp.abs(num[2]), jnp.exp(-(bt + mx)))
        h_scr[pl.ds(r0, L), 0:L] = num[0] * inv
        h_scr[pl.ds(r0, L), L:2 * L] = num[1] * inv
        c_loc = _dot(kw_scr[d, c], v_ext)
        m_new = jnp.maximum(tot + m_state, m_loc)
        sp, sl = jnp.exp(tot + m_state - m_new), jnp.exp(m_loc - m_new)
        c_scr[...] = jnp.concatenate(
            [sp * c_state[:, j * L:(j + 1) * L] + sl * c_loc[:, j * L:(j + 1) * L] for j in range(3)], axis=1)
        return m_new

    def scan(c, carry):
        return one_dir(c, carry[0], 0), one_dir(nc - 1 - c, carry[1], 1)

    neg = jnp.full((1, LANES), -jnp.inf, F32)
    lax.fori_loop(0, nc, scan, (neg, neg), unroll=2)

    real = lax.broadcasted_iota(jnp.int32, (L, DV_PAD), 1) < DV_C

    def epilogue(c, _):
        r0 = pl.multiple_of(c * L, L)
        h = jnp.where(real, hf_scr[pl.ds(r0, L), 0:DV_PAD] + hb_scr[pl.ds(r0, L), 0:DV_PAD], 0.0)
        ms = jnp.sum(h * h, axis=-1, keepdims=True) * (1.0 / DV_C)
        hn = h * lax.rsqrt(ms + EPS) * hg_ref[...]
        y = jax.nn.sigmoid(o_ref[pl.ds(r0, L), :].astype(F32)) * hn
        y_ref[pl.ds(r0, L), :] = y.astype(BF16)
        return 0

    lax.fori_loop(0, nc, epilogue, 0, unroll=2)


def _mlstm(proj, gates, gate_b, head_g):
    bsz, s, _ = proj.shape
    nc = s // CHUNK
    return pl.pallas_call(
        _mlstm_kernel,
        grid=(bsz, N_H_C),
        in_specs=[
            pl.BlockSpec((None, s, DQK_PAD), lambda b, h: (b, 0, OFF_MQ // DQK_PAD + h)),
            pl.BlockSpec((None, s, DQK_PAD), lambda b, h: (b, 0, OFF_MK // DQK_PAD + h)),
            pl.BlockSpec((None, s, DV_PAD), lambda b, h: (b, 0, OFF_MV // DV_PAD + h)),
            pl.BlockSpec((None, s, DV_PAD), lambda b, h: (b, 0, OFF_MO // DV_PAD + h)),
            pl.BlockSpec((None, s, LANES), lambda b, h: (b, 0, 0)),
            pl.BlockSpec((1, LANES), lambda b, h: (0, 0)),
            pl.BlockSpec((None, 1, DV_PAD), lambda b, h: (h, 0, 0)),
        ],
        out_specs=pl.BlockSpec((None, s, DV_PAD), lambda b, h: (b, 0, h)),
        out_shape=jax.ShapeDtypeStruct((bsz, s, W_C_PAD), BF16),
        scratch_shapes=[
            pltpu.VMEM((s, DV_PAD + LANES), F32), pltpu.VMEM((s, DV_PAD + LANES), F32),
            pltpu.VMEM((s, 4 * CHUNK), F32),
            pltpu.VMEM((2, nc, DQK_PAD, CHUNK), BF16),
            pltpu.VMEM((2, 4, nc, CHUNK), F32),
            pltpu.VMEM((2, nc, 3 * CHUNK), F32),
            pltpu.VMEM((DQK_PAD, DV_PAD + LANES), F32), pltpu.VMEM((DQK_PAD, DV_PAD + LANES), F32),
            pltpu.VMEM((nc * 16, CHUNK), F32),
        ],
        compiler_params=_cparams(("parallel", "arbitrary")),
        name="mlstm",
    )(proj, proj, proj, proj, gates, gate_b, head_g)


def _out_kernel(ya_ref, yb_ref, yc_ref, wa_ref, wb_ref, wc_ref, pg_ref, gate_ref, x_ref, o_ref):
    y = _dot(ya_ref[...], wa_ref[...]) + _dot(yb_ref[...], wb_ref[...]) + _dot(yc_ref[...], wc_ref[...])
    ms = jnp.mean(y * y, axis=-1, keepdims=True)
    yn = y * lax.rsqrt(ms + EPS) * pg_ref[...]
    o_ref[...] = x_ref[...] + gate_ref[...] * yn


def _out_proj(ya, yb, yc, wa, wb, wc, pg, modl, x):
    bsz, s, d = x.shape
    tm = min(256, s)
    row = lambda w: pl.BlockSpec((None, tm, w), lambda b, i: (b, i, 0))
    const = lambda shape: pl.BlockSpec(shape, lambda b, i: (0,) * len(shape))
    return pl.pallas_call(
        _out_kernel,
        grid=(bsz, s // tm),
        in_specs=[
            row(W_A), row(W_B), row(W_C_PAD),
            const((W_A, d)), const((W_B, d)), const((W_C_PAD, d)), const((1, d)),
            pl.BlockSpec((None, 1, d), lambda b, i: (b, 0, 2)),
            row(d),
        ],
        out_specs=row(d),
        out_shape=jax.ShapeDtypeStruct((bsz, s, d), F32),
        compiler_params=_cparams(("parallel", "parallel")),
        name="out_proj",
    )(ya, yb, yc, wa, wb, wc, pg, modl, x)


FFN_TF = 512
FFN_VMEM_LIMIT = 62 * 1024 * 1024


def _ffn_kernel(x_ref, g_ref, sc_ref, sh_ref, wg_ref, wu_ref, wd_ref, pg_ref, gate_ref, o_ref, h_scr):
    j = pl.program_id(2)

    @pl.when(j == 0)
    def _():
        h_scr[...] = _mod_norm(x_ref[...], g_ref[...], sc_ref[...], sh_ref[...]).astype(BF16)
        o_ref[...] = jnp.zeros_like(o_ref)

    h = h_scr[...]
    a = _dot(h, wg_ref[...])
    u = _dot(h, wu_ref[...])
    hid = (a * jax.nn.sigmoid(a)) * u
    o_ref[...] += _dot(hid.astype(BF16), wd_ref[...])

    @pl.when(j == pl.num_programs(2) - 1)
    def _():
        f = o_ref[...]
        ms = jnp.mean(f * f, axis=-1, keepdims=True)
        fn = f * lax.rsqrt(ms + EPS) * pg_ref[...]
        o_ref[...] = x_ref[...] + gate_ref[...] * fn


def _ffn(x, g, modl, wg, wu, wd, pg):
    bsz, s, d = x.shape
    f = wg.shape[1]
    tm = min(1024, s)
    row = pl.BlockSpec((None, tm, d), lambda b, i, j: (b, i, 0))
    vec = pl.BlockSpec((1, d), lambda b, i, j: (0, 0))
    modv = lambda k: pl.BlockSpec((None, 1, d), lambda b, i, j: (b, 0, k))
    return pl.pallas_call(
        _ffn_kernel,
        grid=(bsz, s // tm, f // FFN_TF),
        in_specs=[
            row, vec, modv(4), modv(3),
            pl.BlockSpec((d, FFN_TF), lambda b, i, j: (0, j)),
            pl.BlockSpec((d, FFN_TF), lambda b, i, j: (0, j)),
            pl.BlockSpec((FFN_TF, d), lambda b, i, j: (j, 0)),
            vec, modv(5),
        ],
        out_specs=row,
        out_shape=jax.ShapeDtypeStruct((bsz, s, d), F32),
        scratch_shapes=[pltpu.VMEM((tm, d), BF16)],
        compiler_params=_cparams(("parallel", "parallel", "arbitrary"), FFN_VMEM_LIMIT),
        name="ffn",
    )(x, g, modl, modl, wg, wu, wd, pg, modl)


def _pair_heads_a(w, axis):
    shape = w.shape
    w = jnp.moveaxis(w, axis, -1)
    lead = w.shape[:-1]
    w = w.reshape(lead + (2, 2, GROUP_A, DH_A))
    w = jnp.swapaxes(w, -3, -2)
    w = w.reshape(lead + (W_A,))
    return jnp.moveaxis(w, -1, axis).reshape(shape)


def _pad_heads(w, n_heads, width, padded, axis):
    w = jnp.moveaxis(w, axis, -1)
    lead = w.shape[:-1]
    w = w.reshape(lead + (n_heads, width))
    w = jnp.pad(w, [(0, 0)] * len(lead) + [(0, 0), (0, padded - width)])
    w = w.reshape(lead + (n_heads * padded,))
    return jnp.moveaxis(w, -1, axis)


def _layout_w_in(w):
    d = w.shape[0]
    sizes = (W_A, N_KV_A * DH_A, N_KV_A * DH_A, Q_LORA, KV_LORA, D_ROPE,
             N_H_C * DQK_C, N_H_C * DQK_C, W_C, 4 * N_H_C, W_C)
    pts = np.cumsum(sizes)[:-1].tolist()
    aq, ak, av, bcq, bckv, bkr, cq, ck, cv, cg, co = jnp.split(w, pts, axis=1)
    parts = [
        _pair_heads_a(aq, 1), ak, av, bckv,
        jnp.pad(cg, ((0, 0), (0, LANES - 4 * N_H_C))),
        bcq, bkr,
        _pad_heads(cq, N_H_C, DQK_C, DQK_PAD, 1), _pad_heads(ck, N_H_C, DQK_C, DQK_PAD, 1),
        _pad_heads(cv, N_H_C, DV_C, DV_PAD, 1), _pad_heads(co, N_H_C, DV_C, DV_PAD, 1),
    ]
    out = jnp.concatenate(parts, axis=1).astype(BF16)
    assert out.shape == (d, IN_WIDTH_PAD)
    return out


def _layout_w_uq(w):
    w = w.reshape(Q_LORA, N_H_B, D_NOPE + D_ROPE)
    nope = w[:, :, :D_NOPE].reshape(Q_LORA, N_H_B * D_NOPE)
    rope = w[:, :, D_NOPE:].reshape(Q_LORA, N_H_B * D_ROPE)
    w = jnp.concatenate([nope, rope], axis=1)
    return jnp.pad(w, ((0, 512 - Q_LORA), (0, 0))).astype(BF16)


def _layout_w_ukv(w):
    w = w.reshape(KV_LORA, N_H_B, D_NOPE + D_V_B)
    kn = w[:, :, :D_NOPE].reshape(KV_LORA, N_H_B * D_NOPE)
    v = w[:, :, D_NOPE:].reshape(KV_LORA, N_H_B * D_V_B)
    return kn.astype(BF16), v.T.astype(BF16)


def kernel(x, c, positions, mod_w, mod_b, pre_mix_g, post_mix_g, pre_ffn_g, post_ffn_g, w_in, attn_sink, mla_q_norm_g, mla_w_uq, mla_kv_norm_g, mla_w_ukv, mlstm_gate_b, mlstm_head_g, w_out, ffn_w_gate, ffn_w_up, ffn_w_down):
    depth = mod_w.shape[0]
    bsz, s, d = x.shape
    mod = _modulation(c, mod_w, mod_b)
    pos3 = positions.reshape(bsz, s, 1)
    inv = 1.0 / (ROPE_THETA ** (jnp.arange(0, D_ROPE, 2, dtype=F32) / D_ROPE))
    inv128 = jnp.tile(inv, LANES // (D_ROPE // 2)).reshape(1, LANES)
    for l in range(depth):
        modl = mod[l].reshape(bsz, 1, 6 * d)
        proj, gates = _in_proj(x, pre_mix_g[l].reshape(1, d), modl, _layout_w_in(w_in[l]))
        ya = _window_attention(proj, attn_sink[l])
        qg = jnp.pad(mla_q_norm_g[l], (0, 512 - Q_LORA)).reshape(1, 512)
        wk, wvt = _layout_w_ukv(mla_w_ukv[l])
        qb, kb, vtb = _mla_pre(proj, pos3, qg, _layout_w_uq(mla_w_uq[l]),
                               mla_kv_norm_g[l].reshape(1, KV_LORA), wk, wvt, inv128)
        yb = _mla_attention(qb, kb, vtb)
        gate_b = jnp.pad(mlstm_gate_b[l], (0, LANES - 4 * N_H_C)).reshape(1, LANES)
        head_g = jnp.pad(mlstm_head_g[l].reshape(N_H_C, 1, DV_C), ((0, 0), (0, 0), (0, DV_PAD - DV_C)))
        yc = _mlstm(proj, gates, gate_b, head_g)
        wo = w_out[l]
        wa = _pair_heads_a(wo[:W_A], 0).astype(BF16)
        wb = wo[W_A:W_A + W_B].astype(BF16)
        wc = _pad_heads(wo[W_A + W_B:], N_H_C, DV_C, DV_PAD, 0).astype(BF16)
        x = _out_proj(ya, yb, yc, wa, wb, wc, post_mix_g[l].reshape(1, d), modl, x)
        x = _ffn(x, pre_ffn_g[l].reshape(1, d), modl, ffn_w_gate[l].astype(BF16), ffn_w_up[l].astype(BF16),
                 ffn_w_down[l].astype(BF16), post_ffn_g[l].reshape(1, d))
    return x
```

```python
import functools
import math

import numpy as np
import jax
import jax.numpy as jnp
from jax import lax
from jax.experimental import pallas as pl
from jax.experimental.pallas import tpu as pltpu

F32 = jnp.float32
BF16 = jnp.bfloat16

D_MODEL = 2048
EPS = 1e-6
N_Q_A, N_KV_A, DH_A, GROUP_A = 12, 4, 64, 3
WINDOW = 128
BLOCK_A = 128
N_H_B, Q_LORA, KV_LORA, D_NOPE, D_ROPE, D_V_B = 4, 448, 128, 128, 64, 128
ROPE_THETA = 10000.0
N_H_C, DQK_C, DV_C = 4, 96, 192
W_A, W_B, W_C = N_Q_A * DH_A, N_H_B * D_V_B, N_H_C * DV_C
D_FF = 5632

LANES = 128
DQK_PAD = 128
DV_PAD = 256
VMEM_LIMIT = 56 * 1024 * 1024

OFF_AQ, OFF_AK, OFF_AV = 0, 768, 1024
OFF_CKV, OFF_CG, OFF_CQKR = 1280, 1408, 1536
OFF_MQ, OFF_MK, OFF_MV, OFF_MO = 2048, 2560, 3072, 4096
IN_WIDTH_PAD = 5120
W_C_PAD = N_H_C * DV_PAD
MIX_PAD = W_A + W_B + W_C_PAD

CHUNK = 128


def _alibi_slopes(n):
    def pow2(m):
        start = 2.0 ** (-8.0 / m)
        return [start ** (i + 1) for i in range(m)]
    if math.log2(n).is_integer():
        s = pow2(n)
    else:
        p = 2 ** math.floor(math.log2(n))
        s = pow2(p) + pow2(2 * p)[0::2][: n - p]
    return [float(np.float32(v)) for v in s]


SLOPES_A = _alibi_slopes(N_Q_A)


def _cparams(sem, vmem_limit=VMEM_LIMIT):
    return pltpu.CompilerParams(dimension_semantics=sem, vmem_limit_bytes=vmem_limit)


def _dot(a, b):
    return jnp.dot(a, b, preferred_element_type=F32)


def _dot_nt(a, b):
    return lax.dot_general(a, b, (((1,), (1,)), ((), ())), preferred_element_type=F32)


def _dot_tn(a, b):
    return lax.dot_general(a, b, (((0,), (0,)), ((), ())), preferred_element_type=F32)


def _mod_kernel(c_ref, w_ref, b_ref, o_ref):
    c = c_ref[...]
    s = c * jax.nn.sigmoid(c)
    o_ref[...] = _dot(s.astype(BF16), w_ref[...].astype(BF16)) + b_ref[...]


def _modulation(c, mod_w, mod_b):
    depth, d, n = mod_w.shape
    bsz = c.shape[0]
    tn = 1024
    return pl.pallas_call(
        _mod_kernel,
        grid=(depth, n // tn),
        in_specs=[
            pl.BlockSpec((bsz, d), lambda l, j: (0, 0)),
            pl.BlockSpec((None, d, tn), lambda l, j: (l, 0, j)),
            pl.BlockSpec((None, 1, tn), lambda l, j: (l, 0, j)),
        ],
        out_specs=pl.BlockSpec((None, bsz, tn), lambda l, j: (l, 0, j)),
        out_shape=jax.ShapeDtypeStruct((depth, bsz, n), F32),
        compiler_params=_cparams(("parallel", "parallel")),
        name="adaln_mod",
    )(c, mod_w, mod_b.reshape(depth, 1, n))


ROW_SLAB = 16
SUB_ROWS = 256


def _row_slabs(first_row, n_rows):
    return [slice(r, r + ROW_SLAB) for r in range(first_row, first_row + n_rows, ROW_SLAB)]


def _mod_norm_rows(x_ref, g_ref, sc_ref, sh_ref, h_ref, first_row, n_rows):
    gain = g_ref[...] * (1.0 + sc_ref[...])
    shift = sh_ref[...]
    for rows in _row_slabs(first_row, n_rows):
        x = x_ref[rows, :]
        ms = jnp.mean(x * x, axis=-1, keepdims=True)
        h_ref[rows, :] = (x * lax.rsqrt(ms + EPS) * gain + shift).astype(BF16)


def _norm_residual_rows(y_ref, x_ref, pg_ref, gate_ref, o_ref, first_row, n_rows):
    gain = gate_ref[...] * pg_ref[...]
    for rows in _row_slabs(first_row, n_rows):
        y = y_ref[rows, :]
        ms = jnp.mean(y * y, axis=-1, keepdims=True)
        o_ref[rows, :] = x_ref[rows, :] + y * lax.rsqrt(ms + EPS) * gain


IN_TN = 1024
GATE_TILE = OFF_CG // IN_TN
GATE_OFF = OFF_CG % IN_TN


def _in_kernel(x_ref, g_ref, sc_ref, sh_ref, w_ref, cs_ref, o_ref, gate_ref, h_scr):
    j = pl.program_id(2)
    tm = x_ref.shape[0]

    def project(rows):
        acc = _dot(h_scr[rows, :], w_ref[...])
        o_ref[rows, :] = (acc * cs_ref[...]).astype(BF16)
        return acc

    @pl.when(j == 0)
    def _():
        for r in range(0, tm, SUB_ROWS):
            _mod_norm_rows(x_ref, g_ref, sc_ref, sh_ref, h_scr, r, SUB_ROWS)
            project(slice(r, r + SUB_ROWS))

    @pl.when(j != 0)
    def _():
        acc = project(slice(0, tm))

        @pl.when(j == GATE_TILE)
        def _():
            gate_ref[...] = acc[:, GATE_OFF:GATE_OFF + LANES]


def _in_proj(x, g, modl, w):
    bsz, s, d = x.shape
    n = w.shape[1]
    tm = min(1024, s)
    col_scale = jnp.where(jnp.arange(n) < W_A, WIN_QSCALE, 1.0).astype(F32).reshape(1, n)
    return pl.pallas_call(
        _in_kernel,
        grid=(bsz, s // tm, n // IN_TN),
        in_specs=[
            pl.BlockSpec((None, tm, d), lambda b, i, j: (b, i, 0)),
            pl.BlockSpec((1, d), lambda b, i, j: (0, 0)),
            pl.BlockSpec((None, 1, d), lambda b, i, j: (b, 0, 1)),
            pl.BlockSpec((None, 1, d), lambda b, i, j: (b, 0, 0)),
            pl.BlockSpec((d, IN_TN), lambda b, i, j: (0, j)),
            pl.BlockSpec((1, IN_TN), lambda b, i, j: (0, j)),
        ],
        out_specs=[
            pl.BlockSpec((None, tm, IN_TN), lambda b, i, j: (b, i, j)),
            pl.BlockSpec((None, tm, LANES), lambda b, i, j: (b, i, 0)),
        ],
        out_shape=[
            jax.ShapeDtypeStruct((bsz, s, n), BF16),
            jax.ShapeDtypeStruct((bsz, s, LANES), F32),
        ],
        scratch_shapes=[pltpu.VMEM((tm, d), BF16)],
        compiler_params=_cparams(("parallel", "parallel", "arbitrary")),
        name="in_proj",
    )(x, g, modl, modl, w, col_scale)


LOG2E = math.log2(math.e)
WIN_QSCALE = DH_A ** -0.5 * LOG2E


def _win_kernel(sink_ref, q_ref, kp_ref, kc_ref, kn_ref, vp_ref, vc_ref, vn_ref, o_ref, *, nb):
    n = pl.program_id(1)
    blk = BLOCK_A
    nk = 3 * blk
    qi = lax.broadcasted_iota(jnp.int32, (blk, nk), 0)
    kj = lax.broadcasted_iota(jnp.int32, (blk, nk), 1)
    dist = jnp.abs(qi - kj + blk)
    in_seq = ((kj >= blk) | (n > 0)) & ((kj < 2 * blk) | (n < nb - 1))
    dist_masked = jnp.where((dist <= WINDOW) & in_seq, dist.astype(F32), jnp.inf)
    k_all = jnp.concatenate([kp_ref[...], kc_ref[...], kn_ref[...]], axis=0)
    v_all = jnp.concatenate([vp_ref[...], vc_ref[...], vn_ref[...]], axis=0)
    low_k = lax.broadcasted_iota(jnp.int32, (nk, LANES), 1) < DH_A
    low_q = lax.broadcasted_iota(jnp.int32, (blk, LANES), 1) < DH_A
    zero = jnp.zeros((nk, LANES), BF16)
    for p in range(2):
        kt = k_all[:, p * LANES:(p + 1) * LANES]
        vt = v_all[:, p * LANES:(p + 1) * LANES]
        k_big = jnp.concatenate([jnp.where(low_k, kt, zero), jnp.where(low_k, zero, kt)], axis=0)
        v_big = jnp.concatenate([jnp.where(low_k, vt, zero), jnp.where(low_k, zero, vt)], axis=0)
        q3 = q_ref[:, p * GROUP_A * LANES:(p + 1) * GROUP_A * LANES]
        q3 = jnp.concatenate([q3[:, a * LANES:(a + 1) * LANES] for a in range(GROUP_A)], axis=0)
        s3 = _dot_nt(q3, k_big)
        p_rows, inv_rows = [], []
        for a in range(GROUP_A):
            p_halves, inv_halves = [], []
            for half in range(2):
                head = (2 * p + half) * GROUP_A + a
                s = s3[a * blk:(a + 1) * blk, half * nk:(half + 1) * nk] - (SLOPES_A[head] * LOG2E) * dist_masked
                sink = sink_ref[head] * LOG2E
                m = jnp.maximum(jnp.max(s, axis=-1, keepdims=True), sink)
                e = jnp.exp2(s - m)
                l = jnp.sum(e, axis=-1, keepdims=True) + jnp.exp2(sink - m)
                p_halves.append(e.astype(BF16))
                inv_halves.append(1.0 / l)
            p_rows.append(jnp.concatenate(p_halves, axis=1))
            inv_rows.append(jnp.where(low_q, inv_halves[0], inv_halves[1]))
        o3 = _dot(jnp.concatenate(p_rows, axis=0), v_big)
        for a in range(GROUP_A):
            t = p * GROUP_A + a
            o_ref[:, t * LANES:(t + 1) * LANES] = (o3[a * blk:(a + 1) * blk] * inv_rows[a]).astype(BF16)


def _window_attention(proj, sinks):
    bsz, s, _ = proj.shape
    blk = BLOCK_A
    nb = s // blk
    kb, vb = OFF_AK // 256, OFF_AV // 256
    prev = lambda n: jnp.maximum(n - 1, 0)
    nxt = lambda n: jnp.minimum(n + 1, nb - 1)
    return pl.pallas_call(
        functools.partial(_win_kernel, nb=nb),
        grid=(bsz, nb),
        in_specs=[
            pl.BlockSpec(memory_space=pltpu.SMEM),
            pl.BlockSpec((None, blk, W_A), lambda b, n: (b, n, 0)),
            pl.BlockSpec((None, blk, 256), lambda b, n: (b, prev(n), kb)),
            pl.BlockSpec((None, blk, 256), lambda b, n: (b, n, kb)),
            pl.BlockSpec((None, blk, 256), lambda b, n: (b, nxt(n), kb)),
            pl.BlockSpec((None, blk, 256), lambda b, n: (b, prev(n), vb)),
            pl.BlockSpec((None, blk, 256), lambda b, n: (b, n, vb)),
            pl.BlockSpec((None, blk, 256), lambda b, n: (b, nxt(n), vb)),
        ],
        out_specs=pl.BlockSpec((None, blk, W_A), lambda b, n: (b, n, 0)),
        out_shape=jax.ShapeDtypeStruct((bsz, s, W_A), BF16),
        compiler_params=_cparams(("parallel", "parallel")),
        name="win_attn",
    )(sinks, proj, proj, proj, proj, proj, proj, proj)


MLA_TK = 512
MLA_QSCALE = (D_NOPE + D_ROPE) ** -0.5 * math.log2(math.e)


def _mla_pre_kernel(cqkr_ref, ckv_ref, pos_ref, qg_ref, wq_ref, kvg_ref, wk_ref, wvt_ref, inv_ref,
                    q_ref, k_ref, vt_ref):
    tm = cqkr_ref.shape[0]
    t = cqkr_ref[...].astype(F32)
    lane = lax.broadcasted_iota(jnp.int32, (tm, 4 * LANES), 1)
    cq = jnp.where(lane < Q_LORA, t, 0.0)
    ms = jnp.sum(cq * cq, axis=-1, keepdims=True) * (1.0 / Q_LORA)
    qn = cq * lax.rsqrt(ms + EPS) * qg_ref[...]
    q = _dot(qn.astype(BF16), wq_ref[...]) * MLA_QSCALE
    c = ckv_ref[...].astype(F32)
    ms = jnp.mean(c * c, axis=-1, keepdims=True)
    kvn = (c * lax.rsqrt(ms + EPS) * kvg_ref[...]).astype(BF16)
    kn = _dot(kvn, wk_ref[...])
    vt_ref[...] = _dot_nt(wvt_ref[...], kvn).astype(BF16)

    ang = pos_ref[...].astype(F32) * inv_ref[...]
    cs, sn = jnp.cos(ang), jnp.sin(ang)
    l128 = lax.broadcasted_iota(jnp.int32, (tm, LANES), 1)
    first = (l128 % D_ROPE) < (D_ROPE // 2)
    sgn_sn = jnp.where(first, -sn, sn)

    def rope(x):
        swapped = jnp.where(first, pltpu.roll(x, LANES - D_ROPE // 2, 1), pltpu.roll(x, D_ROPE // 2, 1))
        return x * cs + swapped * sgn_sn

    kr = rope(t[:, 3 * LANES:4 * LANES])
    kr_hi = jnp.where(l128 >= D_ROPE, kr, 0.0)
    kr_lo = pltpu.roll(kr_hi, D_ROPE, 1)
    kr_tiles = (kr_lo.astype(BF16), kr_hi.astype(BF16))
    for pair in range(2):
        qr = rope(q[:, (4 + pair) * LANES:(5 + pair) * LANES]).astype(BF16)
        for half in range(2):
            h = 2 * pair + half
            q_ref[:, (2 * h) * LANES:(2 * h + 1) * LANES] = q[:, h * LANES:(h + 1) * LANES].astype(BF16)
            q_ref[:, (2 * h + 1) * LANES:(2 * h + 2) * LANES] = qr
            k_ref[:, (2 * h) * LANES:(2 * h + 1) * LANES] = kn[:, h * LANES:(h + 1) * LANES].astype(BF16)
            k_ref[:, (2 * h + 1) * LANES:(2 * h + 2) * LANES] = kr_tiles[half]


def _mla_pre(proj, pos3, qg, wq, kvg, wk, wvt, inv):
    bsz, s, _ = proj.shape
    tm = MLA_TK
    const = lambda shape: pl.BlockSpec(shape, lambda b, i: (0,) * len(shape))
    return pl.pallas_call(
        _mla_pre_kernel,
        grid=(bsz, s // tm),
        in_specs=[
            pl.BlockSpec((None, tm, 512), lambda b, i: (b, i, OFF_CQKR // 512)),
            pl.BlockSpec((None, tm, LANES), lambda b, i: (b, i, OFF_CKV // LANES)),
            pl.BlockSpec((None, tm, 1), lambda b, i: (b, i, 0)),
            const((1, 512)), const((512, 768)), const((1, LANES)), const((LANES, W_B)), const((W_B, LANES)),
            const((1, LANES)),
        ],
        out_specs=[
            pl.BlockSpec((None, tm, 1024), lambda b, i: (b, i, 0)),
            pl.BlockSpec((None, tm, 1024), lambda b, i: (b, i, 0)),
            pl.BlockSpec((None, None, W_B, tm), lambda b, i: (b, i, 0, 0)),
        ],
        out_shape=[
            jax.ShapeDtypeStruct((bsz, s, 1024), BF16),
            jax.ShapeDtypeStruct((bsz, s, 1024), BF16),
            jax.ShapeDtypeStruct((bsz, s // tm, W_B, tm), BF16),
        ],
        compiler_params=_cparams(("parallel", "parallel")),
        name="mla_pre",
    )(proj, proj, pos3, qg, wq, kvg, wk, wvt, inv)


def _mla_attn_kernel(q_ref, k_ref, vt_ref, o_ref, st_a, st_b, *, tq):
    n_chunks, _, tk = vt_ref.shape
    nt = q_ref.shape[0] // tq

    def scores(t, st_scr):
        q = q_ref[pl.ds(pl.multiple_of(t * tq, tq), tq), :]
        m8 = jnp.full((8, tq), -jnp.inf, F32)
        for c in range(n_chunks):
            st = _dot_nt(k_ref[c * tk:(c + 1) * tk, :], q)
            st_scr[c * tk:(c + 1) * tk, :] = st
            m8 = jnp.maximum(m8, jnp.max(st.reshape(tk // 8, 8, tq), axis=0))
        return jnp.max(m8, axis=0, keepdims=True)

    def output(t, st_scr, m):
        l8 = jnp.zeros((8, tq), F32)
        acc = jnp.zeros((D_V_B, tq), F32)
        for c in range(n_chunks):
            p = jnp.exp2(st_scr[c * tk:(c + 1) * tk, :] - m)
            l8 = l8 + jnp.sum(p.reshape(tk // 8, 8, tq), axis=0)
            acc = acc + _dot(vt_ref[c], p.astype(BF16))
        l = jnp.sum(l8, axis=0, keepdims=True)
        o_ref[pl.ds(pl.multiple_of(t * tq, tq), tq), :] = (acc / l).T.astype(BF16)

    m_first = scores(0, st_a)
    if nt == 1:
        output(0, st_a, m_first)
        return

    def pair(u, m_a):
        t = 2 * u
        m_b = scores(t + 1, st_b)
        output(t, st_a, m_a)
        m_a = scores(t + 2, st_a)
        output(t + 1, st_b, m_b)
        return m_a

    m_a = lax.fori_loop(0, nt // 2 - 1, pair, m_first)
    m_b = scores(nt - 1, st_b)
    output(nt - 2, st_a, m_a)
    output(nt - 1, st_b, m_b)


def _mla_attention(q, k, vt):
    bsz, s, _ = q.shape
    n_chunks, tk = vt.shape[1], vt.shape[3]
    tq = min(512, s)
    assert s // tq == 1 or (s // tq) % 2 == 0
    return pl.pallas_call(
        functools.partial(_mla_attn_kernel, tq=tq),
        grid=(bsz, N_H_B),
        in_specs=[
            pl.BlockSpec((None, s, 256), lambda b, h: (b, 0, h)),
            pl.BlockSpec((None, s, 256), lambda b, h: (b, 0, h)),
            pl.BlockSpec((None, n_chunks, D_V_B, tk), lambda b, h: (b, 0, h, 0)),
        ],
        out_specs=pl.BlockSpec((None, s, D_V_B), lambda b, h: (b, 0, h)),
        out_shape=jax.ShapeDtypeStruct((bsz, s, W_B), BF16),
        scratch_shapes=[pltpu.VMEM((s, tq), F32), pltpu.VMEM((s, tq), F32)],
        compiler_params=_cparams(("parallel", "parallel")),
        name="mla_attn",
    )(q, k, vt)


def _log_sigmoid(x):
    return jnp.minimum(x, 0.0) - jnp.log1p(jnp.exp(-jnp.abs(x)))


def _split3(x):
    hi = x.astype(BF16).astype(F32)
    r1 = x - hi
    mid = r1.astype(BF16).astype(F32)
    return jnp.concatenate([hi, mid, r1 - mid], axis=-1)


def _mlstm_kernel(q_ref, k_ref, v_ref, o_ref, gate_ref, gb_ref, hg_ref, y_ref,
                  hf_scr, hb_scr, tile_scr, kw_scr, rowq_scr, bp_scr, cf_scr, cb_scr, gt_scr):
    s_len = q_ref.shape[0]
    L = CHUNK
    nc = s_len // L
    head = pl.program_id(1)
    ii = lax.broadcasted_iota(jnp.int32, (L, L), 0)
    jj = lax.broadcasted_iota(jnp.int32, (L, L), 1)
    ones = jnp.ones((L, L), BF16)
    eye3 = jnp.concatenate([(ii == jj).astype(BF16)] * 3, axis=1)
    tri_ones = [jnp.concatenate([jnp.concatenate([m.astype(BF16), ones], axis=1)] * 3, axis=0)
                for m in (ii <= jj, ii >= jj)]
    masks = (jj <= ii, jj >= ii)
    h_scrs = (hf_scr, hb_scr)
    c_scrs = (cf_scr, cb_scr)

    def load_v_ext(r0):
        return jnp.concatenate([v_ref[pl.ds(r0, L), :], ones], axis=1)

    def transpose_gates(c, _):
        g = gate_ref[pl.ds(pl.multiple_of(c * L, L), L), :] + gb_ref[...]
        gt_scr[pl.ds(pl.multiple_of(c * 16, 16), 16), :] = g.T[0:16, :]
        return 0

    lax.fori_loop(0, nc, transpose_gates, 0, unroll=4)

    for d in range(2):
        li = gt_scr[pl.ds(d * N_H_C + head, nc, stride=16), :]
        lf = _log_sigmoid(gt_scr[pl.ds((2 + d) * N_H_C + head, nc, stride=16), :])
        bt = _dot(_split3(lf).astype(BF16), tri_ones[d])
        b, tot = bt[:, :L], bt[:, L:]
        r = li - b
        m_loc = tot + jnp.broadcast_to(jnp.max(r, axis=-1, keepdims=True), r.shape)
        rowq_scr[d, 0] = r
        rowq_scr[d, 1] = jnp.exp(tot + r - m_loc)
        rowq_scr[d, 2] = tot
        rowq_scr[d, 3] = m_loc
        bp_scr[d] = _split3(b + math.log(DQK_C ** -0.5))

    def prep(c, _):
        r0 = pl.multiple_of(c * L, L)
        q, k = q_ref[pl.ds(r0, L), :], k_ref[pl.ds(r0, L), :]
        s = _dot_nt(q, k)
        kt = k.astype(F32).T
        p = []
        for d in range(2):
            u = jnp.where(masks[d], rowq_scr[d, 0, pl.ds(c, 1), :], -jnp.inf)
            ct = jnp.broadcast_to(jnp.max(u, axis=-1, keepdims=True), (L, L))
            p.append((s * jnp.exp(u - ct)).astype(BF16))
            rhs = jnp.broadcast_to(bp_scr[d, pl.ds(c, 1), :], (L, 3 * L)).astype(BF16)
            tile_scr[pl.ds(r0, L), (2 * d) * L:(2 * d + 1) * L] = _dot_nt(eye3, rhs)
            tile_scr[pl.ds(r0, L), (2 * d + 1) * L:(2 * d + 2) * L] = ct
            kw_scr[d, c] = (kt * rowq_scr[d, 1, pl.ds(c, 1), :]).astype(BF16)
        intra = _dot(jnp.concatenate(p, axis=0), load_v_ext(r0))
        hf_scr[pl.ds(r0, L), :] = intra[:L]
        hb_scr[pl.ds(r0, L), :] = intra[L:]
        return 0

    lax.fori_loop(0, nc, prep, 0, unroll=2)

    cf_scr[...] = jnp.zeros_like(cf_scr)
    cb_scr[...] = jnp.zeros_like(cb_scr)

    def one_dir(c, m_state, d):
        h_scr, c_scr = h_scrs[d], c_scrs[d]
        r0 = pl.multiple_of(c * L, L)
        tot, m_loc = rowq_scr[d, 2, pl.ds(c, 1), :], rowq_scr[d, 3, pl.ds(c, 1), :]
        bt = tile_scr[pl.ds(r0, L), (2 * d) * L:(2 * d + 1) * L]
        ct = tile_scr[pl.ds(r0, L), (2 * d + 1) * L:(2 * d + 2) * L]
        v_ext = load_v_ext(r0)
        c_state = c_scr[...]
        qc = _dot(q_ref[pl.ds(r0, L), :], c_state.astype(BF16))
        mx = jnp.maximum(m_state, ct)
        iw = jnp.exp(m_state - mx)
        xw = jnp.exp(ct - mx)
        num = [iw * qc[:, j * L:(j + 1) * L] + xw * h_scr[pl.ds(r0, L), j * L:(j + 1) * L] for j in range(3)]
        inv = 1.0 / jnp.maximum(jnp.abs(num[2]), jnp.exp(-(bt + mx)))
        h_scr[pl.ds(r0, L), 0:L] = num[0] * inv
        h_scr[pl.ds(r0, L), L:2 * L] = num[1] * inv
        c_loc = _dot(kw_scr[d, c], v_ext)
        m_new = jnp.maximum(tot + m_state, m_loc)
        sp, sl = jnp.exp(tot + m_state - m_new), jnp.exp(m_loc - m_new)
        c_scr[...] = jnp.concatenate(
            [sp * c_state[:, j * L:(j + 1) * L] + sl * c_loc[:, j * L:(j + 1) * L] for j in range(3)], axis=1)
        return m_new

    def scan(c, carry):
        return one_dir(c, carry[0], 0), one_dir(nc - 1 - c, carry[1], 1)

    neg = jnp.full((1, LANES), -jnp.inf, F32)
    lax.fori_loop(0, nc, scan, (neg, neg), unroll=2)

    real = lax.broadcasted_iota(jnp.int32, (L, DV_PAD), 1) < DV_C

    def epilogue(c, _):
        r0 = pl.multiple_of(c * L, L)
        h = jnp.where(real, hf_scr[pl.ds(r0, L), 0:DV_PAD] + hb_scr[pl.ds(r0, L), 0:DV_PAD], 0.0)
        ms = jnp.sum(h * h, axis=-1, keepdims=True) * (1.0 / DV_C)
        hn = h * lax.rsqrt(ms + EPS) * hg_ref[...]
        y = jax.nn.sigmoid(o_ref[pl.ds(r0, L), :].astype(F32)) * hn
        y_ref[pl.ds(r0, L), :] = y.astype(BF16)
        return 0

    lax.fori_loop(0, nc, epilogue, 0, unroll=2)


def _mlstm(proj, gates, gate_b, head_g):
    bsz, s, _ = proj.shape
    nc = s // CHUNK
    return pl.pallas_call(
        _mlstm_kernel,
        grid=(bsz, N_H_C),
        in_specs=[
            pl.BlockSpec((None, s, DQK_PAD), lambda b, h: (b, 0, OFF_MQ // DQK_PAD + h)),
            pl.BlockSpec((None, s, DQK_PAD), lambda b, h: (b, 0, OFF_MK // DQK_PAD + h)),
            pl.BlockSpec((None, s, DV_PAD), lambda b, h: (b, 0, OFF_MV // DV_PAD + h)),
            pl.BlockSpec((None, s, DV_PAD), lambda b, h: (b, 0, OFF_MO // DV_PAD + h)),
            pl.BlockSpec((None, s, LANES), lambda b, h: (b, 0, 0)),
            pl.BlockSpec((1, LANES), lambda b, h: (0, 0)),
            pl.BlockSpec((None, 1, DV_PAD), lambda b, h: (h, 0, 0)),
        ],
        out_specs=pl.BlockSpec((None, s, DV_PAD), lambda b, h: (b, 0, h)),
        out_shape=jax.ShapeDtypeStruct((bsz, s, W_C_PAD), BF16),
        scratch_shapes=[
            pltpu.VMEM((s, DV_PAD + LANES), F32), pltpu.VMEM((s, DV_PAD + LANES), F32),
            pltpu.VMEM((s, 4 * CHUNK), F32),
            pltpu.VMEM((2, nc, DQK_PAD, CHUNK), BF16),
            pltpu.VMEM((2, 4, nc, CHUNK), F32),
            pltpu.VMEM((2, nc, 3 * CHUNK), F32),
            pltpu.VMEM((DQK_PAD, DV_PAD + LANES), F32), pltpu.VMEM((DQK_PAD, DV_PAD + LANES), F32),
            pltpu.VMEM((nc * 16, CHUNK), F32),
        ],
        compiler_params=_cparams(("parallel", "arbitrary")),
        name="mlstm",
    )(proj, proj, proj, proj, gates, gate_b, head_g)


def _out_kernel(ya_ref, yb_ref, yc_ref, wa_ref, wb_ref, wc_ref, pg_ref, gate_ref, x_ref, o_ref, y_scr):
    for r in range(0, o_ref.shape[0], SUB_ROWS):
        rows = slice(r, r + SUB_ROWS)
        y_scr[rows, :] = (_dot(ya_ref[rows, :], wa_ref[...]) + _dot(yb_ref[rows, :], wb_ref[...])
                          + _dot(yc_ref[rows, :], wc_ref[...]))
        _norm_residual_rows(y_scr, x_ref, pg_ref, gate_ref, o_ref, r, SUB_ROWS)


def _out_proj(ya, yb, yc, wa, wb, wc, pg, modl, x):
    bsz, s, d = x.shape
    tm = min(2 * SUB_ROWS, s)
    row = lambda w: pl.BlockSpec((None, tm, w), lambda b, i: (b, i, 0))
    const = lambda shape: pl.BlockSpec(shape, lambda b, i: (0,) * len(shape), pipeline_mode=pl.Buffered(1))
    return pl.pallas_call(
        _out_kernel,
        grid=(bsz, s // tm),
        in_specs=[
            row(W_A), row(W_B), row(W_C_PAD),
            const((W_A, d)), const((W_B, d)), const((W_C_PAD, d)), const((1, d)),
            pl.BlockSpec((None, 1, d), lambda b, i: (b, 0, 2)),
            row(d),
        ],
        out_specs=row(d),
        out_shape=jax.ShapeDtypeStruct((bsz, s, d), F32),
        scratch_shapes=[pltpu.VMEM((tm, d), F32)],
        compiler_params=_cparams(("parallel", "parallel")),
        name="out_proj",
    )(ya, yb, yc, wa, wb, wc, pg, modl, x)


FFN_TF = 512
FFN_VMEM_LIMIT = 62 * 1024 * 1024


def _ffn_kernel(x_ref, g_ref, sc_ref, sh_ref, wg_ref, wu_ref, wd_ref, pg_ref, gate_ref, o_ref, h_scr):
    j = pl.program_id(2)
    last = pl.num_programs(2) - 1
    tm = x_ref.shape[0]
    subs = [(r, slice(r, r + SUB_ROWS)) for r in range(0, tm, SUB_ROWS)]

    def partial_down(rows):
        h = h_scr[rows, :]
        a = _dot(h, wg_ref[...])
        u = _dot(h, wu_ref[...])
        hid = (a * jax.nn.sigmoid(a)) * u
        return _dot(hid.astype(BF16), wd_ref[...])

    @pl.when(j == 0)
    def _():
        for r, rows in subs:
            _mod_norm_rows(x_ref, g_ref, sc_ref, sh_ref, h_scr, r, SUB_ROWS)
            o_ref[rows, :] = partial_down(rows)

    @pl.when((j > 0) & (j < last))
    def _():
        o_ref[...] += partial_down(slice(0, tm))

    @pl.when(j == last)
    def _():
        for r, rows in subs:
            o_ref[rows, :] += partial_down(rows)
            _norm_residual_rows(o_ref, x_ref, pg_ref, gate_ref, o_ref, r, SUB_ROWS)


def _ffn(x, g, modl, wg, wu, wd, pg):
    bsz, s, d = x.shape
    f = wg.shape[1]
    tm = min(1024, s)
    row = pl.BlockSpec((None, tm, d), lambda b, i, j: (b, i, 0))
    vec = pl.BlockSpec((1, d), lambda b, i, j: (0, 0))
    modv = lambda k: pl.BlockSpec((None, 1, d), lambda b, i, j: (b, 0, k))
    return pl.pallas_call(
        _ffn_kernel,
        grid=(bsz, s // tm, f // FFN_TF),
        in_specs=[
            row, vec, modv(4), modv(3),
            pl.BlockSpec((d, FFN_TF), lambda b, i, j: (0, j)),
            pl.BlockSpec((d, FFN_TF), lambda b, i, j: (0, j)),
            pl.BlockSpec((FFN_TF, d), lambda b, i, j: (j, 0)),
            vec, modv(5),
        ],
        out_specs=row,
        out_shape=jax.ShapeDtypeStruct((bsz, s, d), F32),
        scratch_shapes=[pltpu.VMEM((tm, d), BF16)],
        compiler_params=_cparams(("parallel", "parallel", "arbitrary"), FFN_VMEM_LIMIT),
        name="ffn",
    )(x, g, modl, modl, wg, wu, wd, pg, modl)


def _pair_heads_a(w, axis):
    shape = w.shape
    w = jnp.moveaxis(w, axis, -1)
    lead = w.shape[:-1]
    w = w.reshape(lead + (2, 2, GROUP_A, DH_A))
    w = jnp.swapaxes(w, -3, -2)
    w = w.reshape(lead + (W_A,))
    return jnp.moveaxis(w, -1, axis).reshape(shape)


def _pad_heads(w, n_heads, width, padded, axis):
    w = jnp.moveaxis(w, axis, -1)
    lead = w.shape[:-1]
    w = w.reshape(lead + (n_heads, width))
    w = jnp.pad(w, [(0, 0)] * len(lead) + [(0, 0), (0, padded - width)])
    w = w.reshape(lead + (n_heads * padded,))
    return jnp.moveaxis(w, -1, axis)


def _layout_w_in(w):
    d = w.shape[0]
    sizes = (W_A, N_KV_A * DH_A, N_KV_A * DH_A, Q_LORA, KV_LORA, D_ROPE,
             N_H_C * DQK_C, N_H_C * DQK_C, W_C, 4 * N_H_C, W_C)
    pts = np.cumsum(sizes)[:-1].tolist()
    aq, ak, av, bcq, bckv, bkr, cq, ck, cv, cg, co = jnp.split(w, pts, axis=1)
    parts = [
        _pair_heads_a(aq, 1), ak, av, bckv,
        jnp.pad(cg, ((0, 0), (0, LANES - 4 * N_H_C))),
        bcq, bkr,
        _pad_heads(cq, N_H_C, DQK_C, DQK_PAD, 1), _pad_heads(ck, N_H_C, DQK_C, DQK_PAD, 1),
        _pad_heads(cv, N_H_C, DV_C, DV_PAD, 1), _pad_heads(co, N_H_C, DV_C, DV_PAD, 1),
    ]
    out = jnp.concatenate(parts, axis=1).astype(BF16)
    assert out.shape == (d, IN_WIDTH_PAD)
    return out


def _layout_w_uq(w):
    w = w.reshape(Q_LORA, N_H_B, D_NOPE + D_ROPE)
    nope = w[:, :, :D_NOPE].reshape(Q_LORA, N_H_B * D_NOPE)
    rope = w[:, :, D_NOPE:].reshape(Q_LORA, N_H_B * D_ROPE)
    w = jnp.concatenate([nope, rope], axis=1)
    return jnp.pad(w, ((0, 512 - Q_LORA), (0, 0))).astype(BF16)


def _layout_w_ukv(w):
    w = w.reshape(KV_LORA, N_H_B, D_NOPE + D_V_B)
    kn = w[:, :, :D_NOPE].reshape(KV_LORA, N_H_B * D_NOPE)
    v = w[:, :, D_NOPE:].reshape(KV_LORA, N_H_B * D_V_B)
    return kn.astype(BF16), v.T.astype(BF16)


def kernel(x, c, positions, mod_w, mod_b, pre_mix_g, post_mix_g, pre_ffn_g, post_ffn_g, w_in, attn_sink, mla_q_norm_g, mla_w_uq, mla_kv_norm_g, mla_w_ukv, mlstm_gate_b, mlstm_head_g, w_out, ffn_w_gate, ffn_w_up, ffn_w_down):
    depth = mod_w.shape[0]
    bsz, s, d = x.shape
    mod = _modulation(c, mod_w, mod_b)
    pos3 = positions.reshape(bsz, s, 1)
    inv = 1.0 / (ROPE_THETA ** (jnp.arange(0, D_ROPE, 2, dtype=F32) / D_ROPE))
    inv128 = jnp.tile(inv, LANES // (D_ROPE // 2)).reshape(1, LANES)
    for l in range(depth):
        modl = mod[l].reshape(bsz, 1, 6 * d)
        proj, gates = _in_proj(x, pre_mix_g[l].reshape(1, d), modl, _layout_w_in(w_in[l]))
        ya = _window_attention(proj, attn_sink[l])
        qg = jnp.pad(mla_q_norm_g[l], (0, 512 - Q_LORA)).reshape(1, 512)
        wk, wvt = _layout_w_ukv(mla_w_ukv[l])
        qb, kb, vtb = _mla_pre(proj, pos3, qg, _layout_w_uq(mla_w_uq[l]),
                               mla_kv_norm_g[l].reshape(1, KV_LORA), wk, wvt, inv128)
        yb = _mla_attention(qb, kb, vtb)
        gate_b = jnp.pad(mlstm_gate_b[l], (0, LANES - 4 * N_H_C)).reshape(1, LANES)
        head_g = jnp.pad(mlstm_head_g[l].reshape(N_H_C, 1, DV_C), ((0, 0), (0, 0), (0, DV_PAD - DV_C)))
        yc = _mlstm(proj, gates, gate_b, head_g)
        wo = w_out[l]
        wa = _pair_heads_a(wo[:W_A], 0).astype(BF16)
        wb = wo[W_A:W_A + W_B].astype(BF16)
        wc = _pad_heads(wo[W_A + W_B:], N_H_C, DV_C, DV_PAD, 0).astype(BF16)
        x = _out_proj(ya, yb, yc, wa, wb, wc, post_mix_g[l].reshape(1, d), modl, x)
        x = _ffn(x, pre_ffn_g[l].reshape(1, d), modl, ffn_w_gate[l].astype(BF16), ffn_w_up[l].astype(BF16),
                 ffn_w_down[l].astype(BF16), post_ffn_g[l].reshape(1, d))
    return x
```

```python
import functools
import math

import numpy as np
import jax
import jax.numpy as jnp
from jax import lax
from jax.experimental import pallas as pl
from jax.experimental.pallas import tpu as pltpu

F32 = jnp.float32
BF16 = jnp.bfloat16

D_MODEL = 2048
EPS = 1e-6
N_Q_A, N_KV_A, DH_A, GROUP_A = 12, 4, 64, 3
WINDOW = 128
BLOCK_A = 128
N_H_B, Q_LORA, KV_LORA, D_NOPE, D_ROPE, D_V_B = 4, 448, 128, 128, 64, 128
ROPE_THETA = 10000.0
N_H_C, DQK_C, DV_C = 4, 96, 192
W_A, W_B, W_C = N_Q_A * DH_A, N_H_B * D_V_B, N_H_C * DV_C
D_FF = 5632

LANES = 128
DQK_PAD = 128
DV_PAD = 256
ONES_COL = DV_C
VMEM_LIMIT = 56 * 1024 * 1024

OFF_AQ, OFF_AK, OFF_AV = 0, 768, 1024
OFF_CKV, OFF_CG, OFF_CQKR = 1280, 1408, 1536
OFF_MQ, OFF_MK, OFF_MV, OFF_MO = 2048, 2560, 3072, 4096
IN_WIDTH_PAD = 5120
W_C_PAD = N_H_C * DV_PAD
MIX_PAD = W_A + W_B + W_C_PAD

CHUNK = 128


def _alibi_slopes(n):
    def pow2(m):
        start = 2.0 ** (-8.0 / m)
        return [start ** (i + 1) for i in range(m)]
    if math.log2(n).is_integer():
        s = pow2(n)
    else:
        p = 2 ** math.floor(math.log2(n))
        s = pow2(p) + pow2(2 * p)[0::2][: n - p]
    return [float(np.float32(v)) for v in s]


SLOPES_A = _alibi_slopes(N_Q_A)


def _cparams(sem, vmem_limit=VMEM_LIMIT):
    return pltpu.CompilerParams(dimension_semantics=sem, vmem_limit_bytes=vmem_limit)


def _dot(a, b):
    return jnp.dot(a, b, preferred_element_type=F32)


def _dot_nt(a, b):
    return lax.dot_general(a, b, (((1,), (1,)), ((), ())), preferred_element_type=F32)


def _dot_tn(a, b):
    return lax.dot_general(a, b, (((0,), (0,)), ((), ())), preferred_element_type=F32)


def _mod_kernel(c_ref, w_ref, b_ref, o_ref):
    c = c_ref[...]
    s = c * jax.nn.sigmoid(c)
    o_ref[...] = _dot(s.astype(BF16), w_ref[...].astype(BF16)) + b_ref[...]


def _modulation(c, mod_w, mod_b):
    depth, d, n = mod_w.shape
    bsz = c.shape[0]
    tn = 1024
    return pl.pallas_call(
        _mod_kernel,
        grid=(depth, n // tn),
        in_specs=[
            pl.BlockSpec((bsz, d), lambda l, j: (0, 0)),
            pl.BlockSpec((None, d, tn), lambda l, j: (l, 0, j)),
            pl.BlockSpec((None, 1, tn), lambda l, j: (l, 0, j)),
        ],
        out_specs=pl.BlockSpec((None, bsz, tn), lambda l, j: (l, 0, j)),
        out_shape=jax.ShapeDtypeStruct((depth, bsz, n), F32),
        compiler_params=_cparams(("parallel", "parallel")),
        name="adaln_mod",
    )(c, mod_w, mod_b.reshape(depth, 1, n))


ROW_SLAB = 16
SUB_ROWS = 256


def _row_slabs(first_row, n_rows):
    return [slice(r, r + ROW_SLAB) for r in range(first_row, first_row + n_rows, ROW_SLAB)]


def _mod_norm_rows(x_ref, g_ref, sc_ref, sh_ref, h_ref, first_row, n_rows):
    gain = g_ref[...] * (1.0 + sc_ref[...])
    shift = sh_ref[...]
    for rows in _row_slabs(first_row, n_rows):
        x = x_ref[rows, :]
        ms = jnp.mean(x * x, axis=-1, keepdims=True)
        h_ref[rows, :] = (x * lax.rsqrt(ms + EPS) * gain + shift).astype(BF16)


def _norm_residual_rows(y_ref, x_ref, pg_ref, gate_ref, o_ref, first_row, n_rows):
    gain = gate_ref[...] * pg_ref[...]
    for rows in _row_slabs(first_row, n_rows):
        y = y_ref[rows, :]
        ms = jnp.mean(y * y, axis=-1, keepdims=True)
        o_ref[rows, :] = x_ref[rows, :] + y * lax.rsqrt(ms + EPS) * gain


IN_TN = 1024
GATE_TILE = OFF_CG // IN_TN
GATE_OFF = OFF_CG % IN_TN


def _in_kernel(x_ref, g_ref, sc_ref, sh_ref, w_ref, cs_ref, o_ref, gate_ref, h_scr):
    j = pl.program_id(2)
    tm = x_ref.shape[0]

    def project(rows):
        acc = _dot(h_scr[rows, :], w_ref[...])
        o_ref[rows, :] = (acc * cs_ref[...]).astype(BF16)
        return acc

    @pl.when(j == 0)
    def _():
        for r in range(0, tm, SUB_ROWS):
            _mod_norm_rows(x_ref, g_ref, sc_ref, sh_ref, h_scr, r, SUB_ROWS)
            project(slice(r, r + SUB_ROWS))

    @pl.when(j != 0)
    def _():
        acc = project(slice(0, tm))

        @pl.when(j == GATE_TILE)
        def _():
            gate_ref[...] = acc[:, GATE_OFF:GATE_OFF + LANES]


def _in_proj(x, g, modl, w):
    bsz, s, d = x.shape
    n = w.shape[1]
    tm = min(1024, s)
    col_scale = jnp.where(jnp.arange(n) < W_A, WIN_QSCALE, 1.0).astype(F32).reshape(1, n)
    return pl.pallas_call(
        _in_kernel,
        grid=(bsz, s // tm, n // IN_TN),
        in_specs=[
            pl.BlockSpec((None, tm, d), lambda b, i, j: (b, i, 0)),
            pl.BlockSpec((1, d), lambda b, i, j: (0, 0)),
            pl.BlockSpec((None, 1, d), lambda b, i, j: (b, 0, 1)),
            pl.BlockSpec((None, 1, d), lambda b, i, j: (b, 0, 0)),
            pl.BlockSpec((d, IN_TN), lambda b, i, j: (0, j)),
            pl.BlockSpec((1, IN_TN), lambda b, i, j: (0, j)),
        ],
        out_specs=[
            pl.BlockSpec((None, tm, IN_TN), lambda b, i, j: (b, i, j)),
            pl.BlockSpec((None, tm, LANES), lambda b, i, j: (b, i, 0)),
        ],
        out_shape=[
            jax.ShapeDtypeStruct((bsz, s, n), BF16),
            jax.ShapeDtypeStruct((bsz, s, LANES), F32),
        ],
        scratch_shapes=[pltpu.VMEM((tm, d), BF16)],
        compiler_params=_cparams(("parallel", "parallel", "arbitrary")),
        name="in_proj",
    )(x, g, modl, modl, w, col_scale)


LOG2E = math.log2(math.e)
WIN_QSCALE = DH_A ** -0.5 * LOG2E


def _win_kernel(sink_ref, q_ref, kp_ref, kc_ref, kn_ref, vp_ref, vc_ref, vn_ref, o_ref, *, nb):
    n = pl.program_id(1)
    blk = BLOCK_A
    nk = 3 * blk
    qi = lax.broadcasted_iota(jnp.int32, (blk, nk), 0)
    kj = lax.broadcasted_iota(jnp.int32, (blk, nk), 1)
    dist = jnp.abs(qi - kj + blk)
    in_seq = ((kj >= blk) | (n > 0)) & ((kj < 2 * blk) | (n < nb - 1))
    dist_masked = jnp.where((dist <= WINDOW) & in_seq, dist.astype(F32), jnp.inf)
    k_all = jnp.concatenate([kp_ref[...], kc_ref[...], kn_ref[...]], axis=0)
    v_all = jnp.concatenate([vp_ref[...], vc_ref[...], vn_ref[...]], axis=0)
    low_k = lax.broadcasted_iota(jnp.int32, (nk, LANES), 1) < DH_A
    low_q = lax.broadcasted_iota(jnp.int32, (blk, LANES), 1) < DH_A
    zero = jnp.zeros((nk, LANES), BF16)
    for p in range(2):
        kt = k_all[:, p * LANES:(p + 1) * LANES]
        vt = v_all[:, p * LANES:(p + 1) * LANES]
        k_big = jnp.concatenate([jnp.where(low_k, kt, zero), jnp.where(low_k, zero, kt)], axis=0)
        v_big = jnp.concatenate([jnp.where(low_k, vt, zero), jnp.where(low_k, zero, vt)], axis=0)
        q3 = q_ref[:, p * GROUP_A * LANES:(p + 1) * GROUP_A * LANES]
        q3 = jnp.concatenate([q3[:, a * LANES:(a + 1) * LANES] for a in range(GROUP_A)], axis=0)
        s3 = _dot_nt(q3, k_big)
        p_rows, inv_rows = [], []
        for a in range(GROUP_A):
            p_halves, inv_halves = [], []
            for half in range(2):
                head = (2 * p + half) * GROUP_A + a
                s = s3[a * blk:(a + 1) * blk, half * nk:(half + 1) * nk] - (SLOPES_A[head] * LOG2E) * dist_masked
                sink = sink_ref[head] * LOG2E
                m = jnp.maximum(jnp.max(s, axis=-1, keepdims=True), sink)
                e = jnp.exp2(s - m)
                l = jnp.sum(e, axis=-1, keepdims=True) + jnp.exp2(sink - m)
                p_halves.append(e.astype(BF16))
                inv_halves.append(1.0 / l)
            p_rows.append(jnp.concatenate(p_halves, axis=1))
            inv_rows.append(jnp.where(low_q, inv_halves[0], inv_halves[1]))
        o3 = _dot(jnp.concatenate(p_rows, axis=0), v_big)
        for a in range(GROUP_A):
            t = p * GROUP_A + a
            o_ref[:, t * LANES:(t + 1) * LANES] = (o3[a * blk:(a + 1) * blk] * inv_rows[a]).astype(BF16)


def _window_attention(proj, sinks):
    bsz, s, _ = proj.shape
    blk = BLOCK_A
    nb = s // blk
    kb, vb = OFF_AK // 256, OFF_AV // 256
    prev = lambda n: jnp.maximum(n - 1, 0)
    nxt = lambda n: jnp.minimum(n + 1, nb - 1)
    return pl.pallas_call(
        functools.partial(_win_kernel, nb=nb),
        grid=(bsz, nb),
        in_specs=[
            pl.BlockSpec(memory_space=pltpu.SMEM),
            pl.BlockSpec((None, blk, W_A), lambda b, n: (b, n, 0)),
            pl.BlockSpec((None, blk, 256), lambda b, n: (b, prev(n), kb)),
            pl.BlockSpec((None, blk, 256), lambda b, n: (b, n, kb)),
            pl.BlockSpec((None, blk, 256), lambda b, n: (b, nxt(n), kb)),
            pl.BlockSpec((None, blk, 256), lambda b, n: (b, prev(n), vb)),
            pl.BlockSpec((None, blk, 256), lambda b, n: (b, n, vb)),
            pl.BlockSpec((None, blk, 256), lambda b, n: (b, nxt(n), vb)),
        ],
        out_specs=pl.BlockSpec((None, blk, W_A), lambda b, n: (b, n, 0)),
        out_shape=jax.ShapeDtypeStruct((bsz, s, W_A), BF16),
        compiler_params=_cparams(("parallel", "parallel")),
        name="win_attn",
    )(sinks, proj, proj, proj, proj, proj, proj, proj)


MLA_TK = 512
MLA_QSCALE = (D_NOPE + D_ROPE) ** -0.5 * math.log2(math.e)


def _mla_pre_kernel(cqkr_ref, ckv_ref, pos_ref, qg_ref, wq_ref, kvg_ref, wk_ref, wvt_ref, inv_ref,
                    q_ref, k_ref, vt_ref):
    tm = cqkr_ref.shape[0]
    t = cqkr_ref[...].astype(F32)
    lane = lax.broadcasted_iota(jnp.int32, (tm, 4 * LANES), 1)
    cq = jnp.where(lane < Q_LORA, t, 0.0)
    ms = jnp.sum(cq * cq, axis=-1, keepdims=True) * (1.0 / Q_LORA)
    qn = cq * lax.rsqrt(ms + EPS) * qg_ref[...]
    q = _dot(qn.astype(BF16), wq_ref[...]) * MLA_QSCALE
    c = ckv_ref[...].astype(F32)
    ms = jnp.mean(c * c, axis=-1, keepdims=True)
    kvn = (c * lax.rsqrt(ms + EPS) * kvg_ref[...]).astype(BF16)
    kn = _dot(kvn, wk_ref[...])
    vt_ref[...] = _dot_nt(wvt_ref[...], kvn).astype(BF16)

    ang_t = inv_ref[...] * pos_ref[...].astype(F32)
    cs_t, sn_t = jnp.cos(ang_t), jnp.sin(ang_t)
    cs = jnp.concatenate([cs_t] * 4, axis=0).T
    sgn_sn = jnp.concatenate([-sn_t, sn_t] * 2, axis=0).T
    l128 = lax.broadcasted_iota(jnp.int32, (tm, LANES), 1)
    first = (l128 % D_ROPE) < (D_ROPE // 2)

    def rope(x):
        swapped = jnp.where(first, pltpu.roll(x, LANES - D_ROPE // 2, 1), pltpu.roll(x, D_ROPE // 2, 1))
        return x * cs + swapped * sgn_sn

    kr = rope(t[:, 3 * LANES:4 * LANES])
    kr_hi = jnp.where(l128 >= D_ROPE, kr, 0.0)
    kr_lo = pltpu.roll(kr_hi, D_ROPE, 1)
    kr_tiles = (kr_lo.astype(BF16), kr_hi.astype(BF16))
    for pair in range(2):
        qr = rope(q[:, (4 + pair) * LANES:(5 + pair) * LANES]).astype(BF16)
        for half in range(2):
            h = 2 * pair + half
            q_ref[:, (2 * h) * LANES:(2 * h + 1) * LANES] = q[:, h * LANES:(h + 1) * LANES].astype(BF16)
            q_ref[:, (2 * h + 1) * LANES:(2 * h + 2) * LANES] = qr
            k_ref[:, (2 * h) * LANES:(2 * h + 1) * LANES] = kn[:, h * LANES:(h + 1) * LANES].astype(BF16)
            k_ref[:, (2 * h + 1) * LANES:(2 * h + 2) * LANES] = kr_tiles[half]


def _mla_pre(proj, pos3, qg, wq, kvg, wk, wvt, inv):
    bsz, s, _ = proj.shape
    tm = MLA_TK
    const = lambda shape: pl.BlockSpec(shape, lambda b, i: (0,) * len(shape))
    return pl.pallas_call(
        _mla_pre_kernel,
        grid=(bsz, s // tm),
        in_specs=[
            pl.BlockSpec((None, tm, 512), lambda b, i: (b, i, OFF_CQKR // 512)),
            pl.BlockSpec((None, tm, LANES), lambda b, i: (b, i, OFF_CKV // LANES)),
            pl.BlockSpec((None, 1, tm), lambda b, i: (b, 0, i)),
            const((1, 512)), const((512, 768)), const((1, LANES)), const((LANES, W_B)), const((W_B, LANES)),
            const((D_ROPE // 2, 1)),
        ],
        out_specs=[
            pl.BlockSpec((None, tm, 1024), lambda b, i: (b, i, 0)),
            pl.BlockSpec((None, tm, 1024), lambda b, i: (b, i, 0)),
            pl.BlockSpec((None, None, W_B, tm), lambda b, i: (b, i, 0, 0)),
        ],
        out_shape=[
            jax.ShapeDtypeStruct((bsz, s, 1024), BF16),
            jax.ShapeDtypeStruct((bsz, s, 1024), BF16),
            jax.ShapeDtypeStruct((bsz, s // tm, W_B, tm), BF16),
        ],
        compiler_params=_cparams(("parallel", "parallel")),
        name="mla_pre",
    )(proj, proj, pos3, qg, wq, kvg, wk, wvt, inv)


def _mla_attn_kernel(q_ref, k_ref, vt_ref, o_ref, st_a, st_b, *, tq):
    n_chunks, _, tk = vt_ref.shape
    nt = q_ref.shape[0] // tq

    def scores(t, st_scr):
        q = q_ref[pl.ds(pl.multiple_of(t * tq, tq), tq), :]
        m8 = jnp.full((8, tq), -jnp.inf, F32)
        for c in range(n_chunks):
            st = _dot_nt(k_ref[c * tk:(c + 1) * tk, :], q)
            st_scr[c * tk:(c + 1) * tk, :] = st
            m8 = jnp.maximum(m8, jnp.max(st.reshape(tk // 8, 8, tq), axis=0))
        return jnp.max(m8, axis=0, keepdims=True)

    def output(t, st_scr, m):
        l8 = jnp.zeros((8, tq), F32)
        acc = jnp.zeros((D_V_B, tq), F32)
        for c in range(n_chunks):
            p = jnp.exp2(st_scr[c * tk:(c + 1) * tk, :] - m)
            l8 = l8 + jnp.sum(p.reshape(tk // 8, 8, tq), axis=0)
            acc = acc + _dot(vt_ref[c], p.astype(BF16))
        l = jnp.sum(l8, axis=0, keepdims=True)
        o_ref[pl.ds(pl.multiple_of(t * tq, tq), tq), :] = (acc / l).T.astype(BF16)

    m_first = scores(0, st_a)
    if nt == 1:
        output(0, st_a, m_first)
        return

    def pair(u, m_a):
        t = 2 * u
        m_b = scores(t + 1, st_b)
        output(t, st_a, m_a)
        m_a = scores(t + 2, st_a)
        output(t + 1, st_b, m_b)
        return m_a

    m_a = lax.fori_loop(0, nt // 2 - 1, pair, m_first)
    m_b = scores(nt - 1, st_b)
    output(nt - 2, st_a, m_a)
    output(nt - 1, st_b, m_b)


def _mla_attention(q, k, vt):
    bsz, s, _ = q.shape
    n_chunks, tk = vt.shape[1], vt.shape[3]
    tq = min(512, s)
    assert s // tq == 1 or (s // tq) % 2 == 0
    return pl.pallas_call(
        functools.partial(_mla_attn_kernel, tq=tq),
        grid=(bsz, N_H_B),
        in_specs=[
            pl.BlockSpec((None, s, 256), lambda b, h: (b, 0, h)),
            pl.BlockSpec((None, s, 256), lambda b, h: (b, 0, h)),
            pl.BlockSpec((None, n_chunks, D_V_B, tk), lambda b, h: (b, 0, h, 0)),
        ],
        out_specs=pl.BlockSpec((None, s, D_V_B), lambda b, h: (b, 0, h)),
        out_shape=jax.ShapeDtypeStruct((bsz, s, W_B), BF16),
        scratch_shapes=[pltpu.VMEM((s, tq), F32), pltpu.VMEM((s, tq), F32)],
        compiler_params=_cparams(("parallel", "parallel")),
        name="mla_attn",
    )(q, k, vt)


def _log_sigmoid(x):
    return jnp.minimum(x, 0.0) - jnp.log1p(jnp.exp(-jnp.abs(x)))


def _split3(x):
    hi = x.astype(BF16).astype(F32)
    r1 = x - hi
    mid = r1.astype(BF16).astype(F32)
    return jnp.concatenate([hi, mid, r1 - mid], axis=-1)


def _mlstm_kernel(q_ref, k_ref, v_ref, o_ref, gate_ref, gb_ref, hg_ref, y_ref,
                  hf_scr, hb_scr, tile_scr, kw_scr, rowq_scr, bp_scr, cf_scr, cb_scr, gt_scr):
    s_len = q_ref.shape[0]
    L = CHUNK
    nc = s_len // L
    head = pl.program_id(1)
    ii = lax.broadcasted_iota(jnp.int32, (L, L), 0)
    jj = lax.broadcasted_iota(jnp.int32, (L, L), 1)
    ones = jnp.ones((L, L), BF16)
    ones_lane = lax.broadcasted_iota(jnp.int32, (L, DV_PAD), 1) == ONES_COL
    eye3 = jnp.concatenate([(ii == jj).astype(BF16)] * 3, axis=1)
    tri_ones = [jnp.concatenate([jnp.concatenate([m.astype(BF16), ones], axis=1)] * 3, axis=0)
                for m in (ii <= jj, ii >= jj)]
    masks = (jj <= ii, jj >= ii)
    h_scrs = (hf_scr, hb_scr)
    c_scrs = (cf_scr, cb_scr)

    def load_v_ext(r0):
        return jnp.where(ones_lane, jnp.ones((L, DV_PAD), BF16), v_ref[pl.ds(r0, L), :])

    def transpose_gates(c, _):
        g = gate_ref[pl.ds(pl.multiple_of(c * L, L), L), :] + gb_ref[...]
        gt_scr[pl.ds(pl.multiple_of(c * 16, 16), 16), :] = g.T[0:16, :]
        return 0

    lax.fori_loop(0, nc, transpose_gates, 0, unroll=4)

    for d in range(2):
        li = gt_scr[pl.ds(d * N_H_C + head, nc, stride=16), :]
        lf = _log_sigmoid(gt_scr[pl.ds((2 + d) * N_H_C + head, nc, stride=16), :])
        bt = _dot(_split3(lf).astype(BF16), tri_ones[d])
        b, tot = bt[:, :L], bt[:, L:]
        r = li - b
        m_loc = tot + jnp.broadcast_to(jnp.max(r, axis=-1, keepdims=True), r.shape)
        rowq_scr[d, 0] = r
        rowq_scr[d, 1] = jnp.exp(tot + r - m_loc)
        rowq_scr[d, 2] = tot
        rowq_scr[d, 3] = m_loc
        bp_scr[d] = _split3(b + math.log(DQK_C ** -0.5))

    def prep(c, _):
        r0 = pl.multiple_of(c * L, L)
        q, k = q_ref[pl.ds(r0, L), :], k_ref[pl.ds(r0, L), :]
        s = _dot_nt(q, k)
        kt = k.astype(F32).T
        p = []
        for d in range(2):
            u = jnp.where(masks[d], rowq_scr[d, 0, pl.ds(c, 1), :], -jnp.inf)
            ct = jnp.broadcast_to(jnp.max(u, axis=-1, keepdims=True), (L, L))
            p.append((s * jnp.exp(u - ct)).astype(BF16))
            tile_scr[pl.ds(r0, L), (2 + d) * L:(3 + d) * L] = ct
            kw_scr[d, c] = (kt * rowq_scr[d, 1, pl.ds(c, 1), :]).astype(BF16)
        rhs = jnp.concatenate([jnp.broadcast_to(bp_scr[d, pl.ds(c, 1), :], (L, 3 * L)) for d in range(2)], axis=0)
        tile_scr[pl.ds(r0, L), 0:2 * L] = _dot_nt(eye3, rhs.astype(BF16))
        intra = _dot(jnp.concatenate(p, axis=0), load_v_ext(r0))
        hf_scr[pl.ds(r0, L), :] = intra[:L]
        hb_scr[pl.ds(r0, L), :] = intra[L:]
        return 0

    lax.fori_loop(0, nc, prep, 0, unroll=4)

    cf_scr[...] = jnp.zeros_like(cf_scr)
    cb_scr[...] = jnp.zeros_like(cb_scr)

    def one_dir(c, m_state, d):
        h_scr, c_scr = h_scrs[d], c_scrs[d]
        r0 = pl.multiple_of(c * L, L)
        tot, m_loc = rowq_scr[d, 2, pl.ds(c, 1), :], rowq_scr[d, 3, pl.ds(c, 1), :]
        bt = tile_scr[pl.ds(r0, L), d * L:(d + 1) * L]
        ct = tile_scr[pl.ds(r0, L), (2 + d) * L:(3 + d) * L]
        v_ext = load_v_ext(r0)
        c_state = c_scr[...]
        qc = _dot(q_ref[pl.ds(r0, L), :], c_state.astype(BF16))
        mx = jnp.maximum(m_state, ct)
        iw = jnp.exp(m_state - mx)
        xw = jnp.exp(ct - mx)
        num = [iw * qc[:, j * L:(j + 1) * L] + xw * h_scr[pl.ds(r0, L), j * L:(j + 1) * L] for j in range(2)]
        den = jnp.broadcast_to(num[1][:, ONES_COL - L:ONES_COL - L + 1], (L, L))
        inv = 1.0 / jnp.maximum(jnp.abs(den), jnp.exp(-(bt + mx)))
        h_scr[pl.ds(r0, L), 0:L] = num[0] * inv
        h_scr[pl.ds(r0, L), L:2 * L] = num[1] * inv
        c_loc = _dot(kw_scr[d, c], v_ext)
        m_new = jnp.maximum(tot + m_state, m_loc)
        sp, sl = jnp.exp(tot + m_state - m_new), jnp.exp(m_loc - m_new)
        c_scr[...] = jnp.concatenate(
            [sp * c_state[:, j * L:(j + 1) * L] + sl * c_loc[:, j * L:(j + 1) * L] for j in range(2)], axis=1)
        return m_new

    def scan(c, carry):
        return one_dir(c, carry[0], 0), one_dir(nc - 1 - c, carry[1], 1)

    neg = jnp.full((1, LANES), -jnp.inf, F32)
    lax.fori_loop(0, nc, scan, (neg, neg), unroll=4)

    real = lax.broadcasted_iota(jnp.int32, (L, DV_PAD), 1) < DV_C

    def epilogue(c, _):
        r0 = pl.multiple_of(c * L, L)
        h = jnp.where(real, hf_scr[pl.ds(r0, L), :] + hb_scr[pl.ds(r0, L), :], 0.0)
        ms = jnp.sum(h * h, axis=-1, keepdims=True) * (1.0 / DV_C)
        hn = h * lax.rsqrt(ms + EPS) * hg_ref[...]
        y = jax.nn.sigmoid(o_ref[pl.ds(r0, L), :].astype(F32)) * hn
        y_ref[pl.ds(r0, L), :] = y.astype(BF16)
        return 0

    lax.fori_loop(0, nc, epilogue, 0, unroll=2)


def _mlstm(proj, gates, gate_b, head_g):
    bsz, s, _ = proj.shape
    nc = s // CHUNK
    return pl.pallas_call(
        _mlstm_kernel,
        grid=(bsz, N_H_C),
        in_specs=[
            pl.BlockSpec((None, s, DQK_PAD), lambda b, h: (b, 0, OFF_MQ // DQK_PAD + h)),
            pl.BlockSpec((None, s, DQK_PAD), lambda b, h: (b, 0, OFF_MK // DQK_PAD + h)),
            pl.BlockSpec((None, s, DV_PAD), lambda b, h: (b, 0, OFF_MV // DV_PAD + h)),
            pl.BlockSpec((None, s, DV_PAD), lambda b, h: (b, 0, OFF_MO // DV_PAD + h)),
            pl.BlockSpec((None, s, LANES), lambda b, h: (b, 0, 0)),
            pl.BlockSpec((1, LANES), lambda b, h: (0, 0)),
            pl.BlockSpec((None, 1, DV_PAD), lambda b, h: (h, 0, 0)),
        ],
        out_specs=pl.BlockSpec((None, s, DV_PAD), lambda b, h: (b, 0, h)),
        out_shape=jax.ShapeDtypeStruct((bsz, s, W_C_PAD), BF16),
        scratch_shapes=[
            pltpu.VMEM((s, DV_PAD), F32), pltpu.VMEM((s, DV_PAD), F32),
            pltpu.VMEM((s, 4 * CHUNK), F32),
            pltpu.VMEM((2, nc, DQK_PAD, CHUNK), BF16),
            pltpu.VMEM((2, 4, nc, CHUNK), F32),
            pltpu.VMEM((2, nc, 3 * CHUNK), F32),
            pltpu.VMEM((DQK_PAD, DV_PAD), F32), pltpu.VMEM((DQK_PAD, DV_PAD), F32),
            pltpu.VMEM((nc * 16, CHUNK), F32),
        ],
        compiler_params=_cparams(("parallel", "arbitrary")),
        name="mlstm",
    )(proj, proj, proj, proj, gates, gate_b, head_g)


def _out_kernel(ya_ref, yb_ref, yc_ref, wa_ref, wb_ref, wc_ref, pg_ref, gate_ref, x_ref, o_ref, y_scr):
    for r in range(0, o_ref.shape[0], SUB_ROWS):
        rows = slice(r, r + SUB_ROWS)
        y_scr[rows, :] = (_dot(ya_ref[rows, :], wa_ref[...]) + _dot(yb_ref[rows, :], wb_ref[...])
                          + _dot(yc_ref[rows, :], wc_ref[...]))
        _norm_residual_rows(y_scr, x_ref, pg_ref, gate_ref, o_ref, r, SUB_ROWS)


def _out_proj(ya, yb, yc, wa, wb, wc, pg, modl, x):
    bsz, s, d = x.shape
    tm = min(2 * SUB_ROWS, s)
    row = lambda w: pl.BlockSpec((None, tm, w), lambda b, i: (b, i, 0))
    const = lambda shape: pl.BlockSpec(shape, lambda b, i: (0,) * len(shape), pipeline_mode=pl.Buffered(1))
    return pl.pallas_call(
        _out_kernel,
        grid=(bsz, s // tm),
        in_specs=[
            row(W_A), row(W_B), row(W_C_PAD),
            const((W_A, d)), const((W_B, d)), const((W_C_PAD, d)), const((1, d)),
            pl.BlockSpec((None, 1, d), lambda b, i: (b, 0, 2)),
            row(d),
        ],
        out_specs=row(d),
        out_shape=jax.ShapeDtypeStruct((bsz, s, d), F32),
        scratch_shapes=[pltpu.VMEM((tm, d), F32)],
        compiler_params=_cparams(("parallel", "parallel")),
        name="out_proj",
    )(ya, yb, yc, wa, wb, wc, pg, modl, x)


FFN_TF = 512
FFN_VMEM_LIMIT = 62 * 1024 * 1024


def _ffn_kernel(x_ref, g_ref, sc_ref, sh_ref, wg_ref, wu_ref, wd_ref, pg_ref, gate_ref, o_ref, h_scr):
    j = pl.program_id(2)
    last = pl.num_programs(2) - 1
    tm = x_ref.shape[0]
    subs = [(r, slice(r, r + SUB_ROWS)) for r in range(0, tm, SUB_ROWS)]

    def partial_down(rows):
        h = h_scr[rows, :]
        a = _dot(h, wg_ref[...])
        u = _dot(h, wu_ref[...])
        hid = (a * jax.nn.sigmoid(a)) * u
        return _dot(hid.astype(BF16), wd_ref[...])

    @pl.when(j == 0)
    def _():
        for r, rows in subs:
            _mod_norm_rows(x_ref, g_ref, sc_ref, sh_ref, h_scr, r, SUB_ROWS)
            o_ref[rows, :] = partial_down(rows)

    @pl.when((j > 0) & (j < last))
    def _():
        o_ref[...] += partial_down(slice(0, tm))

    @pl.when(j == last)
    def _():
        for r, rows in subs:
            o_ref[rows, :] += partial_down(rows)
            _norm_residual_rows(o_ref, x_ref, pg_ref, gate_ref, o_ref, r, SUB_ROWS)


def _ffn(x, g, modl, wg, wu, wd, pg):
    bsz, s, d = x.shape
    f = wg.shape[1]
    tm = min(1024, s)
    row = pl.BlockSpec((None, tm, d), lambda b, i, j: (b, i, 0))
    vec = pl.BlockSpec((1, d), lambda b, i, j: (0, 0))
    modv = lambda k: pl.BlockSpec((None, 1, d), lambda b, i, j: (b, 0, k))
    return pl.pallas_call(
        _ffn_kernel,
        grid=(bsz, s // tm, f // FFN_TF),
        in_specs=[
            row, vec, modv(4), modv(3),
            pl.BlockSpec((d, FFN_TF), lambda b, i, j: (0, j)),
            pl.BlockSpec((d, FFN_TF), lambda b, i, j: (0, j)),
            pl.BlockSpec((FFN_TF, d), lambda b, i, j: (j, 0)),
            vec, modv(5),
        ],
        out_specs=row,
        out_shape=jax.ShapeDtypeStruct((bsz, s, d), F32),
        scratch_shapes=[pltpu.VMEM((tm, d), BF16)],
        compiler_params=_cparams(("parallel", "parallel", "arbitrary"), FFN_VMEM_LIMIT),
        name="ffn",
    )(x, g, modl, modl, wg, wu, wd, pg, modl)


def _pair_heads_a(w, axis):
    shape = w.shape
    w = jnp.moveaxis(w, axis, -1)
    lead = w.shape[:-1]
    w = w.reshape(lead + (2, 2, GROUP_A, DH_A))
    w = jnp.swapaxes(w, -3, -2)
    w = w.reshape(lead + (W_A,))
    return jnp.moveaxis(w, -1, axis).reshape(shape)


def _pad_heads(w, n_heads, width, padded, axis):
    w = jnp.moveaxis(w, axis, -1)
    lead = w.shape[:-1]
    w = w.reshape(lead + (n_heads, width))
    w = jnp.pad(w, [(0, 0)] * len(lead) + [(0, 0), (0, padded - width)])
    w = w.reshape(lead + (n_heads * padded,))
    return jnp.moveaxis(w, -1, axis)


def _layout_w_in(w):
    d = w.shape[0]
    sizes = (W_A, N_KV_A * DH_A, N_KV_A * DH_A, Q_LORA, KV_LORA, D_ROPE,
             N_H_C * DQK_C, N_H_C * DQK_C, W_C, 4 * N_H_C, W_C)
    pts = np.cumsum(sizes)[:-1].tolist()
    aq, ak, av, bcq, bckv, bkr, cq, ck, cv, cg, co = jnp.split(w, pts, axis=1)
    parts = [
        _pair_heads_a(aq, 1), ak, av, bckv,
        jnp.pad(cg, ((0, 0), (0, LANES - 4 * N_H_C))),
        bcq, bkr,
        _pad_heads(cq, N_H_C, DQK_C, DQK_PAD, 1), _pad_heads(ck, N_H_C, DQK_C, DQK_PAD, 1),
        _pad_heads(cv, N_H_C, DV_C, DV_PAD, 1), _pad_heads(co, N_H_C, DV_C, DV_PAD, 1),
    ]
    out = jnp.concatenate(parts, axis=1).astype(BF16)
    assert out.shape == (d, IN_WIDTH_PAD)
    return out


def _layout_w_uq(w):
    w = w.reshape(Q_LORA, N_H_B, D_NOPE + D_ROPE)
    nope = w[:, :, :D_NOPE].reshape(Q_LORA, N_H_B * D_NOPE)
    rope = w[:, :, D_NOPE:].reshape(Q_LORA, N_H_B * D_ROPE)
    w = jnp.concatenate([nope, rope], axis=1)
    return jnp.pad(w, ((0, 512 - Q_LORA), (0, 0))).astype(BF16)


def _layout_w_ukv(w):
    w = w.reshape(KV_LORA, N_H_B, D_NOPE + D_V_B)
    kn = w[:, :, :D_NOPE].reshape(KV_LORA, N_H_B * D_NOPE)
    v = w[:, :, D_NOPE:].reshape(KV_LORA, N_H_B * D_V_B)
    return kn.astype(BF16), v.T.astype(BF16)


def kernel(x, c, positions, mod_w, mod_b, pre_mix_g, post_mix_g, pre_ffn_g, post_ffn_g, w_in, attn_sink, mla_q_norm_g, mla_w_uq, mla_kv_norm_g, mla_w_ukv, mlstm_gate_b, mlstm_head_g, w_out, ffn_w_gate, ffn_w_up, ffn_w_down):
    depth = mod_w.shape[0]
    bsz, s, d = x.shape
    mod = _modulation(c, mod_w, mod_b)
    pos3 = positions.reshape(bsz, 1, s)
    inv = 1.0 / (ROPE_THETA ** (jnp.arange(0, D_ROPE, 2, dtype=F32) / D_ROPE))
    inv_col = inv.reshape(D_ROPE // 2, 1)
    for l in range(depth):
        modl = mod[l].reshape(bsz, 1, 6 * d)
        proj, gates = _in_proj(x, pre_mix_g[l].reshape(1, d), modl, _layout_w_in(w_in[l]))
        ya = _window_attention(proj, attn_sink[l])
        qg = jnp.pad(mla_q_norm_g[l], (0, 512 - Q_LORA)).reshape(1, 512)
        wk, wvt = _layout_w_ukv(mla_w_ukv[l])
        qb, kb, vtb = _mla_pre(proj, pos3, qg, _layout_w_uq(mla_w_uq[l]),
                               mla_kv_norm_g[l].reshape(1, KV_LORA), wk, wvt, inv_col)
        yb = _mla_attention(qb, kb, vtb)
        gate_b = jnp.pad(mlstm_gate_b[l], (0, LANES - 4 * N_H_C)).reshape(1, LANES)
        head_g = jnp.pad(mlstm_head_g[l].reshape(N_H_C, 1, DV_C), ((0, 0), (0, 0), (0, DV_PAD - DV_C)))
        yc = _mlstm(proj, gates, gate_b, head_g)
        wo = w_out[l]
        wa = _pair_heads_a(wo[:W_A], 0).astype(BF16)
        wb = wo[W_A:W_A + W_B].astype(BF16)
        wc = _pad_heads(wo[W_A + W_B:], N_H_C, DV_C, DV_PAD, 0).astype(BF16)
        x = _out_proj(ya, yb, yc, wa, wb, wc, post_mix_g[l].reshape(1, d), modl, x)
        x = _ffn(x, pre_ffn_g[l].reshape(1, d), modl, ffn_w_gate[l].astype(BF16), ffn_w_up[l].astype(BF16),
                 ffn_w_down[l].astype(BF16), post_ffn_g[l].reshape(1, d))
    return x
```

```python
import functools
import math

import numpy as np
import jax
import jax.numpy as jnp
from jax import lax
from jax.experimental import pallas as pl
from jax.experimental.pallas import tpu as pltpu

F32 = jnp.float32
BF16 = jnp.bfloat16

D_MODEL = 2048
EPS = 1e-6
N_Q_A, N_KV_A, DH_A, GROUP_A = 12, 4, 64, 3
WINDOW = 128
BLOCK_A = 128
N_H_B, Q_LORA, KV_LORA, D_NOPE, D_ROPE, D_V_B = 4, 448, 128, 128, 64, 128
ROPE_THETA = 10000.0
N_H_C, DQK_C, DV_C = 4, 96, 192
W_A, W_B, W_C = N_Q_A * DH_A, N_H_B * D_V_B, N_H_C * DV_C
D_FF = 5632

LANES = 128
DQK_PAD = 128
DV_PAD = 256
ONES_COL = DV_C
VMEM_LIMIT = 56 * 1024 * 1024

OFF_AQ, OFF_AK, OFF_AV = 0, 768, 1024
OFF_CKV, OFF_CG, OFF_CQKR = 1280, 1408, 1536
OFF_MQ, OFF_MK, OFF_MV, OFF_MO = 2048, 2560, 3072, 4096
IN_WIDTH_PAD = 5120
W_C_PAD = N_H_C * DV_PAD
MIX_PAD = W_A + W_B + W_C_PAD

CHUNK = 128


def _alibi_slopes(n):
    def pow2(m):
        start = 2.0 ** (-8.0 / m)
        return [start ** (i + 1) for i in range(m)]
    if math.log2(n).is_integer():
        s = pow2(n)
    else:
        p = 2 ** math.floor(math.log2(n))
        s = pow2(p) + pow2(2 * p)[0::2][: n - p]
    return [float(np.float32(v)) for v in s]


SLOPES_A = _alibi_slopes(N_Q_A)


def _cparams(sem, vmem_limit=VMEM_LIMIT):
    return pltpu.CompilerParams(dimension_semantics=sem, vmem_limit_bytes=vmem_limit)


def _dot(a, b):
    return jnp.dot(a, b, preferred_element_type=F32)


def _dot_nt(a, b):
    return lax.dot_general(a, b, (((1,), (1,)), ((), ())), preferred_element_type=F32)


def _dot_tn(a, b):
    return lax.dot_general(a, b, (((0,), (0,)), ((), ())), preferred_element_type=F32)


def _mod_kernel(c_ref, w_ref, b_ref, o_ref):
    c = c_ref[...]
    s = c * jax.nn.sigmoid(c)
    o_ref[...] = _dot(s.astype(BF16), w_ref[...].astype(BF16)) + b_ref[...]


def _modulation(c, mod_w, mod_b):
    depth, d, n = mod_w.shape
    bsz = c.shape[0]
    tn = 1024
    return pl.pallas_call(
        _mod_kernel,
        grid=(depth, n // tn),
        in_specs=[
            pl.BlockSpec((bsz, d), lambda l, j: (0, 0)),
            pl.BlockSpec((None, d, tn), lambda l, j: (l, 0, j)),
            pl.BlockSpec((None, 1, tn), lambda l, j: (l, 0, j)),
        ],
        out_specs=pl.BlockSpec((None, bsz, tn), lambda l, j: (l, 0, j)),
        out_shape=jax.ShapeDtypeStruct((depth, bsz, n), F32),
        compiler_params=_cparams(("parallel", "parallel")),
        name="adaln_mod",
    )(c, mod_w, mod_b.reshape(depth, 1, n))


ROW_SLAB = 16
SUB_ROWS = 256


def _row_slabs(first_row, n_rows):
    return [slice(r, r + ROW_SLAB) for r in range(first_row, first_row + n_rows, ROW_SLAB)]


def _mod_norm_rows(x_ref, g_ref, sc_ref, sh_ref, h_ref, first_row, n_rows):
    gain = g_ref[...] * (1.0 + sc_ref[...])
    shift = sh_ref[...]
    for rows in _row_slabs(first_row, n_rows):
        x = x_ref[rows, :]
        ms = jnp.mean(x * x, axis=-1, keepdims=True)
        h_ref[rows, :] = (x * lax.rsqrt(ms + EPS) * gain + shift).astype(BF16)


def _norm_residual_rows(y_ref, x_ref, pg_ref, gate_ref, o_ref, first_row, n_rows):
    gain = gate_ref[...] * pg_ref[...]
    for rows in _row_slabs(first_row, n_rows):
        y = y_ref[rows, :]
        ms = jnp.mean(y * y, axis=-1, keepdims=True)
        o_ref[rows, :] = x_ref[rows, :] + y * lax.rsqrt(ms + EPS) * gain


IN_TN = 1024
GATE_TILE = OFF_CG // IN_TN
GATE_OFF = OFF_CG % IN_TN


def _in_kernel(x_ref, g_ref, sc_ref, sh_ref, w_ref, cs_ref, o_ref, gate_ref, h_scr):
    j = pl.program_id(2)
    tm = x_ref.shape[0]

    def project(rows):
        acc = _dot(h_scr[rows, :], w_ref[...])
        o_ref[rows, :] = (acc * cs_ref[...]).astype(BF16)
        return acc

    @pl.when(j == 0)
    def _():
        for r in range(0, tm, SUB_ROWS):
            _mod_norm_rows(x_ref, g_ref, sc_ref, sh_ref, h_scr, r, SUB_ROWS)
            project(slice(r, r + SUB_ROWS))

    @pl.when(j != 0)
    def _():
        acc = project(slice(0, tm))

        @pl.when(j == GATE_TILE)
        def _():
            gate_ref[...] = acc[:, GATE_OFF:GATE_OFF + LANES]


def _in_proj(x, g, modl, w):
    bsz, s, d = x.shape
    n = w.shape[1]
    tm = min(1024, s)
    col_scale = jnp.where(jnp.arange(n) < W_A, WIN_QSCALE, 1.0).astype(F32).reshape(1, n)
    return pl.pallas_call(
        _in_kernel,
        grid=(bsz, s // tm, n // IN_TN),
        in_specs=[
            pl.BlockSpec((None, tm, d), lambda b, i, j: (b, i, 0)),
            pl.BlockSpec((1, d), lambda b, i, j: (0, 0)),
            pl.BlockSpec((None, 1, d), lambda b, i, j: (b, 0, 1)),
            pl.BlockSpec((None, 1, d), lambda b, i, j: (b, 0, 0)),
            pl.BlockSpec((d, IN_TN), lambda b, i, j: (0, j)),
            pl.BlockSpec((1, IN_TN), lambda b, i, j: (0, j)),
        ],
        out_specs=[
            pl.BlockSpec((None, tm, IN_TN), lambda b, i, j: (b, i, j)),
            pl.BlockSpec((None, tm, LANES), lambda b, i, j: (b, i, 0)),
        ],
        out_shape=[
            jax.ShapeDtypeStruct((bsz, s, n), BF16),
            jax.ShapeDtypeStruct((bsz, s, LANES), F32),
        ],
        scratch_shapes=[pltpu.VMEM((tm, d), BF16)],
        compiler_params=_cparams(("parallel", "parallel", "arbitrary")),
        name="in_proj",
    )(x, g, modl, modl, w, col_scale)


LOG2E = math.log2(math.e)
WIN_QSCALE = DH_A ** -0.5 * LOG2E


WIN_QB = 4


def _win_kernel(sink_ref, q_ref, kp_ref, kc_ref, kn_ref, vp_ref, vc_ref, vn_ref, o_ref, *, nb):
    n = pl.program_id(1)
    blk = BLOCK_A
    nk = 3 * blk
    qi = lax.broadcasted_iota(jnp.int32, (blk, nk), 0)
    kj = lax.broadcasted_iota(jnp.int32, (blk, nk), 1)
    dist = jnp.abs(qi - kj + blk)
    k_all = jnp.concatenate([kp_ref[...], kc_ref[...], kn_ref[...]], axis=0)
    v_all = jnp.concatenate([vp_ref[...], vc_ref[...], vn_ref[...]], axis=0)
    low_k = lax.broadcasted_iota(jnp.int32, (nk, LANES), 1) < DH_A
    low_q = lax.broadcasted_iota(jnp.int32, (blk, LANES), 1) < DH_A
    zero = jnp.zeros((nk, LANES), BF16)
    for sub in range(WIN_QB):
        g = n * WIN_QB + sub
        rows = slice(sub * blk, (sub + 1) * blk)
        in_seq = ((kj >= blk) | (g > 0)) & ((kj < 2 * blk) | (g < nb - 1))
        dist_masked = jnp.where((dist <= WINDOW) & in_seq, dist.astype(F32), jnp.inf)
        for p in range(2):
            kt = k_all[sub * blk:sub * blk + nk, p * LANES:(p + 1) * LANES]
            vt = v_all[sub * blk:sub * blk + nk, p * LANES:(p + 1) * LANES]
            k_big = jnp.concatenate([jnp.where(low_k, kt, zero), jnp.where(low_k, zero, kt)], axis=0)
            v_big = jnp.concatenate([jnp.where(low_k, vt, zero), jnp.where(low_k, zero, vt)], axis=0)
            q3 = jnp.concatenate([q_ref[rows, (p * GROUP_A + a) * LANES:(p * GROUP_A + a + 1) * LANES]
                                  for a in range(GROUP_A)], axis=0)
            s3 = _dot_nt(q3, k_big)
            p_rows, inv_rows = [], []
            for a in range(GROUP_A):
                p_halves, inv_halves = [], []
                for half in range(2):
                    head = (2 * p + half) * GROUP_A + a
                    s = (s3[a * blk:(a + 1) * blk, half * nk:(half + 1) * nk]
                         - (SLOPES_A[head] * LOG2E) * dist_masked)
                    sink = sink_ref[head] * LOG2E
                    m = jnp.maximum(jnp.max(s, axis=-1, keepdims=True), sink)
                    e = jnp.exp2(s - m)
                    l = jnp.sum(e, axis=-1, keepdims=True) + jnp.exp2(sink - m)
                    p_halves.append(e.astype(BF16))
                    inv_halves.append(1.0 / l)
                p_rows.append(jnp.concatenate(p_halves, axis=1))
                inv_rows.append(jnp.where(low_q, inv_halves[0], inv_halves[1]))
            o3 = _dot(jnp.concatenate(p_rows, axis=0), v_big)
            for a in range(GROUP_A):
                t = p * GROUP_A + a
                o_ref[rows, t * LANES:(t + 1) * LANES] = (o3[a * blk:(a + 1) * blk] * inv_rows[a]).astype(BF16)


def _window_attention(proj, sinks):
    bsz, s, _ = proj.shape
    blk = BLOCK_A
    nb = s // blk
    assert nb % WIN_QB == 0
    tq = WIN_QB * blk
    kb, vb = OFF_AK // 256, OFF_AV // 256
    prev = lambda n: jnp.maximum(n * WIN_QB - 1, 0)
    nxt = lambda n: jnp.minimum((n + 1) * WIN_QB, nb - 1)
    return pl.pallas_call(
        functools.partial(_win_kernel, nb=nb),
        grid=(bsz, nb // WIN_QB),
        in_specs=[
            pl.BlockSpec(memory_space=pltpu.SMEM),
            pl.BlockSpec((None, tq, W_A), lambda b, n: (b, n, 0)),
            pl.BlockSpec((None, blk, 256), lambda b, n: (b, prev(n), kb)),
            pl.BlockSpec((None, tq, 256), lambda b, n: (b, n, kb)),
            pl.BlockSpec((None, blk, 256), lambda b, n: (b, nxt(n), kb)),
            pl.BlockSpec((None, blk, 256), lambda b, n: (b, prev(n), vb)),
            pl.BlockSpec((None, tq, 256), lambda b, n: (b, n, vb)),
            pl.BlockSpec((None, blk, 256), lambda b, n: (b, nxt(n), vb)),
        ],
        out_specs=pl.BlockSpec((None, tq, W_A), lambda b, n: (b, n, 0)),
        out_shape=jax.ShapeDtypeStruct((bsz, s, W_A), BF16),
        compiler_params=_cparams(("parallel", "parallel")),
        name="win_attn",
    )(sinks, proj, proj, proj, proj, proj, proj, proj)


MLA_TK = 512
MLA_QSCALE = (D_NOPE + D_ROPE) ** -0.5 * math.log2(math.e)


def _mla_pre_kernel(cqkr_ref, ckv_ref, pos_ref, qg_ref, wq_ref, kvg_ref, wk_ref, wvt_ref, inv_ref,
                    q_ref, k_ref, vt_ref):
    tm = cqkr_ref.shape[0]
    t = cqkr_ref[...].astype(F32)
    lane = lax.broadcasted_iota(jnp.int32, (tm, 4 * LANES), 1)
    cq = jnp.where(lane < Q_LORA, t, 0.0)
    ms = jnp.sum(cq * cq, axis=-1, keepdims=True) * (1.0 / Q_LORA)
    qn = cq * lax.rsqrt(ms + EPS) * qg_ref[...]
    q = _dot(qn.astype(BF16), wq_ref[...]) * MLA_QSCALE
    c = ckv_ref[...].astype(F32)
    ms = jnp.mean(c * c, axis=-1, keepdims=True)
    kvn = (c * lax.rsqrt(ms + EPS) * kvg_ref[...]).astype(BF16)
    kn = _dot(kvn, wk_ref[...])
    vt_ref[...] = _dot_nt(wvt_ref[...], kvn).astype(BF16)

    ang_t = inv_ref[...] * pos_ref[...].astype(F32)
    cs_t, sn_t = jnp.cos(ang_t), jnp.sin(ang_t)
    cs = jnp.concatenate([cs_t] * 4, axis=0).T
    sgn_sn = jnp.concatenate([-sn_t, sn_t] * 2, axis=0).T
    l128 = lax.broadcasted_iota(jnp.int32, (tm, LANES), 1)
    first = (l128 % D_ROPE) < (D_ROPE // 2)

    def rope(x):
        swapped = jnp.where(first, pltpu.roll(x, LANES - D_ROPE // 2, 1), pltpu.roll(x, D_ROPE // 2, 1))
        return x * cs + swapped * sgn_sn

    kr = rope(t[:, 3 * LANES:4 * LANES])
    kr_hi = jnp.where(l128 >= D_ROPE, kr, 0.0)
    kr_lo = pltpu.roll(kr_hi, D_ROPE, 1)
    kr_tiles = (kr_lo.astype(BF16), kr_hi.astype(BF16))
    for pair in range(2):
        qr = rope(q[:, (4 + pair) * LANES:(5 + pair) * LANES]).astype(BF16)
        for half in range(2):
            h = 2 * pair + half
            q_ref[:, (2 * h) * LANES:(2 * h + 1) * LANES] = q[:, h * LANES:(h + 1) * LANES].astype(BF16)
            q_ref[:, (2 * h + 1) * LANES:(2 * h + 2) * LANES] = qr
            k_ref[:, (2 * h) * LANES:(2 * h + 1) * LANES] = kn[:, h * LANES:(h + 1) * LANES].astype(BF16)
            k_ref[:, (2 * h + 1) * LANES:(2 * h + 2) * LANES] = kr_tiles[half]


def _mla_pre(proj, pos3, qg, wq, kvg, wk, wvt, inv):
    bsz, s, _ = proj.shape
    tm = MLA_TK
    const = lambda shape: pl.BlockSpec(shape, lambda b, i: (0,) * len(shape))
    return pl.pallas_call(
        _mla_pre_kernel,
        grid=(bsz, s // tm),
        in_specs=[
            pl.BlockSpec((None, tm, 512), lambda b, i: (b, i, OFF_CQKR // 512)),
            pl.BlockSpec((None, tm, LANES), lambda b, i: (b, i, OFF_CKV // LANES)),
            pl.BlockSpec((None, 1, tm), lambda b, i: (b, 0, i)),
            const((1, 512)), const((512, 768)), const((1, LANES)), const((LANES, W_B)), const((W_B, LANES)),
            const((D_ROPE // 2, 1)),
        ],
        out_specs=[
            pl.BlockSpec((None, tm, 1024), lambda b, i: (b, i, 0)),
            pl.BlockSpec((None, tm, 1024), lambda b, i: (b, i, 0)),
            pl.BlockSpec((None, None, W_B, tm), lambda b, i: (b, i, 0, 0)),
        ],
        out_shape=[
            jax.ShapeDtypeStruct((bsz, s, 1024), BF16),
            jax.ShapeDtypeStruct((bsz, s, 1024), BF16),
            jax.ShapeDtypeStruct((bsz, s // tm, W_B, tm), BF16),
        ],
        compiler_params=_cparams(("parallel", "parallel")),
        name="mla_pre",
    )(proj, proj, pos3, qg, wq, kvg, wk, wvt, inv)


def _mla_attn_kernel(q_ref, k_ref, vt_ref, o_ref, st_a, st_b, *, tq):
    n_chunks, _, tk = vt_ref.shape
    nt = q_ref.shape[0] // tq

    def scores(t, st_scr):
        q = q_ref[pl.ds(pl.multiple_of(t * tq, tq), tq), :]
        m8 = jnp.full((8, tq), -jnp.inf, F32)
        for c in range(n_chunks):
            st = _dot_nt(k_ref[c * tk:(c + 1) * tk, :], q)
            st_scr[c * tk:(c + 1) * tk, :] = st
            m8 = jnp.maximum(m8, jnp.max(st.reshape(tk // 8, 8, tq), axis=0))
        return jnp.max(m8, axis=0, keepdims=True)

    def output(t, st_scr, m):
        l8 = jnp.zeros((8, tq), F32)
        acc = jnp.zeros((D_V_B, tq), F32)
        for c in range(n_chunks):
            p = jnp.exp2(st_scr[c * tk:(c + 1) * tk, :] - m)
            l8 = l8 + jnp.sum(p.reshape(tk // 8, 8, tq), axis=0)
            acc = acc + _dot(vt_ref[c], p.astype(BF16))
        l = jnp.sum(l8, axis=0, keepdims=True)
        o_ref[pl.ds(pl.multiple_of(t * tq, tq), tq), :] = (acc / l).T.astype(BF16)

    m_first = scores(0, st_a)
    if nt == 1:
        output(0, st_a, m_first)
        return

    def pair(u, m_a):
        t = 2 * u
        m_b = scores(t + 1, st_b)
        output(t, st_a, m_a)
        m_a = scores(t + 2, st_a)
        output(t + 1, st_b, m_b)
        return m_a

    m_a = lax.fori_loop(0, nt // 2 - 1, pair, m_first)
    m_b = scores(nt - 1, st_b)
    output(nt - 2, st_a, m_a)
    output(nt - 1, st_b, m_b)


def _mla_attention(q, k, vt):
    bsz, s, _ = q.shape
    n_chunks, tk = vt.shape[1], vt.shape[3]
    tq = min(512, s)
    assert s // tq == 1 or (s // tq) % 2 == 0
    return pl.pallas_call(
        functools.partial(_mla_attn_kernel, tq=tq),
        grid=(bsz, N_H_B),
        in_specs=[
            pl.BlockSpec((None, s, 256), lambda b, h: (b, 0, h)),
            pl.BlockSpec((None, s, 256), lambda b, h: (b, 0, h)),
            pl.BlockSpec((None, n_chunks, D_V_B, tk), lambda b, h: (b, 0, h, 0)),
        ],
        out_specs=pl.BlockSpec((None, s, D_V_B), lambda b, h: (b, 0, h)),
        out_shape=jax.ShapeDtypeStruct((bsz, s, W_B), BF16),
        scratch_shapes=[pltpu.VMEM((s, tq), F32), pltpu.VMEM((s, tq), F32)],
        compiler_params=_cparams(("parallel", "parallel")),
        name="mla_attn",
    )(q, k, vt)


def _log_sigmoid(x):
    return jnp.minimum(x, 0.0) - jnp.log1p(jnp.exp(-jnp.abs(x)))


def _split3(x):
    hi = x.astype(BF16).astype(F32)
    r1 = x - hi
    mid = r1.astype(BF16).astype(F32)
    return jnp.concatenate([hi, mid, r1 - mid], axis=-1)


def _mlstm_kernel(q_ref, k_ref, v_ref, o_ref, gate_ref, gb_ref, hg_ref, y_ref,
                  hf_scr, hb_scr, tile_scr, kw_scr, rowq_scr, bp_scr, cf_scr, cb_scr, gt_scr):
    s_len = q_ref.shape[0]
    L = CHUNK
    nc = s_len // L
    head = pl.program_id(1)
    ii = lax.broadcasted_iota(jnp.int32, (L, L), 0)
    jj = lax.broadcasted_iota(jnp.int32, (L, L), 1)
    ones = jnp.ones((L, L), BF16)
    ones_lane = lax.broadcasted_iota(jnp.int32, (L, DV_PAD), 1) == ONES_COL
    eye3 = jnp.concatenate([(ii == jj).astype(BF16)] * 3, axis=1)
    tri_ones = [jnp.concatenate([jnp.concatenate([m.astype(BF16), ones], axis=1)] * 3, axis=0)
                for m in (ii <= jj, ii >= jj)]
    masks = (jj <= ii, jj >= ii)
    h_scrs = (hf_scr, hb_scr)
    c_scrs = (cf_scr, cb_scr)

    def load_v_ext(r0):
        return jnp.where(ones_lane, jnp.ones((L, DV_PAD), BF16), v_ref[pl.ds(r0, L), :])

    def transpose_gates(c, _):
        g = gate_ref[pl.ds(pl.multiple_of(c * L, L), L), :] + gb_ref[...]
        gt_scr[pl.ds(pl.multiple_of(c * 16, 16), 16), :] = g.T[0:16, :]
        return 0

    lax.fori_loop(0, nc, transpose_gates, 0, unroll=4)

    for d in range(2):
        li = gt_scr[pl.ds(d * N_H_C + head, nc, stride=16), :]
        lf = _log_sigmoid(gt_scr[pl.ds((2 + d) * N_H_C + head, nc, stride=16), :])
        bt = _dot(_split3(lf).astype(BF16), tri_ones[d])
        b, tot = bt[:, :L], bt[:, L:]
        r = li - b
        m_loc = tot + jnp.broadcast_to(jnp.max(r, axis=-1, keepdims=True), r.shape)
        rowq_scr[d, 0] = r
        rowq_scr[d, 1] = jnp.exp(tot + r - m_loc)
        rowq_scr[d, 2] = tot
        rowq_scr[d, 3] = m_loc
        bp_scr[d] = _split3(b + math.log(DQK_C ** -0.5))

    def prep(c, _):
        r0 = pl.multiple_of(c * L, L)
        q, k = q_ref[pl.ds(r0, L), :], k_ref[pl.ds(r0, L), :]
        s = _dot_nt(q, k)
        kt = k.astype(F32).T
        p = []
        for d in range(2):
            u = jnp.where(masks[d], rowq_scr[d, 0, pl.ds(c, 1), :], -jnp.inf)
            ct = jnp.broadcast_to(jnp.max(u, axis=-1, keepdims=True), (L, L))
            p.append((s * jnp.exp(u - ct)).astype(BF16))
            tile_scr[pl.ds(r0, L), (2 + d) * L:(3 + d) * L] = ct
            kw_scr[d, c] = (kt * rowq_scr[d, 1, pl.ds(c, 1), :]).astype(BF16)
        rhs = jnp.concatenate([jnp.broadcast_to(bp_scr[d, pl.ds(c, 1), :], (L, 3 * L)) for d in range(2)], axis=0)
        tile_scr[pl.ds(r0, L), 0:2 * L] = _dot_nt(eye3, rhs.astype(BF16))
        intra = _dot(jnp.concatenate(p, axis=0), load_v_ext(r0))
        hf_scr[pl.ds(r0, L), :] = intra[:L]
        hb_scr[pl.ds(r0, L), :] = intra[L:]
        return 0

    lax.fori_loop(0, nc, prep, 0, unroll=4)

    cf_scr[...] = jnp.zeros_like(cf_scr)
    cb_scr[...] = jnp.zeros_like(cb_scr)

    def one_dir(c, m_state, d):
        h_scr, c_scr = h_scrs[d], c_scrs[d]
        r0 = pl.multiple_of(c * L, L)
        tot, m_loc = rowq_scr[d, 2, pl.ds(c, 1), :], rowq_scr[d, 3, pl.ds(c, 1), :]
        bt = tile_scr[pl.ds(r0, L), d * L:(d + 1) * L]
        ct = tile_scr[pl.ds(r0, L), (2 + d) * L:(3 + d) * L]
        v_ext = load_v_ext(r0)
        c_state = c_scr[...]
        qc = _dot(q_ref[pl.ds(r0, L), :], c_state.astype(BF16))
        mx = jnp.maximum(m_state, ct)
        iw = jnp.exp(m_state - mx)
        xw = jnp.exp(ct - mx)
        num = [iw * qc[:, j * L:(j + 1) * L] + xw * h_scr[pl.ds(r0, L), j * L:(j + 1) * L] for j in range(2)]
        den = jnp.broadcast_to(num[1][:, ONES_COL - L:ONES_COL - L + 1], (L, L))
        inv = 1.0 / jnp.maximum(jnp.abs(den), jnp.exp(-(bt + mx)))
        h_scr[pl.ds(r0, L), 0:L] = num[0] * inv
        h_scr[pl.ds(r0, L), L:2 * L] = num[1] * inv
        c_loc = _dot(kw_scr[d, c], v_ext)
        m_new = jnp.maximum(tot + m_state, m_loc)
        sp, sl = jnp.exp(tot + m_state - m_new), jnp.exp(m_loc - m_new)
        c_scr[...] = jnp.concatenate(
            [sp * c_state[:, j * L:(j + 1) * L] + sl * c_loc[:, j * L:(j + 1) * L] for j in range(2)], axis=1)
        return m_new

    def scan(c, carry):
        return one_dir(c, carry[0], 0), one_dir(nc - 1 - c, carry[1], 1)

    neg = jnp.full((1, LANES), -jnp.inf, F32)
    lax.fori_loop(0, nc, scan, (neg, neg), unroll=4)

    real = lax.broadcasted_iota(jnp.int32, (L, DV_PAD), 1) < DV_C

    def epilogue(c, _):
        r0 = pl.multiple_of(c * L, L)
        h = jnp.where(real, hf_scr[pl.ds(r0, L), :] + hb_scr[pl.ds(r0, L), :], 0.0)
        ms = jnp.sum(h * h, axis=-1, keepdims=True) * (1.0 / DV_C)
        hn = h * lax.rsqrt(ms + EPS) * hg_ref[...]
        y = jax.nn.sigmoid(o_ref[pl.ds(r0, L), :].astype(F32)) * hn
        y_ref[pl.ds(r0, L), :] = y.astype(BF16)
        return 0

    lax.fori_loop(0, nc, epilogue, 0, unroll=2)


def _mlstm(proj, gates, gate_b, head_g):
    bsz, s, _ = proj.shape
    nc = s // CHUNK
    return pl.pallas_call(
        _mlstm_kernel,
        grid=(bsz, N_H_C),
        in_specs=[
            pl.BlockSpec((None, s, DQK_PAD), lambda b, h: (b, 0, OFF_MQ // DQK_PAD + h)),
            pl.BlockSpec((None, s, DQK_PAD), lambda b, h: (b, 0, OFF_MK // DQK_PAD + h)),
            pl.BlockSpec((None, s, DV_PAD), lambda b, h: (b, 0, OFF_MV // DV_PAD + h)),
            pl.BlockSpec((None, s, DV_PAD), lambda b, h: (b, 0, OFF_MO // DV_PAD + h)),
            pl.BlockSpec((None, s, LANES), lambda b, h: (b, 0, 0)),
            pl.BlockSpec((1, LANES), lambda b, h: (0, 0)),
            pl.BlockSpec((None, 1, DV_PAD), lambda b, h: (h, 0, 0)),
        ],
        out_specs=pl.BlockSpec((None, s, DV_PAD), lambda b, h: (b, 0, h)),
        out_shape=jax.ShapeDtypeStruct((bsz, s, W_C_PAD), BF16),
        scratch_shapes=[
            pltpu.VMEM((s, DV_PAD), F32), pltpu.VMEM((s, DV_PAD), F32),
            pltpu.VMEM((s, 4 * CHUNK), F32),
            pltpu.VMEM((2, nc, DQK_PAD, CHUNK), BF16),
            pltpu.VMEM((2, 4, nc, CHUNK), F32),
            pltpu.VMEM((2, nc, 3 * CHUNK), F32),
            pltpu.VMEM((DQK_PAD, DV_PAD), F32), pltpu.VMEM((DQK_PAD, DV_PAD), F32),
            pltpu.VMEM((nc * 16, CHUNK), F32),
        ],
        compiler_params=_cparams(("parallel", "arbitrary")),
        name="mlstm",
    )(proj, proj, proj, proj, gates, gate_b, head_g)


def _out_kernel(ya_ref, yb_ref, yc_ref, wa_ref, wb_ref, wc_ref, pg_ref, gate_ref, x_ref, o_ref, y_scr):
    for r in range(0, o_ref.shape[0], SUB_ROWS):
        rows = slice(r, r + SUB_ROWS)
        y_scr[rows, :] = (_dot(ya_ref[rows, :], wa_ref[...]) + _dot(yb_ref[rows, :], wb_ref[...])
                          + _dot(yc_ref[rows, :], wc_ref[...]))
        _norm_residual_rows(y_scr, x_ref, pg_ref, gate_ref, o_ref, r, SUB_ROWS)


def _out_proj(ya, yb, yc, wa, wb, wc, pg, modl, x):
    bsz, s, d = x.shape
    tm = min(2 * SUB_ROWS, s)
    row = lambda w: pl.BlockSpec((None, tm, w), lambda b, i: (b, i, 0))
    const = lambda shape: pl.BlockSpec(shape, lambda b, i: (0,) * len(shape), pipeline_mode=pl.Buffered(1))
    return pl.pallas_call(
        _out_kernel,
        grid=(bsz, s // tm),
        in_specs=[
            row(W_A), row(W_B), row(W_C_PAD),
            const((W_A, d)), const((W_B, d)), const((W_C_PAD, d)), const((1, d)),
            pl.BlockSpec((None, 1, d), lambda b, i: (b, 0, 2)),
            row(d),
        ],
        out_specs=row(d),
        out_shape=jax.ShapeDtypeStruct((bsz, s, d), F32),
        scratch_shapes=[pltpu.VMEM((tm, d), F32)],
        compiler_params=_cparams(("parallel", "parallel")),
        name="out_proj",
    )(ya, yb, yc, wa, wb, wc, pg, modl, x)


FFN_TF = 512
FFN_VMEM_LIMIT = 62 * 1024 * 1024


def _ffn_kernel(x_ref, g_ref, sc_ref, sh_ref, wg_ref, wu_ref, wd_ref, pg_ref, gate_ref, o_ref, h_scr):
    j = pl.program_id(2)
    last = pl.num_programs(2) - 1
    tm = x_ref.shape[0]
    subs = [(r, slice(r, r + SUB_ROWS)) for r in range(0, tm, SUB_ROWS)]

    def partial_down(rows):
        h = h_scr[rows, :]
        a = _dot(h, wg_ref[...])
        u = _dot(h, wu_ref[...])
        hid = (a * jax.nn.sigmoid(a)) * u
        return _dot(hid.astype(BF16), wd_ref[...])

    @pl.when(j == 0)
    def _():
        for r, rows in subs:
            _mod_norm_rows(x_ref, g_ref, sc_ref, sh_ref, h_scr, r, SUB_ROWS)
            o_ref[rows, :] = partial_down(rows)

    @pl.when((j > 0) & (j < last))
    def _():
        o_ref[...] += partial_down(slice(0, tm))

    @pl.when(j == last)
    def _():
        for r, rows in subs:
            o_ref[rows, :] += partial_down(rows)
            _norm_residual_rows(o_ref, x_ref, pg_ref, gate_ref, o_ref, r, SUB_ROWS)


def _ffn(x, g, modl, wg, wu, wd, pg, layer):
    bsz, s, d = x.shape
    f = wg.shape[2]
    tm = min(1024, s)
    row = pl.BlockSpec((None, tm, d), lambda b, i, j: (b, i, 0))
    vec = pl.BlockSpec((1, d), lambda b, i, j: (0, 0))
    modv = lambda k: pl.BlockSpec((None, 1, d), lambda b, i, j: (b, 0, k))
    return pl.pallas_call(
        _ffn_kernel,
        grid=(bsz, s // tm, f // FFN_TF),
        in_specs=[
            row, vec, modv(4), modv(3),
            pl.BlockSpec((None, d, FFN_TF), lambda b, i, j: (layer, 0, j)),
            pl.BlockSpec((None, d, FFN_TF), lambda b, i, j: (layer, 0, j)),
            pl.BlockSpec((None, FFN_TF, d), lambda b, i, j: (layer, j, 0)),
            vec, modv(5),
        ],
        out_specs=row,
        out_shape=jax.ShapeDtypeStruct((bsz, s, d), F32),
        scratch_shapes=[pltpu.VMEM((tm, d), BF16)],
        compiler_params=_cparams(("parallel", "parallel", "arbitrary"), FFN_VMEM_LIMIT),
        name="ffn",
    )(x, g, modl, modl, wg, wu, wd, pg, modl)


CAST_BLOCK_BYTES = 4 * 1024 * 1024


def _cast_kernel(w_ref, o_ref):
    o_ref[...] = w_ref[...].astype(BF16)


def _to_bf16(w):
    depth, rows, cols = w.shape
    rb = rows
    while rb * cols * 4 > CAST_BLOCK_BYTES and rb % 16 == 0:
        rb //= 2
    spec = pl.BlockSpec((None, rb, cols), lambda l, i: (l, i, 0))
    return pl.pallas_call(
        _cast_kernel,
        grid=(depth, rows // rb),
        in_specs=[spec],
        out_specs=spec,
        out_shape=jax.ShapeDtypeStruct(w.shape, BF16),
        compiler_params=_cparams(("parallel", "parallel")),
        name="cast_bf16",
    )(w)


def _pair_heads_a(w, axis):
    shape = w.shape
    w = jnp.moveaxis(w, axis, -1)
    lead = w.shape[:-1]
    w = w.reshape(lead + (2, 2, GROUP_A, DH_A))
    w = jnp.swapaxes(w, -3, -2)
    w = w.reshape(lead + (W_A,))
    return jnp.moveaxis(w, -1, axis).reshape(shape)


def _pad_heads(w, n_heads, width, padded, axis):
    w = jnp.moveaxis(w, axis, -1)
    lead = w.shape[:-1]
    w = w.reshape(lead + (n_heads, width))
    w = jnp.pad(w, [(0, 0)] * len(lead) + [(0, 0), (0, padded - width)])
    w = w.reshape(lead + (n_heads * padded,))
    return jnp.moveaxis(w, -1, axis)


def _layout_w_in(w):
    d = w.shape[0]
    sizes = (W_A, N_KV_A * DH_A, N_KV_A * DH_A, Q_LORA, KV_LORA, D_ROPE,
             N_H_C * DQK_C, N_H_C * DQK_C, W_C, 4 * N_H_C, W_C)
    pts = np.cumsum(sizes)[:-1].tolist()
    aq, ak, av, bcq, bckv, bkr, cq, ck, cv, cg, co = jnp.split(w, pts, axis=1)
    parts = [
        _pair_heads_a(aq, 1), ak, av, bckv,
        jnp.pad(cg, ((0, 0), (0, LANES - 4 * N_H_C))),
        bcq, bkr,
        _pad_heads(cq, N_H_C, DQK_C, DQK_PAD, 1), _pad_heads(ck, N_H_C, DQK_C, DQK_PAD, 1),
        _pad_heads(cv, N_H_C, DV_C, DV_PAD, 1), _pad_heads(co, N_H_C, DV_C, DV_PAD, 1),
    ]
    out = jnp.concatenate(parts, axis=1).astype(BF16)
    assert out.shape == (d, IN_WIDTH_PAD)
    return out


def _layout_w_uq(w):
    w = w.reshape(Q_LORA, N_H_B, D_NOPE + D_ROPE)
    nope = w[:, :, :D_NOPE].reshape(Q_LORA, N_H_B * D_NOPE)
    rope = w[:, :, D_NOPE:].reshape(Q_LORA, N_H_B * D_ROPE)
    w = jnp.concatenate([nope, rope], axis=1)
    return jnp.pad(w, ((0, 512 - Q_LORA), (0, 0))).astype(BF16)


def _layout_w_ukv(w):
    w = w.reshape(KV_LORA, N_H_B, D_NOPE + D_V_B)
    kn = w[:, :, :D_NOPE].reshape(KV_LORA, N_H_B * D_NOPE)
    v = w[:, :, D_NOPE:].reshape(KV_LORA, N_H_B * D_V_B)
    return kn.astype(BF16), v.T.astype(BF16)


def kernel(x, c, positions, mod_w, mod_b, pre_mix_g, post_mix_g, pre_ffn_g, post_ffn_g, w_in, attn_sink, mla_q_norm_g, mla_w_uq, mla_kv_norm_g, mla_w_ukv, mlstm_gate_b, mlstm_head_g, w_out, ffn_w_gate, ffn_w_up, ffn_w_down):
    depth = mod_w.shape[0]
    bsz, s, d = x.shape
    mod = _modulation(c, mod_w, mod_b)
    pos3 = positions.reshape(bsz, 1, s)
    inv = 1.0 / (ROPE_THETA ** (jnp.arange(0, D_ROPE, 2, dtype=F32) / D_ROPE))
    inv_col = inv.reshape(D_ROPE // 2, 1)
    wg_bf, wu_bf, wd_bf = _to_bf16(ffn_w_gate), _to_bf16(ffn_w_up), _to_bf16(ffn_w_down)
    for l in range(depth):
        modl = mod[l].reshape(bsz, 1, 6 * d)
        proj, gates = _in_proj(x, pre_mix_g[l].reshape(1, d), modl, _layout_w_in(w_in[l]))
        ya = _window_attention(proj, attn_sink[l])
        qg = jnp.pad(mla_q_norm_g[l], (0, 512 - Q_LORA)).reshape(1, 512)
        wk, wvt = _layout_w_ukv(mla_w_ukv[l])
        qb, kb, vtb = _mla_pre(proj, pos3, qg, _layout_w_uq(mla_w_uq[l]),
                               mla_kv_norm_g[l].reshape(1, KV_LORA), wk, wvt, inv_col)
        yb = _mla_attention(qb, kb, vtb)
        gate_b = jnp.pad(mlstm_gate_b[l], (0, LANES - 4 * N_H_C)).reshape(1, LANES)
        head_g = jnp.pad(mlstm_head_g[l].reshape(N_H_C, 1, DV_C), ((0, 0), (0, 0), (0, DV_PAD - DV_C)))
        yc = _mlstm(proj, gates, gate_b, head_g)
        wo = w_out[l]
        wa = _pair_heads_a(wo[:W_A], 0).astype(BF16)
        wb = wo[W_A:W_A + W_B].astype(BF16)
        wc = _pad_heads(wo[W_A + W_B:], N_H_C, DV_C, DV_PAD, 0).astype(BF16)
        x = _out_proj(ya, yb, yc, wa, wb, wc, post_mix_g[l].reshape(1, d), modl, x)
        x = _ffn(x, pre_ffn_g[l].reshape(1, d), modl, wg_bf, wu_bf, wd_bf, post_ffn_g[l].reshape(1, d), l)
    return x
```

```python
import functools
import math

import numpy as np
import jax
import jax.numpy as jnp
from jax import lax
from jax.experimental import pallas as pl
from jax.experimental.pallas import tpu as pltpu

F32 = jnp.float32
BF16 = jnp.bfloat16

D_MODEL = 2048
EPS = 1e-6
N_Q_A, N_KV_A, DH_A, GROUP_A = 12, 4, 64, 3
WINDOW = 128
BLOCK_A = 128
N_H_B, Q_LORA, KV_LORA, D_NOPE, D_ROPE, D_V_B = 4, 448, 128, 128, 64, 128
ROPE_THETA = 10000.0
N_H_C, DQK_C, DV_C = 4, 96, 192
W_A, W_B, W_C = N_Q_A * DH_A, N_H_B * D_V_B, N_H_C * DV_C
D_FF = 5632

LANES = 128
DQK_PAD = 128
DV_PAD = 256
ONES_COL = DV_C
VMEM_LIMIT = 56 * 1024 * 1024

OFF_AQ, OFF_AK, OFF_AV = 0, 768, 1024
OFF_CKV, OFF_CG, OFF_CQKR = 1280, 1408, 1536
OFF_MQ, OFF_MK, OFF_MV, OFF_MO = 2048, 2560, 3072, 4096
IN_WIDTH_PAD = 5120
W_C_PAD = N_H_C * DV_PAD
MIX_PAD = W_A + W_B + W_C_PAD

CHUNK = 128


def _alibi_slopes(n):
    def pow2(m):
        start = 2.0 ** (-8.0 / m)
        return [start ** (i + 1) for i in range(m)]
    if math.log2(n).is_integer():
        s = pow2(n)
    else:
        p = 2 ** math.floor(math.log2(n))
        s = pow2(p) + pow2(2 * p)[0::2][: n - p]
    return [float(np.float32(v)) for v in s]


SLOPES_A = _alibi_slopes(N_Q_A)


def _cparams(sem, vmem_limit=VMEM_LIMIT):
    return pltpu.CompilerParams(dimension_semantics=sem, vmem_limit_bytes=vmem_limit)


def _dot(a, b):
    return jnp.dot(a, b, preferred_element_type=F32)


def _dot_nt(a, b):
    return lax.dot_general(a, b, (((1,), (1,)), ((), ())), preferred_element_type=F32)


def _dot_tn(a, b):
    return lax.dot_general(a, b, (((0,), (0,)), ((), ())), preferred_element_type=F32)


def _mod_kernel(c_ref, w_ref, b_ref, o_ref):
    c = c_ref[...]
    s = c * jax.nn.sigmoid(c)
    o_ref[...] = _dot(s.astype(BF16), w_ref[...].astype(BF16)) + b_ref[...]


def _modulation(c, mod_w, mod_b):
    depth, d, n = mod_w.shape
    bsz = c.shape[0]
    tn = 1024
    return pl.pallas_call(
        _mod_kernel,
        grid=(depth, n // tn),
        in_specs=[
            pl.BlockSpec((bsz, d), lambda l, j: (0, 0)),
            pl.BlockSpec((None, d, tn), lambda l, j: (l, 0, j)),
            pl.BlockSpec((None, 1, tn), lambda l, j: (l, 0, j)),
        ],
        out_specs=pl.BlockSpec((None, bsz, tn), lambda l, j: (l, 0, j)),
        out_shape=jax.ShapeDtypeStruct((depth, bsz, n), F32),
        compiler_params=_cparams(("parallel", "parallel")),
        name="adaln_mod",
    )(c, mod_w, mod_b.reshape(depth, 1, n))


ROW_SLAB = 16
SUB_ROWS = 256


def _row_slabs(first_row, n_rows):
    return [slice(r, r + ROW_SLAB) for r in range(first_row, first_row + n_rows, ROW_SLAB)]


def _mod_norm_rows(x_ref, g_ref, sc_ref, sh_ref, h_ref, first_row, n_rows):
    gain = g_ref[...] * (1.0 + sc_ref[...])
    shift = sh_ref[...]
    for rows in _row_slabs(first_row, n_rows):
        x = x_ref[rows, :]
        ms = jnp.mean(x * x, axis=-1, keepdims=True)
        h_ref[rows, :] = (x * lax.rsqrt(ms + EPS) * gain + shift).astype(BF16)


def _norm_residual_rows(y_ref, x_ref, pg_ref, gate_ref, o_ref, first_row, n_rows):
    gain = gate_ref[...] * pg_ref[...]
    for rows in _row_slabs(first_row, n_rows):
        y = y_ref[rows, :]
        ms = jnp.mean(y * y, axis=-1, keepdims=True)
        o_ref[rows, :] = x_ref[rows, :] + y * lax.rsqrt(ms + EPS) * gain


IN_TN = 1024
GATE_TILE = OFF_CG // IN_TN
GATE_OFF = OFF_CG % IN_TN


def _in_kernel(x_ref, g_ref, sc_ref, sh_ref, w_ref, cs_ref, o_ref, gate_ref, h_scr):
    j = pl.program_id(2)
    tm = x_ref.shape[0]

    def project(rows):
        acc = _dot(h_scr[rows, :], w_ref[j])
        o_ref[rows, :] = (acc * cs_ref[...]).astype(BF16)
        return acc

    @pl.when(j == 0)
    def _():
        for r in range(0, tm, SUB_ROWS):
            _mod_norm_rows(x_ref, g_ref, sc_ref, sh_ref, h_scr, r, SUB_ROWS)
            project(slice(r, r + SUB_ROWS))

    @pl.when(j != 0)
    def _():
        acc = project(slice(0, tm))

        @pl.when(j == GATE_TILE)
        def _():
            gate_ref[...] = acc[:, GATE_OFF:GATE_OFF + LANES]


def _in_proj(x, g, modl, w):
    bsz, s, d = x.shape
    n = w.shape[0] * IN_TN
    tm = min(1024, s)
    col_scale = jnp.where(jnp.arange(n) < W_A, WIN_QSCALE, 1.0).astype(F32).reshape(1, n)
    return pl.pallas_call(
        _in_kernel,
        grid=(bsz, s // tm, n // IN_TN),
        in_specs=[
            pl.BlockSpec((None, tm, d), lambda b, i, j: (b, i, 0)),
            pl.BlockSpec((1, d), lambda b, i, j: (0, 0)),
            pl.BlockSpec((None, 1, d), lambda b, i, j: (b, 0, 1)),
            pl.BlockSpec((None, 1, d), lambda b, i, j: (b, 0, 0)),
            pl.BlockSpec((n // IN_TN, d, IN_TN), lambda b, i, j: (0, 0, 0), pipeline_mode=pl.Buffered(1)),
            pl.BlockSpec((1, IN_TN), lambda b, i, j: (0, j)),
        ],
        out_specs=[
            pl.BlockSpec((None, tm, IN_TN), lambda b, i, j: (b, i, j)),
            pl.BlockSpec((None, tm, LANES), lambda b, i, j: (b, i, 0)),
        ],
        out_shape=[
            jax.ShapeDtypeStruct((bsz, s, n), BF16),
            jax.ShapeDtypeStruct((bsz, s, LANES), F32),
        ],
        scratch_shapes=[pltpu.VMEM((tm, d), BF16)],
        compiler_params=_cparams(("parallel", "parallel", "arbitrary")),
        name="in_proj",
    )(x, g, modl, modl, w, col_scale)


LOG2E = math.log2(math.e)
WIN_QSCALE = DH_A ** -0.5 * LOG2E


WIN_QB = 4


def _win_kernel(sink_ref, q_ref, kp_ref, kc_ref, kn_ref, vp_ref, vc_ref, vn_ref, o_ref, *, nb):
    n = pl.program_id(1)
    blk = BLOCK_A
    nk = 3 * blk
    qi = lax.broadcasted_iota(jnp.int32, (blk, nk), 0)
    kj = lax.broadcasted_iota(jnp.int32, (blk, nk), 1)
    dist = jnp.abs(qi - kj + blk)
    k_all = jnp.concatenate([kp_ref[...], kc_ref[...], kn_ref[...]], axis=0)
    v_all = jnp.concatenate([vp_ref[...], vc_ref[...], vn_ref[...]], axis=0)
    low_k = lax.broadcasted_iota(jnp.int32, (nk, LANES), 1) < DH_A
    low_q = lax.broadcasted_iota(jnp.int32, (blk, LANES), 1) < DH_A
    zero = jnp.zeros((nk, LANES), BF16)
    for sub in range(WIN_QB):
        g = n * WIN_QB + sub
        rows = slice(sub * blk, (sub + 1) * blk)
        in_seq = ((kj >= blk) | (g > 0)) & ((kj < 2 * blk) | (g < nb - 1))
        dist_masked = jnp.where((dist <= WINDOW) & in_seq, dist.astype(F32), jnp.inf)
        for p in range(2):
            kt = k_all[sub * blk:sub * blk + nk, p * LANES:(p + 1) * LANES]
            vt = v_all[sub * blk:sub * blk + nk, p * LANES:(p + 1) * LANES]
            k_big = jnp.concatenate([jnp.where(low_k, kt, zero), jnp.where(low_k, zero, kt)], axis=0)
            v_big = jnp.concatenate([jnp.where(low_k, vt, zero), jnp.where(low_k, zero, vt)], axis=0)
            q3 = jnp.concatenate([q_ref[rows, (p * GROUP_A + a) * LANES:(p * GROUP_A + a + 1) * LANES]
                                  for a in range(GROUP_A)], axis=0)
            s3 = _dot_nt(q3, k_big)
            p_rows, inv_rows = [], []
            for a in range(GROUP_A):
                p_halves, inv_halves = [], []
                for half in range(2):
                    head = (2 * p + half) * GROUP_A + a
                    s = (s3[a * blk:(a + 1) * blk, half * nk:(half + 1) * nk]
                         - (SLOPES_A[head] * LOG2E) * dist_masked)
                    sink = sink_ref[head] * LOG2E
                    m = jnp.maximum(jnp.max(s, axis=-1, keepdims=True), sink)
                    e = jnp.exp2(s - m)
                    l = jnp.sum(e, axis=-1, keepdims=True) + jnp.exp2(sink - m)
                    p_halves.append(e.astype(BF16))
                    inv_halves.append(1.0 / l)
                p_rows.append(jnp.concatenate(p_halves, axis=1))
                inv_rows.append(jnp.where(low_q, inv_halves[0], inv_halves[1]))
            o3 = _dot(jnp.concatenate(p_rows, axis=0), v_big)
            for a in range(GROUP_A):
                t = p * GROUP_A + a
                o_ref[rows, t * LANES:(t + 1) * LANES] = (o3[a * blk:(a + 1) * blk] * inv_rows[a]).astype(BF16)


def _window_attention(proj, sinks):
    bsz, s, _ = proj.shape
    blk = BLOCK_A
    nb = s // blk
    assert nb % WIN_QB == 0
    tq = WIN_QB * blk
    kb, vb = OFF_AK // 256, OFF_AV // 256
    prev = lambda n: jnp.maximum(n * WIN_QB - 1, 0)
    nxt = lambda n: jnp.minimum((n + 1) * WIN_QB, nb - 1)
    return pl.pallas_call(
        functools.partial(_win_kernel, nb=nb),
        grid=(bsz, nb // WIN_QB),
        in_specs=[
            pl.BlockSpec(memory_space=pltpu.SMEM),
            pl.BlockSpec((None, tq, W_A), lambda b, n: (b, n, 0)),
            pl.BlockSpec((None, blk, 256), lambda b, n: (b, prev(n), kb)),
            pl.BlockSpec((None, tq, 256), lambda b, n: (b, n, kb)),
            pl.BlockSpec((None, blk, 256), lambda b, n: (b, nxt(n), kb)),
            pl.BlockSpec((None, blk, 256), lambda b, n: (b, prev(n), vb)),
            pl.BlockSpec((None, tq, 256), lambda b, n: (b, n, vb)),
            pl.BlockSpec((None, blk, 256), lambda b, n: (b, nxt(n), vb)),
        ],
        out_specs=pl.BlockSpec((None, tq, W_A), lambda b, n: (b, n, 0)),
        out_shape=jax.ShapeDtypeStruct((bsz, s, W_A), BF16),
        compiler_params=_cparams(("parallel", "parallel")),
        name="win_attn",
    )(sinks, proj, proj, proj, proj, proj, proj, proj)


MLA_TK = 512
MLA_QSCALE = (D_NOPE + D_ROPE) ** -0.5 * math.log2(math.e)


def _mla_pre_kernel(cqkr_ref, ckv_ref, pos_ref, qg_ref, wq_ref, kvg_ref, wk_ref, wvt_ref, inv_ref,
                    q_ref, k_ref, vt_ref):
    tm = cqkr_ref.shape[0]
    t = cqkr_ref[...].astype(F32)
    lane = lax.broadcasted_iota(jnp.int32, (tm, 4 * LANES), 1)
    cq = jnp.where(lane < Q_LORA, t, 0.0)
    ms = jnp.sum(cq * cq, axis=-1, keepdims=True) * (1.0 / Q_LORA)
    qn = cq * lax.rsqrt(ms + EPS) * qg_ref[...]
    q = _dot(qn.astype(BF16), wq_ref[...]) * MLA_QSCALE
    c = ckv_ref[...].astype(F32)
    ms = jnp.mean(c * c, axis=-1, keepdims=True)
    kvn = (c * lax.rsqrt(ms + EPS) * kvg_ref[...]).astype(BF16)
    kn = _dot(kvn, wk_ref[...])
    vt_ref[...] = _dot_nt(wvt_ref[...], kvn).astype(BF16)

    ang_t = inv_ref[...] * pos_ref[...].astype(F32)
    cs_t, sn_t = jnp.cos(ang_t), jnp.sin(ang_t)
    cs = jnp.concatenate([cs_t] * 4, axis=0).T
    sgn_sn = jnp.concatenate([-sn_t, sn_t] * 2, axis=0).T
    l128 = lax.broadcasted_iota(jnp.int32, (tm, LANES), 1)
    first = (l128 % D_ROPE) < (D_ROPE // 2)

    def rope(x):
        swapped = jnp.where(first, pltpu.roll(x, LANES - D_ROPE // 2, 1), pltpu.roll(x, D_ROPE // 2, 1))
        return x * cs + swapped * sgn_sn

    kr = rope(t[:, 3 * LANES:4 * LANES])
    kr_hi = jnp.where(l128 >= D_ROPE, kr, 0.0)
    kr_lo = pltpu.roll(kr_hi, D_ROPE, 1)
    kr_tiles = (kr_lo.astype(BF16), kr_hi.astype(BF16))
    for pair in range(2):
        qr = rope(q[:, (4 + pair) * LANES:(5 + pair) * LANES]).astype(BF16)
        for half in range(2):
            h = 2 * pair + half
            q_ref[:, (2 * h) * LANES:(2 * h + 1) * LANES] = q[:, h * LANES:(h + 1) * LANES].astype(BF16)
            q_ref[:, (2 * h + 1) * LANES:(2 * h + 2) * LANES] = qr
            k_ref[:, (2 * h) * LANES:(2 * h + 1) * LANES] = kn[:, h * LANES:(h + 1) * LANES].astype(BF16)
            k_ref[:, (2 * h + 1) * LANES:(2 * h + 2) * LANES] = kr_tiles[half]


def _mla_pre(proj, pos3, qg, wq, kvg, wk, wvt, inv):
    bsz, s, _ = proj.shape
    tm = MLA_TK
    const = lambda shape: pl.BlockSpec(shape, lambda b, i: (0,) * len(shape))
    return pl.pallas_call(
        _mla_pre_kernel,
        grid=(bsz, s // tm),
        in_specs=[
            pl.BlockSpec((None, tm, 512), lambda b, i: (b, i, OFF_CQKR // 512)),
            pl.BlockSpec((None, tm, LANES), lambda b, i: (b, i, OFF_CKV // LANES)),
            pl.BlockSpec((None, 1, tm), lambda b, i: (b, 0, i)),
            const((1, 512)), const((512, 768)), const((1, LANES)), const((LANES, W_B)), const((W_B, LANES)),
            const((D_ROPE // 2, 1)),
        ],
        out_specs=[
            pl.BlockSpec((None, tm, 1024), lambda b, i: (b, i, 0)),
            pl.BlockSpec((None, tm, 1024), lambda b, i: (b, i, 0)),
            pl.BlockSpec((None, None, W_B, tm), lambda b, i: (b, i, 0, 0)),
        ],
        out_shape=[
            jax.ShapeDtypeStruct((bsz, s, 1024), BF16),
            jax.ShapeDtypeStruct((bsz, s, 1024), BF16),
            jax.ShapeDtypeStruct((bsz, s // tm, W_B, tm), BF16),
        ],
        compiler_params=_cparams(("parallel", "parallel")),
        name="mla_pre",
    )(proj, proj, pos3, qg, wq, kvg, wk, wvt, inv)


def _mla_attn_kernel(q_ref, k_ref, vt_ref, o_ref, st_a, st_b, *, tq):
    n_chunks, _, tk = vt_ref.shape
    nt = q_ref.shape[0] // tq

    def scores(t, st_scr):
        q = q_ref[pl.ds(pl.multiple_of(t * tq, tq), tq), :]
        m8 = jnp.full((8, tq), -jnp.inf, F32)
        for c in range(n_chunks):
            st = _dot_nt(k_ref[c * tk:(c + 1) * tk, :], q)
            st_scr[c * tk:(c + 1) * tk, :] = st
            m8 = jnp.maximum(m8, jnp.max(st.reshape(tk // 8, 8, tq), axis=0))
        return jnp.max(m8, axis=0, keepdims=True)

    def output(t, st_scr, m):
        l8 = jnp.zeros((8, tq), F32)
        acc = jnp.zeros((D_V_B, tq), F32)
        for c in range(n_chunks):
            p = jnp.exp2(st_scr[c * tk:(c + 1) * tk, :] - m)
            l8 = l8 + jnp.sum(p.reshape(tk // 8, 8, tq), axis=0)
            acc = acc + _dot(vt_ref[c], p.astype(BF16))
        l = jnp.sum(l8, axis=0, keepdims=True)
        o_ref[pl.ds(pl.multiple_of(t * tq, tq), tq), :] = (acc / l).T.astype(BF16)

    m_first = scores(0, st_a)
    if nt == 1:
        output(0, st_a, m_first)
        return

    def pair(u, m_a):
        t = 2 * u
        m_b = scores(t + 1, st_b)
        output(t, st_a, m_a)
        m_a = scores(t + 2, st_a)
        output(t + 1, st_b, m_b)
        return m_a

    m_a = lax.fori_loop(0, nt // 2 - 1, pair, m_first)
    m_b = scores(nt - 1, st_b)
    output(nt - 2, st_a, m_a)
    output(nt - 1, st_b, m_b)


def _mla_attention(q, k, vt):
    bsz, s, _ = q.shape
    n_chunks, tk = vt.shape[1], vt.shape[3]
    tq = min(512, s)
    assert s // tq == 1 or (s // tq) % 2 == 0
    return pl.pallas_call(
        functools.partial(_mla_attn_kernel, tq=tq),
        grid=(bsz, N_H_B),
        in_specs=[
            pl.BlockSpec((None, s, 256), lambda b, h: (b, 0, h)),
            pl.BlockSpec((None, s, 256), lambda b, h: (b, 0, h)),
            pl.BlockSpec((None, n_chunks, D_V_B, tk), lambda b, h: (b, 0, h, 0)),
        ],
        out_specs=pl.BlockSpec((None, s, D_V_B), lambda b, h: (b, 0, h)),
        out_shape=jax.ShapeDtypeStruct((bsz, s, W_B), BF16),
        scratch_shapes=[pltpu.VMEM((s, tq), F32), pltpu.VMEM((s, tq), F32)],
        compiler_params=_cparams(("parallel", "parallel")),
        name="mla_attn",
    )(q, k, vt)


def _log_sigmoid(x):
    return jnp.minimum(x, 0.0) - jnp.log1p(jnp.exp(-jnp.abs(x)))


def _split3(x):
    hi = x.astype(BF16).astype(F32)
    r1 = x - hi
    mid = r1.astype(BF16).astype(F32)
    return jnp.concatenate([hi, mid, r1 - mid], axis=-1)


def _mlstm_kernel(q_ref, k_ref, v_ref, o_ref, gate_ref, gb_ref, hg_ref, y_ref,
                  hf_scr, hb_scr, tile_scr, kw_scr, rowq_scr, bp_scr, cf_scr, cb_scr, gt_scr):
    s_len = q_ref.shape[0]
    L = CHUNK
    nc = s_len // L
    head = pl.program_id(1)
    ii = lax.broadcasted_iota(jnp.int32, (L, L), 0)
    jj = lax.broadcasted_iota(jnp.int32, (L, L), 1)
    ones = jnp.ones((L, L), BF16)
    ones_lane = lax.broadcasted_iota(jnp.int32, (L, DV_PAD), 1) == ONES_COL
    eye3 = jnp.concatenate([(ii == jj).astype(BF16)] * 3, axis=1)
    tri_ones = [jnp.concatenate([jnp.concatenate([m.astype(BF16), ones], axis=1)] * 3, axis=0)
                for m in (ii <= jj, ii >= jj)]
    masks = (jj <= ii, jj >= ii)
    h_scrs = (hf_scr, hb_scr)
    c_scrs = (cf_scr, cb_scr)

    def load_v_ext(r0):
        return jnp.where(ones_lane, jnp.ones((L, DV_PAD), BF16), v_ref[pl.ds(r0, L), :])

    def transpose_gates(c, _):
        g = gate_ref[pl.ds(pl.multiple_of(c * L, L), L), :] + gb_ref[...]
        gt_scr[pl.ds(pl.multiple_of(c * 16, 16), 16), :] = g.T[0:16, :]
        return 0

    lax.fori_loop(0, nc, transpose_gates, 0, unroll=4)

    for d in range(2):
        li = gt_scr[pl.ds(d * N_H_C + head, nc, stride=16), :]
        lf = _log_sigmoid(gt_scr[pl.ds((2 + d) * N_H_C + head, nc, stride=16), :])
        bt = _dot(_split3(lf).astype(BF16), tri_ones[d])
        b, tot = bt[:, :L], bt[:, L:]
        r = li - b
        m_loc = tot + jnp.broadcast_to(jnp.max(r, axis=-1, keepdims=True), r.shape)
        rowq_scr[d, 0] = r
        rowq_scr[d, 1] = jnp.exp(tot + r - m_loc)
        rowq_scr[d, 2] = tot
        rowq_scr[d, 3] = m_loc
        bp_scr[d] = _split3(b + math.log(DQK_C ** -0.5))

    def prep(c, _):
        r0 = pl.multiple_of(c * L, L)
        q, k = q_ref[pl.ds(r0, L), :], k_ref[pl.ds(r0, L), :]
        s = _dot_nt(q, k)
        kt = k.astype(F32).T
        p = []
        for d in range(2):
            u = jnp.where(masks[d], rowq_scr[d, 0, pl.ds(c, 1), :], -jnp.inf)
            ct = jnp.broadcast_to(jnp.max(u, axis=-1, keepdims=True), (L, L))
            p.append((s * jnp.exp(u - ct)).astype(BF16))
            tile_scr[pl.ds(r0, L), (2 + d) * L:(3 + d) * L] = ct
            kw_scr[d, c] = (kt * rowq_scr[d, 1, pl.ds(c, 1), :]).astype(BF16)
        rhs = jnp.concatenate([jnp.broadcast_to(bp_scr[d, pl.ds(c, 1), :], (L, 3 * L)) for d in range(2)], axis=0)
        tile_scr[pl.ds(r0, L), 0:2 * L] = _dot_nt(eye3, rhs.astype(BF16))
        intra = _dot(jnp.concatenate(p, axis=0), load_v_ext(r0))
        hf_scr[pl.ds(r0, L), :] = intra[:L]
        hb_scr[pl.ds(r0, L), :] = intra[L:]
        return 0

    lax.fori_loop(0, nc, prep, 0, unroll=4)

    cf_scr[...] = jnp.zeros_like(cf_scr)
    cb_scr[...] = jnp.zeros_like(cb_scr)

    def one_dir(c, m_state, d):
        h_scr, c_scr = h_scrs[d], c_scrs[d]
        r0 = pl.multiple_of(c * L, L)
        tot, m_loc = rowq_scr[d, 2, pl.ds(c, 1), :], rowq_scr[d, 3, pl.ds(c, 1), :]
        bt = tile_scr[pl.ds(r0, L), d * L:(d + 1) * L]
        ct = tile_scr[pl.ds(r0, L), (2 + d) * L:(3 + d) * L]
        v_ext = load_v_ext(r0)
        c_state = c_scr[...]
        qc = _dot(q_ref[pl.ds(r0, L), :], c_state.astype(BF16))
        mx = jnp.maximum(m_state, ct)
        iw = jnp.exp(m_state - mx)
        xw = jnp.exp(ct - mx)
        num = [iw * qc[:, j * L:(j + 1) * L] + xw * h_scr[pl.ds(r0, L), j * L:(j + 1) * L] for j in range(2)]
        den = jnp.broadcast_to(num[1][:, ONES_COL - L:ONES_COL - L + 1], (L, L))
        inv = 1.0 / jnp.maximum(jnp.abs(den), jnp.exp(-(bt + mx)))
        h_scr[pl.ds(r0, L), 0:L] = num[0] * inv
        h_scr[pl.ds(r0, L), L:2 * L] = num[1] * inv
        c_loc = _dot(kw_scr[d, c], v_ext)
        m_new = jnp.maximum(tot + m_state, m_loc)
        sp, sl = jnp.exp(tot + m_state - m_new), jnp.exp(m_loc - m_new)
        c_scr[...] = jnp.concatenate(
            [sp * c_state[:, j * L:(j + 1) * L] + sl * c_loc[:, j * L:(j + 1) * L] for j in range(2)], axis=1)
        return m_new

    def scan(c, carry):
        return one_dir(c, carry[0], 0), one_dir(nc - 1 - c, carry[1], 1)

    neg = jnp.full((1, LANES), -jnp.inf, F32)
    lax.fori_loop(0, nc, scan, (neg, neg), unroll=4)

    real = lax.broadcasted_iota(jnp.int32, (L, DV_PAD), 1) < DV_C

    def epilogue(c, _):
        r0 = pl.multiple_of(c * L, L)
        h = jnp.where(real, hf_scr[pl.ds(r0, L), :] + hb_scr[pl.ds(r0, L), :], 0.0)
        ms = jnp.sum(h * h, axis=-1, keepdims=True) * (1.0 / DV_C)
        hn = h * lax.rsqrt(ms + EPS) * hg_ref[...]
        y = jax.nn.sigmoid(o_ref[pl.ds(r0, L), :].astype(F32)) * hn
        y_ref[pl.ds(r0, L), :] = y.astype(BF16)
        return 0

    lax.fori_loop(0, nc, epilogue, 0, unroll=2)


def _mlstm(proj, gates, gate_b, head_g):
    bsz, s, _ = proj.shape
    nc = s // CHUNK
    return pl.pallas_call(
        _mlstm_kernel,
        grid=(bsz, N_H_C),
        in_specs=[
            pl.BlockSpec((None, s, DQK_PAD), lambda b, h: (b, 0, OFF_MQ // DQK_PAD + h)),
            pl.BlockSpec((None, s, DQK_PAD), lambda b, h: (b, 0, OFF_MK // DQK_PAD + h)),
            pl.BlockSpec((None, s, DV_PAD), lambda b, h: (b, 0, OFF_MV // DV_PAD + h)),
            pl.BlockSpec((None, s, DV_PAD), lambda b, h: (b, 0, OFF_MO // DV_PAD + h)),
            pl.BlockSpec((None, s, LANES), lambda b, h: (b, 0, 0)),
            pl.BlockSpec((1, LANES), lambda b, h: (0, 0)),
            pl.BlockSpec((None, 1, DV_PAD), lambda b, h: (h, 0, 0)),
        ],
        out_specs=pl.BlockSpec((None, s, DV_PAD), lambda b, h: (b, 0, h)),
        out_shape=jax.ShapeDtypeStruct((bsz, s, W_C_PAD), BF16),
        scratch_shapes=[
            pltpu.VMEM((s, DV_PAD), F32), pltpu.VMEM((s, DV_PAD), F32),
            pltpu.VMEM((s, 4 * CHUNK), F32),
            pltpu.VMEM((2, nc, DQK_PAD, CHUNK), BF16),
            pltpu.VMEM((2, 4, nc, CHUNK), F32),
            pltpu.VMEM((2, nc, 3 * CHUNK), F32),
            pltpu.VMEM((DQK_PAD, DV_PAD), F32), pltpu.VMEM((DQK_PAD, DV_PAD), F32),
            pltpu.VMEM((nc * 16, CHUNK), F32),
        ],
        compiler_params=_cparams(("parallel", "arbitrary")),
        name="mlstm",
    )(proj, proj, proj, proj, gates, gate_b, head_g)


def _out_kernel(ya_ref, yb_ref, yc_ref, wa_ref, wb_ref, wc_ref, pg_ref, gate_ref, x_ref, o_ref, y_scr):
    for r in range(0, o_ref.shape[0], SUB_ROWS):
        rows = slice(r, r + SUB_ROWS)
        y_scr[rows, :] = (_dot(ya_ref[rows, :], wa_ref[...]) + _dot(yb_ref[rows, :], wb_ref[...])
                          + _dot(yc_ref[rows, :], wc_ref[...]))
        _norm_residual_rows(y_scr, x_ref, pg_ref, gate_ref, o_ref, r, SUB_ROWS)


def _out_proj(ya, yb, yc, wa, wb, wc, pg, modl, x):
    bsz, s, d = x.shape
    tm = min(2 * SUB_ROWS, s)
    row = lambda w: pl.BlockSpec((None, tm, w), lambda b, i: (b, i, 0))
    const = lambda shape: pl.BlockSpec(shape, lambda b, i: (0,) * len(shape), pipeline_mode=pl.Buffered(1))
    return pl.pallas_call(
        _out_kernel,
        grid=(bsz, s // tm),
        in_specs=[
            row(W_A), row(W_B), row(W_C_PAD),
            const((W_A, d)), const((W_B, d)), const((W_C_PAD, d)), const((1, d)),
            pl.BlockSpec((None, 1, d), lambda b, i: (b, 0, 2)),
            row(d),
        ],
        out_specs=row(d),
        out_shape=jax.ShapeDtypeStruct((bsz, s, d), F32),
        scratch_shapes=[pltpu.VMEM((tm, d), F32)],
        compiler_params=_cparams(("parallel", "parallel")),
        name="out_proj",
    )(ya, yb, yc, wa, wb, wc, pg, modl, x)


FFN_TF = 512
FFN_VMEM_LIMIT = 62 * 1024 * 1024


def _ffn_kernel(x_ref, g_ref, sc_ref, sh_ref, wg_ref, wu_ref, wd_ref, pg_ref, gate_ref, o_ref, h_scr):
    j = pl.program_id(2)
    last = pl.num_programs(2) - 1
    tm = x_ref.shape[0]
    subs = [(r, slice(r, r + SUB_ROWS)) for r in range(0, tm, SUB_ROWS)]

    def partial_down(rows):
        h = h_scr[rows, :]
        a = _dot(h, wg_ref[...])
        u = _dot(h, wu_ref[...])
        hid = (a * jax.nn.sigmoid(a)) * u
        return _dot(hid.astype(BF16), wd_ref[...])

    @pl.when(j == 0)
    def _():
        for r, rows in subs:
            _mod_norm_rows(x_ref, g_ref, sc_ref, sh_ref, h_scr, r, SUB_ROWS)
            o_ref[rows, :] = partial_down(rows)

    @pl.when((j > 0) & (j < last))
    def _():
        o_ref[...] += partial_down(slice(0, tm))

    @pl.when(j == last)
    def _():
        for r, rows in subs:
            o_ref[rows, :] += partial_down(rows)
            _norm_residual_rows(o_ref, x_ref, pg_ref, gate_ref, o_ref, r, SUB_ROWS)


def _ffn(x, g, modl, wg, wu, wd, pg, layer):
    bsz, s, d = x.shape
    f = wg.shape[2]
    tm = min(1024, s)
    row = pl.BlockSpec((None, tm, d), lambda b, i, j: (b, i, 0))
    vec = pl.BlockSpec((1, d), lambda b, i, j: (0, 0))
    modv = lambda k: pl.BlockSpec((None, 1, d), lambda b, i, j: (b, 0, k))
    return pl.pallas_call(
        _ffn_kernel,
        grid=(bsz, s // tm, f // FFN_TF),
        in_specs=[
            row, vec, modv(4), modv(3),
            pl.BlockSpec((None, d, FFN_TF), lambda b, i, j: (layer, 0, j)),
            pl.BlockSpec((None, d, FFN_TF), lambda b, i, j: (layer, 0, j)),
            pl.BlockSpec((None, FFN_TF, d), lambda b, i, j: (layer, j, 0)),
            vec, modv(5),
        ],
        out_specs=row,
        out_shape=jax.ShapeDtypeStruct((bsz, s, d), F32),
        scratch_shapes=[pltpu.VMEM((tm, d), BF16)],
        compiler_params=_cparams(("parallel", "parallel", "arbitrary"), FFN_VMEM_LIMIT),
        name="ffn",
    )(x, g, modl, modl, wg, wu, wd, pg, modl)


CAST_BLOCK_BYTES = 4 * 1024 * 1024


def _cast_kernel(w_ref, o_ref):
    o_ref[...] = w_ref[...].astype(BF16)


def _to_bf16(w):
    depth, rows, cols = w.shape
    rb = rows
    while rb * cols * 4 > CAST_BLOCK_BYTES and rb % 16 == 0:
        rb //= 2
    spec = pl.BlockSpec((None, rb, cols), lambda l, i: (l, i, 0))
    return pl.pallas_call(
        _cast_kernel,
        grid=(depth, rows // rb),
        in_specs=[spec],
        out_specs=spec,
        out_shape=jax.ShapeDtypeStruct(w.shape, BF16),
        compiler_params=_cparams(("parallel", "parallel")),
        name="cast_bf16",
    )(w)


def _pair_heads_a(w, axis):
    shape = w.shape
    w = jnp.moveaxis(w, axis, -1)
    lead = w.shape[:-1]
    w = w.reshape(lead + (2, 2, GROUP_A, DH_A))
    w = jnp.swapaxes(w, -3, -2)
    w = w.reshape(lead + (W_A,))
    return jnp.moveaxis(w, -1, axis).reshape(shape)


def _pad_heads(w, n_heads, width, padded, axis):
    w = jnp.moveaxis(w, axis, -1)
    lead = w.shape[:-1]
    w = w.reshape(lead + (n_heads, width))
    w = jnp.pad(w, [(0, 0)] * len(lead) + [(0, 0), (0, padded - width)])
    w = w.reshape(lead + (n_heads * padded,))
    return jnp.moveaxis(w, -1, axis)


def _layout_w_in(w):
    d = w.shape[0]
    sizes = (W_A, N_KV_A * DH_A, N_KV_A * DH_A, Q_LORA, KV_LORA, D_ROPE,
             N_H_C * DQK_C, N_H_C * DQK_C, W_C, 4 * N_H_C, W_C)
    pts = np.cumsum(sizes)[:-1].tolist()
    aq, ak, av, bcq, bckv, bkr, cq, ck, cv, cg, co = jnp.split(w, pts, axis=1)
    parts = [
        _pair_heads_a(aq, 1), ak, av, bckv,
        jnp.pad(cg, ((0, 0), (0, LANES - 4 * N_H_C))),
        bcq, bkr,
        _pad_heads(cq, N_H_C, DQK_C, DQK_PAD, 1), _pad_heads(ck, N_H_C, DQK_C, DQK_PAD, 1),
        _pad_heads(cv, N_H_C, DV_C, DV_PAD, 1), _pad_heads(co, N_H_C, DV_C, DV_PAD, 1),
    ]
    out = jnp.concatenate(parts, axis=1).astype(BF16)
    assert out.shape == (d, IN_WIDTH_PAD)
    return out.reshape(d, IN_WIDTH_PAD // IN_TN, IN_TN).transpose(1, 0, 2)


def _layout_w_uq(w):
    w = w.reshape(Q_LORA, N_H_B, D_NOPE + D_ROPE)
    nope = w[:, :, :D_NOPE].reshape(Q_LORA, N_H_B * D_NOPE)
    rope = w[:, :, D_NOPE:].reshape(Q_LORA, N_H_B * D_ROPE)
    w = jnp.concatenate([nope, rope], axis=1)
    return jnp.pad(w, ((0, 512 - Q_LORA), (0, 0))).astype(BF16)


def _layout_w_ukv(w):
    w = w.reshape(KV_LORA, N_H_B, D_NOPE + D_V_B)
    kn = w[:, :, :D_NOPE].reshape(KV_LORA, N_H_B * D_NOPE)
    v = w[:, :, D_NOPE:].reshape(KV_LORA, N_H_B * D_V_B)
    return kn.astype(BF16), v.T.astype(BF16)


def kernel(x, c, positions, mod_w, mod_b, pre_mix_g, post_mix_g, pre_ffn_g, post_ffn_g, w_in, attn_sink, mla_q_norm_g, mla_w_uq, mla_kv_norm_g, mla_w_ukv, mlstm_gate_b, mlstm_head_g, w_out, ffn_w_gate, ffn_w_up, ffn_w_down):
    depth = mod_w.shape[0]
    bsz, s, d = x.shape
    mod = _modulation(c, mod_w, mod_b)
    pos3 = positions.reshape(bsz, 1, s)
    inv = 1.0 / (ROPE_THETA ** (jnp.arange(0, D_ROPE, 2, dtype=F32) / D_ROPE))
    inv_col = inv.reshape(D_ROPE // 2, 1)
    wg_bf, wu_bf, wd_bf = _to_bf16(ffn_w_gate), _to_bf16(ffn_w_up), _to_bf16(ffn_w_down)
    for l in range(depth):
        modl = mod[l].reshape(bsz, 1, 6 * d)
        proj, gates = _in_proj(x, pre_mix_g[l].reshape(1, d), modl, _layout_w_in(w_in[l]))
        ya = _window_attention(proj, attn_sink[l])
        qg = jnp.pad(mla_q_norm_g[l], (0, 512 - Q_LORA)).reshape(1, 512)
        wk, wvt = _layout_w_ukv(mla_w_ukv[l])
        qb, kb, vtb = _mla_pre(proj, pos3, qg, _layout_w_uq(mla_w_uq[l]),
                               mla_kv_norm_g[l].reshape(1, KV_LORA), wk, wvt, inv_col)
        yb = _mla_attention(qb, kb, vtb)
        gate_b = jnp.pad(mlstm_gate_b[l], (0, LANES - 4 * N_H_C)).reshape(1, LANES)
        head_g = jnp.pad(mlstm_head_g[l].reshape(N_H_C, 1, DV_C), ((0, 0), (0, 0), (0, DV_PAD - DV_C)))
        yc = _mlstm(proj, gates, gate_b, head_g)
        wo = w_out[l]
        wa = _pair_heads_a(wo[:W_A], 0).astype(BF16)
        wb = wo[W_A:W_A + W_B].astype(BF16)
        wc = _pad_heads(wo[W_A + W_B:], N_H_C, DV_C, DV_PAD, 0).astype(BF16)
        x = _out_proj(ya, yb, yc, wa, wb, wc, post_mix_g[l].reshape(1, d), modl, x)
        x = _ffn(x, pre_ffn_g[l].reshape(1, d), modl, wg_bf, wu_bf, wd_bf, post_ffn_g[l].reshape(1, d), l)
    return x
```

```python
import functools
import math

import numpy as np
import jax
import jax.numpy as jnp
from jax import lax
from jax.experimental import pallas as pl
from jax.experimental.pallas import tpu as pltpu

F32 = jnp.float32
BF16 = jnp.bfloat16

D_MODEL = 2048
EPS = 1e-6
N_Q_A, N_KV_A, DH_A, GROUP_A = 12, 4, 64, 3
WINDOW = 128
BLOCK_A = 128
N_H_B, Q_LORA, KV_LORA, D_NOPE, D_ROPE, D_V_B = 4, 448, 128, 128, 64, 128
ROPE_THETA = 10000.0
N_H_C, DQK_C, DV_C = 4, 96, 192
W_A, W_B, W_C = N_Q_A * DH_A, N_H_B * D_V_B, N_H_C * DV_C
D_FF = 5632

LANES = 128
DQK_PAD = 128
DV_PAD = 256
ONES_COL = DV_C
VMEM_LIMIT = 56 * 1024 * 1024

OFF_AQ, OFF_AK, OFF_AV = 0, 768, 1024
OFF_CKV, OFF_CG, OFF_CQKR = 1280, 1408, 1536
OFF_MQ, OFF_MK, OFF_MV, OFF_MO = 2048, 2560, 3072, 4096
IN_WIDTH_PAD = 5120
W_C_PAD = N_H_C * DV_PAD
MIX_PAD = W_A + W_B + W_C_PAD

CHUNK = 128


def _alibi_slopes(n):
    def pow2(m):
        start = 2.0 ** (-8.0 / m)
        return [start ** (i + 1) for i in range(m)]
    if math.log2(n).is_integer():
        s = pow2(n)
    else:
        p = 2 ** math.floor(math.log2(n))
        s = pow2(p) + pow2(2 * p)[0::2][: n - p]
    return [float(np.float32(v)) for v in s]


SLOPES_A = _alibi_slopes(N_Q_A)


def _cparams(sem, vmem_limit=VMEM_LIMIT):
    return pltpu.CompilerParams(dimension_semantics=sem, vmem_limit_bytes=vmem_limit)


def _dot(a, b):
    return jnp.dot(a, b, preferred_element_type=F32)


def _dot_nt(a, b):
    return lax.dot_general(a, b, (((1,), (1,)), ((), ())), preferred_element_type=F32)


def _dot_tn(a, b):
    return lax.dot_general(a, b, (((0,), (0,)), ((), ())), preferred_element_type=F32)


def _mod_kernel(c_ref, w_ref, b_ref, o_ref):
    c = c_ref[...]
    s = c * jax.nn.sigmoid(c)
    o_ref[...] = _dot(s.astype(BF16), w_ref[...].astype(BF16)) + b_ref[...]


def _modulation(c, mod_w, mod_b):
    depth, d, n = mod_w.shape
    bsz = c.shape[0]
    tn = 1024
    return pl.pallas_call(
        _mod_kernel,
        grid=(depth, n // tn),
        in_specs=[
            pl.BlockSpec((bsz, d), lambda l, j: (0, 0)),
            pl.BlockSpec((None, d, tn), lambda l, j: (l, 0, j)),
            pl.BlockSpec((None, 1, tn), lambda l, j: (l, 0, j)),
        ],
        out_specs=pl.BlockSpec((None, bsz, tn), lambda l, j: (l, 0, j)),
        out_shape=jax.ShapeDtypeStruct((depth, bsz, n), F32),
        compiler_params=_cparams(("parallel", "parallel")),
        name="adaln_mod",
    )(c, mod_w, mod_b.reshape(depth, 1, n))


ROW_SLAB = 16
SUB_ROWS = 256


def _row_slabs(first_row, n_rows):
    return [slice(r, r + ROW_SLAB) for r in range(first_row, first_row + n_rows, ROW_SLAB)]


def _mod_norm_rows(x_ref, g_ref, sc_ref, sh_ref, h_ref, first_row, n_rows):
    gain = g_ref[...] * (1.0 + sc_ref[...])
    shift = sh_ref[...]
    for rows in _row_slabs(first_row, n_rows):
        x = x_ref[rows, :]
        ms = jnp.mean(x * x, axis=-1, keepdims=True)
        h_ref[rows, :] = (x * lax.rsqrt(ms + EPS) * gain + shift).astype(BF16)


def _norm_residual_rows(y_ref, x_ref, pg_ref, gate_ref, o_ref, first_row, n_rows):
    gain = gate_ref[...] * pg_ref[...]
    for rows in _row_slabs(first_row, n_rows):
        y = y_ref[rows, :]
        ms = jnp.mean(y * y, axis=-1, keepdims=True)
        o_ref[rows, :] = x_ref[rows, :] + y * lax.rsqrt(ms + EPS) * gain


IN_TN = 1024
GATE_TILE = OFF_CG // IN_TN
GATE_OFF = OFF_CG % IN_TN


def _in_kernel(x_ref, g_ref, sc_ref, sh_ref, w_ref, cs_ref, o_ref, gate_ref, h_scr):
    j = pl.program_id(2)
    tm = x_ref.shape[0]

    def project(rows):
        acc = _dot(h_scr[rows, :], w_ref[j])
        o_ref[rows, :] = (acc * cs_ref[...]).astype(BF16)
        return acc

    @pl.when(j == 0)
    def _():
        for r in range(0, tm, SUB_ROWS):
            _mod_norm_rows(x_ref, g_ref, sc_ref, sh_ref, h_scr, r, SUB_ROWS)
            project(slice(r, r + SUB_ROWS))

    @pl.when(j != 0)
    def _():
        acc = project(slice(0, tm))

        @pl.when(j == GATE_TILE)
        def _():
            gate_ref[...] = acc[:, GATE_OFF:GATE_OFF + LANES]


def _in_proj(x, g, modl, w):
    bsz, s, d = x.shape
    n = w.shape[0] * IN_TN
    tm = min(1024, s)
    col_scale = jnp.where(jnp.arange(n) < W_A, WIN_QSCALE, 1.0).astype(F32).reshape(1, n)
    return pl.pallas_call(
        _in_kernel,
        grid=(bsz, s // tm, n // IN_TN),
        in_specs=[
            pl.BlockSpec((None, tm, d), lambda b, i, j: (b, i, 0)),
            pl.BlockSpec((1, d), lambda b, i, j: (0, 0)),
            pl.BlockSpec((None, 1, d), lambda b, i, j: (b, 0, 1)),
            pl.BlockSpec((None, 1, d), lambda b, i, j: (b, 0, 0)),
            pl.BlockSpec((n // IN_TN, d, IN_TN), lambda b, i, j: (0, 0, 0), pipeline_mode=pl.Buffered(1)),
            pl.BlockSpec((1, IN_TN), lambda b, i, j: (0, j)),
        ],
        out_specs=[
            pl.BlockSpec((None, tm, IN_TN), lambda b, i, j: (b, i, j)),
            pl.BlockSpec((None, tm, LANES), lambda b, i, j: (b, i, 0)),
        ],
        out_shape=[
            jax.ShapeDtypeStruct((bsz, s, n), BF16),
            jax.ShapeDtypeStruct((bsz, s, LANES), F32),
        ],
        scratch_shapes=[pltpu.VMEM((tm, d), BF16)],
        compiler_params=_cparams(("parallel", "parallel", "arbitrary")),
        name="in_proj",
    )(x, g, modl, modl, w, col_scale)


LOG2E = math.log2(math.e)
WIN_QSCALE = DH_A ** -0.5 * LOG2E


WIN_QB = 4


def _win_kernel(sink_ref, q_ref, kp_ref, kc_ref, kn_ref, vp_ref, vc_ref, vn_ref, o_ref, *, nb):
    n = pl.program_id(1)
    blk = BLOCK_A
    nk = 3 * blk
    qi = lax.broadcasted_iota(jnp.int32, (blk, nk), 0)
    kj = lax.broadcasted_iota(jnp.int32, (blk, nk), 1)
    dist = jnp.abs(qi - kj + blk)
    k_all = jnp.concatenate([kp_ref[...], kc_ref[...], kn_ref[...]], axis=0)
    v_all = jnp.concatenate([vp_ref[...], vc_ref[...], vn_ref[...]], axis=0)
    low_k = lax.broadcasted_iota(jnp.int32, (nk, LANES), 1) < DH_A
    low_q = lax.broadcasted_iota(jnp.int32, (blk, LANES), 1) < DH_A
    zero = jnp.zeros((nk, LANES), BF16)
    for sub in range(WIN_QB):
        g = n * WIN_QB + sub
        rows = slice(sub * blk, (sub + 1) * blk)
        in_seq = ((kj >= blk) | (g > 0)) & ((kj < 2 * blk) | (g < nb - 1))
        dist_masked = jnp.where((dist <= WINDOW) & in_seq, dist.astype(F32), jnp.inf)
        for p in range(2):
            kt = k_all[sub * blk:sub * blk + nk, p * LANES:(p + 1) * LANES]
            vt = v_all[sub * blk:sub * blk + nk, p * LANES:(p + 1) * LANES]
            k_big = jnp.concatenate([jnp.where(low_k, kt, zero), jnp.where(low_k, zero, kt)], axis=0)
            v_big = jnp.concatenate([jnp.where(low_k, vt, zero), jnp.where(low_k, zero, vt)], axis=0)
            q3 = jnp.concatenate([q_ref[rows, (p * GROUP_A + a) * LANES:(p * GROUP_A + a + 1) * LANES]
                                  for a in range(GROUP_A)], axis=0)
            s3 = _dot_nt(q3, k_big)
            p_rows, inv_rows = [], []
            for a in range(GROUP_A):
                p_halves, inv_halves = [], []
                for half in range(2):
                    head = (2 * p + half) * GROUP_A + a
                    s = (s3[a * blk:(a + 1) * blk, half * nk:(half + 1) * nk]
                         - (SLOPES_A[head] * LOG2E) * dist_masked)
                    sink = sink_ref[head] * LOG2E
                    m = jnp.maximum(jnp.max(s, axis=-1, keepdims=True), sink)
                    e = jnp.exp2(s - m)
                    l = jnp.sum(e, axis=-1, keepdims=True) + jnp.exp2(sink - m)
                    p_halves.append(e.astype(BF16))
                    inv_halves.append(1.0 / l)
                p_rows.append(jnp.concatenate(p_halves, axis=1))
                inv_rows.append(jnp.where(low_q, inv_halves[0], inv_halves[1]))
            o3 = _dot(jnp.concatenate(p_rows, axis=0), v_big)
            for a in range(GROUP_A):
                t = p * GROUP_A + a
                o_ref[rows, t * LANES:(t + 1) * LANES] = (o3[a * blk:(a + 1) * blk] * inv_rows[a]).astype(BF16)


def _window_attention(proj, sinks):
    bsz, s, _ = proj.shape
    blk = BLOCK_A
    nb = s // blk
    assert nb % WIN_QB == 0
    tq = WIN_QB * blk
    kb, vb = OFF_AK // 256, OFF_AV // 256
    prev = lambda n: jnp.maximum(n * WIN_QB - 1, 0)
    nxt = lambda n: jnp.minimum((n + 1) * WIN_QB, nb - 1)
    return pl.pallas_call(
        functools.partial(_win_kernel, nb=nb),
        grid=(bsz, nb // WIN_QB),
        in_specs=[
            pl.BlockSpec(memory_space=pltpu.SMEM),
            pl.BlockSpec((None, tq, W_A), lambda b, n: (b, n, 0)),
            pl.BlockSpec((None, blk, 256), lambda b, n: (b, prev(n), kb)),
            pl.BlockSpec((None, tq, 256), lambda b, n: (b, n, kb)),
            pl.BlockSpec((None, blk, 256), lambda b, n: (b, nxt(n), kb)),
            pl.BlockSpec((None, blk, 256), lambda b, n: (b, prev(n), vb)),
            pl.BlockSpec((None, tq, 256), lambda b, n: (b, n, vb)),
            pl.BlockSpec((None, blk, 256), lambda b, n: (b, nxt(n), vb)),
        ],
        out_specs=pl.BlockSpec((None, tq, W_A), lambda b, n: (b, n, 0)),
        out_shape=jax.ShapeDtypeStruct((bsz, s, W_A), BF16),
        compiler_params=_cparams(("parallel", "parallel")),
        name="win_attn",
    )(sinks, proj, proj, proj, proj, proj, proj, proj)


MLA_TK = 512
MLA_QSCALE = (D_NOPE + D_ROPE) ** -0.5 * math.log2(math.e)


def _mla_pre_kernel(cqkr_ref, ckv_ref, pos_ref, qg_ref, wq_ref, kvg_ref, wk_ref, wvt_ref, inv_ref,
                    qt_ref, k_ref, vt_ref):
    tm = cqkr_ref.shape[0]
    t = cqkr_ref[...].astype(F32)
    lane = lax.broadcasted_iota(jnp.int32, (tm, 4 * LANES), 1)
    cq = jnp.where(lane < Q_LORA, t, 0.0)
    ms = jnp.sum(cq * cq, axis=-1, keepdims=True) * (1.0 / Q_LORA)
    qn = cq * lax.rsqrt(ms + EPS) * qg_ref[...]
    q = _dot(qn.astype(BF16), wq_ref[...]) * MLA_QSCALE
    c = ckv_ref[...].astype(F32)
    ms = jnp.mean(c * c, axis=-1, keepdims=True)
    kvn = (c * lax.rsqrt(ms + EPS) * kvg_ref[...]).astype(BF16)
    kn = _dot(kvn, wk_ref[...])
    vt_ref[...] = _dot_nt(wvt_ref[...], kvn).astype(BF16)

    ang_t = inv_ref[...] * pos_ref[...].astype(F32)
    cs_t, sn_t = jnp.cos(ang_t), jnp.sin(ang_t)
    cs = jnp.concatenate([cs_t] * 4, axis=0).T
    sgn_sn = jnp.concatenate([-sn_t, sn_t] * 2, axis=0).T
    l128 = lax.broadcasted_iota(jnp.int32, (tm, LANES), 1)
    first = (l128 % D_ROPE) < (D_ROPE // 2)

    def rope(x):
        swapped = jnp.where(first, pltpu.roll(x, LANES - D_ROPE // 2, 1), pltpu.roll(x, D_ROPE // 2, 1))
        return x * cs + swapped * sgn_sn

    kr = rope(t[:, 3 * LANES:4 * LANES])
    kr_hi = jnp.where(l128 >= D_ROPE, kr, 0.0)
    kr_lo = pltpu.roll(kr_hi, D_ROPE, 1)
    kr_tiles = (kr_lo.astype(BF16), kr_hi.astype(BF16))
    for pair in range(2):
        qr_t = rope(q[:, (4 + pair) * LANES:(5 + pair) * LANES]).T.astype(BF16)
        for half in range(2):
            h = 2 * pair + half
            qt_ref[(2 * h) * LANES:(2 * h + 1) * LANES, :] = q[:, h * LANES:(h + 1) * LANES].T.astype(BF16)
            qt_ref[(2 * h + 1) * LANES:(2 * h + 2) * LANES, :] = qr_t
            k_ref[:, (2 * h) * LANES:(2 * h + 1) * LANES] = kn[:, h * LANES:(h + 1) * LANES].astype(BF16)
            k_ref[:, (2 * h + 1) * LANES:(2 * h + 2) * LANES] = kr_tiles[half]


def _mla_pre(proj, pos3, qg, wq, kvg, wk, wvt, inv):
    bsz, s, _ = proj.shape
    tm = MLA_TK
    const = lambda shape: pl.BlockSpec(shape, lambda b, i: (0,) * len(shape))
    return pl.pallas_call(
        _mla_pre_kernel,
        grid=(bsz, s // tm),
        in_specs=[
            pl.BlockSpec((None, tm, 512), lambda b, i: (b, i, OFF_CQKR // 512)),
            pl.BlockSpec((None, tm, LANES), lambda b, i: (b, i, OFF_CKV // LANES)),
            pl.BlockSpec((None, 1, tm), lambda b, i: (b, 0, i)),
            const((1, 512)), const((512, 768)), const((1, LANES)), const((LANES, W_B)), const((W_B, LANES)),
            const((D_ROPE // 2, 1)),
        ],
        out_specs=[
            pl.BlockSpec((None, None, 1024, tm), lambda b, i: (b, i, 0, 0)),
            pl.BlockSpec((None, tm, 1024), lambda b, i: (b, i, 0)),
            pl.BlockSpec((None, None, W_B, tm), lambda b, i: (b, i, 0, 0)),
        ],
        out_shape=[
            jax.ShapeDtypeStruct((bsz, s // tm, 1024, tm), BF16),
            jax.ShapeDtypeStruct((bsz, s, 1024), BF16),
            jax.ShapeDtypeStruct((bsz, s // tm, W_B, tm), BF16),
        ],
        compiler_params=_cparams(("parallel", "parallel")),
        name="mla_pre",
    )(proj, proj, pos3, qg, wq, kvg, wk, wvt, inv)


def _mla_attn_kernel(qt_ref, k_ref, vt_ref, o_ref, st_a, st_b):
    n_chunks, _, tk = vt_ref.shape
    nt, _, tq = qt_ref.shape

    def scores(t, st_scr):
        qt = qt_ref[t]
        m8 = jnp.full((8, tq), -jnp.inf, F32)
        for c in range(n_chunks):
            st = _dot(k_ref[c * tk:(c + 1) * tk, :], qt)
            st_scr[c * tk:(c + 1) * tk, :] = st
            m8 = jnp.maximum(m8, jnp.max(st.reshape(tk // 8, 8, tq), axis=0))
        return jnp.max(m8, axis=0, keepdims=True)

    def output(t, st_scr, m):
        l8 = jnp.zeros((8, tq), F32)
        acc = jnp.zeros((D_V_B, tq), F32)
        for c in range(n_chunks):
            p = jnp.exp2(st_scr[c * tk:(c + 1) * tk, :] - m)
            l8 = l8 + jnp.sum(p.reshape(tk // 8, 8, tq), axis=0)
            acc = acc + _dot(vt_ref[c], p.astype(BF16))
        l = jnp.sum(l8, axis=0, keepdims=True)
        o_ref[pl.ds(pl.multiple_of(t * tq, tq), tq), :] = (acc / l).T.astype(BF16)

    m_first = scores(0, st_a)
    if nt == 1:
        output(0, st_a, m_first)
        return

    def pair(u, m_a):
        t = 2 * u
        m_b = scores(t + 1, st_b)
        output(t, st_a, m_a)
        m_a = scores(t + 2, st_a)
        output(t + 1, st_b, m_b)
        return m_a

    m_a = lax.fori_loop(0, nt // 2 - 1, pair, m_first)
    m_b = scores(nt - 1, st_b)
    output(nt - 2, st_a, m_a)
    output(nt - 1, st_b, m_b)


def _mla_attention(qt, k, vt):
    bsz, s, _ = k.shape
    n_chunks, tk = vt.shape[1], vt.shape[3]
    nt, tq = qt.shape[1], qt.shape[3]
    assert nt == 1 or nt % 2 == 0
    return pl.pallas_call(
        _mla_attn_kernel,
        grid=(bsz, N_H_B),
        in_specs=[
            pl.BlockSpec((None, nt, 256, tq), lambda b, h: (b, 0, h, 0)),
            pl.BlockSpec((None, s, 256), lambda b, h: (b, 0, h)),
            pl.BlockSpec((None, n_chunks, D_V_B, tk), lambda b, h: (b, 0, h, 0)),
        ],
        out_specs=pl.BlockSpec((None, s, D_V_B), lambda b, h: (b, 0, h)),
        out_shape=jax.ShapeDtypeStruct((bsz, s, W_B), BF16),
        scratch_shapes=[pltpu.VMEM((s, tq), F32), pltpu.VMEM((s, tq), F32)],
        compiler_params=_cparams(("parallel", "parallel")),
        name="mla_attn",
    )(qt, k, vt)


def _log_sigmoid(x):
    return jnp.minimum(x, 0.0) - jnp.log1p(jnp.exp(-jnp.abs(x)))


def _split3(x):
    hi = x.astype(BF16).astype(F32)
    r1 = x - hi
    mid = r1.astype(BF16).astype(F32)
    return jnp.concatenate([hi, mid, r1 - mid], axis=-1)


def _mlstm_kernel(q_ref, k_ref, v_ref, o_ref, gate_ref, gb_ref, hg_ref, y_ref,
                  hf_scr, hb_scr, tile_scr, kw_scr, rowq_scr, bp_scr, cf_scr, cb_scr, gt_scr):
    s_len = q_ref.shape[0]
    L = CHUNK
    nc = s_len // L
    head = pl.program_id(1)
    ii = lax.broadcasted_iota(jnp.int32, (L, L), 0)
    jj = lax.broadcasted_iota(jnp.int32, (L, L), 1)
    ones = jnp.ones((L, L), BF16)
    ones_lane = lax.broadcasted_iota(jnp.int32, (L, DV_PAD), 1) == ONES_COL
    eye3 = jnp.concatenate([(ii == jj).astype(BF16)] * 3, axis=1)
    tri_ones = [jnp.concatenate([jnp.concatenate([m.astype(BF16), ones], axis=1)] * 3, axis=0)
                for m in (ii <= jj, ii >= jj)]
    masks = (jj <= ii, jj >= ii)
    h_scrs = (hf_scr, hb_scr)
    c_scrs = (cf_scr, cb_scr)

    def load_v_ext(r0):
        return jnp.where(ones_lane, jnp.ones((L, DV_PAD), BF16), v_ref[pl.ds(r0, L), :])

    def transpose_gates(c, _):
        g = gate_ref[pl.ds(pl.multiple_of(c * L, L), L), :] + gb_ref[...]
        gt_scr[pl.ds(pl.multiple_of(c * 16, 16), 16), :] = g.T[0:16, :]
        return 0

    lax.fori_loop(0, nc, transpose_gates, 0, unroll=4)

    for d in range(2):
        li = gt_scr[pl.ds(d * N_H_C + head, nc, stride=16), :]
        lf = _log_sigmoid(gt_scr[pl.ds((2 + d) * N_H_C + head, nc, stride=16), :])
        bt = _dot(_split3(lf).astype(BF16), tri_ones[d])
        b, tot = bt[:, :L], bt[:, L:]
        r = li - b
        m_loc = tot + jnp.broadcast_to(jnp.max(r, axis=-1, keepdims=True), r.shape)
        rowq_scr[d, 0] = r
        rowq_scr[d, 1] = jnp.exp(tot + r - m_loc)
        rowq_scr[d, 2] = tot
        rowq_scr[d, 3] = m_loc
        bp_scr[d] = _split3(b + math.log(DQK_C ** -0.5))

    def prep(c, _):
        r0 = pl.multiple_of(c * L, L)
        q, k = q_ref[pl.ds(r0, L), :], k_ref[pl.ds(r0, L), :]
        s = _dot_nt(q, k)
        kt = k.astype(F32).T
        p = []
        for d in range(2):
            u = jnp.where(masks[d], rowq_scr[d, 0, pl.ds(c, 1), :], -jnp.inf)
            ct = jnp.broadcast_to(jnp.max(u, axis=-1, keepdims=True), (L, L))
            p.append((s * jnp.exp(u - ct)).astype(BF16))
            tile_scr[pl.ds(r0, L), (2 + d) * L:(3 + d) * L] = ct
            kw_scr[d, c] = (kt * rowq_scr[d, 1, pl.ds(c, 1), :]).astype(BF16)
        rhs = jnp.concatenate([jnp.broadcast_to(bp_scr[d, pl.ds(c, 1), :], (L, 3 * L)) for d in range(2)], axis=0)
        tile_scr[pl.ds(r0, L), 0:2 * L] = _dot_nt(eye3, rhs.astype(BF16))
        intra = _dot(jnp.concatenate(p, axis=0), load_v_ext(r0))
        hf_scr[pl.ds(r0, L), :] = intra[:L]
        hb_scr[pl.ds(r0, L), :] = intra[L:]
        return 0

    lax.fori_loop(0, nc, prep, 0, unroll=4)

    cf_scr[...] = jnp.zeros_like(cf_scr)
    cb_scr[...] = jnp.zeros_like(cb_scr)

    def one_dir(c, m_state, d):
        h_scr, c_scr = h_scrs[d], c_scrs[d]
        r0 = pl.multiple_of(c * L, L)
        tot, m_loc = rowq_scr[d, 2, pl.ds(c, 1), :], rowq_scr[d, 3, pl.ds(c, 1), :]
        bt = tile_scr[pl.ds(r0, L), d * L:(d + 1) * L]
        ct = tile_scr[pl.ds(r0, L), (2 + d) * L:(3 + d) * L]
        v_ext = load_v_ext(r0)
        c_state = c_scr[...]
        qc = _dot(q_ref[pl.ds(r0, L), :], c_state.astype(BF16))
        mx = jnp.maximum(m_state, ct)
        iw = jnp.exp(m_state - mx)
        xw = jnp.exp(ct - mx)
        num = [iw * qc[:, j * L:(j + 1) * L] + xw * h_scr[pl.ds(r0, L), j * L:(j + 1) * L] for j in range(2)]
        den = jnp.broadcast_to(num[1][:, ONES_COL - L:ONES_COL - L + 1], (L, L))
        inv = 1.0 / jnp.maximum(jnp.abs(den), jnp.exp(-(bt + mx)))
        h_scr[pl.ds(r0, L), 0:L] = num[0] * inv
        h_scr[pl.ds(r0, L), L:2 * L] = num[1] * inv
        c_loc = _dot(kw_scr[d, c], v_ext)
        m_new = jnp.maximum(tot + m_state, m_loc)
        sp, sl = jnp.exp(tot + m_state - m_new), jnp.exp(m_loc - m_new)
        c_scr[...] = jnp.concatenate(
            [sp * c_state[:, j * L:(j + 1) * L] + sl * c_loc[:, j * L:(j + 1) * L] for j in range(2)], axis=1)
        return m_new

    def scan(c, carry):
        return one_dir(c, carry[0], 0), one_dir(nc - 1 - c, carry[1], 1)

    neg = jnp.full((1, LANES), -jnp.inf, F32)
    lax.fori_loop(0, nc, scan, (neg, neg), unroll=4)

    real = lax.broadcasted_iota(jnp.int32, (L, DV_PAD), 1) < DV_C

    def epilogue(c, _):
        r0 = pl.multiple_of(c * L, L)
        h = jnp.where(real, hf_scr[pl.ds(r0, L), :] + hb_scr[pl.ds(r0, L), :], 0.0)
        ms = jnp.sum(h * h, axis=-1, keepdims=True) * (1.0 / DV_C)
        hn = h * lax.rsqrt(ms + EPS) * hg_ref[...]
        y = jax.nn.sigmoid(o_ref[pl.ds(r0, L), :].astype(F32)) * hn
        y_ref[pl.ds(r0, L), :] = y.astype(BF16)
        return 0

    lax.fori_loop(0, nc, epilogue, 0, unroll=2)


def _mlstm(proj, gates, gate_b, head_g):
    bsz, s, _ = proj.shape
    nc = s // CHUNK
    return pl.pallas_call(
        _mlstm_kernel,
        grid=(bsz, N_H_C),
        in_specs=[
            pl.BlockSpec((None, s, DQK_PAD), lambda b, h: (b, 0, OFF_MQ // DQK_PAD + h)),
            pl.BlockSpec((None, s, DQK_PAD), lambda b, h: (b, 0, OFF_MK // DQK_PAD + h)),
            pl.BlockSpec((None, s, DV_PAD), lambda b, h: (b, 0, OFF_MV // DV_PAD + h)),
            pl.BlockSpec((None, s, DV_PAD), lambda b, h: (b, 0, OFF_MO // DV_PAD + h)),
            pl.BlockSpec((None, s, LANES), lambda b, h: (b, 0, 0)),
            pl.BlockSpec((1, LANES), lambda b, h: (0, 0)),
            pl.BlockSpec((None, 1, DV_PAD), lambda b, h: (h, 0, 0)),
        ],
        out_specs=pl.BlockSpec((None, s, DV_PAD), lambda b, h: (b, 0, h)),
        out_shape=jax.ShapeDtypeStruct((bsz, s, W_C_PAD), BF16),
        scratch_shapes=[
            pltpu.VMEM((s, DV_PAD), F32), pltpu.VMEM((s, DV_PAD), F32),
            pltpu.VMEM((s, 4 * CHUNK), F32),
            pltpu.VMEM((2, nc, DQK_PAD, CHUNK), BF16),
            pltpu.VMEM((2, 4, nc, CHUNK), F32),
            pltpu.VMEM((2, nc, 3 * CHUNK), F32),
            pltpu.VMEM((DQK_PAD, DV_PAD), F32), pltpu.VMEM((DQK_PAD, DV_PAD), F32),
            pltpu.VMEM((nc * 16, CHUNK), F32),
        ],
        compiler_params=_cparams(("parallel", "arbitrary")),
        name="mlstm",
    )(proj, proj, proj, proj, gates, gate_b, head_g)


def _out_kernel(ya_ref, yb_ref, yc_ref, wa_ref, wb_ref, wc_ref, pg_ref, gate_ref, x_ref, o_ref, y_scr):
    for r in range(0, o_ref.shape[0], SUB_ROWS):
        rows = slice(r, r + SUB_ROWS)
        y_scr[rows, :] = (_dot(ya_ref[rows, :], wa_ref[...]) + _dot(yb_ref[rows, :], wb_ref[...])
                          + _dot(yc_ref[rows, :], wc_ref[...]))
        _norm_residual_rows(y_scr, x_ref, pg_ref, gate_ref, o_ref, r, SUB_ROWS)


def _out_proj(ya, yb, yc, wa, wb, wc, pg, modl, x):
    bsz, s, d = x.shape
    tm = min(2 * SUB_ROWS, s)
    row = lambda w: pl.BlockSpec((None, tm, w), lambda b, i: (b, i, 0))
    const = lambda shape: pl.BlockSpec(shape, lambda b, i: (0,) * len(shape), pipeline_mode=pl.Buffered(1))
    return pl.pallas_call(
        _out_kernel,
        grid=(bsz, s // tm),
        in_specs=[
            row(W_A), row(W_B), row(W_C_PAD),
            const((W_A, d)), const((W_B, d)), const((W_C_PAD, d)), const((1, d)),
            pl.BlockSpec((None, 1, d), lambda b, i: (b, 0, 2)),
            row(d),
        ],
        out_specs=row(d),
        out_shape=jax.ShapeDtypeStruct((bsz, s, d), F32),
        scratch_shapes=[pltpu.VMEM((tm, d), F32)],
        compiler_params=_cparams(("parallel", "parallel")),
        name="out_proj",
    )(ya, yb, yc, wa, wb, wc, pg, modl, x)


FFN_TF = 512
FFN_VMEM_LIMIT = 62 * 1024 * 1024


def _ffn_kernel(x_ref, g_ref, sc_ref, sh_ref, wg_ref, wu_ref, wd_ref, pg_ref, gate_ref, o_ref, h_scr):
    j = pl.program_id(2)
    last = pl.num_programs(2) - 1
    tm = x_ref.shape[0]
    subs = [(r, slice(r, r + SUB_ROWS)) for r in range(0, tm, SUB_ROWS)]

    def partial_down(rows):
        h = h_scr[rows, :]
        a = _dot(h, wg_ref[...])
        u = _dot(h, wu_ref[...])
        hid = (a * jax.nn.sigmoid(a)) * u
        return _dot(hid.astype(BF16), wd_ref[...])

    @pl.when(j == 0)
    def _():
        for r, rows in subs:
            _mod_norm_rows(x_ref, g_ref, sc_ref, sh_ref, h_scr, r, SUB_ROWS)
            o_ref[rows, :] = partial_down(rows)

    @pl.when((j > 0) & (j < last))
    def _():
        o_ref[...] += partial_down(slice(0, tm))

    @pl.when(j == last)
    def _():
        for r, rows in subs:
            o_ref[rows, :] += partial_down(rows)
            _norm_residual_rows(o_ref, x_ref, pg_ref, gate_ref, o_ref, r, SUB_ROWS)


def _ffn(x, g, modl, wg, wu, wd, pg, layer):
    bsz, s, d = x.shape
    f = wg.shape[2]
    tm = min(1024, s)
    row = pl.BlockSpec((None, tm, d), lambda b, i, j: (b, i, 0))
    vec = pl.BlockSpec((1, d), lambda b, i, j: (0, 0))
    modv = lambda k: pl.BlockSpec((None, 1, d), lambda b, i, j: (b, 0, k))
    return pl.pallas_call(
        _ffn_kernel,
        grid=(bsz, s // tm, f // FFN_TF),
        in_specs=[
            row, vec, modv(4), modv(3),
            pl.BlockSpec((None, d, FFN_TF), lambda b, i, j: (layer, 0, j)),
            pl.BlockSpec((None, d, FFN_TF), lambda b, i, j: (layer, 0, j)),
            pl.BlockSpec((None, FFN_TF, d), lambda b, i, j: (layer, j, 0)),
            vec, modv(5),
        ],
        out_specs=row,
        out_shape=jax.ShapeDtypeStruct((bsz, s, d), F32),
        scratch_shapes=[pltpu.VMEM((tm, d), BF16)],
        compiler_params=_cparams(("parallel", "parallel", "arbitrary"), FFN_VMEM_LIMIT),
        name="ffn",
    )(x, g, modl, modl, wg, wu, wd, pg, modl)


CAST_BLOCK_BYTES = 4 * 1024 * 1024


def _cast_kernel(w_ref, o_ref):
    o_ref[...] = w_ref[...].astype(BF16)


def _to_bf16(w):
    depth, rows, cols = w.shape
    rb = rows
    while rb * cols * 4 > CAST_BLOCK_BYTES and rb % 16 == 0:
        rb //= 2
    spec = pl.BlockSpec((None, rb, cols), lambda l, i: (l, i, 0))
    return pl.pallas_call(
        _cast_kernel,
        grid=(depth, rows // rb),
        in_specs=[spec],
        out_specs=spec,
        out_shape=jax.ShapeDtypeStruct(w.shape, BF16),
        compiler_params=_cparams(("parallel", "parallel")),
        name="cast_bf16",
    )(w)


def _pair_heads_a(w, axis):
    shape = w.shape
    w = jnp.moveaxis(w, axis, -1)
    lead = w.shape[:-1]
    w = w.reshape(lead + (2, 2, GROUP_A, DH_A))
    w = jnp.swapaxes(w, -3, -2)
    w = w.reshape(lead + (W_A,))
    return jnp.moveaxis(w, -1, axis).reshape(shape)


def _pad_heads(w, n_heads, width, padded, axis):
    w = jnp.moveaxis(w, axis, -1)
    lead = w.shape[:-1]
    w = w.reshape(lead + (n_heads, width))
    w = jnp.pad(w, [(0, 0)] * len(lead) + [(0, 0), (0, padded - width)])
    w = w.reshape(lead + (n_heads * padded,))
    return jnp.moveaxis(w, -1, axis)


def _layout_w_in(w):
    d = w.shape[0]
    sizes = (W_A, N_KV_A * DH_A, N_KV_A * DH_A, Q_LORA, KV_LORA, D_ROPE,
             N_H_C * DQK_C, N_H_C * DQK_C, W_C, 4 * N_H_C, W_C)
    pts = np.cumsum(sizes)[:-1].tolist()
    aq, ak, av, bcq, bckv, bkr, cq, ck, cv, cg, co = jnp.split(w, pts, axis=1)
    parts = [
        _pair_heads_a(aq, 1), ak, av, bckv,
        jnp.pad(cg, ((0, 0), (0, LANES - 4 * N_H_C))),
        bcq, bkr,
        _pad_heads(cq, N_H_C, DQK_C, DQK_PAD, 1), _pad_heads(ck, N_H_C, DQK_C, DQK_PAD, 1),
        _pad_heads(cv, N_H_C, DV_C, DV_PAD, 1), _pad_heads(co, N_H_C, DV_C, DV_PAD, 1),
    ]
    out = jnp.concatenate(parts, axis=1).astype(BF16)
    assert out.shape == (d, IN_WIDTH_PAD)
    return out.reshape(d, IN_WIDTH_PAD // IN_TN, IN_TN).transpose(1, 0, 2)


def _layout_w_uq(w):
    w = w.reshape(Q_LORA, N_H_B, D_NOPE + D_ROPE)
    nope = w[:, :, :D_NOPE].reshape(Q_LORA, N_H_B * D_NOPE)
    rope = w[:, :, D_NOPE:].reshape(Q_LORA, N_H_B * D_ROPE)
    w = jnp.concatenate([nope, rope], axis=1)
    return jnp.pad(w, ((0, 512 - Q_LORA), (0, 0))).astype(BF16)


def _layout_w_ukv(w):
    w = w.reshape(KV_LORA, N_H_B, D_NOPE + D_V_B)
    kn = w[:, :, :D_NOPE].reshape(KV_LORA, N_H_B * D_NOPE)
    v = w[:, :, D_NOPE:].reshape(KV_LORA, N_H_B * D_V_B)
    return kn.astype(BF16), v.T.astype(BF16)


def kernel(x, c, positions, mod_w, mod_b, pre_mix_g, post_mix_g, pre_ffn_g, post_ffn_g, w_in, attn_sink, mla_q_norm_g, mla_w_uq, mla_kv_norm_g, mla_w_ukv, mlstm_gate_b, mlstm_head_g, w_out, ffn_w_gate, ffn_w_up, ffn_w_down):
    depth = mod_w.shape[0]
    bsz, s, d = x.shape
    mod = _modulation(c, mod_w, mod_b)
    pos3 = positions.reshape(bsz, 1, s)
    inv = 1.0 / (ROPE_THETA ** (jnp.arange(0, D_ROPE, 2, dtype=F32) / D_ROPE))
    inv_col = inv.reshape(D_ROPE // 2, 1)
    wg_bf, wu_bf, wd_bf = _to_bf16(ffn_w_gate), _to_bf16(ffn_w_up), _to_bf16(ffn_w_down)
    for l in range(depth):
        modl = mod[l].reshape(bsz, 1, 6 * d)
        proj, gates = _in_proj(x, pre_mix_g[l].reshape(1, d), modl, _layout_w_in(w_in[l]))
        ya = _window_attention(proj, attn_sink[l])
        qg = jnp.pad(mla_q_norm_g[l], (0, 512 - Q_LORA)).reshape(1, 512)
        wk, wvt = _layout_w_ukv(mla_w_ukv[l])
        qb, kb, vtb = _mla_pre(proj, pos3, qg, _layout_w_uq(mla_w_uq[l]),
                               mla_kv_norm_g[l].reshape(1, KV_LORA), wk, wvt, inv_col)
        yb = _mla_attention(qb, kb, vtb)
        gate_b = jnp.pad(mlstm_gate_b[l], (0, LANES - 4 * N_H_C)).reshape(1, LANES)
        head_g = jnp.pad(mlstm_head_g[l].reshape(N_H_C, 1, DV_C), ((0, 0), (0, 0), (0, DV_PAD - DV_C)))
        yc = _mlstm(proj, gates, gate_b, head_g)
        wo = w_out[l]
        wa = _pair_heads_a(wo[:W_A], 0).astype(BF16)
        wb = wo[W_A:W_A + W_B].astype(BF16)
        wc = _pad_heads(wo[W_A + W_B:], N_H_C, DV_C, DV_PAD, 0).astype(BF16)
        x = _out_proj(ya, yb, yc, wa, wb, wc, post_mix_g[l].reshape(1, d), modl, x)
        x = _ffn(x, pre_ffn_g[l].reshape(1, d), modl, wg_bf, wu_bf, wd_bf, post_ffn_g[l].reshape(1, d), l)
    return x
```

```python
import functools
import math

import numpy as np
import jax
import jax.numpy as jnp
from jax import lax
from jax.experimental import pallas as pl
from jax.experimental.pallas import tpu as pltpu

F32 = jnp.float32
BF16 = jnp.bfloat16

D_MODEL = 2048
EPS = 1e-6
N_Q_A, N_KV_A, DH_A, GROUP_A = 12, 4, 64, 3
WINDOW = 128
BLOCK_A = 128
N_H_B, Q_LORA, KV_LORA, D_NOPE, D_ROPE, D_V_B = 4, 448, 128, 128, 64, 128
ROPE_THETA = 10000.0
N_H_C, DQK_C, DV_C = 4, 96, 192
W_A, W_B, W_C = N_Q_A * DH_A, N_H_B * D_V_B, N_H_C * DV_C
D_FF = 5632

LANES = 128
DQK_PAD = 128
DV_PAD = 256
ONES_COL = DV_C
VMEM_LIMIT = 56 * 1024 * 1024

OFF_AQ, OFF_AK, OFF_AV = 0, 768, 1024
OFF_CKV, OFF_CG, OFF_CQKR = 1280, 1408, 1536
OFF_MQ, OFF_MK, OFF_MV, OFF_MO = 2048, 2560, 3072, 4096
IN_WIDTH_PAD = 5120
W_C_PAD = N_H_C * DV_PAD
MIX_PAD = W_A + W_B + W_C_PAD

CHUNK = 128


def _alibi_slopes(n):
    def pow2(m):
        start = 2.0 ** (-8.0 / m)
        return [start ** (i + 1) for i in range(m)]
    if math.log2(n).is_integer():
        s = pow2(n)
    else:
        p = 2 ** math.floor(math.log2(n))
        s = pow2(p) + pow2(2 * p)[0::2][: n - p]
    return [float(np.float32(v)) for v in s]


SLOPES_A = _alibi_slopes(N_Q_A)


def _cparams(sem, vmem_limit=VMEM_LIMIT):
    return pltpu.CompilerParams(dimension_semantics=sem, vmem_limit_bytes=vmem_limit)


def _dot(a, b):
    return jnp.dot(a, b, preferred_element_type=F32)


def _dot_nt(a, b):
    return lax.dot_general(a, b, (((1,), (1,)), ((), ())), preferred_element_type=F32)


def _dot_tn(a, b):
    return lax.dot_general(a, b, (((0,), (0,)), ((), ())), preferred_element_type=F32)


def _mod_kernel(c_ref, w_ref, b_ref, o_ref):
    c = c_ref[...]
    s = c * jax.nn.sigmoid(c)
    o_ref[...] = _dot(s.astype(BF16), w_ref[...].astype(BF16)) + b_ref[...]


def _modulation(c, mod_w, mod_b):
    depth, d, n = mod_w.shape
    bsz = c.shape[0]
    tn = 1024
    return pl.pallas_call(
        _mod_kernel,
        grid=(depth, n // tn),
        in_specs=[
            pl.BlockSpec((bsz, d), lambda l, j: (0, 0)),
            pl.BlockSpec((None, d, tn), lambda l, j: (l, 0, j)),
            pl.BlockSpec((None, 1, tn), lambda l, j: (l, 0, j)),
        ],
        out_specs=pl.BlockSpec((None, bsz, tn), lambda l, j: (l, 0, j)),
        out_shape=jax.ShapeDtypeStruct((depth, bsz, n), F32),
        compiler_params=_cparams(("parallel", "parallel")),
        name="adaln_mod",
    )(c, mod_w, mod_b.reshape(depth, 1, n))


ROW_SLAB = 16
SUB_ROWS = 256


def _row_slabs(first_row, n_rows):
    return [slice(r, r + ROW_SLAB) for r in range(first_row, first_row + n_rows, ROW_SLAB)]


def _mod_norm_rows(x_ref, g_ref, sc_ref, sh_ref, h_ref, first_row, n_rows):
    gain = g_ref[...] * (1.0 + sc_ref[...])
    shift = sh_ref[...]
    for rows in _row_slabs(first_row, n_rows):
        x = x_ref[rows, :]
        ms = jnp.mean(x * x, axis=-1, keepdims=True)
        h_ref[rows, :] = (x * lax.rsqrt(ms + EPS) * gain + shift).astype(BF16)


def _norm_residual_rows(y_ref, x_ref, pg_ref, gate_ref, o_ref, first_row, n_rows):
    gain = gate_ref[...] * pg_ref[...]
    for rows in _row_slabs(first_row, n_rows):
        y = y_ref[rows, :]
        ms = jnp.mean(y * y, axis=-1, keepdims=True)
        o_ref[rows, :] = x_ref[rows, :] + y * lax.rsqrt(ms + EPS) * gain


IN_TN = 1024
GATE_TILE = OFF_CG // IN_TN
GATE_OFF = OFF_CG % IN_TN


def _in_kernel(x_ref, g_ref, sc_ref, sh_ref, w_ref, cs_ref, o_ref, gate_ref, h_scr):
    j = pl.program_id(2)
    tm = x_ref.shape[0]

    def project(rows):
        acc = _dot(h_scr[rows, :], w_ref[j])
        o_ref[rows, :] = (acc * cs_ref[...]).astype(BF16)
        return acc

    @pl.when(j == 0)
    def _():
        for r in range(0, tm, SUB_ROWS):
            _mod_norm_rows(x_ref, g_ref, sc_ref, sh_ref, h_scr, r, SUB_ROWS)
            project(slice(r, r + SUB_ROWS))

    @pl.when(j != 0)
    def _():
        acc = project(slice(0, tm))

        @pl.when(j == GATE_TILE)
        def _():
            gate_ref[...] = acc[:, GATE_OFF:GATE_OFF + LANES]


def _in_proj(x, g, modl, w):
    bsz, s, d = x.shape
    n = w.shape[0] * IN_TN
    tm = min(1024, s)
    col_scale = jnp.where(jnp.arange(n) < W_A, WIN_QSCALE, 1.0).astype(F32).reshape(1, n)
    return pl.pallas_call(
        _in_kernel,
        grid=(bsz, s // tm, n // IN_TN),
        in_specs=[
            pl.BlockSpec((None, tm, d), lambda b, i, j: (b, i, 0)),
            pl.BlockSpec((1, d), lambda b, i, j: (0, 0)),
            pl.BlockSpec((None, 1, d), lambda b, i, j: (b, 0, 1)),
            pl.BlockSpec((None, 1, d), lambda b, i, j: (b, 0, 0)),
            pl.BlockSpec((n // IN_TN, d, IN_TN), lambda b, i, j: (0, 0, 0), pipeline_mode=pl.Buffered(1)),
            pl.BlockSpec((1, IN_TN), lambda b, i, j: (0, j)),
        ],
        out_specs=[
            pl.BlockSpec((None, tm, IN_TN), lambda b, i, j: (b, i, j)),
            pl.BlockSpec((None, tm, LANES), lambda b, i, j: (b, i, 0)),
        ],
        out_shape=[
            jax.ShapeDtypeStruct((bsz, s, n), BF16),
            jax.ShapeDtypeStruct((bsz, s, LANES), F32),
        ],
        scratch_shapes=[pltpu.VMEM((tm, d), BF16)],
        compiler_params=_cparams(("parallel", "parallel", "arbitrary")),
        name="in_proj",
    )(x, g, modl, modl, w, col_scale)


LOG2E = math.log2(math.e)
WIN_QSCALE = DH_A ** -0.5 * LOG2E


WIN_QB = 4


def _win_kernel(sink_ref, q_ref, kp_ref, kc_ref, kn_ref, vp_ref, vc_ref, vn_ref, o_ref, *, nb):
    n = pl.program_id(1)
    blk = BLOCK_A
    nk = 3 * blk
    qi = lax.broadcasted_iota(jnp.int32, (blk, nk), 0)
    kj = lax.broadcasted_iota(jnp.int32, (blk, nk), 1)
    dist = jnp.abs(qi - kj + blk)
    k_all = jnp.concatenate([kp_ref[...], kc_ref[...], kn_ref[...]], axis=0)
    v_all = jnp.concatenate([vp_ref[...], vc_ref[...], vn_ref[...]], axis=0)
    low_k = lax.broadcasted_iota(jnp.int32, (nk, LANES), 1) < DH_A
    low_q = lax.broadcasted_iota(jnp.int32, (blk, LANES), 1) < DH_A
    zero = jnp.zeros((nk, LANES), BF16)
    for sub in range(WIN_QB):
        g = n * WIN_QB + sub
        rows = slice(sub * blk, (sub + 1) * blk)
        in_seq = ((kj >= blk) | (g > 0)) & ((kj < 2 * blk) | (g < nb - 1))
        dist_masked = jnp.where((dist <= WINDOW) & in_seq, dist.astype(F32), jnp.inf)
        for p in range(2):
            kt = k_all[sub * blk:sub * blk + nk, p * LANES:(p + 1) * LANES]
            vt = v_all[sub * blk:sub * blk + nk, p * LANES:(p + 1) * LANES]
            k_big = jnp.concatenate([jnp.where(low_k, kt, zero), jnp.where(low_k, zero, kt)], axis=0)
            v_big = jnp.concatenate([jnp.where(low_k, vt, zero), jnp.where(low_k, zero, vt)], axis=0)
            q3 = jnp.concatenate([q_ref[rows, (p * GROUP_A + a) * LANES:(p * GROUP_A + a + 1) * LANES]
                                  for a in range(GROUP_A)], axis=0)
            s3 = _dot_nt(q3, k_big)
            p_rows, inv_rows = [], []
            for a in range(GROUP_A):
                p_halves, inv_halves = [], []
                for half in range(2):
                    head = (2 * p + half) * GROUP_A + a
                    s = (s3[a * blk:(a + 1) * blk, half * nk:(half + 1) * nk]
                         - (SLOPES_A[head] * LOG2E) * dist_masked)
                    sink = sink_ref[head] * LOG2E
                    m = jnp.maximum(jnp.max(s, axis=-1, keepdims=True), sink)
                    e = jnp.exp2(s - m)
                    l = jnp.sum(e, axis=-1, keepdims=True) + jnp.exp2(sink - m)
                    p_halves.append(e.astype(BF16))
                    inv_halves.append(1.0 / l)
                p_rows.append(jnp.concatenate(p_halves, axis=1))
                inv_rows.append(jnp.where(low_q, inv_halves[0], inv_halves[1]))
            o3 = _dot(jnp.concatenate(p_rows, axis=0), v_big)
            for a in range(GROUP_A):
                t = p * GROUP_A + a
                o_ref[rows, t * LANES:(t + 1) * LANES] = (o3[a * blk:(a + 1) * blk] * inv_rows[a]).astype(BF16)


def _window_attention(proj, sinks):
    bsz, s, _ = proj.shape
    blk = BLOCK_A
    nb = s // blk
    assert nb % WIN_QB == 0
    tq = WIN_QB * blk
    kb, vb = OFF_AK // 256, OFF_AV // 256
    prev = lambda n: jnp.maximum(n * WIN_QB - 1, 0)
    nxt = lambda n: jnp.minimum((n + 1) * WIN_QB, nb - 1)
    return pl.pallas_call(
        functools.partial(_win_kernel, nb=nb),
        grid=(bsz, nb // WIN_QB),
        in_specs=[
            pl.BlockSpec(memory_space=pltpu.SMEM),
            pl.BlockSpec((None, tq, W_A), lambda b, n: (b, n, 0)),
            pl.BlockSpec((None, blk, 256), lambda b, n: (b, prev(n), kb)),
            pl.BlockSpec((None, tq, 256), lambda b, n: (b, n, kb)),
            pl.BlockSpec((None, blk, 256), lambda b, n: (b, nxt(n), kb)),
            pl.BlockSpec((None, blk, 256), lambda b, n: (b, prev(n), vb)),
            pl.BlockSpec((None, tq, 256), lambda b, n: (b, n, vb)),
            pl.BlockSpec((None, blk, 256), lambda b, n: (b, nxt(n), vb)),
        ],
        out_specs=pl.BlockSpec((None, tq, W_A), lambda b, n: (b, n, 0)),
        out_shape=jax.ShapeDtypeStruct((bsz, s, W_A), BF16),
        compiler_params=_cparams(("parallel", "parallel")),
        name="win_attn",
    )(sinks, proj, proj, proj, proj, proj, proj, proj)


MLA_TK = 512
MLA_QSCALE = (D_NOPE + D_ROPE) ** -0.5 * math.log2(math.e)


def _mla_pre_kernel(cqkr_ref, ckv_ref, pos_ref, qg_ref, wq_ref, kvg_ref, wk_ref, wvt_ref, inv_ref,
                    qt_ref, k_ref, vt_ref):
    tm = cqkr_ref.shape[0]
    t = cqkr_ref[...].astype(F32)
    lane = lax.broadcasted_iota(jnp.int32, (tm, 4 * LANES), 1)
    cq = jnp.where(lane < Q_LORA, t, 0.0)
    ms = jnp.sum(cq * cq, axis=-1, keepdims=True) * (1.0 / Q_LORA)
    qn = cq * lax.rsqrt(ms + EPS) * qg_ref[...]
    q = _dot(qn.astype(BF16), wq_ref[...]) * MLA_QSCALE
    c = ckv_ref[...].astype(F32)
    ms = jnp.mean(c * c, axis=-1, keepdims=True)
    kvn = (c * lax.rsqrt(ms + EPS) * kvg_ref[...]).astype(BF16)
    kn = _dot(kvn, wk_ref[...])
    vt_ref[...] = _dot_nt(wvt_ref[...], kvn).astype(BF16)

    ang_t = inv_ref[...] * pos_ref[...].astype(F32)
    cs_t, sn_t = jnp.cos(ang_t), jnp.sin(ang_t)
    cs = jnp.concatenate([cs_t] * 4, axis=0).T
    sgn_sn = jnp.concatenate([-sn_t, sn_t] * 2, axis=0).T
    l128 = lax.broadcasted_iota(jnp.int32, (tm, LANES), 1)
    first = (l128 % D_ROPE) < (D_ROPE // 2)

    def rope(x):
        swapped = jnp.where(first, pltpu.roll(x, LANES - D_ROPE // 2, 1), pltpu.roll(x, D_ROPE // 2, 1))
        return x * cs + swapped * sgn_sn

    kr = rope(t[:, 3 * LANES:4 * LANES])
    kr_hi = jnp.where(l128 >= D_ROPE, kr, 0.0)
    kr_lo = pltpu.roll(kr_hi, D_ROPE, 1)
    kr_tiles = (kr_lo.astype(BF16), kr_hi.astype(BF16))
    for pair in range(2):
        qr_t = rope(q[:, (4 + pair) * LANES:(5 + pair) * LANES]).T.astype(BF16)
        for half in range(2):
            h = 2 * pair + half
            qt_ref[(2 * h) * LANES:(2 * h + 1) * LANES, :] = q[:, h * LANES:(h + 1) * LANES].T.astype(BF16)
            qt_ref[(2 * h + 1) * LANES:(2 * h + 2) * LANES, :] = qr_t
            k_ref[:, (2 * h) * LANES:(2 * h + 1) * LANES] = kn[:, h * LANES:(h + 1) * LANES].astype(BF16)
            k_ref[:, (2 * h + 1) * LANES:(2 * h + 2) * LANES] = kr_tiles[half]


def _mla_pre(proj, pos3, qg, wq, kvg, wk, wvt, inv):
    bsz, s, _ = proj.shape
    tm = MLA_TK
    const = lambda shape: pl.BlockSpec(shape, lambda b, i: (0,) * len(shape))
    return pl.pallas_call(
        _mla_pre_kernel,
        grid=(bsz, s // tm),
        in_specs=[
            pl.BlockSpec((None, tm, 512), lambda b, i: (b, i, OFF_CQKR // 512)),
            pl.BlockSpec((None, tm, LANES), lambda b, i: (b, i, OFF_CKV // LANES)),
            pl.BlockSpec((None, 1, tm), lambda b, i: (b, 0, i)),
            const((1, 512)), const((512, 768)), const((1, LANES)), const((LANES, W_B)), const((W_B, LANES)),
            const((D_ROPE // 2, 1)),
        ],
        out_specs=[
            pl.BlockSpec((None, None, 1024, tm), lambda b, i: (b, i, 0, 0)),
            pl.BlockSpec((None, tm, 1024), lambda b, i: (b, i, 0)),
            pl.BlockSpec((None, None, W_B, tm), lambda b, i: (b, i, 0, 0)),
        ],
        out_shape=[
            jax.ShapeDtypeStruct((bsz, s // tm, 1024, tm), BF16),
            jax.ShapeDtypeStruct((bsz, s, 1024), BF16),
            jax.ShapeDtypeStruct((bsz, s // tm, W_B, tm), BF16),
        ],
        compiler_params=_cparams(("parallel", "parallel")),
        name="mla_pre",
    )(proj, proj, pos3, qg, wq, kvg, wk, wvt, inv)


def _mla_attn_kernel(qt_ref, k_ref, vt_ref, o_ref, st_a, st_b):
    n_chunks, _, tk = vt_ref.shape
    nt, _, tq = qt_ref.shape

    def scores(t, st_scr):
        qt = qt_ref[t]
        m8 = jnp.full((8, tq), -jnp.inf, F32)
        for c in range(n_chunks):
            st = _dot(k_ref[c * tk:(c + 1) * tk, :], qt)
            st_scr[c * tk:(c + 1) * tk, :] = st
            m8 = jnp.maximum(m8, jnp.max(st.reshape(tk // 8, 8, tq), axis=0))
        return jnp.max(m8, axis=0, keepdims=True)

    def output(t, st_scr, m):
        l8 = jnp.zeros((8, tq), F32)
        acc = jnp.zeros((D_V_B, tq), F32)
        for c in range(n_chunks):
            p = jnp.exp2(st_scr[c * tk:(c + 1) * tk, :] - m)
            l8 = l8 + jnp.sum(p.reshape(tk // 8, 8, tq), axis=0)
            acc = acc + _dot(vt_ref[c], p.astype(BF16))
        l = jnp.sum(l8, axis=0, keepdims=True)
        o_ref[pl.ds(pl.multiple_of(t * tq, tq), tq), :] = (acc / l).T.astype(BF16)

    m_first = scores(0, st_a)
    if nt == 1:
        output(0, st_a, m_first)
        return

    def pair(u, m_a):
        t = 2 * u
        m_b = scores(t + 1, st_b)
        output(t, st_a, m_a)
        m_a = scores(t + 2, st_a)
        output(t + 1, st_b, m_b)
        return m_a

    m_a = lax.fori_loop(0, nt // 2 - 1, pair, m_first)
    m_b = scores(nt - 1, st_b)
    output(nt - 2, st_a, m_a)
    output(nt - 1, st_b, m_b)


def _mla_attention(qt, k, vt):
    bsz, s, _ = k.shape
    n_chunks, tk = vt.shape[1], vt.shape[3]
    nt, tq = qt.shape[1], qt.shape[3]
    assert nt == 1 or nt % 2 == 0
    return pl.pallas_call(
        _mla_attn_kernel,
        grid=(bsz, N_H_B),
        in_specs=[
            pl.BlockSpec((None, nt, 256, tq), lambda b, h: (b, 0, h, 0)),
            pl.BlockSpec((None, s, 256), lambda b, h: (b, 0, h)),
            pl.BlockSpec((None, n_chunks, D_V_B, tk), lambda b, h: (b, 0, h, 0)),
        ],
        out_specs=pl.BlockSpec((None, s, D_V_B), lambda b, h: (b, 0, h)),
        out_shape=jax.ShapeDtypeStruct((bsz, s, W_B), BF16),
        scratch_shapes=[pltpu.VMEM((s, tq), F32), pltpu.VMEM((s, tq), F32)],
        compiler_params=_cparams(("parallel", "parallel")),
        name="mla_attn",
    )(qt, k, vt)


def _log_sigmoid(x):
    return jnp.minimum(x, 0.0) - jnp.log1p(jnp.exp(-jnp.abs(x)))


def _split3(x):
    hi = x.astype(BF16).astype(F32)
    r1 = x - hi
    mid = r1.astype(BF16).astype(F32)
    return jnp.concatenate([hi, mid, r1 - mid], axis=-1)


def _mlstm_kernel(q_ref, k_ref, v_ref, o_ref, gate_ref, gb_ref, hg_ref, y_ref,
                  hf_scr, hb_scr, tile_scr, kw_scr, rowq_scr, bp_scr, cf_scr, cb_scr, gt_scr):
    s_len = q_ref.shape[0]
    L = CHUNK
    nc = s_len // L
    head = pl.program_id(1)
    ii = lax.broadcasted_iota(jnp.int32, (L, L), 0)
    jj = lax.broadcasted_iota(jnp.int32, (L, L), 1)
    ones = jnp.ones((L, L), BF16)
    ones_lane = lax.broadcasted_iota(jnp.int32, (L, DV_PAD), 1) == ONES_COL
    eye3 = jnp.concatenate([(ii == jj).astype(BF16)] * 3, axis=1)
    tri_ones = [jnp.concatenate([jnp.concatenate([m.astype(BF16), ones], axis=1)] * 3, axis=0)
                for m in (ii <= jj, ii >= jj)]
    masks = (jj <= ii, jj >= ii)
    h_scrs = (hf_scr, hb_scr)
    c_scrs = (cf_scr, cb_scr)

    def load_v_ext(r0):
        return jnp.where(ones_lane, jnp.ones((L, DV_PAD), BF16), v_ref[pl.ds(r0, L), :])

    def transpose_gates(c, _):
        g = gate_ref[pl.ds(pl.multiple_of(c * L, L), L), :] + gb_ref[...]
        gt_scr[pl.ds(pl.multiple_of(c * 16, 16), 16), :] = g.T[0:16, :]
        return 0

    lax.fori_loop(0, nc, transpose_gates, 0, unroll=4)

    for d in range(2):
        li = gt_scr[pl.ds(d * N_H_C + head, nc, stride=16), :]
        lf = _log_sigmoid(gt_scr[pl.ds((2 + d) * N_H_C + head, nc, stride=16), :])
        bt = _dot(_split3(lf).astype(BF16), tri_ones[d])
        b, tot = bt[:, :L], bt[:, L:]
        r = li - b
        m_loc = tot + jnp.broadcast_to(jnp.max(r, axis=-1, keepdims=True), r.shape)
        rowq_scr[d, 0] = r
        rowq_scr[d, 1] = jnp.exp(tot + r - m_loc)
        rowq_scr[d, 2] = tot
        rowq_scr[d, 3] = m_loc
        bp_scr[d] = _split3(b + math.log(DQK_C ** -0.5))

    def prep(c, _):
        r0 = pl.multiple_of(c * L, L)
        q, k = q_ref[pl.ds(r0, L), :], k_ref[pl.ds(r0, L), :]
        s = _dot_nt(q, k)
        kt = k.astype(F32).T
        p = []
        for d in range(2):
            u = jnp.where(masks[d], rowq_scr[d, 0, pl.ds(c, 1), :], -jnp.inf)
            ct = jnp.broadcast_to(jnp.max(u, axis=-1, keepdims=True), (L, L))
            p.append((s * jnp.exp(u - ct)).astype(BF16))
            tile_scr[pl.ds(r0, L), (2 + d) * L:(3 + d) * L] = ct
            kw_scr[d, c] = (kt * rowq_scr[d, 1, pl.ds(c, 1), :]).astype(BF16)
        rhs = jnp.concatenate([jnp.broadcast_to(bp_scr[d, pl.ds(c, 1), :], (L, 3 * L)) for d in range(2)], axis=0)
        tile_scr[pl.ds(r0, L), 0:2 * L] = _dot_nt(eye3, rhs.astype(BF16))
        intra = _dot(jnp.concatenate(p, axis=0), load_v_ext(r0))
        hf_scr[pl.ds(r0, L), :] = intra[:L]
        hb_scr[pl.ds(r0, L), :] = intra[L:]
        return 0

    lax.fori_loop(0, nc, prep, 0, unroll=4)

    cf_scr[...] = jnp.zeros_like(cf_scr)
    cb_scr[...] = jnp.zeros_like(cb_scr)

    def one_dir(c, m_state, d):
        h_scr, c_scr = h_scrs[d], c_scrs[d]
        r0 = pl.multiple_of(c * L, L)
        tot, m_loc = rowq_scr[d, 2, pl.ds(c, 1), :], rowq_scr[d, 3, pl.ds(c, 1), :]
        bt = tile_scr[pl.ds(r0, L), d * L:(d + 1) * L]
        ct = tile_scr[pl.ds(r0, L), (2 + d) * L:(3 + d) * L]
        v_ext = load_v_ext(r0)
        c_state = c_scr[...]
        qc = _dot(q_ref[pl.ds(r0, L), :], c_state.astype(BF16))
        mx = jnp.maximum(m_state, ct)
        iw = jnp.exp(m_state - mx)
        xw = jnp.exp(ct - mx)
        num = [iw * qc[:, j * L:(j + 1) * L] + xw * h_scr[pl.ds(r0, L), j * L:(j + 1) * L] for j in range(2)]
        den = jnp.broadcast_to(num[1][:, ONES_COL - L:ONES_COL - L + 1], (L, L))
        inv = 1.0 / jnp.maximum(jnp.abs(den), jnp.exp(-(bt + mx)))
        h_scr[pl.ds(r0, L), 0:L] = num[0] * inv
        h_scr[pl.ds(r0, L), L:2 * L] = num[1] * inv
        c_loc = _dot(kw_scr[d, c], v_ext)
        m_new = jnp.maximum(tot + m_state, m_loc)
        sp, sl = jnp.exp(tot + m_state - m_new), jnp.exp(m_loc - m_new)
        c_scr[...] = jnp.concatenate(
            [sp * c_state[:, j * L:(j + 1) * L] + sl * c_loc[:, j * L:(j + 1) * L] for j in range(2)], axis=1)
        return m_new

    real = lax.broadcasted_iota(jnp.int32, (L, DV_PAD), 1) < DV_C

    def finish(c):
        r0 = pl.multiple_of(c * L, L)
        h = jnp.where(real, hf_scr[pl.ds(r0, L), :] + hb_scr[pl.ds(r0, L), :], 0.0)
        ms = jnp.sum(h * h, axis=-1, keepdims=True) * (1.0 / DV_C)
        hn = h * lax.rsqrt(ms + EPS) * hg_ref[...]
        y = jax.nn.sigmoid(o_ref[pl.ds(r0, L), :].astype(F32)) * hn
        y_ref[pl.ds(r0, L), :] = y.astype(BF16)

    def scan(c, carry):
        return one_dir(c, carry[0], 0), one_dir(nc - 1 - c, carry[1], 1)

    def scan_and_finish(c, carry):
        carry = scan(c, carry)
        finish(c)
        finish(nc - 1 - c)
        return carry

    neg = jnp.full((1, LANES), -jnp.inf, F32)
    carry = lax.fori_loop(0, nc // 2, scan, (neg, neg), unroll=4)
    lax.fori_loop(nc // 2, nc, scan_and_finish, carry, unroll=4)


def _mlstm(proj, gates, gate_b, head_g):
    bsz, s, _ = proj.shape
    nc = s // CHUNK
    return pl.pallas_call(
        _mlstm_kernel,
        grid=(bsz, N_H_C),
        in_specs=[
            pl.BlockSpec((None, s, DQK_PAD), lambda b, h: (b, 0, OFF_MQ // DQK_PAD + h)),
            pl.BlockSpec((None, s, DQK_PAD), lambda b, h: (b, 0, OFF_MK // DQK_PAD + h)),
            pl.BlockSpec((None, s, DV_PAD), lambda b, h: (b, 0, OFF_MV // DV_PAD + h)),
            pl.BlockSpec((None, s, DV_PAD), lambda b, h: (b, 0, OFF_MO // DV_PAD + h)),
            pl.BlockSpec((None, s, LANES), lambda b, h: (b, 0, 0)),
            pl.BlockSpec((1, LANES), lambda b, h: (0, 0)),
            pl.BlockSpec((None, 1, DV_PAD), lambda b, h: (h, 0, 0)),
        ],
        out_specs=pl.BlockSpec((None, s, DV_PAD), lambda b, h: (b, 0, h)),
        out_shape=jax.ShapeDtypeStruct((bsz, s, W_C_PAD), BF16),
        scratch_shapes=[
            pltpu.VMEM((s, DV_PAD), F32), pltpu.VMEM((s, DV_PAD), F32),
            pltpu.VMEM((s, 4 * CHUNK), F32),
            pltpu.VMEM((2, nc, DQK_PAD, CHUNK), BF16),
            pltpu.VMEM((2, 4, nc, CHUNK), F32),
            pltpu.VMEM((2, nc, 3 * CHUNK), F32),
            pltpu.VMEM((DQK_PAD, DV_PAD), F32), pltpu.VMEM((DQK_PAD, DV_PAD), F32),
            pltpu.VMEM((nc * 16, CHUNK), F32),
        ],
        compiler_params=_cparams(("parallel", "arbitrary")),
        name="mlstm",
    )(proj, proj, proj, proj, gates, gate_b, head_g)


def _out_kernel(ya_ref, yb_ref, yc_ref, wa_ref, wb_ref, wc_ref, pg_ref, gate_ref, x_ref, o_ref, y_scr):
    for r in range(0, o_ref.shape[0], SUB_ROWS):
        rows = slice(r, r + SUB_ROWS)
        y_scr[rows, :] = (_dot(ya_ref[rows, :], wa_ref[...]) + _dot(yb_ref[rows, :], wb_ref[...])
                          + _dot(yc_ref[rows, :], wc_ref[...]))
        _norm_residual_rows(y_scr, x_ref, pg_ref, gate_ref, o_ref, r, SUB_ROWS)


def _out_proj(ya, yb, yc, wa, wb, wc, pg, modl, x):
    bsz, s, d = x.shape
    tm = min(2 * SUB_ROWS, s)
    row = lambda w: pl.BlockSpec((None, tm, w), lambda b, i: (b, i, 0))
    const = lambda shape: pl.BlockSpec(shape, lambda b, i: (0,) * len(shape), pipeline_mode=pl.Buffered(1))
    return pl.pallas_call(
        _out_kernel,
        grid=(bsz, s // tm),
        in_specs=[
            row(W_A), row(W_B), row(W_C_PAD),
            const((W_A, d)), const((W_B, d)), const((W_C_PAD, d)), const((1, d)),
            pl.BlockSpec((None, 1, d), lambda b, i: (b, 0, 2)),
            row(d),
        ],
        out_specs=row(d),
        out_shape=jax.ShapeDtypeStruct((bsz, s, d), F32),
        scratch_shapes=[pltpu.VMEM((tm, d), F32)],
        compiler_params=_cparams(("parallel", "parallel")),
        name="out_proj",
    )(ya, yb, yc, wa, wb, wc, pg, modl, x)


FFN_TF = 512
FFN_VMEM_LIMIT = 62 * 1024 * 1024


def _ffn_kernel(x_ref, g_ref, sc_ref, sh_ref, wg_ref, wu_ref, wd_ref, pg_ref, gate_ref, o_ref, h_scr):
    j = pl.program_id(2)
    last = pl.num_programs(2) - 1
    tm = x_ref.shape[0]
    subs = [(r, slice(r, r + SUB_ROWS)) for r in range(0, tm, SUB_ROWS)]

    def partial_down(rows):
        h = h_scr[rows, :]
        a = _dot(h, wg_ref[...])
        u = _dot(h, wu_ref[...])
        hid = (a * jax.nn.sigmoid(a)) * u
        return _dot(hid.astype(BF16), wd_ref[...])

    @pl.when(j == 0)
    def _():
        for r, rows in subs:
            _mod_norm_rows(x_ref, g_ref, sc_ref, sh_ref, h_scr, r, SUB_ROWS)
            o_ref[rows, :] = partial_down(rows)

    @pl.when((j > 0) & (j < last))
    def _():
        o_ref[...] += partial_down(slice(0, tm))

    @pl.when(j == last)
    def _():
        for r, rows in subs:
            o_ref[rows, :] += partial_down(rows)
            _norm_residual_rows(o_ref, x_ref, pg_ref, gate_ref, o_ref, r, SUB_ROWS)


def _ffn(x, g, modl, wg, wu, wd, pg, layer):
    bsz, s, d = x.shape
    f = wg.shape[2]
    tm = min(1024, s)
    row = pl.BlockSpec((None, tm, d), lambda b, i, j: (b, i, 0))
    vec = pl.BlockSpec((1, d), lambda b, i, j: (0, 0))
    modv = lambda k: pl.BlockSpec((None, 1, d), lambda b, i, j: (b, 0, k))
    return pl.pallas_call(
        _ffn_kernel,
        grid=(bsz, s // tm, f // FFN_TF),
        in_specs=[
            row, vec, modv(4), modv(3),
            pl.BlockSpec((None, d, FFN_TF), lambda b, i, j: (layer, 0, j)),
            pl.BlockSpec((None, d, FFN_TF), lambda b, i, j: (layer, 0, j)),
            pl.BlockSpec((None, FFN_TF, d), lambda b, i, j: (layer, j, 0)),
            vec, modv(5),
        ],
        out_specs=row,
        out_shape=jax.ShapeDtypeStruct((bsz, s, d), F32),
        scratch_shapes=[pltpu.VMEM((tm, d), BF16)],
        compiler_params=_cparams(("parallel", "parallel", "arbitrary"), FFN_VMEM_LIMIT),
        name="ffn",
    )(x, g, modl, modl, wg, wu, wd, pg, modl)


CAST_BLOCK_BYTES = 4 * 1024 * 1024


def _cast_kernel(w_ref, o_ref):
    o_ref[...] = w_ref[...].astype(BF16)


def _to_bf16(w):
    depth, rows, cols = w.shape
    rb = rows
    while rb * cols * 4 > CAST_BLOCK_BYTES and rb % 16 == 0:
        rb //= 2
    spec = pl.BlockSpec((None, rb, cols), lambda l, i: (l, i, 0))
    return pl.pallas_call(
        _cast_kernel,
        grid=(depth, rows // rb),
        in_specs=[spec],
        out_specs=spec,
        out_shape=jax.ShapeDtypeStruct(w.shape, BF16),
        compiler_params=_cparams(("parallel", "parallel")),
        name="cast_bf16",
    )(w)


def _pair_heads_a(w, axis):
    shape = w.shape
    w = jnp.moveaxis(w, axis, -1)
    lead = w.shape[:-1]
    w = w.reshape(lead + (2, 2, GROUP_A, DH_A))
    w = jnp.swapaxes(w, -3, -2)
    w = w.reshape(lead + (W_A,))
    return jnp.moveaxis(w, -1, axis).reshape(shape)


def _pad_heads(w, n_heads, width, padded, axis):
    w = jnp.moveaxis(w, axis, -1)
    lead = w.shape[:-1]
    w = w.reshape(lead + (n_heads, width))
    w = jnp.pad(w, [(0, 0)] * len(lead) + [(0, 0), (0, padded - width)])
    w = w.reshape(lead + (n_heads * padded,))
    return jnp.moveaxis(w, -1, axis)


def _layout_w_in(w):
    d = w.shape[0]
    sizes = (W_A, N_KV_A * DH_A, N_KV_A * DH_A, Q_LORA, KV_LORA, D_ROPE,
             N_H_C * DQK_C, N_H_C * DQK_C, W_C, 4 * N_H_C, W_C)
    pts = np.cumsum(sizes)[:-1].tolist()
    aq, ak, av, bcq, bckv, bkr, cq, ck, cv, cg, co = jnp.split(w, pts, axis=1)
    parts = [
        _pair_heads_a(aq, 1), ak, av, bckv,
        jnp.pad(cg, ((0, 0), (0, LANES - 4 * N_H_C))),
        bcq, bkr,
        _pad_heads(cq, N_H_C, DQK_C, DQK_PAD, 1), _pad_heads(ck, N_H_C, DQK_C, DQK_PAD, 1),
        _pad_heads(cv, N_H_C, DV_C, DV_PAD, 1), _pad_heads(co, N_H_C, DV_C, DV_PAD, 1),
    ]
    out = jnp.concatenate(parts, axis=1).astype(BF16)
    assert out.shape == (d, IN_WIDTH_PAD)
    return out.reshape(d, IN_WIDTH_PAD // IN_TN, IN_TN).transpose(1, 0, 2)


def _layout_w_uq(w):
    w = w.reshape(Q_LORA, N_H_B, D_NOPE + D_ROPE)
    nope = w[:, :, :D_NOPE].reshape(Q_LORA, N_H_B * D_NOPE)
    rope = w[:, :, D_NOPE:].reshape(Q_LORA, N_H_B * D_ROPE)
    w = jnp.concatenate([nope, rope], axis=1)
    return jnp.pad(w, ((0, 512 - Q_LORA), (0, 0))).astype(BF16)


def _layout_w_ukv(w):
    w = w.reshape(KV_LORA, N_H_B, D_NOPE + D_V_B)
    kn = w[:, :, :D_NOPE].reshape(KV_LORA, N_H_B * D_NOPE)
    v = w[:, :, D_NOPE:].reshape(KV_LORA, N_H_B * D_V_B)
    return kn.astype(BF16), v.T.astype(BF16)


def kernel(x, c, positions, mod_w, mod_b, pre_mix_g, post_mix_g, pre_ffn_g, post_ffn_g, w_in, attn_sink, mla_q_norm_g, mla_w_uq, mla_kv_norm_g, mla_w_ukv, mlstm_gate_b, mlstm_head_g, w_out, ffn_w_gate, ffn_w_up, ffn_w_down):
    depth = mod_w.shape[0]
    bsz, s, d = x.shape
    mod = _modulation(c, mod_w, mod_b)
    pos3 = positions.reshape(bsz, 1, s)
    inv = 1.0 / (ROPE_THETA ** (jnp.arange(0, D_ROPE, 2, dtype=F32) / D_ROPE))
    inv_col = inv.reshape(D_ROPE // 2, 1)
    wg_bf, wu_bf, wd_bf = _to_bf16(ffn_w_gate), _to_bf16(ffn_w_up), _to_bf16(ffn_w_down)
    for l in range(depth):
        modl = mod[l].reshape(bsz, 1, 6 * d)
        proj, gates = _in_proj(x, pre_mix_g[l].reshape(1, d), modl, _layout_w_in(w_in[l]))
        ya = _window_attention(proj, attn_sink[l])
        qg = jnp.pad(mla_q_norm_g[l], (0, 512 - Q_LORA)).reshape(1, 512)
        wk, wvt = _layout_w_ukv(mla_w_ukv[l])
        qb, kb, vtb = _mla_pre(proj, pos3, qg, _layout_w_uq(mla_w_uq[l]),
                               mla_kv_norm_g[l].reshape(1, KV_LORA), wk, wvt, inv_col)
        yb = _mla_attention(qb, kb, vtb)
        gate_b = jnp.pad(mlstm_gate_b[l], (0, LANES - 4 * N_H_C)).reshape(1, LANES)
        head_g = jnp.pad(mlstm_head_g[l].reshape(N_H_C, 1, DV_C), ((0, 0), (0, 0), (0, DV_PAD - DV_C)))
        yc = _mlstm(proj, gates, gate_b, head_g)
        wo = w_out[l]
        wa = _pair_heads_a(wo[:W_A], 0).astype(BF16)
        wb = wo[W_A:W_A + W_B].astype(BF16)
        wc = _pad_heads(wo[W_A + W_B:], N_H_C, DV_C, DV_PAD, 0).astype(BF16)
        x = _out_proj(ya, yb, yc, wa, wb, wc, post_mix_g[l].reshape(1, d), modl, x)
        x = _ffn(x, pre_ffn_g[l].reshape(1, d), modl, wg_bf, wu_bf, wd_bf, post_ffn_g[l].reshape(1, d), l)
    return x
```

```python
import functools
import math

import numpy as np
import jax
import jax.numpy as jnp
from jax import lax
from jax.experimental import pallas as pl
from jax.experimental.pallas import tpu as pltpu

F32 = jnp.float32
BF16 = jnp.bfloat16

D_MODEL = 2048
EPS = 1e-6
N_Q_A, N_KV_A, DH_A, GROUP_A = 12, 4, 64, 3
WINDOW = 128
BLOCK_A = 128
N_H_B, Q_LORA, KV_LORA, D_NOPE, D_ROPE, D_V_B = 4, 448, 128, 128, 64, 128
ROPE_THETA = 10000.0
N_H_C, DQK_C, DV_C = 4, 96, 192
W_A, W_B, W_C = N_Q_A * DH_A, N_H_B * D_V_B, N_H_C * DV_C
D_FF = 5632

LANES = 128
DQK_PAD = 128
DV_PAD = 256
ONES_COL = DV_C
VMEM_LIMIT = 56 * 1024 * 1024

OFF_AQ, OFF_AK, OFF_AV = 0, 768, 1024
OFF_CKV, OFF_CG, OFF_CQKR = 1280, 1408, 1536
OFF_MQ, OFF_MK, OFF_MV, OFF_MO = 2048, 2560, 3072, 4096
IN_WIDTH_PAD = 5120
W_C_PAD = N_H_C * DV_PAD
MIX_PAD = W_A + W_B + W_C_PAD

CHUNK = 128


def _alibi_slopes(n):
    def pow2(m):
        start = 2.0 ** (-8.0 / m)
        return [start ** (i + 1) for i in range(m)]
    if math.log2(n).is_integer():
        s = pow2(n)
    else:
        p = 2 ** math.floor(math.log2(n))
        s = pow2(p) + pow2(2 * p)[0::2][: n - p]
    return [float(np.float32(v)) for v in s]


SLOPES_A = _alibi_slopes(N_Q_A)


def _cparams(sem, vmem_limit=VMEM_LIMIT):
    return pltpu.CompilerParams(dimension_semantics=sem, vmem_limit_bytes=vmem_limit)


def _dot(a, b):
    return jnp.dot(a, b, preferred_element_type=F32)


def _dot_nt(a, b):
    return lax.dot_general(a, b, (((1,), (1,)), ((), ())), preferred_element_type=F32)


def _dot_tn(a, b):
    return lax.dot_general(a, b, (((0,), (0,)), ((), ())), preferred_element_type=F32)


def _mod_kernel(c_ref, w_ref, b_ref, o_ref):
    c = c_ref[...]
    s = c * jax.nn.sigmoid(c)
    o_ref[...] = _dot(s.astype(BF16), w_ref[...].astype(BF16)) + b_ref[...]


def _modulation(c, mod_w, mod_b):
    depth, d, n = mod_w.shape
    bsz = c.shape[0]
    tn = 1024
    return pl.pallas_call(
        _mod_kernel,
        grid=(depth, n // tn),
        in_specs=[
            pl.BlockSpec((bsz, d), lambda l, j: (0, 0)),
            pl.BlockSpec((None, d, tn), lambda l, j: (l, 0, j)),
            pl.BlockSpec((None, 1, tn), lambda l, j: (l, 0, j)),
        ],
        out_specs=pl.BlockSpec((None, bsz, tn), lambda l, j: (l, 0, j)),
        out_shape=jax.ShapeDtypeStruct((depth, bsz, n), F32),
        compiler_params=_cparams(("parallel", "parallel")),
        name="adaln_mod",
    )(c, mod_w, mod_b.reshape(depth, 1, n))


ROW_SLAB = 16
SUB_ROWS = 256


def _row_slabs(first_row, n_rows):
    return [slice(r, r + ROW_SLAB) for r in range(first_row, first_row + n_rows, ROW_SLAB)]


def _mod_norm_rows(x_ref, g_ref, sc_ref, sh_ref, h_ref, first_row, n_rows):
    gain = g_ref[...] * (1.0 + sc_ref[...])
    shift = sh_ref[...]
    for rows in _row_slabs(first_row, n_rows):
        x = x_ref[rows, :]
        ms = jnp.mean(x * x, axis=-1, keepdims=True)
        h_ref[rows, :] = (x * lax.rsqrt(ms + EPS) * gain + shift).astype(BF16)


def _norm_residual_rows(y_ref, x_ref, pg_ref, gate_ref, o_ref, first_row, n_rows):
    gain = gate_ref[...] * pg_ref[...]
    for rows in _row_slabs(first_row, n_rows):
        y = y_ref[rows, :]
        ms = jnp.mean(y * y, axis=-1, keepdims=True)
        o_ref[rows, :] = x_ref[rows, :] + y * lax.rsqrt(ms + EPS) * gain


IN_TN = 1024
GATE_TILE = OFF_CG // IN_TN
GATE_OFF = OFF_CG % IN_TN


def _in_kernel(x_ref, g_ref, sc_ref, sh_ref, w_ref, cs_ref, o_ref, gate_ref, h_scr):
    j = pl.program_id(2)
    tm = x_ref.shape[0]

    def project(rows):
        acc = _dot(h_scr[rows, :], w_ref[j])
        o_ref[rows, :] = (acc * cs_ref[...]).astype(BF16)
        return acc

    @pl.when(j == 0)
    def _():
        for r in range(0, tm, SUB_ROWS):
            _mod_norm_rows(x_ref, g_ref, sc_ref, sh_ref, h_scr, r, SUB_ROWS)
            project(slice(r, r + SUB_ROWS))

    @pl.when(j != 0)
    def _():
        acc = project(slice(0, tm))

        @pl.when(j == GATE_TILE)
        def _():
            gate_ref[...] = acc[:, GATE_OFF:GATE_OFF + LANES]


def _in_proj(x, g, modl, w):
    bsz, s, d = x.shape
    n = w.shape[0] * IN_TN
    tm = min(1024, s)
    col_scale = jnp.where(jnp.arange(n) < W_A, WIN_QSCALE, 1.0).astype(F32).reshape(1, n)
    return pl.pallas_call(
        _in_kernel,
        grid=(bsz, s // tm, n // IN_TN),
        in_specs=[
            pl.BlockSpec((None, tm, d), lambda b, i, j: (b, i, 0)),
            pl.BlockSpec((1, d), lambda b, i, j: (0, 0)),
            pl.BlockSpec((None, 1, d), lambda b, i, j: (b, 0, 1)),
            pl.BlockSpec((None, 1, d), lambda b, i, j: (b, 0, 0)),
            pl.BlockSpec((n // IN_TN, d, IN_TN), lambda b, i, j: (0, 0, 0), pipeline_mode=pl.Buffered(1)),
            pl.BlockSpec((1, IN_TN), lambda b, i, j: (0, j)),
        ],
        out_specs=[
            pl.BlockSpec((None, tm, IN_TN), lambda b, i, j: (b, i, j)),
            pl.BlockSpec((None, tm, LANES), lambda b, i, j: (b, i, 0)),
        ],
        out_shape=[
            jax.ShapeDtypeStruct((bsz, s, n), BF16),
            jax.ShapeDtypeStruct((bsz, s, LANES), F32),
        ],
        scratch_shapes=[pltpu.VMEM((tm, d), BF16)],
        compiler_params=_cparams(("parallel", "parallel", "arbitrary")),
        name="in_proj",
    )(x, g, modl, modl, w, col_scale)


LOG2E = math.log2(math.e)
WIN_QSCALE = DH_A ** -0.5 * LOG2E


WIN_QB = 4


def _win_kernel(sink_ref, q_ref, kp_ref, kc_ref, kn_ref, vp_ref, vc_ref, vn_ref, o_ref, *, nb):
    n = pl.program_id(1)
    blk = BLOCK_A
    nk = 3 * blk
    qi = lax.broadcasted_iota(jnp.int32, (blk, nk), 0)
    kj = lax.broadcasted_iota(jnp.int32, (blk, nk), 1)
    dist = jnp.abs(qi - kj + blk)
    k_all = jnp.concatenate([kp_ref[...], kc_ref[...], kn_ref[...]], axis=0)
    v_all = jnp.concatenate([vp_ref[...], vc_ref[...], vn_ref[...]], axis=0)
    low_k = lax.broadcasted_iota(jnp.int32, (nk, LANES), 1) < DH_A
    low_q = lax.broadcasted_iota(jnp.int32, (blk, LANES), 1) < DH_A
    zero = jnp.zeros((nk, LANES), BF16)
    for sub in range(WIN_QB):
        g = n * WIN_QB + sub
        rows = slice(sub * blk, (sub + 1) * blk)
        in_seq = ((kj >= blk) | (g > 0)) & ((kj < 2 * blk) | (g < nb - 1))
        dist_masked = jnp.where((dist <= WINDOW) & in_seq, dist.astype(F32), jnp.inf)
        for p in range(2):
            kt = k_all[sub * blk:sub * blk + nk, p * LANES:(p + 1) * LANES]
            vt = v_all[sub * blk:sub * blk + nk, p * LANES:(p + 1) * LANES]
            k_big = jnp.concatenate([jnp.where(low_k, kt, zero), jnp.where(low_k, zero, kt)], axis=0)
            v_big = jnp.concatenate([jnp.where(low_k, vt, zero), jnp.where(low_k, zero, vt)], axis=0)
            q3 = jnp.concatenate([q_ref[rows, (p * GROUP_A + a) * LANES:(p * GROUP_A + a + 1) * LANES]
                                  for a in range(GROUP_A)], axis=0)
            s3 = _dot_nt(q3, k_big)
            p_rows, inv_rows = [], []
            for a in range(GROUP_A):
                p_halves, inv_halves = [], []
                for half in range(2):
                    head = (2 * p + half) * GROUP_A + a
                    s = (s3[a * blk:(a + 1) * blk, half * nk:(half + 1) * nk]
                         - (SLOPES_A[head] * LOG2E) * dist_masked)
                    sink = sink_ref[head] * LOG2E
                    m = jnp.maximum(jnp.max(s, axis=-1, keepdims=True), sink)
                    e = jnp.exp2(s - m)
                    l = jnp.sum(e, axis=-1, keepdims=True) + jnp.exp2(sink - m)
                    p_halves.append(e.astype(BF16))
                    inv_halves.append(1.0 / l)
                p_rows.append(jnp.concatenate(p_halves, axis=1))
                inv_rows.append(jnp.where(low_q, inv_halves[0], inv_halves[1]))
            o3 = _dot(jnp.concatenate(p_rows, axis=0), v_big)
            for a in range(GROUP_A):
                t = p * GROUP_A + a
                o_ref[rows, t * LANES:(t + 1) * LANES] = (o3[a * blk:(a + 1) * blk] * inv_rows[a]).astype(BF16)


def _window_attention(proj, sinks):
    bsz, s, _ = proj.shape
    blk = BLOCK_A
    nb = s // blk
    assert nb % WIN_QB == 0
    tq = WIN_QB * blk
    kb, vb = OFF_AK // 256, OFF_AV // 256
    prev = lambda n: jnp.maximum(n * WIN_QB - 1, 0)
    nxt = lambda n: jnp.minimum((n + 1) * WIN_QB, nb - 1)
    return pl.pallas_call(
        functools.partial(_win_kernel, nb=nb),
        grid=(bsz, nb // WIN_QB),
        in_specs=[
            pl.BlockSpec(memory_space=pltpu.SMEM),
            pl.BlockSpec((None, tq, W_A), lambda b, n: (b, n, 0)),
            pl.BlockSpec((None, blk, 256), lambda b, n: (b, prev(n), kb)),
            pl.BlockSpec((None, tq, 256), lambda b, n: (b, n, kb)),
            pl.BlockSpec((None, blk, 256), lambda b, n: (b, nxt(n), kb)),
            pl.BlockSpec((None, blk, 256), lambda b, n: (b, prev(n), vb)),
            pl.BlockSpec((None, tq, 256), lambda b, n: (b, n, vb)),
            pl.BlockSpec((None, blk, 256), lambda b, n: (b, nxt(n), vb)),
        ],
        out_specs=pl.BlockSpec((None, tq, W_A), lambda b, n: (b, n, 0)),
        out_shape=jax.ShapeDtypeStruct((bsz, s, W_A), BF16),
        compiler_params=_cparams(("parallel", "parallel")),
        name="win_attn",
    )(sinks, proj, proj, proj, proj, proj, proj, proj)


MLA_TK = 512
MLA_QSCALE = (D_NOPE + D_ROPE) ** -0.5 * math.log2(math.e)


def _mla_pre_kernel(cqkr_ref, ckv_ref, pos_ref, qg_ref, wq_ref, kvg_ref, wk_ref, wvt_ref, inv_ref,
                    qt_ref, k_ref, vt_ref):
    tm = cqkr_ref.shape[0]
    t = cqkr_ref[...].astype(F32)
    lane = lax.broadcasted_iota(jnp.int32, (tm, 4 * LANES), 1)
    cq = jnp.where(lane < Q_LORA, t, 0.0)
    ms = jnp.sum(cq * cq, axis=-1, keepdims=True) * (1.0 / Q_LORA)
    qn = cq * lax.rsqrt(ms + EPS) * qg_ref[...]
    q = _dot(qn.astype(BF16), wq_ref[...]) * MLA_QSCALE
    c = ckv_ref[...].astype(F32)
    ms = jnp.mean(c * c, axis=-1, keepdims=True)
    kvn = (c * lax.rsqrt(ms + EPS) * kvg_ref[...]).astype(BF16)
    kn = _dot(kvn, wk_ref[...])
    vt_ref[...] = _dot_nt(wvt_ref[...], kvn).astype(BF16)

    ang_t = inv_ref[...] * pos_ref[...].astype(F32)
    cs_t, sn_t = jnp.cos(ang_t), jnp.sin(ang_t)
    cs = jnp.concatenate([cs_t] * 4, axis=0).T
    sgn_sn = jnp.concatenate([-sn_t, sn_t] * 2, axis=0).T
    l128 = lax.broadcasted_iota(jnp.int32, (tm, LANES), 1)
    first = (l128 % D_ROPE) < (D_ROPE // 2)

    def rope(x):
        swapped = jnp.where(first, pltpu.roll(x, LANES - D_ROPE // 2, 1), pltpu.roll(x, D_ROPE // 2, 1))
        return x * cs + swapped * sgn_sn

    kr = rope(t[:, 3 * LANES:4 * LANES])
    kr_hi = jnp.where(l128 >= D_ROPE, kr, 0.0)
    kr_lo = pltpu.roll(kr_hi, D_ROPE, 1)
    kr_tiles = (kr_lo.astype(BF16), kr_hi.astype(BF16))
    for pair in range(2):
        qr_t = rope(q[:, (4 + pair) * LANES:(5 + pair) * LANES]).T.astype(BF16)
        for half in range(2):
            h = 2 * pair + half
            qt_ref[(2 * h) * LANES:(2 * h + 1) * LANES, :] = q[:, h * LANES:(h + 1) * LANES].T.astype(BF16)
            qt_ref[(2 * h + 1) * LANES:(2 * h + 2) * LANES, :] = qr_t
            k_ref[:, (2 * h) * LANES:(2 * h + 1) * LANES] = kn[:, h * LANES:(h + 1) * LANES].astype(BF16)
            k_ref[:, (2 * h + 1) * LANES:(2 * h + 2) * LANES] = kr_tiles[half]


def _mla_pre(proj, pos3, qg, wq, kvg, wk, wvt, inv):
    bsz, s, _ = proj.shape
    tm = MLA_TK
    const = lambda shape: pl.BlockSpec(shape, lambda b, i: (0,) * len(shape))
    return pl.pallas_call(
        _mla_pre_kernel,
        grid=(bsz, s // tm),
        in_specs=[
            pl.BlockSpec((None, tm, 512), lambda b, i: (b, i, OFF_CQKR // 512)),
            pl.BlockSpec((None, tm, LANES), lambda b, i: (b, i, OFF_CKV // LANES)),
            pl.BlockSpec((None, 1, tm), lambda b, i: (b, 0, i)),
            const((1, 512)), const((512, 768)), const((1, LANES)), const((LANES, W_B)), const((W_B, LANES)),
            const((D_ROPE // 2, 1)),
        ],
        out_specs=[
            pl.BlockSpec((None, None, 1024, tm), lambda b, i: (b, i, 0, 0)),
            pl.BlockSpec((None, tm, 1024), lambda b, i: (b, i, 0)),
            pl.BlockSpec((None, None, W_B, tm), lambda b, i: (b, i, 0, 0)),
        ],
        out_shape=[
            jax.ShapeDtypeStruct((bsz, s // tm, 1024, tm), BF16),
            jax.ShapeDtypeStruct((bsz, s, 1024), BF16),
            jax.ShapeDtypeStruct((bsz, s // tm, W_B, tm), BF16),
        ],
        compiler_params=_cparams(("parallel", "parallel")),
        name="mla_pre",
    )(proj, proj, pos3, qg, wq, kvg, wk, wvt, inv)


def _mla_attn_kernel(qt_ref, k_ref, vt_ref, o_ref, st_a, st_b):
    n_chunks, _, tk = vt_ref.shape
    nt, _, tq = qt_ref.shape

    def scores(t, st_scr):
        qt = qt_ref[t]
        m8 = jnp.full((8, tq), -jnp.inf, F32)
        for c in range(n_chunks):
            st = _dot(k_ref[c * tk:(c + 1) * tk, :], qt)
            st_scr[c * tk:(c + 1) * tk, :] = st
            m8 = jnp.maximum(m8, jnp.max(st.reshape(tk // 8, 8, tq), axis=0))
        return jnp.max(m8, axis=0, keepdims=True)

    def output(t, st_scr, m):
        l8 = jnp.zeros((8, tq), F32)
        acc = jnp.zeros((D_V_B, tq), F32)
        for c in range(n_chunks):
            p = jnp.exp2(st_scr[c * tk:(c + 1) * tk, :] - m)
            l8 = l8 + jnp.sum(p.reshape(tk // 8, 8, tq), axis=0)
            acc = acc + _dot(vt_ref[c], p.astype(BF16))
        l = jnp.sum(l8, axis=0, keepdims=True)
        o_ref[pl.ds(pl.multiple_of(t * tq, tq), tq), :] = (acc / l).T.astype(BF16)

    m_first = scores(0, st_a)
    if nt == 1:
        output(0, st_a, m_first)
        return

    def pair(u, m_a):
        t = 2 * u
        m_b = scores(t + 1, st_b)
        output(t, st_a, m_a)
        m_a = scores(t + 2, st_a)
        output(t + 1, st_b, m_b)
        return m_a

    m_a = lax.fori_loop(0, nt // 2 - 1, pair, m_first)
    m_b = scores(nt - 1, st_b)
    output(nt - 2, st_a, m_a)
    output(nt - 1, st_b, m_b)


def _mla_attention(qt, k, vt):
    bsz, s, _ = k.shape
    n_chunks, tk = vt.shape[1], vt.shape[3]
    nt, tq = qt.shape[1], qt.shape[3]
    assert nt == 1 or nt % 2 == 0
    return pl.pallas_call(
        _mla_attn_kernel,
        grid=(bsz, N_H_B),
        in_specs=[
            pl.BlockSpec((None, nt, 256, tq), lambda b, h: (b, 0, h, 0)),
            pl.BlockSpec((None, s, 256), lambda b, h: (b, 0, h)),
            pl.BlockSpec((None, n_chunks, D_V_B, tk), lambda b, h: (b, 0, h, 0)),
        ],
        out_specs=pl.BlockSpec((None, s, D_V_B), lambda b, h: (b, 0, h)),
        out_shape=jax.ShapeDtypeStruct((bsz, s, W_B), BF16),
        scratch_shapes=[pltpu.VMEM((s, tq), F32), pltpu.VMEM((s, tq), F32)],
        compiler_params=_cparams(("parallel", "parallel")),
        name="mla_attn",
    )(qt, k, vt)


def _log_sigmoid(x):
    return jnp.minimum(x, 0.0) - jnp.log1p(jnp.exp(-jnp.abs(x)))


def _split3(x):
    hi = x.astype(BF16).astype(F32)
    r1 = x - hi
    mid = r1.astype(BF16).astype(F32)
    return jnp.concatenate([hi, mid, r1 - mid], axis=-1)


def _mlstm_kernel(q_ref, k_ref, v_ref, o_ref, gate_ref, gb_ref, hg_ref, y_ref,
                  hf_scr, hb_scr, tile_scr, kw_scr, rowq_scr, bp_scr, cf_scr, cb_scr, gt_scr):
    s_len = q_ref.shape[0]
    L = CHUNK
    nc = s_len // L
    head = pl.program_id(1)
    ii = lax.broadcasted_iota(jnp.int32, (L, L), 0)
    jj = lax.broadcasted_iota(jnp.int32, (L, L), 1)
    ones = jnp.ones((L, L), BF16)
    ones_lane = lax.broadcasted_iota(jnp.int32, (L, DV_PAD), 1) == ONES_COL
    eye3 = jnp.concatenate([(ii == jj).astype(BF16)] * 3, axis=1)
    tri_ones = [jnp.concatenate([jnp.concatenate([m.astype(BF16), ones], axis=1)] * 3, axis=0)
                for m in (ii <= jj, ii >= jj)]
    masks = (jj <= ii, jj >= ii)
    h_scrs = (hf_scr, hb_scr)
    c_scrs = (cf_scr, cb_scr)

    def load_v_ext(r0):
        return jnp.where(ones_lane, jnp.ones((L, DV_PAD), BF16), v_ref[pl.ds(r0, L), :])

    def transpose_gates(c, _):
        g = gate_ref[pl.ds(pl.multiple_of(c * L, L), L), :] + gb_ref[...]
        gt_scr[pl.ds(pl.multiple_of(c * 16, 16), 16), :] = g.T[0:16, :]
        return 0

    lax.fori_loop(0, nc, transpose_gates, 0, unroll=4)

    for d in range(2):
        li = gt_scr[pl.ds(d * N_H_C + head, nc, stride=16), :]
        lf = _log_sigmoid(gt_scr[pl.ds((2 + d) * N_H_C + head, nc, stride=16), :])
        bt = _dot(_split3(lf).astype(BF16), tri_ones[d])
        b, tot = bt[:, :L], bt[:, L:]
        r = li - b
        m_loc = tot + jnp.broadcast_to(jnp.max(r, axis=-1, keepdims=True), r.shape)
        rowq_scr[d, 0] = r
        rowq_scr[d, 1] = jnp.exp(tot + r - m_loc)
        rowq_scr[d, 2] = tot
        rowq_scr[d, 3] = m_loc
        bp_scr[d] = _split3(b + math.log(DQK_C ** -0.5))

    def prep(c, _):
        r0 = pl.multiple_of(c * L, L)
        q, k = q_ref[pl.ds(r0, L), :], k_ref[pl.ds(r0, L), :]
        s = _dot_nt(q, k)
        kt = k.astype(F32).T
        p = []
        for d in range(2):
            u = jnp.where(masks[d], rowq_scr[d, 0, pl.ds(c, 1), :], -jnp.inf)
            ct = jnp.broadcast_to(jnp.max(u, axis=-1, keepdims=True), (L, L))
            p.append((s * jnp.exp(u - ct)).astype(BF16))
            tile_scr[pl.ds(r0, L), (2 + d) * L:(3 + d) * L] = ct
            kw_scr[d, c] = (kt * rowq_scr[d, 1, pl.ds(c, 1), :]).astype(BF16)
        rhs = jnp.concatenate([jnp.broadcast_to(bp_scr[d, pl.ds(c, 1), :], (L, 3 * L)) for d in range(2)], axis=0)
        tile_scr[pl.ds(r0, L), 0:2 * L] = _dot_nt(eye3, rhs.astype(BF16))
        intra = _dot(jnp.concatenate(p, axis=0), load_v_ext(r0))
        hf_scr[pl.ds(r0, L), :] = intra[:L]
        hb_scr[pl.ds(r0, L), :] = intra[L:]
        return 0

    lax.fori_loop(0, nc, prep, 0, unroll=4)

    cf_scr[...] = jnp.zeros_like(cf_scr)
    cb_scr[...] = jnp.zeros_like(cb_scr)

    def one_dir(c, m_state, d):
        h_scr, c_scr = h_scrs[d], c_scrs[d]
        r0 = pl.multiple_of(c * L, L)
        tot, m_loc = rowq_scr[d, 2, pl.ds(c, 1), :], rowq_scr[d, 3, pl.ds(c, 1), :]
        bt = tile_scr[pl.ds(r0, L), d * L:(d + 1) * L]
        ct = tile_scr[pl.ds(r0, L), (2 + d) * L:(3 + d) * L]
        v_ext = load_v_ext(r0)
        c_state = c_scr[...]
        qc = _dot(q_ref[pl.ds(r0, L), :], c_state.astype(BF16))
        mx = jnp.maximum(m_state, ct)
        iw = jnp.exp(m_state - mx)
        xw = jnp.exp(ct - mx)
        num = [iw * qc[:, j * L:(j + 1) * L] + xw * h_scr[pl.ds(r0, L), j * L:(j + 1) * L] for j in range(2)]
        den = jnp.broadcast_to(num[1][:, ONES_COL - L:ONES_COL - L + 1], (L, L))
        inv = 1.0 / jnp.maximum(jnp.abs(den), jnp.exp(-(bt + mx)))
        h_scr[pl.ds(r0, L), 0:L] = num[0] * inv
        h_scr[pl.ds(r0, L), L:2 * L] = num[1] * inv
        c_loc = _dot(kw_scr[d, c], v_ext)
        m_new = jnp.maximum(tot + m_state, m_loc)
        sp, sl = jnp.exp(tot + m_state - m_new), jnp.exp(m_loc - m_new)
        c_scr[...] = jnp.concatenate(
            [sp * c_state[:, j * L:(j + 1) * L] + sl * c_loc[:, j * L:(j + 1) * L] for j in range(2)], axis=1)
        return m_new

    real = lax.broadcasted_iota(jnp.int32, (L, DV_PAD), 1) < DV_C

    def finish(c):
        r0 = pl.multiple_of(c * L, L)
        h = jnp.where(real, hf_scr[pl.ds(r0, L), :] + hb_scr[pl.ds(r0, L), :], 0.0)
        ms = jnp.sum(h * h, axis=-1, keepdims=True) * (1.0 / DV_C)
        hn = h * lax.rsqrt(ms + EPS) * hg_ref[...]
        y = jax.nn.sigmoid(o_ref[pl.ds(r0, L), :].astype(F32)) * hn
        y_ref[pl.ds(r0, L), :] = y.astype(BF16)

    def scan(c, carry):
        return one_dir(c, carry[0], 0), one_dir(nc - 1 - c, carry[1], 1)

    def scan_and_finish(c, carry):
        carry = scan(c, carry)
        finish(c)
        finish(nc - 1 - c)
        return carry

    neg = jnp.full((1, LANES), -jnp.inf, F32)
    carry = lax.fori_loop(0, nc // 2, scan, (neg, neg), unroll=4)
    lax.fori_loop(nc // 2, nc, scan_and_finish, carry, unroll=4)


def _mlstm(proj, gates, gate_b, head_g):
    bsz, s, _ = proj.shape
    nc = s // CHUNK
    return pl.pallas_call(
        _mlstm_kernel,
        grid=(bsz, N_H_C),
        in_specs=[
            pl.BlockSpec((None, s, DQK_PAD), lambda b, h: (b, 0, OFF_MQ // DQK_PAD + h)),
            pl.BlockSpec((None, s, DQK_PAD), lambda b, h: (b, 0, OFF_MK // DQK_PAD + h)),
            pl.BlockSpec((None, s, DV_PAD), lambda b, h: (b, 0, OFF_MV // DV_PAD + h)),
            pl.BlockSpec((None, s, DV_PAD), lambda b, h: (b, 0, OFF_MO // DV_PAD + h)),
            pl.BlockSpec((None, s, LANES), lambda b, h: (b, 0, 0)),
            pl.BlockSpec((1, LANES), lambda b, h: (0, 0)),
            pl.BlockSpec((None, 1, DV_PAD), lambda b, h: (h, 0, 0)),
        ],
        out_specs=pl.BlockSpec((None, s, DV_PAD), lambda b, h: (b, 0, h)),
        out_shape=jax.ShapeDtypeStruct((bsz, s, W_C_PAD), BF16),
        scratch_shapes=[
            pltpu.VMEM((s, DV_PAD), F32), pltpu.VMEM((s, DV_PAD), F32),
            pltpu.VMEM((s, 4 * CHUNK), F32),
            pltpu.VMEM((2, nc, DQK_PAD, CHUNK), BF16),
            pltpu.VMEM((2, 4, nc, CHUNK), F32),
            pltpu.VMEM((2, nc, 3 * CHUNK), F32),
            pltpu.VMEM((DQK_PAD, DV_PAD), F32), pltpu.VMEM((DQK_PAD, DV_PAD), F32),
            pltpu.VMEM((nc * 16, CHUNK), F32),
        ],
        compiler_params=_cparams(("parallel", "arbitrary")),
        name="mlstm",
    )(proj, proj, proj, proj, gates, gate_b, head_g)


def _out_kernel(ya_ref, yb_ref, yc_ref, wa_ref, wb_ref, wc_ref, pg_ref, gate_ref, x_ref, o_ref, y_scr):
    for r in range(0, o_ref.shape[0], SUB_ROWS):
        rows = slice(r, r + SUB_ROWS)
        y_scr[rows, :] = (_dot(ya_ref[rows, :], wa_ref[...]) + _dot(yb_ref[rows, :], wb_ref[...])
                          + _dot(yc_ref[rows, :], wc_ref[...]))
        _norm_residual_rows(y_scr, x_ref, pg_ref, gate_ref, o_ref, r, SUB_ROWS)


def _out_proj(ya, yb, yc, wa, wb, wc, pg, modl, x):
    bsz, s, d = x.shape
    tm = min(2 * SUB_ROWS, s)
    row = lambda w: pl.BlockSpec((None, tm, w), lambda b, i: (b, i, 0))
    const = lambda shape: pl.BlockSpec(shape, lambda b, i: (0,) * len(shape), pipeline_mode=pl.Buffered(1))
    return pl.pallas_call(
        _out_kernel,
        grid=(bsz, s // tm),
        in_specs=[
            row(W_A), row(W_B), row(W_C_PAD),
            const((W_A, d)), const((W_B, d)), const((W_C_PAD, d)), const((1, d)),
            pl.BlockSpec((None, 1, d), lambda b, i: (b, 0, 2)),
            row(d),
        ],
        out_specs=row(d),
        out_shape=jax.ShapeDtypeStruct((bsz, s, d), F32),
        scratch_shapes=[pltpu.VMEM((tm, d), F32)],
        compiler_params=_cparams(("parallel", "parallel")),
        name="out_proj",
    )(ya, yb, yc, wa, wb, wc, pg, modl, x)


FFN_TF = 512
FFN_VMEM_LIMIT = 62 * 1024 * 1024


def _ffn_kernel(x_ref, g_ref, sc_ref, sh_ref, wg_hbm, wu_hbm, wd_hbm, pg_ref, gate_ref, o_ref,
                h_scr, wg_buf, wu_buf, wd_buf, sem, *, layer, nf):
    tm = x_ref.shape[0]
    tf = wg_buf.shape[2]
    step = pl.program_id(0) * pl.num_programs(1) + pl.program_id(1)
    n_steps = pl.num_programs(0) * pl.num_programs(1)
    subs = [(r, slice(r, r + SUB_ROWS)) for r in range(0, tm, SUB_ROWS)]

    def slot_of(k):
        return (k + step) % 2

    def copies(k, slot):
        cols = pl.ds(pl.multiple_of(k * tf, tf), tf)
        return (pltpu.make_async_copy(wg_hbm.at[layer, :, cols], wg_buf.at[slot], sem.at[0, slot]),
                pltpu.make_async_copy(wu_hbm.at[layer, :, cols], wu_buf.at[slot], sem.at[1, slot]),
                pltpu.make_async_copy(wd_hbm.at[layer, cols, :], wd_buf.at[slot], sem.at[2, slot]))

    def fetch(k, slot):
        for cp in copies(k, slot):
            cp.start()

    def wait(k, slot):
        for cp in copies(k, slot):
            cp.wait()

    def partial_down(rows, slot):
        h = h_scr[rows, :]
        a = _dot(h, wg_buf[slot])
        u = _dot(h, wu_buf[slot])
        hid = (a * jax.nn.sigmoid(a)) * u
        return _dot(hid.astype(BF16), wd_buf[slot])

    @pl.when(step == 0)
    def _():
        fetch(0, slot_of(0))

    fetch(1, slot_of(1))
    wait(0, slot_of(0))
    for r, rows in subs:
        _mod_norm_rows(x_ref, g_ref, sc_ref, sh_ref, h_scr, r, SUB_ROWS)
        o_ref[rows, :] = partial_down(rows, slot_of(0))

    def middle(k):
        fetch(k + 1, slot_of(k + 1))
        wait(k, slot_of(k))
        o_ref[...] += partial_down(slice(0, tm), slot_of(k))

    def pair(u, carry):
        middle(1 + 2 * u)
        middle(2 + 2 * u)
        return carry

    lax.fori_loop(0, (nf - 3) // 2, pair, 0)
    middle(nf - 2)

    @pl.when(step < n_steps - 1)
    def _():
        fetch(0, slot_of(nf))
    wait(nf - 1, slot_of(nf - 1))
    for r, rows in subs:
        o_ref[rows, :] += partial_down(rows, slot_of(nf - 1))
        _norm_residual_rows(o_ref, x_ref, pg_ref, gate_ref, o_ref, r, SUB_ROWS)


def _ffn(x, g, modl, wg, wu, wd, pg, layer):
    bsz, s, d = x.shape
    f = wg.shape[2]
    nf = f // FFN_TF
    assert nf % 2 == 1 and nf >= 3
    tm = min(1024, s)
    row = pl.BlockSpec((None, tm, d), lambda b, i: (b, i, 0))
    vec = pl.BlockSpec((1, d), lambda b, i: (0, 0))
    modv = lambda k: pl.BlockSpec((None, 1, d), lambda b, i: (b, 0, k))
    hbm = pl.BlockSpec(memory_space=pl.ANY)
    return pl.pallas_call(
        functools.partial(_ffn_kernel, layer=layer, nf=nf),
        grid=(bsz, s // tm),
        in_specs=[row, vec, modv(4), modv(3), hbm, hbm, hbm, vec, modv(5)],
        out_specs=row,
        out_shape=jax.ShapeDtypeStruct((bsz, s, d), F32),
        scratch_shapes=[
            pltpu.VMEM((tm, d), BF16),
            pltpu.VMEM((2, d, FFN_TF), BF16), pltpu.VMEM((2, d, FFN_TF), BF16), pltpu.VMEM((2, FFN_TF, d), BF16),
            pltpu.SemaphoreType.DMA((3, 2)),
        ],
        compiler_params=_cparams(("arbitrary", "arbitrary"), FFN_VMEM_LIMIT),
        name="ffn",
    )(x, g, modl, modl, wg, wu, wd, pg, modl)


CAST_BLOCK_BYTES = 4 * 1024 * 1024


def _cast_kernel(w_ref, o_ref):
    o_ref[...] = w_ref[...].astype(BF16)


def _to_bf16(w):
    depth, rows, cols = w.shape
    rb = rows
    while rb * cols * 4 > CAST_BLOCK_BYTES and rb % 16 == 0:
        rb //= 2
    spec = pl.BlockSpec((None, rb, cols), lambda l, i: (l, i, 0))
    return pl.pallas_call(
        _cast_kernel,
        grid=(depth, rows // rb),
        in_specs=[spec],
        out_specs=spec,
        out_shape=jax.ShapeDtypeStruct(w.shape, BF16),
        compiler_params=_cparams(("parallel", "parallel")),
        name="cast_bf16",
    )(w)


def _pair_heads_a(w, axis):
    shape = w.shape
    w = jnp.moveaxis(w, axis, -1)
    lead = w.shape[:-1]
    w = w.reshape(lead + (2, 2, GROUP_A, DH_A))
    w = jnp.swapaxes(w, -3, -2)
    w = w.reshape(lead + (W_A,))
    return jnp.moveaxis(w, -1, axis).reshape(shape)


def _pad_heads(w, n_heads, width, padded, axis):
    w = jnp.moveaxis(w, axis, -1)
    lead = w.shape[:-1]
    w = w.reshape(lead + (n_heads, width))
    w = jnp.pad(w, [(0, 0)] * len(lead) + [(0, 0), (0, padded - width)])
    w = w.reshape(lead + (n_heads * padded,))
    return jnp.moveaxis(w, -1, axis)


def _layout_w_in(w):
    d = w.shape[0]
    sizes = (W_A, N_KV_A * DH_A, N_KV_A * DH_A, Q_LORA, KV_LORA, D_ROPE,
             N_H_C * DQK_C, N_H_C * DQK_C, W_C, 4 * N_H_C, W_C)
    pts = np.cumsum(sizes)[:-1].tolist()
    aq, ak, av, bcq, bckv, bkr, cq, ck, cv, cg, co = jnp.split(w, pts, axis=1)
    parts = [
        _pair_heads_a(aq, 1), ak, av, bckv,
        jnp.pad(cg, ((0, 0), (0, LANES - 4 * N_H_C))),
        bcq, bkr,
        _pad_heads(cq, N_H_C, DQK_C, DQK_PAD, 1), _pad_heads(ck, N_H_C, DQK_C, DQK_PAD, 1),
        _pad_heads(cv, N_H_C, DV_C, DV_PAD, 1), _pad_heads(co, N_H_C, DV_C, DV_PAD, 1),
    ]
    out = jnp.concatenate(parts, axis=1).astype(BF16)
    assert out.shape == (d, IN_WIDTH_PAD)
    return out.reshape(d, IN_WIDTH_PAD // IN_TN, IN_TN).transpose(1, 0, 2)


def _layout_w_uq(w):
    w = w.reshape(Q_LORA, N_H_B, D_NOPE + D_ROPE)
    nope = w[:, :, :D_NOPE].reshape(Q_LORA, N_H_B * D_NOPE)
    rope = w[:, :, D_NOPE:].reshape(Q_LORA, N_H_B * D_ROPE)
    w = jnp.concatenate([nope, rope], axis=1)
    return jnp.pad(w, ((0, 512 - Q_LORA), (0, 0))).astype(BF16)


def _layout_w_ukv(w):
    w = w.reshape(KV_LORA, N_H_B, D_NOPE + D_V_B)
    kn = w[:, :, :D_NOPE].reshape(KV_LORA, N_H_B * D_NOPE)
    v = w[:, :, D_NOPE:].reshape(KV_LORA, N_H_B * D_V_B)
    return kn.astype(BF16), v.T.astype(BF16)


def kernel(x, c, positions, mod_w, mod_b, pre_mix_g, post_mix_g, pre_ffn_g, post_ffn_g, w_in, attn_sink, mla_q_norm_g, mla_w_uq, mla_kv_norm_g, mla_w_ukv, mlstm_gate_b, mlstm_head_g, w_out, ffn_w_gate, ffn_w_up, ffn_w_down):
    depth = mod_w.shape[0]
    bsz, s, d = x.shape
    mod = _modulation(c, mod_w, mod_b)
    pos3 = positions.reshape(bsz, 1, s)
    inv = 1.0 / (ROPE_THETA ** (jnp.arange(0, D_ROPE, 2, dtype=F32) / D_ROPE))
    inv_col = inv.reshape(D_ROPE // 2, 1)
    wg_bf, wu_bf, wd_bf = _to_bf16(ffn_w_gate), _to_bf16(ffn_w_up), _to_bf16(ffn_w_down)
    for l in range(depth):
        modl = mod[l].reshape(bsz, 1, 6 * d)
        proj, gates = _in_proj(x, pre_mix_g[l].reshape(1, d), modl, _layout_w_in(w_in[l]))
        ya = _window_attention(proj, attn_sink[l])
        qg = jnp.pad(mla_q_norm_g[l], (0, 512 - Q_LORA)).reshape(1, 512)
        wk, wvt = _layout_w_ukv(mla_w_ukv[l])
        qb, kb, vtb = _mla_pre(proj, pos3, qg, _layout_w_uq(mla_w_uq[l]),
                               mla_kv_norm_g[l].reshape(1, KV_LORA), wk, wvt, inv_col)
        yb = _mla_attention(qb, kb, vtb)
        gate_b = jnp.pad(mlstm_gate_b[l], (0, LANES - 4 * N_H_C)).reshape(1, LANES)
        head_g = jnp.pad(mlstm_head_g[l].reshape(N_H_C, 1, DV_C), ((0, 0), (0, 0), (0, DV_PAD - DV_C)))
        yc = _mlstm(proj, gates, gate_b, head_g)
        wo = w_out[l]
        wa = _pair_heads_a(wo[:W_A], 0).astype(BF16)
        wb = wo[W_A:W_A + W_B].astype(BF16)
        wc = _pad_heads(wo[W_A + W_B:], N_H_C, DV_C, DV_PAD, 0).astype(BF16)
        x = _out_proj(ya, yb, yc, wa, wb, wc, post_mix_g[l].reshape(1, d), modl, x)
        x = _ffn(x, pre_ffn_g[l].reshape(1, d), modl, wg_bf, wu_bf, wd_bf, post_ffn_g[l].reshape(1, d), l)
    return x
```

```python
import functools
import math

import numpy as np
import jax
import jax.numpy as jnp
from jax import lax
from jax.experimental import pallas as pl
from jax.experimental.pallas import tpu as pltpu

F32 = jnp.float32
BF16 = jnp.bfloat16

EPS = 1e-6
N_Q_A, N_KV_A, DH_A, GROUP_A = 12, 4, 64, 3
WINDOW = 128
BLOCK_A = 128
N_H_B, Q_LORA, KV_LORA, D_NOPE, D_ROPE, D_V_B = 4, 448, 128, 128, 64, 128
ROPE_THETA = 10000.0
N_H_C, DQK_C, DV_C = 4, 96, 192
W_A, W_B, W_C = N_Q_A * DH_A, N_H_B * D_V_B, N_H_C * DV_C
W_KV_A = N_KV_A * DH_A
W_CQKR = Q_LORA + D_ROPE
W_UQ_B = N_H_B * (D_NOPE + D_ROPE)

LANES = 128
DQK_PAD = 128
DV_PAD = 256
ONES_COL = DV_C
D_QK_B = D_NOPE + LANES
W_QK_B = N_H_B * D_QK_B
VMEM_LIMIT = 56 * 1024 * 1024

W_C_PAD = N_H_C * DV_PAD
OFF_AK = W_A
OFF_AV = OFF_AK + W_KV_A
OFF_CKV = OFF_AV + W_KV_A
OFF_CG = OFF_CKV + KV_LORA
OFF_CQKR = OFF_CG + LANES
OFF_MQ = OFF_CQKR + W_CQKR
OFF_MK = OFF_MQ + N_H_C * DQK_PAD
OFF_MV = OFF_MK + N_H_C * DQK_PAD
OFF_MO = OFF_MV + W_C_PAD
IN_WIDTH_PAD = OFF_MO + W_C_PAD

CHUNK = 128


def _alibi_slopes(n):
    def pow2(m):
        start = 2.0 ** (-8.0 / m)
        return [start ** (i + 1) for i in range(m)]
    if math.log2(n).is_integer():
        s = pow2(n)
    else:
        p = 2 ** math.floor(math.log2(n))
        s = pow2(p) + pow2(2 * p)[0::2][: n - p]
    return [float(np.float32(v)) for v in s]


SLOPES_A = _alibi_slopes(N_Q_A)


def _cparams(sem, vmem_limit=VMEM_LIMIT):
    return pltpu.CompilerParams(dimension_semantics=sem, vmem_limit_bytes=vmem_limit)


def _dot(a, b):
    return jnp.dot(a, b, preferred_element_type=F32)


def _dot_nt(a, b):
    return lax.dot_general(a, b, (((1,), (1,)), ((), ())), preferred_element_type=F32)


def _mod_kernel(c_ref, w_ref, b_ref, o_ref):
    c = c_ref[...]
    s = c * jax.nn.sigmoid(c)
    o_ref[...] = _dot(s.astype(BF16), w_ref[...].astype(BF16)) + b_ref[...]


MOD_TN = 1024


def _modulation(c, mod_w, mod_b):
    depth, d, n = mod_w.shape
    bsz = c.shape[0]
    tn = MOD_TN
    return pl.pallas_call(
        _mod_kernel,
        grid=(depth, n // tn),
        in_specs=[
            pl.BlockSpec((bsz, d), lambda l, j: (0, 0)),
            pl.BlockSpec((None, d, tn), lambda l, j: (l, 0, j)),
            pl.BlockSpec((None, 1, tn), lambda l, j: (l, 0, j)),
        ],
        out_specs=pl.BlockSpec((None, bsz, tn), lambda l, j: (l, 0, j)),
        out_shape=jax.ShapeDtypeStruct((depth, bsz, n), F32),
        compiler_params=_cparams(("parallel", "parallel")),
        name="adaln_mod",
    )(c, mod_w, mod_b.reshape(depth, 1, n))


GEMM_TM = 1024
ROW_SLAB = 16
SUB_ROWS = 256


def _row_slabs(first_row, n_rows):
    return [slice(r, r + ROW_SLAB) for r in range(first_row, first_row + n_rows, ROW_SLAB)]


def _mod_norm_rows(x_ref, g_ref, sc_ref, sh_ref, h_ref, first_row, n_rows):
    gain = g_ref[...] * (1.0 + sc_ref[...])
    shift = sh_ref[...]
    for rows in _row_slabs(first_row, n_rows):
        x = x_ref[rows, :]
        ms = jnp.mean(x * x, axis=-1, keepdims=True)
        h_ref[rows, :] = (x * lax.rsqrt(ms + EPS) * gain + shift).astype(BF16)


def _norm_residual_rows(y_ref, x_ref, pg_ref, gate_ref, o_ref, first_row, n_rows):
    gain = gate_ref[...] * pg_ref[...]
    for rows in _row_slabs(first_row, n_rows):
        y = y_ref[rows, :]
        ms = jnp.mean(y * y, axis=-1, keepdims=True)
        o_ref[rows, :] = x_ref[rows, :] + y * lax.rsqrt(ms + EPS) * gain


IN_TN = 1024
GATE_TILE = OFF_CG // IN_TN
GATE_OFF = OFF_CG % IN_TN


def _in_kernel(x_ref, g_ref, sc_ref, sh_ref, w_ref, cs_ref, o_ref, gate_ref, h_scr):
    j = pl.program_id(2)
    tm = x_ref.shape[0]

    def project(rows):
        acc = _dot(h_scr[rows, :], w_ref[j])
        o_ref[rows, :] = (acc * cs_ref[...]).astype(BF16)
        return acc

    @pl.when(j == 0)
    def _():
        for r in range(0, tm, SUB_ROWS):
            _mod_norm_rows(x_ref, g_ref, sc_ref, sh_ref, h_scr, r, SUB_ROWS)
            project(slice(r, r + SUB_ROWS))

    @pl.when(j != 0)
    def _():
        acc = project(slice(0, tm))

        @pl.when(j == GATE_TILE)
        def _():
            gate_ref[...] = acc[:, GATE_OFF:GATE_OFF + LANES]


def _in_proj(x, g, modl, w):
    bsz, s, d = x.shape
    n = w.shape[0] * IN_TN
    tm = min(GEMM_TM, s)
    col_scale = jnp.where(jnp.arange(n) < W_A, WIN_QSCALE, 1.0).astype(F32).reshape(1, n)
    return pl.pallas_call(
        _in_kernel,
        grid=(bsz, s // tm, n // IN_TN),
        in_specs=[
            pl.BlockSpec((None, tm, d), lambda b, i, j: (b, i, 0)),
            pl.BlockSpec((1, d), lambda b, i, j: (0, 0)),
            pl.BlockSpec((None, 1, d), lambda b, i, j: (b, 0, 1)),
            pl.BlockSpec((None, 1, d), lambda b, i, j: (b, 0, 0)),
            pl.BlockSpec((n // IN_TN, d, IN_TN), lambda b, i, j: (0, 0, 0), pipeline_mode=pl.Buffered(1)),
            pl.BlockSpec((1, IN_TN), lambda b, i, j: (0, j)),
        ],
        out_specs=[
            pl.BlockSpec((None, tm, IN_TN), lambda b, i, j: (b, i, j)),
            pl.BlockSpec((None, tm, LANES), lambda b, i, j: (b, i, 0)),
        ],
        out_shape=[
            jax.ShapeDtypeStruct((bsz, s, n), BF16),
            jax.ShapeDtypeStruct((bsz, s, LANES), F32),
        ],
        scratch_shapes=[pltpu.VMEM((tm, d), BF16)],
        compiler_params=_cparams(("parallel", "parallel", "arbitrary")),
        name="in_proj",
    )(x, g, modl, modl, w, col_scale)


LOG2E = math.log2(math.e)
WIN_QSCALE = DH_A ** -0.5 * LOG2E


WIN_QB = 4


def _win_kernel(sink_ref, q_ref, kp_ref, kc_ref, kn_ref, vp_ref, vc_ref, vn_ref, o_ref, *, nb):
    n = pl.program_id(1)
    blk = BLOCK_A
    nk = 3 * blk
    qi = lax.broadcasted_iota(jnp.int32, (blk, nk), 0)
    kj = lax.broadcasted_iota(jnp.int32, (blk, nk), 1)
    dist = jnp.abs(qi - kj + blk)
    k_all = jnp.concatenate([kp_ref[...], kc_ref[...], kn_ref[...]], axis=0)
    v_all = jnp.concatenate([vp_ref[...], vc_ref[...], vn_ref[...]], axis=0)
    low_k = lax.broadcasted_iota(jnp.int32, (nk, LANES), 1) < DH_A
    low_q = lax.broadcasted_iota(jnp.int32, (blk, LANES), 1) < DH_A
    zero = jnp.zeros((nk, LANES), BF16)
    for sub in range(WIN_QB):
        g = n * WIN_QB + sub
        rows = slice(sub * blk, (sub + 1) * blk)
        in_seq = ((kj >= blk) | (g > 0)) & ((kj < 2 * blk) | (g < nb - 1))
        dist_masked = jnp.where((dist <= WINDOW) & in_seq, dist.astype(F32), jnp.inf)
        for p in range(2):
            kt = k_all[sub * blk:sub * blk + nk, p * LANES:(p + 1) * LANES]
            vt = v_all[sub * blk:sub * blk + nk, p * LANES:(p + 1) * LANES]
            k_big = jnp.concatenate([jnp.where(low_k, kt, zero), jnp.where(low_k, zero, kt)], axis=0)
            v_big = jnp.concatenate([jnp.where(low_k, vt, zero), jnp.where(low_k, zero, vt)], axis=0)
            q3 = jnp.concatenate([q_ref[rows, (p * GROUP_A + a) * LANES:(p * GROUP_A + a + 1) * LANES]
                                  for a in range(GROUP_A)], axis=0)
            s3 = _dot_nt(q3, k_big)
            p_rows, inv_rows = [], []
            for a in range(GROUP_A):
                p_halves, inv_halves = [], []
                for half in range(2):
                    head = (2 * p + half) * GROUP_A + a
                    s = (s3[a * blk:(a + 1) * blk, half * nk:(half + 1) * nk]
                         - (SLOPES_A[head] * LOG2E) * dist_masked)
                    sink = sink_ref[head] * LOG2E
                    m = jnp.maximum(jnp.max(s, axis=-1, keepdims=True), sink)
                    e = jnp.exp2(s - m)
                    l = jnp.sum(e, axis=-1, keepdims=True) + jnp.exp2(sink - m)
                    p_halves.append(e.astype(BF16))
                    inv_halves.append(1.0 / l)
                p_rows.append(jnp.concatenate(p_halves, axis=1))
                inv_rows.append(jnp.where(low_q, inv_halves[0], inv_halves[1]))
            o3 = _dot(jnp.concatenate(p_rows, axis=0), v_big)
            for a in range(GROUP_A):
                t = p * GROUP_A + a
                o_ref[rows, t * LANES:(t + 1) * LANES] = (o3[a * blk:(a + 1) * blk] * inv_rows[a]).astype(BF16)


def _window_attention(proj, sinks):
    bsz, s, _ = proj.shape
    blk = BLOCK_A
    nb = s // blk
    assert nb % WIN_QB == 0
    tq = WIN_QB * blk
    kb, vb = OFF_AK // W_KV_A, OFF_AV // W_KV_A
    prev = lambda n: jnp.maximum(n * WIN_QB - 1, 0)
    nxt = lambda n: jnp.minimum((n + 1) * WIN_QB, nb - 1)
    return pl.pallas_call(
        functools.partial(_win_kernel, nb=nb),
        grid=(bsz, nb // WIN_QB),
        in_specs=[
            pl.BlockSpec(memory_space=pltpu.SMEM),
            pl.BlockSpec((None, tq, W_A), lambda b, n: (b, n, 0)),
            pl.BlockSpec((None, blk, W_KV_A), lambda b, n: (b, prev(n), kb)),
            pl.BlockSpec((None, tq, W_KV_A), lambda b, n: (b, n, kb)),
            pl.BlockSpec((None, blk, W_KV_A), lambda b, n: (b, nxt(n), kb)),
            pl.BlockSpec((None, blk, W_KV_A), lambda b, n: (b, prev(n), vb)),
            pl.BlockSpec((None, tq, W_KV_A), lambda b, n: (b, n, vb)),
            pl.BlockSpec((None, blk, W_KV_A), lambda b, n: (b, nxt(n), vb)),
        ],
        out_specs=pl.BlockSpec((None, tq, W_A), lambda b, n: (b, n, 0)),
        out_shape=jax.ShapeDtypeStruct((bsz, s, W_A), BF16),
        compiler_params=_cparams(("parallel", "parallel")),
        name="win_attn",
    )(sinks, proj, proj, proj, proj, proj, proj, proj)


MLA_TK = 512
MLA_QSCALE = (D_NOPE + D_ROPE) ** -0.5 * math.log2(math.e)


def _mla_pre_kernel(cqkr_ref, ckv_ref, pos_ref, qg_ref, wq_ref, kvg_ref, wk_ref, wvt_ref, inv_ref,
                    qt_ref, k_ref, vt_ref):
    tm = cqkr_ref.shape[0]
    t = cqkr_ref[...].astype(F32)
    lane = lax.broadcasted_iota(jnp.int32, (tm, 4 * LANES), 1)
    cq = jnp.where(lane < Q_LORA, t, 0.0)
    ms = jnp.sum(cq * cq, axis=-1, keepdims=True) * (1.0 / Q_LORA)
    qn = cq * lax.rsqrt(ms + EPS) * qg_ref[...]
    q = _dot(qn.astype(BF16), wq_ref[...]) * MLA_QSCALE
    c = ckv_ref[...].astype(F32)
    ms = jnp.mean(c * c, axis=-1, keepdims=True)
    kvn = (c * lax.rsqrt(ms + EPS) * kvg_ref[...]).astype(BF16)
    kn = _dot(kvn, wk_ref[...])
    vt_ref[...] = _dot_nt(wvt_ref[...], kvn).astype(BF16)

    ang_t = inv_ref[...] * pos_ref[...].astype(F32)
    cs_t, sn_t = jnp.cos(ang_t), jnp.sin(ang_t)
    cs = jnp.concatenate([cs_t] * 4, axis=0).T
    sgn_sn = jnp.concatenate([-sn_t, sn_t] * 2, axis=0).T
    l128 = lax.broadcasted_iota(jnp.int32, (tm, LANES), 1)
    first = (l128 % D_ROPE) < (D_ROPE // 2)

    def rope(x):
        swapped = jnp.where(first, pltpu.roll(x, LANES - D_ROPE // 2, 1), pltpu.roll(x, D_ROPE // 2, 1))
        return x * cs + swapped * sgn_sn

    kr = rope(t[:, 3 * LANES:4 * LANES])
    kr_hi = jnp.where(l128 >= D_ROPE, kr, 0.0)
    kr_lo = pltpu.roll(kr_hi, D_ROPE, 1)
    kr_tiles = (kr_lo.astype(BF16), kr_hi.astype(BF16))
    for pair in range(2):
        qr_t = rope(q[:, (4 + pair) * LANES:(5 + pair) * LANES]).T.astype(BF16)
        for half in range(2):
            h = 2 * pair + half
            qt_ref[(2 * h) * LANES:(2 * h + 1) * LANES, :] = q[:, h * LANES:(h + 1) * LANES].T.astype(BF16)
            qt_ref[(2 * h + 1) * LANES:(2 * h + 2) * LANES, :] = qr_t
            k_ref[:, (2 * h) * LANES:(2 * h + 1) * LANES] = kn[:, h * LANES:(h + 1) * LANES].astype(BF16)
            k_ref[:, (2 * h + 1) * LANES:(2 * h + 2) * LANES] = kr_tiles[half]


def _mla_pre(proj, pos3, qg, wq, kvg, wk, wvt, inv):
    bsz, s, _ = proj.shape
    tm = MLA_TK
    const = lambda shape: pl.BlockSpec(shape, lambda b, i: (0,) * len(shape))
    return pl.pallas_call(
        _mla_pre_kernel,
        grid=(bsz, s // tm),
        in_specs=[
            pl.BlockSpec((None, tm, W_CQKR), lambda b, i: (b, i, OFF_CQKR // W_CQKR)),
            pl.BlockSpec((None, tm, KV_LORA), lambda b, i: (b, i, OFF_CKV // KV_LORA)),
            pl.BlockSpec((None, 1, tm), lambda b, i: (b, 0, i)),
            const((1, W_CQKR)), const((W_CQKR, W_UQ_B)), const((1, KV_LORA)), const((KV_LORA, W_B)),
            const((W_B, KV_LORA)), const((D_ROPE // 2, 1)),
        ],
        out_specs=[
            pl.BlockSpec((None, None, W_QK_B, tm), lambda b, i: (b, i, 0, 0)),
            pl.BlockSpec((None, tm, W_QK_B), lambda b, i: (b, i, 0)),
            pl.BlockSpec((None, None, W_B, tm), lambda b, i: (b, i, 0, 0)),
        ],
        out_shape=[
            jax.ShapeDtypeStruct((bsz, s // tm, W_QK_B, tm), BF16),
            jax.ShapeDtypeStruct((bsz, s, W_QK_B), BF16),
            jax.ShapeDtypeStruct((bsz, s // tm, W_B, tm), BF16),
        ],
        compiler_params=_cparams(("parallel", "parallel")),
        name="mla_pre",
    )(proj, proj, pos3, qg, wq, kvg, wk, wvt, inv)


def _mla_attn_kernel(qt_ref, k_ref, vt_ref, o_ref, st_a, st_b):
    n_chunks, _, tk = vt_ref.shape
    nt, _, tq = qt_ref.shape

    def scores(t, st_scr):
        qt = qt_ref[t]
        m8 = jnp.full((8, tq), -jnp.inf, F32)
        for c in range(n_chunks):
            st = _dot(k_ref[c * tk:(c + 1) * tk, :], qt)
            st_scr[c * tk:(c + 1) * tk, :] = st
            m8 = jnp.maximum(m8, jnp.max(st.reshape(tk // 8, 8, tq), axis=0))
        return jnp.max(m8, axis=0, keepdims=True)

    def output(t, st_scr, m):
        l8 = jnp.zeros((8, tq), F32)
        acc = jnp.zeros((D_V_B, tq), F32)
        for c in range(n_chunks):
            p = jnp.exp2(st_scr[c * tk:(c + 1) * tk, :] - m)
            l8 = l8 + jnp.sum(p.reshape(tk // 8, 8, tq), axis=0)
            acc = acc + _dot(vt_ref[c], p.astype(BF16))
        l = jnp.sum(l8, axis=0, keepdims=True)
        o_ref[pl.ds(pl.multiple_of(t * tq, tq), tq), :] = (acc / l).T.astype(BF16)

    m_first = scores(0, st_a)
    if nt == 1:
        output(0, st_a, m_first)
        return

    def pair(u, m_a):
        t = 2 * u
        m_b = scores(t + 1, st_b)
        output(t, st_a, m_a)
        m_a = scores(t + 2, st_a)
        output(t + 1, st_b, m_b)
        return m_a

    m_a = lax.fori_loop(0, nt // 2 - 1, pair, m_first)
    m_b = scores(nt - 1, st_b)
    output(nt - 2, st_a, m_a)
    output(nt - 1, st_b, m_b)


def _mla_attention(qt, k, vt):
    bsz, s, _ = k.shape
    n_chunks, tk = vt.shape[1], vt.shape[3]
    nt, tq = qt.shape[1], qt.shape[3]
    assert nt == 1 or nt % 2 == 0
    return pl.pallas_call(
        _mla_attn_kernel,
        grid=(bsz, N_H_B),
        in_specs=[
            pl.BlockSpec((None, nt, D_QK_B, tq), lambda b, h: (b, 0, h, 0)),
            pl.BlockSpec((None, s, D_QK_B), lambda b, h: (b, 0, h)),
            pl.BlockSpec((None, n_chunks, D_V_B, tk), lambda b, h: (b, 0, h, 0)),
        ],
        out_specs=pl.BlockSpec((None, s, D_V_B), lambda b, h: (b, 0, h)),
        out_shape=jax.ShapeDtypeStruct((bsz, s, W_B), BF16),
        scratch_shapes=[pltpu.VMEM((s, tq), F32), pltpu.VMEM((s, tq), F32)],
        compiler_params=_cparams(("parallel", "parallel")),
        name="mla_attn",
    )(qt, k, vt)


def _log_sigmoid(x):
    return jnp.minimum(x, 0.0) - jnp.log1p(jnp.exp(-jnp.abs(x)))


def _split3(x):
    hi = x.astype(BF16).astype(F32)
    r1 = x - hi
    mid = r1.astype(BF16).astype(F32)
    return jnp.concatenate([hi, mid, r1 - mid], axis=-1)


def _mlstm_kernel(q_ref, k_ref, v_ref, o_ref, gate_ref, gb_ref, hg_ref, y_ref,
                  hf_scr, hb_scr, tile_scr, kw_scr, rowq_scr, bp_scr, cf_scr, cb_scr, gt_scr):
    s_len = q_ref.shape[0]
    L = CHUNK
    nc = s_len // L
    head = pl.program_id(1)
    ii = lax.broadcasted_iota(jnp.int32, (L, L), 0)
    jj = lax.broadcasted_iota(jnp.int32, (L, L), 1)
    ones = jnp.ones((L, L), BF16)
    ones_lane = lax.broadcasted_iota(jnp.int32, (L, DV_PAD), 1) == ONES_COL
    eye3 = jnp.concatenate([(ii == jj).astype(BF16)] * 3, axis=1)
    tri_ones = [jnp.concatenate([jnp.concatenate([m.astype(BF16), ones], axis=1)] * 3, axis=0)
                for m in (ii <= jj, ii >= jj)]
    masks = (jj <= ii, jj >= ii)
    h_scrs = (hf_scr, hb_scr)
    c_scrs = (cf_scr, cb_scr)

    def load_v_ext(r0):
        return jnp.where(ones_lane, jnp.ones((L, DV_PAD), BF16), v_ref[pl.ds(r0, L), :])

    def transpose_gates(c, _):
        g = gate_ref[pl.ds(pl.multiple_of(c * L, L), L), :] + gb_ref[...]
        gt_scr[pl.ds(pl.multiple_of(c * 16, 16), 16), :] = g.T[0:16, :]
        return 0

    lax.fori_loop(0, nc, transpose_gates, 0, unroll=4)

    for d in range(2):
        li = gt_scr[pl.ds(d * N_H_C + head, nc, stride=16), :]
        lf = _log_sigmoid(gt_scr[pl.ds((2 + d) * N_H_C + head, nc, stride=16), :])
        bt = _dot(_split3(lf).astype(BF16), tri_ones[d])
        b, tot = bt[:, :L], bt[:, L:]
        r = li - b
        m_loc = tot + jnp.broadcast_to(jnp.max(r, axis=-1, keepdims=True), r.shape)
        rowq_scr[d, 0] = r
        rowq_scr[d, 1] = jnp.exp(tot + r - m_loc)
        rowq_scr[d, 2] = tot
        rowq_scr[d, 3] = m_loc
        bp_scr[d] = _split3(b + math.log(DQK_C ** -0.5))

    def prep(c, _):
        r0 = pl.multiple_of(c * L, L)
        q, k = q_ref[pl.ds(r0, L), :], k_ref[pl.ds(r0, L), :]
        s = _dot_nt(q, k)
        kt = k.astype(F32).T
        p = []
        for d in range(2):
            u = jnp.where(masks[d], rowq_scr[d, 0, pl.ds(c, 1), :], -jnp.inf)
            ct = jnp.broadcast_to(jnp.max(u, axis=-1, keepdims=True), (L, L))
            p.append((s * jnp.exp(u - ct)).astype(BF16))
            tile_scr[pl.ds(r0, L), (2 + d) * L:(3 + d) * L] = ct
            kw_scr[d, c] = (kt * rowq_scr[d, 1, pl.ds(c, 1), :]).astype(BF16)
        rhs = jnp.concatenate([jnp.broadcast_to(bp_scr[d, pl.ds(c, 1), :], (L, 3 * L)) for d in range(2)], axis=0)
        tile_scr[pl.ds(r0, L), 0:2 * L] = _dot_nt(eye3, rhs.astype(BF16))
        intra = _dot(jnp.concatenate(p, axis=0), load_v_ext(r0))
        hf_scr[pl.ds(r0, L), :] = intra[:L]
        hb_scr[pl.ds(r0, L), :] = intra[L:]
        return 0

    lax.fori_loop(0, nc, prep, 0, unroll=4)

    cf_scr[...] = jnp.zeros_like(cf_scr)
    cb_scr[...] = jnp.zeros_like(cb_scr)

    def one_dir(c, m_state, d):
        h_scr, c_scr = h_scrs[d], c_scrs[d]
        r0 = pl.multiple_of(c * L, L)
        tot, m_loc = rowq_scr[d, 2, pl.ds(c, 1), :], rowq_scr[d, 3, pl.ds(c, 1), :]
        bt = tile_scr[pl.ds(r0, L), d * L:(d + 1) * L]
        ct = tile_scr[pl.ds(r0, L), (2 + d) * L:(3 + d) * L]
        v_ext = load_v_ext(r0)
        c_state = c_scr[...]
        qc = _dot(q_ref[pl.ds(r0, L), :], c_state.astype(BF16))
        mx = jnp.maximum(m_state, ct)
        iw = jnp.exp(m_state - mx)
        xw = jnp.exp(ct - mx)
        num = [iw * qc[:, j * L:(j + 1) * L] + xw * h_scr[pl.ds(r0, L), j * L:(j + 1) * L] for j in range(2)]
        den = jnp.broadcast_to(num[1][:, ONES_COL - L:ONES_COL - L + 1], (L, L))
        inv = 1.0 / jnp.maximum(jnp.abs(den), jnp.exp(-(bt + mx)))
        h_scr[pl.ds(r0, L), 0:L] = num[0] * inv
        h_scr[pl.ds(r0, L), L:2 * L] = num[1] * inv
        c_loc = _dot(kw_scr[d, c], v_ext)
        m_new = jnp.maximum(tot + m_state, m_loc)
        sp, sl = jnp.exp(tot + m_state - m_new), jnp.exp(m_loc - m_new)
        c_scr[...] = jnp.concatenate(
            [sp * c_state[:, j * L:(j + 1) * L] + sl * c_loc[:, j * L:(j + 1) * L] for j in range(2)], axis=1)
        return m_new

    real = lax.broadcasted_iota(jnp.int32, (L, DV_PAD), 1) < DV_C

    def finish(c):
        r0 = pl.multiple_of(c * L, L)
        h = jnp.where(real, hf_scr[pl.ds(r0, L), :] + hb_scr[pl.ds(r0, L), :], 0.0)
        ms = jnp.sum(h * h, axis=-1, keepdims=True) * (1.0 / DV_C)
        hn = h * lax.rsqrt(ms + EPS) * hg_ref[...]
        y = jax.nn.sigmoid(o_ref[pl.ds(r0, L), :].astype(F32)) * hn
        y_ref[pl.ds(r0, L), :] = y.astype(BF16)

    def scan(c, carry):
        return one_dir(c, carry[0], 0), one_dir(nc - 1 - c, carry[1], 1)

    def scan_and_finish(c, carry):
        carry = scan(c, carry)
        finish(c)
        finish(nc - 1 - c)
        return carry

    neg = jnp.full((1, LANES), -jnp.inf, F32)
    carry = lax.fori_loop(0, nc // 2, scan, (neg, neg), unroll=4)
    lax.fori_loop(nc // 2, nc, scan_and_finish, carry, unroll=4)


def _mlstm(proj, gates, gate_b, head_g):
    bsz, s, _ = proj.shape
    nc = s // CHUNK
    return pl.pallas_call(
        _mlstm_kernel,
        grid=(bsz, N_H_C),
        in_specs=[
            pl.BlockSpec((None, s, DQK_PAD), lambda b, h: (b, 0, OFF_MQ // DQK_PAD + h)),
            pl.BlockSpec((None, s, DQK_PAD), lambda b, h: (b, 0, OFF_MK // DQK_PAD + h)),
            pl.BlockSpec((None, s, DV_PAD), lambda b, h: (b, 0, OFF_MV // DV_PAD + h)),
            pl.BlockSpec((None, s, DV_PAD), lambda b, h: (b, 0, OFF_MO // DV_PAD + h)),
            pl.BlockSpec((None, s, LANES), lambda b, h: (b, 0, 0)),
            pl.BlockSpec((1, LANES), lambda b, h: (0, 0)),
            pl.BlockSpec((None, 1, DV_PAD), lambda b, h: (h, 0, 0)),
        ],
        out_specs=pl.BlockSpec((None, s, DV_PAD), lambda b, h: (b, 0, h)),
        out_shape=jax.ShapeDtypeStruct((bsz, s, W_C_PAD), BF16),
        scratch_shapes=[
            pltpu.VMEM((s, DV_PAD), F32), pltpu.VMEM((s, DV_PAD), F32),
            pltpu.VMEM((s, 4 * CHUNK), F32),
            pltpu.VMEM((2, nc, DQK_PAD, CHUNK), BF16),
            pltpu.VMEM((2, 4, nc, CHUNK), F32),
            pltpu.VMEM((2, nc, 3 * CHUNK), F32),
            pltpu.VMEM((DQK_PAD, DV_PAD), F32), pltpu.VMEM((DQK_PAD, DV_PAD), F32),
            pltpu.VMEM((nc * 16, CHUNK), F32),
        ],
        compiler_params=_cparams(("parallel", "arbitrary")),
        name="mlstm",
    )(proj, proj, proj, proj, gates, gate_b, head_g)


def _out_kernel(ya_ref, yb_ref, yc_ref, wa_ref, wb_ref, wc_ref, pg_ref, gate_ref, x_ref, o_ref, y_scr):
    for r in range(0, o_ref.shape[0], SUB_ROWS):
        rows = slice(r, r + SUB_ROWS)
        y_scr[rows, :] = (_dot(ya_ref[rows, :], wa_ref[...]) + _dot(yb_ref[rows, :], wb_ref[...])
                          + _dot(yc_ref[rows, :], wc_ref[...]))
        _norm_residual_rows(y_scr, x_ref, pg_ref, gate_ref, o_ref, r, SUB_ROWS)


def _out_proj(ya, yb, yc, wa, wb, wc, pg, modl, x):
    bsz, s, d = x.shape
    tm = min(2 * SUB_ROWS, s)
    row = lambda w: pl.BlockSpec((None, tm, w), lambda b, i: (b, i, 0))
    const = lambda shape: pl.BlockSpec(shape, lambda b, i: (0,) * len(shape), pipeline_mode=pl.Buffered(1))
    return pl.pallas_call(
        _out_kernel,
        grid=(bsz, s // tm),
        in_specs=[
            row(W_A), row(W_B), row(W_C_PAD),
            const((W_A, d)), const((W_B, d)), const((W_C_PAD, d)), const((1, d)),
            pl.BlockSpec((None, 1, d), lambda b, i: (b, 0, 2)),
            row(d),
        ],
        out_specs=row(d),
        out_shape=jax.ShapeDtypeStruct((bsz, s, d), F32),
        scratch_shapes=[pltpu.VMEM((tm, d), F32)],
        compiler_params=_cparams(("parallel", "parallel")),
        name="out_proj",
    )(ya, yb, yc, wa, wb, wc, pg, modl, x)


FFN_TF = 512
FFN_VMEM_LIMIT = 62 * 1024 * 1024


def _ffn_kernel(x_ref, g_ref, sc_ref, sh_ref, wg_hbm, wu_hbm, wd_hbm, pg_ref, gate_ref, o_ref,
                h_scr, wg_buf, wu_buf, wd_buf, sem, *, layer, nf):
    tm = x_ref.shape[0]
    tf = wg_buf.shape[2]
    step = pl.program_id(0) * pl.num_programs(1) + pl.program_id(1)
    n_steps = pl.num_programs(0) * pl.num_programs(1)
    subs = [(r, slice(r, r + SUB_ROWS)) for r in range(0, tm, SUB_ROWS)]

    def slot_of(k):
        return (k + step) % 2

    def copies(k, slot):
        cols = pl.ds(pl.multiple_of(k * tf, tf), tf)
        return (pltpu.make_async_copy(wg_hbm.at[layer, :, cols], wg_buf.at[slot], sem.at[0, slot]),
                pltpu.make_async_copy(wu_hbm.at[layer, :, cols], wu_buf.at[slot], sem.at[1, slot]),
                pltpu.make_async_copy(wd_hbm.at[layer, cols, :], wd_buf.at[slot], sem.at[2, slot]))

    def fetch(k, slot):
        for cp in copies(k, slot):
            cp.start()

    def wait(k, slot):
        for cp in copies(k, slot):
            cp.wait()

    def partial_down(rows, slot):
        h = h_scr[rows, :]
        a = _dot(h, wg_buf[slot])
        u = _dot(h, wu_buf[slot])
        hid = (a * jax.nn.sigmoid(a)) * u
        return _dot(hid.astype(BF16), wd_buf[slot])

    @pl.when(step == 0)
    def _():
        fetch(0, slot_of(0))

    fetch(1, slot_of(1))
    wait(0, slot_of(0))
    for r, rows in subs:
        _mod_norm_rows(x_ref, g_ref, sc_ref, sh_ref, h_scr, r, SUB_ROWS)
        o_ref[rows, :] = partial_down(rows, slot_of(0))

    def middle(k):
        fetch(k + 1, slot_of(k + 1))
        wait(k, slot_of(k))
        o_ref[...] += partial_down(slice(0, tm), slot_of(k))

    def pair(u, carry):
        middle(1 + 2 * u)
        middle(2 + 2 * u)
        return carry

    lax.fori_loop(0, (nf - 3) // 2, pair, 0)
    middle(nf - 2)

    @pl.when(step < n_steps - 1)
    def _():
        fetch(0, slot_of(nf))
    wait(nf - 1, slot_of(nf - 1))
    for r, rows in subs:
        o_ref[rows, :] += partial_down(rows, slot_of(nf - 1))
        _norm_residual_rows(o_ref, x_ref, pg_ref, gate_ref, o_ref, r, SUB_ROWS)


def _ffn(x, g, modl, wg, wu, wd, pg, layer):
    bsz, s, d = x.shape
    f = wg.shape[2]
    nf = f // FFN_TF
    assert nf % 2 == 1 and nf >= 3
    tm = min(GEMM_TM, s)
    row = pl.BlockSpec((None, tm, d), lambda b, i: (b, i, 0))
    vec = pl.BlockSpec((1, d), lambda b, i: (0, 0))
    modv = lambda k: pl.BlockSpec((None, 1, d), lambda b, i: (b, 0, k))
    hbm = pl.BlockSpec(memory_space=pl.ANY)
    return pl.pallas_call(
        functools.partial(_ffn_kernel, layer=layer, nf=nf),
        grid=(bsz, s // tm),
        in_specs=[row, vec, modv(4), modv(3), hbm, hbm, hbm, vec, modv(5)],
        out_specs=row,
        out_shape=jax.ShapeDtypeStruct((bsz, s, d), F32),
        scratch_shapes=[
            pltpu.VMEM((tm, d), BF16),
            pltpu.VMEM((2, d, FFN_TF), BF16), pltpu.VMEM((2, d, FFN_TF), BF16), pltpu.VMEM((2, FFN_TF, d), BF16),
            pltpu.SemaphoreType.DMA((3, 2)),
        ],
        compiler_params=_cparams(("arbitrary", "arbitrary"), FFN_VMEM_LIMIT),
        name="ffn",
    )(x, g, modl, modl, wg, wu, wd, pg, modl)


CAST_BLOCK_BYTES = 4 * 1024 * 1024


def _cast_kernel(w_ref, o_ref):
    o_ref[...] = w_ref[...].astype(BF16)


def _to_bf16(w):
    depth, rows, cols = w.shape
    rb = rows
    while rb * cols * 4 > CAST_BLOCK_BYTES and rb % 16 == 0:
        rb //= 2
    spec = pl.BlockSpec((None, rb, cols), lambda l, i: (l, i, 0))
    return pl.pallas_call(
        _cast_kernel,
        grid=(depth, rows // rb),
        in_specs=[spec],
        out_specs=spec,
        out_shape=jax.ShapeDtypeStruct(w.shape, BF16),
        compiler_params=_cparams(("parallel", "parallel")),
        name="cast_bf16",
    )(w)


def _pair_heads_a(w, axis):
    shape = w.shape
    w = jnp.moveaxis(w, axis, -1)
    lead = w.shape[:-1]
    w = w.reshape(lead + (2, 2, GROUP_A, DH_A))
    w = jnp.swapaxes(w, -3, -2)
    w = w.reshape(lead + (W_A,))
    return jnp.moveaxis(w, -1, axis).reshape(shape)


def _pad_heads(w, n_heads, width, padded, axis):
    w = jnp.moveaxis(w, axis, -1)
    lead = w.shape[:-1]
    w = w.reshape(lead + (n_heads, width))
    w = jnp.pad(w, [(0, 0)] * len(lead) + [(0, 0), (0, padded - width)])
    w = w.reshape(lead + (n_heads * padded,))
    return jnp.moveaxis(w, -1, axis)


def _layout_w_in(w):
    d = w.shape[0]
    sizes = (W_A, N_KV_A * DH_A, N_KV_A * DH_A, Q_LORA, KV_LORA, D_ROPE,
             N_H_C * DQK_C, N_H_C * DQK_C, W_C, 4 * N_H_C, W_C)
    pts = np.cumsum(sizes)[:-1].tolist()
    aq, ak, av, bcq, bckv, bkr, cq, ck, cv, cg, co = jnp.split(w, pts, axis=1)
    parts = [
        _pair_heads_a(aq, 1), ak, av, bckv,
        jnp.pad(cg, ((0, 0), (0, LANES - 4 * N_H_C))),
        bcq, bkr,
        _pad_heads(cq, N_H_C, DQK_C, DQK_PAD, 1), _pad_heads(ck, N_H_C, DQK_C, DQK_PAD, 1),
        _pad_heads(cv, N_H_C, DV_C, DV_PAD, 1), _pad_heads(co, N_H_C, DV_C, DV_PAD, 1),
    ]
    out = jnp.concatenate(parts, axis=1).astype(BF16)
    assert out.shape == (d, IN_WIDTH_PAD)
    return out.reshape(d, IN_WIDTH_PAD // IN_TN, IN_TN).transpose(1, 0, 2)


def _layout_w_uq(w):
    w = w.reshape(Q_LORA, N_H_B, D_NOPE + D_ROPE)
    nope = w[:, :, :D_NOPE].reshape(Q_LORA, N_H_B * D_NOPE)
    rope = w[:, :, D_NOPE:].reshape(Q_LORA, N_H_B * D_ROPE)
    w = jnp.concatenate([nope, rope], axis=1)
    return jnp.pad(w, ((0, W_CQKR - Q_LORA), (0, 0))).astype(BF16)


def _layout_w_ukv(w):
    w = w.reshape(KV_LORA, N_H_B, D_NOPE + D_V_B)
    kn = w[:, :, :D_NOPE].reshape(KV_LORA, N_H_B * D_NOPE)
    v = w[:, :, D_NOPE:].reshape(KV_LORA, N_H_B * D_V_B)
    return kn.astype(BF16), v.T.astype(BF16)


def kernel(x, c, positions, mod_w, mod_b, pre_mix_g, post_mix_g, pre_ffn_g, post_ffn_g, w_in, attn_sink, mla_q_norm_g, mla_w_uq, mla_kv_norm_g, mla_w_ukv, mlstm_gate_b, mlstm_head_g, w_out, ffn_w_gate, ffn_w_up, ffn_w_down):
    depth = mod_w.shape[0]
    bsz, s, d = x.shape
    mod = _modulation(c, mod_w, mod_b)
    pos3 = positions.reshape(bsz, 1, s)
    inv = 1.0 / (ROPE_THETA ** (jnp.arange(0, D_ROPE, 2, dtype=F32) / D_ROPE))
    inv_col = inv.reshape(D_ROPE // 2, 1)
    wg_bf, wu_bf, wd_bf = _to_bf16(ffn_w_gate), _to_bf16(ffn_w_up), _to_bf16(ffn_w_down)
    for l in range(depth):
        modl = mod[l].reshape(bsz, 1, 6 * d)
        proj, gates = _in_proj(x, pre_mix_g[l].reshape(1, d), modl, _layout_w_in(w_in[l]))
        ya = _window_attention(proj, attn_sink[l])
        qg = jnp.pad(mla_q_norm_g[l], (0, W_CQKR - Q_LORA)).reshape(1, W_CQKR)
        wk, wvt = _layout_w_ukv(mla_w_ukv[l])
        qb, kb, vtb = _mla_pre(proj, pos3, qg, _layout_w_uq(mla_w_uq[l]),
                               mla_kv_norm_g[l].reshape(1, KV_LORA), wk, wvt, inv_col)
        yb = _mla_attention(qb, kb, vtb)
        gate_b = jnp.pad(mlstm_gate_b[l], (0, LANES - 4 * N_H_C)).reshape(1, LANES)
        head_g = jnp.pad(mlstm_head_g[l].reshape(N_H_C, 1, DV_C), ((0, 0), (0, 0), (0, DV_PAD - DV_C)))
        yc = _mlstm(proj, gates, gate_b, head_g)
        wo = w_out[l]
        wa = _pair_heads_a(wo[:W_A], 0).astype(BF16)
        wb = wo[W_A:W_A + W_B].astype(BF16)
        wc = _pad_heads(wo[W_A + W_B:], N_H_C, DV_C, DV_PAD, 0).astype(BF16)
        x = _out_proj(ya, yb, yc, wa, wb, wc, post_mix_g[l].reshape(1, d), modl, x)
        x = _ffn(x, pre_ffn_g[l].reshape(1, d), modl, wg_bf, wu_bf, wd_bf, post_ffn_g[l].reshape(1, d), l)
    return x
```

```python
import functools
import math

import numpy as np
import jax
import jax.numpy as jnp
from jax import lax
from jax.experimental import pallas as pl
from jax.experimental.pallas import tpu as pltpu

F32 = jnp.float32
BF16 = jnp.bfloat16

EPS = 1e-6
N_Q_A, N_KV_A, DH_A, GROUP_A = 12, 4, 64, 3
WINDOW = 128
BLOCK_A = 128
N_H_B, Q_LORA, KV_LORA, D_NOPE, D_ROPE, D_V_B = 4, 448, 128, 128, 64, 128
ROPE_THETA = 10000.0
N_H_C, DQK_C, DV_C = 4, 96, 192
W_A, W_B, W_C = N_Q_A * DH_A, N_H_B * D_V_B, N_H_C * DV_C
W_KV_A = N_KV_A * DH_A
W_CQKR = Q_LORA + D_ROPE
W_UQ_B = N_H_B * (D_NOPE + D_ROPE)

LANES = 128
DQK_PAD = 128
DV_PAD = 256
ONES_COL = DV_C
D_QK_B = D_NOPE + LANES
W_QK_B = N_H_B * D_QK_B
VMEM_LIMIT = 56 * 1024 * 1024

W_C_PAD = N_H_C * DV_PAD
OFF_AK = W_A
OFF_AV = OFF_AK + W_KV_A
OFF_CKV = OFF_AV + W_KV_A
OFF_CG = OFF_CKV + KV_LORA
OFF_CQKR = OFF_CG + LANES
OFF_MQ = OFF_CQKR + W_CQKR
OFF_MK = OFF_MQ + N_H_C * DQK_PAD
OFF_MV = OFF_MK + N_H_C * DQK_PAD
OFF_MO = OFF_MV + W_C_PAD
IN_WIDTH_PAD = OFF_MO + W_C_PAD

CHUNK = 128


def _alibi_slopes(n):
    def pow2(m):
        start = 2.0 ** (-8.0 / m)
        return [start ** (i + 1) for i in range(m)]
    if math.log2(n).is_integer():
        s = pow2(n)
    else:
        p = 2 ** math.floor(math.log2(n))
        s = pow2(p) + pow2(2 * p)[0::2][: n - p]
    return [float(np.float32(v)) for v in s]


SLOPES_A = _alibi_slopes(N_Q_A)


def _cparams(sem, vmem_limit=VMEM_LIMIT):
    return pltpu.CompilerParams(dimension_semantics=sem, vmem_limit_bytes=vmem_limit)


def _dot(a, b):
    return jnp.dot(a, b, preferred_element_type=F32)


def _dot_nt(a, b):
    return lax.dot_general(a, b, (((1,), (1,)), ((), ())), preferred_element_type=F32)


def _mod_kernel(c_ref, w_ref, b_ref, o_ref):
    c = c_ref[...]
    s = c * jax.nn.sigmoid(c)
    o_ref[...] = _dot(s.astype(BF16), w_ref[...].astype(BF16)) + b_ref[...]


MOD_TN = 2048


def _modulation(c, mod_w, mod_b):
    depth, d, n = mod_w.shape
    bsz = c.shape[0]
    tn = MOD_TN
    return pl.pallas_call(
        _mod_kernel,
        grid=(depth, n // tn),
        in_specs=[
            pl.BlockSpec((bsz, d), lambda l, j: (0, 0)),
            pl.BlockSpec((None, d, tn), lambda l, j: (l, 0, j)),
            pl.BlockSpec((None, 1, tn), lambda l, j: (l, 0, j)),
        ],
        out_specs=pl.BlockSpec((None, bsz, tn), lambda l, j: (l, 0, j)),
        out_shape=jax.ShapeDtypeStruct((depth, bsz, n), F32),
        compiler_params=_cparams(("parallel", "parallel")),
        name="adaln_mod",
    )(c, mod_w, mod_b.reshape(depth, 1, n))


GEMM_TM = 1024
ROW_SLAB = 16
SUB_ROWS = 256


def _row_slabs(first_row, n_rows):
    return [slice(r, r + ROW_SLAB) for r in range(first_row, first_row + n_rows, ROW_SLAB)]


def _mod_norm_rows(x_ref, g_ref, sc_ref, sh_ref, h_ref, first_row, n_rows):
    gain = g_ref[...] * (1.0 + sc_ref[...])
    shift = sh_ref[...]
    for rows in _row_slabs(first_row, n_rows):
        x = x_ref[rows, :]
        ms = jnp.mean(x * x, axis=-1, keepdims=True)
        h_ref[rows, :] = (x * lax.rsqrt(ms + EPS) * gain + shift).astype(BF16)


def _norm_residual_rows(y_ref, x_ref, pg_ref, gate_ref, o_ref, first_row, n_rows):
    gain = gate_ref[...] * pg_ref[...]
    for rows in _row_slabs(first_row, n_rows):
        y = y_ref[rows, :]
        ms = jnp.mean(y * y, axis=-1, keepdims=True)
        o_ref[rows, :] = x_ref[rows, :] + y * lax.rsqrt(ms + EPS) * gain


IN_TN = 1024
GATE_TILE = OFF_CG // IN_TN
GATE_OFF = OFF_CG % IN_TN


def _in_kernel(x_ref, g_ref, sc_ref, sh_ref, w_ref, cs_ref, o_ref, gate_ref, h_scr):
    j = pl.program_id(2)
    tm = x_ref.shape[0]

    def project(rows):
        acc = _dot(h_scr[rows, :], w_ref[j])
        o_ref[rows, :] = (acc * cs_ref[...]).astype(BF16)
        return acc

    @pl.when(j == 0)
    def _():
        for r in range(0, tm, SUB_ROWS):
            _mod_norm_rows(x_ref, g_ref, sc_ref, sh_ref, h_scr, r, SUB_ROWS)
            project(slice(r, r + SUB_ROWS))

    @pl.when(j != 0)
    def _():
        acc = project(slice(0, tm))

        @pl.when(j == GATE_TILE)
        def _():
            gate_ref[...] = acc[:, GATE_OFF:GATE_OFF + LANES]


def _in_proj(x, g, modl, w):
    bsz, s, d = x.shape
    n = w.shape[0] * IN_TN
    tm = min(GEMM_TM, s)
    col_scale = jnp.where(jnp.arange(n) < W_A, WIN_QSCALE, 1.0).astype(F32).reshape(1, n)
    return pl.pallas_call(
        _in_kernel,
        grid=(bsz, s // tm, n // IN_TN),
        in_specs=[
            pl.BlockSpec((None, tm, d), lambda b, i, j: (b, i, 0)),
            pl.BlockSpec((1, d), lambda b, i, j: (0, 0)),
            pl.BlockSpec((None, 1, d), lambda b, i, j: (b, 0, 1)),
            pl.BlockSpec((None, 1, d), lambda b, i, j: (b, 0, 0)),
            pl.BlockSpec((n // IN_TN, d, IN_TN), lambda b, i, j: (0, 0, 0), pipeline_mode=pl.Buffered(1)),
            pl.BlockSpec((1, IN_TN), lambda b, i, j: (0, j)),
        ],
        out_specs=[
            pl.BlockSpec((None, tm, IN_TN), lambda b, i, j: (b, i, j)),
            pl.BlockSpec((None, tm, LANES), lambda b, i, j: (b, i, 0)),
        ],
        out_shape=[
            jax.ShapeDtypeStruct((bsz, s, n), BF16),
            jax.ShapeDtypeStruct((bsz, s, LANES), F32),
        ],
        scratch_shapes=[pltpu.VMEM((tm, d), BF16)],
        compiler_params=_cparams(("parallel", "parallel", "arbitrary")),
        name="in_proj",
    )(x, g, modl, modl, w, col_scale)


LOG2E = math.log2(math.e)
WIN_QSCALE = DH_A ** -0.5 * LOG2E


WIN_QB = 8


def _win_kernel(sink_ref, q_ref, kp_ref, kc_ref, kn_ref, vp_ref, vc_ref, vn_ref, o_ref, *, nb):
    n = pl.program_id(1)
    blk = BLOCK_A
    n_sub = q_ref.shape[0] // blk
    nk = 3 * blk
    qi = lax.broadcasted_iota(jnp.int32, (blk, nk), 0)
    kj = lax.broadcasted_iota(jnp.int32, (blk, nk), 1)
    dist = jnp.abs(qi - kj + blk)
    k_all = jnp.concatenate([kp_ref[...], kc_ref[...], kn_ref[...]], axis=0)
    v_all = jnp.concatenate([vp_ref[...], vc_ref[...], vn_ref[...]], axis=0)
    low_k = lax.broadcasted_iota(jnp.int32, (nk, LANES), 1) < DH_A
    low_q = lax.broadcasted_iota(jnp.int32, (blk, LANES), 1) < DH_A
    zero = jnp.zeros((nk, LANES), BF16)
    for sub in range(n_sub):
        g = n * n_sub + sub
        rows = slice(sub * blk, (sub + 1) * blk)
        in_seq = ((kj >= blk) | (g > 0)) & ((kj < 2 * blk) | (g < nb - 1))
        dist_masked = jnp.where((dist <= WINDOW) & in_seq, dist.astype(F32), jnp.inf)
        for p in range(2):
            kt = k_all[sub * blk:sub * blk + nk, p * LANES:(p + 1) * LANES]
            vt = v_all[sub * blk:sub * blk + nk, p * LANES:(p + 1) * LANES]
            k_big = jnp.concatenate([jnp.where(low_k, kt, zero), jnp.where(low_k, zero, kt)], axis=0)
            v_big = jnp.concatenate([jnp.where(low_k, vt, zero), jnp.where(low_k, zero, vt)], axis=0)
            q3 = jnp.concatenate([q_ref[rows, (p * GROUP_A + a) * LANES:(p * GROUP_A + a + 1) * LANES]
                                  for a in range(GROUP_A)], axis=0)
            s3 = _dot_nt(q3, k_big)
            p_rows, inv_rows = [], []
            for a in range(GROUP_A):
                p_halves, inv_halves = [], []
                for half in range(2):
                    head = (2 * p + half) * GROUP_A + a
                    s = (s3[a * blk:(a + 1) * blk, half * nk:(half + 1) * nk]
                         - (SLOPES_A[head] * LOG2E) * dist_masked)
                    sink = sink_ref[head] * LOG2E
                    m = jnp.maximum(jnp.max(s, axis=-1, keepdims=True), sink)
                    e = jnp.exp2(s - m)
                    l = jnp.sum(e, axis=-1, keepdims=True) + jnp.exp2(sink - m)
                    p_halves.append(e.astype(BF16))
                    inv_halves.append(1.0 / l)
                p_rows.append(jnp.concatenate(p_halves, axis=1))
                inv_rows.append(jnp.where(low_q, inv_halves[0], inv_halves[1]))
            o3 = _dot(jnp.concatenate(p_rows, axis=0), v_big)
            for a in range(GROUP_A):
                t = p * GROUP_A + a
                o_ref[rows, t * LANES:(t + 1) * LANES] = (o3[a * blk:(a + 1) * blk] * inv_rows[a]).astype(BF16)


def _window_attention(proj, sinks):
    bsz, s, _ = proj.shape
    blk = BLOCK_A
    nb = s // blk
    qb = math.gcd(nb, WIN_QB)
    tq = qb * blk
    kb, vb = OFF_AK // W_KV_A, OFF_AV // W_KV_A
    prev = lambda n: jnp.maximum(n * qb - 1, 0)
    nxt = lambda n: jnp.minimum((n + 1) * qb, nb - 1)
    return pl.pallas_call(
        functools.partial(_win_kernel, nb=nb),
        grid=(bsz, nb // qb),
        in_specs=[
            pl.BlockSpec(memory_space=pltpu.SMEM),
            pl.BlockSpec((None, tq, W_A), lambda b, n: (b, n, 0)),
            pl.BlockSpec((None, blk, W_KV_A), lambda b, n: (b, prev(n), kb)),
            pl.BlockSpec((None, tq, W_KV_A), lambda b, n: (b, n, kb)),
            pl.BlockSpec((None, blk, W_KV_A), lambda b, n: (b, nxt(n), kb)),
            pl.BlockSpec((None, blk, W_KV_A), lambda b, n: (b, prev(n), vb)),
            pl.BlockSpec((None, tq, W_KV_A), lambda b, n: (b, n, vb)),
            pl.BlockSpec((None, blk, W_KV_A), lambda b, n: (b, nxt(n), vb)),
        ],
        out_specs=pl.BlockSpec((None, tq, W_A), lambda b, n: (b, n, 0)),
        out_shape=jax.ShapeDtypeStruct((bsz, s, W_A), BF16),
        compiler_params=_cparams(("parallel", "parallel")),
        name="win_attn",
    )(sinks, proj, proj, proj, proj, proj, proj, proj)


MLA_TK = 512
MLA_QSCALE = (D_NOPE + D_ROPE) ** -0.5 * math.log2(math.e)


def _mla_pre_kernel(cqkr_ref, ckv_ref, pos_ref, qg_ref, wq_ref, kvg_ref, wk_ref, wvt_ref, inv_ref,
                    qt_ref, k_ref, vt_ref):
    tm = cqkr_ref.shape[0]
    t = cqkr_ref[...].astype(F32)
    lane = lax.broadcasted_iota(jnp.int32, (tm, 4 * LANES), 1)
    cq = jnp.where(lane < Q_LORA, t, 0.0)
    ms = jnp.sum(cq * cq, axis=-1, keepdims=True) * (1.0 / Q_LORA)
    qn = cq * lax.rsqrt(ms + EPS) * qg_ref[...]
    q = _dot(qn.astype(BF16), wq_ref[...]) * MLA_QSCALE
    c = ckv_ref[...].astype(F32)
    ms = jnp.mean(c * c, axis=-1, keepdims=True)
    kvn = (c * lax.rsqrt(ms + EPS) * kvg_ref[...]).astype(BF16)
    kn = _dot(kvn, wk_ref[...])
    vt_ref[...] = _dot_nt(wvt_ref[...], kvn).astype(BF16)

    ang_t = inv_ref[...] * pos_ref[...].astype(F32)
    cs_t, sn_t = jnp.cos(ang_t), jnp.sin(ang_t)
    cs = jnp.concatenate([cs_t] * 4, axis=0).T
    sgn_sn = jnp.concatenate([-sn_t, sn_t] * 2, axis=0).T
    l128 = lax.broadcasted_iota(jnp.int32, (tm, LANES), 1)
    first = (l128 % D_ROPE) < (D_ROPE // 2)

    def rope(x):
        swapped = jnp.where(first, pltpu.roll(x, LANES - D_ROPE // 2, 1), pltpu.roll(x, D_ROPE // 2, 1))
        return x * cs + swapped * sgn_sn

    kr = rope(t[:, 3 * LANES:4 * LANES])
    kr_hi = jnp.where(l128 >= D_ROPE, kr, 0.0)
    kr_lo = pltpu.roll(kr_hi, D_ROPE, 1)
    kr_tiles = (kr_lo.astype(BF16), kr_hi.astype(BF16))
    for pair in range(2):
        qr_t = rope(q[:, (4 + pair) * LANES:(5 + pair) * LANES]).T.astype(BF16)
        for half in range(2):
            h = 2 * pair + half
            qt_ref[(2 * h) * LANES:(2 * h + 1) * LANES, :] = q[:, h * LANES:(h + 1) * LANES].T.astype(BF16)
            qt_ref[(2 * h + 1) * LANES:(2 * h + 2) * LANES, :] = qr_t
            k_ref[:, (2 * h) * LANES:(2 * h + 1) * LANES] = kn[:, h * LANES:(h + 1) * LANES].astype(BF16)
            k_ref[:, (2 * h + 1) * LANES:(2 * h + 2) * LANES] = kr_tiles[half]


def _mla_pre(proj, pos3, qg, wq, kvg, wk, wvt, inv):
    bsz, s, _ = proj.shape
    tm = MLA_TK
    const = lambda shape: pl.BlockSpec(shape, lambda b, i: (0,) * len(shape))
    return pl.pallas_call(
        _mla_pre_kernel,
        grid=(bsz, s // tm),
        in_specs=[
            pl.BlockSpec((None, tm, W_CQKR), lambda b, i: (b, i, OFF_CQKR // W_CQKR)),
            pl.BlockSpec((None, tm, KV_LORA), lambda b, i: (b, i, OFF_CKV // KV_LORA)),
            pl.BlockSpec((None, 1, tm), lambda b, i: (b, 0, i)),
            const((1, W_CQKR)), const((W_CQKR, W_UQ_B)), const((1, KV_LORA)), const((KV_LORA, W_B)),
            const((W_B, KV_LORA)), const((D_ROPE // 2, 1)),
        ],
        out_specs=[
            pl.BlockSpec((None, None, W_QK_B, tm), lambda b, i: (b, i, 0, 0)),
            pl.BlockSpec((None, tm, W_QK_B), lambda b, i: (b, i, 0)),
            pl.BlockSpec((None, None, W_B, tm), lambda b, i: (b, i, 0, 0)),
        ],
        out_shape=[
            jax.ShapeDtypeStruct((bsz, s // tm, W_QK_B, tm), BF16),
            jax.ShapeDtypeStruct((bsz, s, W_QK_B), BF16),
            jax.ShapeDtypeStruct((bsz, s // tm, W_B, tm), BF16),
        ],
        compiler_params=_cparams(("parallel", "parallel")),
        name="mla_pre",
    )(proj, proj, pos3, qg, wq, kvg, wk, wvt, inv)


def _mla_attn_kernel(qt_ref, k_ref, vt_ref, o_ref, st_a, st_b):
    n_chunks, _, tk = vt_ref.shape
    nt, _, tq = qt_ref.shape

    def scores(t, st_scr):
        qt = qt_ref[t]
        m8 = jnp.full((8, tq), -jnp.inf, F32)
        for c in range(n_chunks):
            st = _dot(k_ref[c * tk:(c + 1) * tk, :], qt)
            st_scr[c * tk:(c + 1) * tk, :] = st
            m8 = jnp.maximum(m8, jnp.max(st.reshape(tk // 8, 8, tq), axis=0))
        return jnp.max(m8, axis=0, keepdims=True)

    def output(t, st_scr, m):
        l8 = jnp.zeros((8, tq), F32)
        acc = jnp.zeros((D_V_B, tq), F32)
        for c in range(n_chunks):
            p = jnp.exp2(st_scr[c * tk:(c + 1) * tk, :] - m)
            l8 = l8 + jnp.sum(p.reshape(tk // 8, 8, tq), axis=0)
            acc = acc + _dot(vt_ref[c], p.astype(BF16))
        l = jnp.sum(l8, axis=0, keepdims=True)
        o_ref[pl.ds(pl.multiple_of(t * tq, tq), tq), :] = (acc / l).T.astype(BF16)

    m_first = scores(0, st_a)
    if nt == 1:
        output(0, st_a, m_first)
        return

    def pair(u, m_a):
        t = 2 * u
        m_b = scores(t + 1, st_b)
        output(t, st_a, m_a)
        m_a = scores(t + 2, st_a)
        output(t + 1, st_b, m_b)
        return m_a

    m_a = lax.fori_loop(0, nt // 2 - 1, pair, m_first)
    m_b = scores(nt - 1, st_b)
    output(nt - 2, st_a, m_a)
    output(nt - 1, st_b, m_b)


def _mla_attention(qt, k, vt):
    bsz, s, _ = k.shape
    n_chunks, tk = vt.shape[1], vt.shape[3]
    nt, tq = qt.shape[1], qt.shape[3]
    assert nt == 1 or nt % 2 == 0
    return pl.pallas_call(
        _mla_attn_kernel,
        grid=(bsz, N_H_B),
        in_specs=[
            pl.BlockSpec((None, nt, D_QK_B, tq), lambda b, h: (b, 0, h, 0)),
            pl.BlockSpec((None, s, D_QK_B), lambda b, h: (b, 0, h)),
            pl.BlockSpec((None, n_chunks, D_V_B, tk), lambda b, h: (b, 0, h, 0)),
        ],
        out_specs=pl.BlockSpec((None, s, D_V_B), lambda b, h: (b, 0, h)),
        out_shape=jax.ShapeDtypeStruct((bsz, s, W_B), BF16),
        scratch_shapes=[pltpu.VMEM((s, tq), F32), pltpu.VMEM((s, tq), F32)],
        compiler_params=_cparams(("parallel", "parallel")),
        name="mla_attn",
    )(qt, k, vt)


def _log_sigmoid(x):
    return jnp.minimum(x, 0.0) - jnp.log1p(jnp.exp(-jnp.abs(x)))


def _split3(x):
    hi = x.astype(BF16).astype(F32)
    r1 = x - hi
    mid = r1.astype(BF16).astype(F32)
    return jnp.concatenate([hi, mid, r1 - mid], axis=-1)


def _mlstm_kernel(q_ref, k_ref, v_ref, o_ref, gate_ref, gb_ref, hg_ref, y_ref,
                  hf_scr, hb_scr, tile_scr, kw_scr, rowq_scr, bp_scr, cf_scr, cb_scr, gt_scr):
    s_len = q_ref.shape[0]
    L = CHUNK
    nc = s_len // L
    head = pl.program_id(1)
    ii = lax.broadcasted_iota(jnp.int32, (L, L), 0)
    jj = lax.broadcasted_iota(jnp.int32, (L, L), 1)
    ones = jnp.ones((L, L), BF16)
    ones_lane = lax.broadcasted_iota(jnp.int32, (L, DV_PAD), 1) == ONES_COL
    eye3 = jnp.concatenate([(ii == jj).astype(BF16)] * 3, axis=1)
    tri_ones = [jnp.concatenate([jnp.concatenate([m.astype(BF16), ones], axis=1)] * 3, axis=0)
                for m in (ii <= jj, ii >= jj)]
    masks = (jj <= ii, jj >= ii)
    h_scrs = (hf_scr, hb_scr)
    c_scrs = (cf_scr, cb_scr)

    def load_v_ext(r0):
        return jnp.where(ones_lane, jnp.ones((L, DV_PAD), BF16), v_ref[pl.ds(r0, L), :])

    def transpose_gates(c, _):
        g = gate_ref[pl.ds(pl.multiple_of(c * L, L), L), :] + gb_ref[...]
        gt_scr[pl.ds(pl.multiple_of(c * 16, 16), 16), :] = g.T[0:16, :]
        return 0

    lax.fori_loop(0, nc, transpose_gates, 0, unroll=4)

    for d in range(2):
        li = gt_scr[pl.ds(d * N_H_C + head, nc, stride=16), :]
        lf = _log_sigmoid(gt_scr[pl.ds((2 + d) * N_H_C + head, nc, stride=16), :])
        bt = _dot(_split3(lf).astype(BF16), tri_ones[d])
        b, tot = bt[:, :L] * LOG2E, bt[:, L:] * LOG2E
        r = li * LOG2E - b
        m_loc = tot + jnp.broadcast_to(jnp.max(r, axis=-1, keepdims=True), r.shape)
        rowq_scr[d, 0] = r
        rowq_scr[d, 1] = jnp.exp2(tot + r - m_loc)
        rowq_scr[d, 2] = tot
        rowq_scr[d, 3] = m_loc
        bp_scr[d] = _split3(b + math.log2(DQK_C ** -0.5))

    def prep(c, _):
        r0 = pl.multiple_of(c * L, L)
        q, k = q_ref[pl.ds(r0, L), :], k_ref[pl.ds(r0, L), :]
        s = _dot_nt(q, k)
        kt = k.astype(F32).T
        p = []
        for d in range(2):
            u = jnp.where(masks[d], rowq_scr[d, 0, pl.ds(c, 1), :], -jnp.inf)
            ct = jnp.broadcast_to(jnp.max(u, axis=-1, keepdims=True), (L, L))
            p.append((s * jnp.exp2(u - ct)).astype(BF16))
            tile_scr[pl.ds(r0, L), (2 + d) * L:(3 + d) * L] = ct
            kw_scr[d, c] = (kt * rowq_scr[d, 1, pl.ds(c, 1), :]).astype(BF16)
        rhs = jnp.concatenate([jnp.broadcast_to(bp_scr[d, pl.ds(c, 1), :], (L, 3 * L)) for d in range(2)], axis=0)
        tile_scr[pl.ds(r0, L), 0:2 * L] = _dot_nt(eye3, rhs.astype(BF16))
        intra = _dot(jnp.concatenate(p, axis=0), load_v_ext(r0))
        hf_scr[pl.ds(r0, L), :] = intra[:L]
        hb_scr[pl.ds(r0, L), :] = intra[L:]
        return 0

    lax.fori_loop(0, nc, prep, 0, unroll=4)

    cf_scr[...] = jnp.zeros_like(cf_scr)
    cb_scr[...] = jnp.zeros_like(cb_scr)

    def one_dir(c, m_state, d):
        h_scr, c_scr = h_scrs[d], c_scrs[d]
        r0 = pl.multiple_of(c * L, L)
        tot, m_loc = rowq_scr[d, 2, pl.ds(c, 1), :], rowq_scr[d, 3, pl.ds(c, 1), :]
        bt = tile_scr[pl.ds(r0, L), d * L:(d + 1) * L]
        ct = tile_scr[pl.ds(r0, L), (2 + d) * L:(3 + d) * L]
        v_ext = load_v_ext(r0)
        c_state = c_scr[...]
        qc = _dot(q_ref[pl.ds(r0, L), :], c_state.astype(BF16))
        mx = jnp.maximum(m_state, ct)
        iw = jnp.exp2(m_state - mx)
        xw = jnp.exp2(ct - mx)
        num = [iw * qc[:, j * L:(j + 1) * L] + xw * h_scr[pl.ds(r0, L), j * L:(j + 1) * L] for j in range(2)]
        den = jnp.broadcast_to(num[1][:, ONES_COL - L:ONES_COL - L + 1], (L, L))
        inv = 1.0 / jnp.maximum(jnp.abs(den), jnp.exp2(-(bt + mx)))
        h_scr[pl.ds(r0, L), 0:L] = num[0] * inv
        h_scr[pl.ds(r0, L), L:2 * L] = num[1] * inv
        c_loc = _dot(kw_scr[d, c], v_ext)
        m_new = jnp.maximum(tot + m_state, m_loc)
        sp, sl = jnp.exp2(tot + m_state - m_new), jnp.exp2(m_loc - m_new)
        c_scr[...] = jnp.concatenate(
            [sp * c_state[:, j * L:(j + 1) * L] + sl * c_loc[:, j * L:(j + 1) * L] for j in range(2)], axis=1)
        return m_new

    real = lax.broadcasted_iota(jnp.int32, (L, DV_PAD), 1) < DV_C

    def finish(c):
        r0 = pl.multiple_of(c * L, L)
        h = jnp.where(real, hf_scr[pl.ds(r0, L), :] + hb_scr[pl.ds(r0, L), :], 0.0)
        ms = jnp.sum(h * h, axis=-1, keepdims=True) * (1.0 / DV_C)
        hn = h * lax.rsqrt(ms + EPS) * hg_ref[...]
        y = jax.nn.sigmoid(o_ref[pl.ds(r0, L), :].astype(F32)) * hn
        y_ref[pl.ds(r0, L), :] = y.astype(BF16)

    def scan(c, carry):
        return one_dir(c, carry[0], 0), one_dir(nc - 1 - c, carry[1], 1)

    def scan_and_finish(c, carry):
        carry = scan(c, carry)
        finish(c)
        finish(nc - 1 - c)
        return carry

    neg = jnp.full((1, LANES), -jnp.inf, F32)
    carry = lax.fori_loop(0, nc // 2, scan, (neg, neg), unroll=4)
    lax.fori_loop(nc // 2, nc, scan_and_finish, carry, unroll=4)


def _mlstm(proj, gates, gate_b, head_g):
    bsz, s, _ = proj.shape
    nc = s // CHUNK
    return pl.pallas_call(
        _mlstm_kernel,
        grid=(bsz, N_H_C),
        in_specs=[
            pl.BlockSpec((None, s, DQK_PAD), lambda b, h: (b, 0, OFF_MQ // DQK_PAD + h)),
            pl.BlockSpec((None, s, DQK_PAD), lambda b, h: (b, 0, OFF_MK // DQK_PAD + h)),
            pl.BlockSpec((None, s, DV_PAD), lambda b, h: (b, 0, OFF_MV // DV_PAD + h)),
            pl.BlockSpec((None, s, DV_PAD), lambda b, h: (b, 0, OFF_MO // DV_PAD + h)),
            pl.BlockSpec((None, s, LANES), lambda b, h: (b, 0, 0)),
            pl.BlockSpec((1, LANES), lambda b, h: (0, 0)),
            pl.BlockSpec((None, 1, DV_PAD), lambda b, h: (h, 0, 0)),
        ],
        out_specs=pl.BlockSpec((None, s, DV_PAD), lambda b, h: (b, 0, h)),
        out_shape=jax.ShapeDtypeStruct((bsz, s, W_C_PAD), BF16),
        scratch_shapes=[
            pltpu.VMEM((s, DV_PAD), F32), pltpu.VMEM((s, DV_PAD), F32),
            pltpu.VMEM((s, 4 * CHUNK), F32),
            pltpu.VMEM((2, nc, DQK_PAD, CHUNK), BF16),
            pltpu.VMEM((2, 4, nc, CHUNK), F32),
            pltpu.VMEM((2, nc, 3 * CHUNK), F32),
            pltpu.VMEM((DQK_PAD, DV_PAD), F32), pltpu.VMEM((DQK_PAD, DV_PAD), F32),
            pltpu.VMEM((nc * 16, CHUNK), F32),
        ],
        compiler_params=_cparams(("parallel", "arbitrary")),
        name="mlstm",
    )(proj, proj, proj, proj, gates, gate_b, head_g)


def _out_kernel(ya_ref, yb_ref, yc_ref, wa_ref, wb_ref, wc_ref, pg_ref, gate_ref, x_ref, o_ref, y_scr):
    for r in range(0, o_ref.shape[0], SUB_ROWS):
        rows = slice(r, r + SUB_ROWS)
        y_scr[rows, :] = (_dot(ya_ref[rows, :], wa_ref[...]) + _dot(yb_ref[rows, :], wb_ref[...])
                          + _dot(yc_ref[rows, :], wc_ref[...]))
        _norm_residual_rows(y_scr, x_ref, pg_ref, gate_ref, o_ref, r, SUB_ROWS)


def _out_proj(ya, yb, yc, wa, wb, wc, pg, modl, x):
    bsz, s, d = x.shape
    tm = min(2 * SUB_ROWS, s)
    row = lambda w: pl.BlockSpec((None, tm, w), lambda b, i: (b, i, 0))
    const = lambda shape: pl.BlockSpec(shape, lambda b, i: (0,) * len(shape), pipeline_mode=pl.Buffered(1))
    return pl.pallas_call(
        _out_kernel,
        grid=(bsz, s // tm),
        in_specs=[
            row(W_A), row(W_B), row(W_C_PAD),
            const((W_A, d)), const((W_B, d)), const((W_C_PAD, d)), const((1, d)),
            pl.BlockSpec((None, 1, d), lambda b, i: (b, 0, 2)),
            row(d),
        ],
        out_specs=row(d),
        out_shape=jax.ShapeDtypeStruct((bsz, s, d), F32),
        scratch_shapes=[pltpu.VMEM((tm, d), F32)],
        compiler_params=_cparams(("parallel", "parallel")),
        name="out_proj",
    )(ya, yb, yc, wa, wb, wc, pg, modl, x)


FFN_TF = 512
FFN_VMEM_LIMIT = 62 * 1024 * 1024


def _ffn_kernel(x_ref, g_ref, sc_ref, sh_ref, wg_hbm, wu_hbm, wd_hbm, pg_ref, gate_ref, o_ref,
                h_scr, wg_buf, wu_buf, wd_buf, sem, *, layer, nf):
    tm = x_ref.shape[0]
    tf = wg_buf.shape[2]
    step = pl.program_id(0) * pl.num_programs(1) + pl.program_id(1)
    n_steps = pl.num_programs(0) * pl.num_programs(1)
    subs = [(r, slice(r, r + SUB_ROWS)) for r in range(0, tm, SUB_ROWS)]

    def slot_of(k):
        return (k + step) % 2

    def copies(k, slot):
        cols = pl.ds(pl.multiple_of(k * tf, tf), tf)
        return (pltpu.make_async_copy(wg_hbm.at[layer, :, cols], wg_buf.at[slot], sem.at[0, slot]),
                pltpu.make_async_copy(wu_hbm.at[layer, :, cols], wu_buf.at[slot], sem.at[1, slot]),
                pltpu.make_async_copy(wd_hbm.at[layer, cols, :], wd_buf.at[slot], sem.at[2, slot]))

    def fetch(k, slot):
        for cp in copies(k, slot):
            cp.start()

    def wait(k, slot):
        for cp in copies(k, slot):
            cp.wait()

    def partial_down(rows, slot):
        h = h_scr[rows, :]
        a = _dot(h, wg_buf[slot])
        u = _dot(h, wu_buf[slot])
        hid = (a * jax.nn.sigmoid(a)) * u
        return _dot(hid.astype(BF16), wd_buf[slot])

    @pl.when(step == 0)
    def _():
        fetch(0, slot_of(0))

    fetch(1, slot_of(1))
    wait(0, slot_of(0))
    for r, rows in subs:
        _mod_norm_rows(x_ref, g_ref, sc_ref, sh_ref, h_scr, r, SUB_ROWS)
        o_ref[rows, :] = partial_down(rows, slot_of(0))

    def middle(k):
        fetch(k + 1, slot_of(k + 1))
        wait(k, slot_of(k))
        o_ref[...] += partial_down(slice(0, tm), slot_of(k))

    def pair(u, carry):
        middle(1 + 2 * u)
        middle(2 + 2 * u)
        return carry

    lax.fori_loop(0, (nf - 3) // 2, pair, 0)
    middle(nf - 2)

    @pl.when(step < n_steps - 1)
    def _():
        fetch(0, slot_of(nf))
    wait(nf - 1, slot_of(nf - 1))
    for r, rows in subs:
        o_ref[rows, :] += partial_down(rows, slot_of(nf - 1))
        _norm_residual_rows(o_ref, x_ref, pg_ref, gate_ref, o_ref, r, SUB_ROWS)


def _ffn(x, g, modl, wg, wu, wd, pg, layer):
    bsz, s, d = x.shape
    f = wg.shape[2]
    nf = f // FFN_TF
    assert nf % 2 == 1 and nf >= 3
    tm = min(GEMM_TM, s)
    row = pl.BlockSpec((None, tm, d), lambda b, i: (b, i, 0))
    vec = pl.BlockSpec((1, d), lambda b, i: (0, 0))
    modv = lambda k: pl.BlockSpec((None, 1, d), lambda b, i: (b, 0, k))
    hbm = pl.BlockSpec(memory_space=pl.ANY)
    return pl.pallas_call(
        functools.partial(_ffn_kernel, layer=layer, nf=nf),
        grid=(bsz, s // tm),
        in_specs=[row, vec, modv(4), modv(3), hbm, hbm, hbm, vec, modv(5)],
        out_specs=row,
        out_shape=jax.ShapeDtypeStruct((bsz, s, d), F32),
        scratch_shapes=[
            pltpu.VMEM((tm, d), BF16),
            pltpu.VMEM((2, d, FFN_TF), BF16), pltpu.VMEM((2, d, FFN_TF), BF16), pltpu.VMEM((2, FFN_TF, d), BF16),
            pltpu.SemaphoreType.DMA((3, 2)),
        ],
        compiler_params=_cparams(("arbitrary", "arbitrary"), FFN_VMEM_LIMIT),
        name="ffn",
    )(x, g, modl, modl, wg, wu, wd, pg, modl)


CAST_BLOCK_BYTES = 4 * 1024 * 1024


def _cast_kernel(w_ref, o_ref):
    o_ref[...] = w_ref[...].astype(BF16)


def _to_bf16(w):
    depth, rows, cols = w.shape
    rb = rows
    while rb * cols * 4 > CAST_BLOCK_BYTES and rb % 16 == 0:
        rb //= 2
    spec = pl.BlockSpec((None, rb, cols), lambda l, i: (l, i, 0))
    return pl.pallas_call(
        _cast_kernel,
        grid=(depth, rows // rb),
        in_specs=[spec],
        out_specs=spec,
        out_shape=jax.ShapeDtypeStruct(w.shape, BF16),
        compiler_params=_cparams(("parallel", "parallel")),
        name="cast_bf16",
    )(w)


def _pair_heads_a(w, axis):
    shape = w.shape
    w = jnp.moveaxis(w, axis, -1)
    lead = w.shape[:-1]
    w = w.reshape(lead + (2, 2, GROUP_A, DH_A))
    w = jnp.swapaxes(w, -3, -2)
    w = w.reshape(lead + (W_A,))
    return jnp.moveaxis(w, -1, axis).reshape(shape)


def _pad_heads(w, n_heads, width, padded, axis):
    w = jnp.moveaxis(w, axis, -1)
    lead = w.shape[:-1]
    w = w.reshape(lead + (n_heads, width))
    w = jnp.pad(w, [(0, 0)] * len(lead) + [(0, 0), (0, padded - width)])
    w = w.reshape(lead + (n_heads * padded,))
    return jnp.moveaxis(w, -1, axis)


def _layout_w_in(w):
    d = w.shape[0]
    sizes = (W_A, N_KV_A * DH_A, N_KV_A * DH_A, Q_LORA, KV_LORA, D_ROPE,
             N_H_C * DQK_C, N_H_C * DQK_C, W_C, 4 * N_H_C, W_C)
    pts = np.cumsum(sizes)[:-1].tolist()
    aq, ak, av, bcq, bckv, bkr, cq, ck, cv, cg, co = jnp.split(w, pts, axis=1)
    parts = [
        _pair_heads_a(aq, 1), ak, av, bckv,
        jnp.pad(cg, ((0, 0), (0, LANES - 4 * N_H_C))),
        bcq, bkr,
        _pad_heads(cq, N_H_C, DQK_C, DQK_PAD, 1), _pad_heads(ck, N_H_C, DQK_C, DQK_PAD, 1),
        _pad_heads(cv, N_H_C, DV_C, DV_PAD, 1), _pad_heads(co, N_H_C, DV_C, DV_PAD, 1),
    ]
    out = jnp.concatenate(parts, axis=1).astype(BF16)
    assert out.shape == (d, IN_WIDTH_PAD)
    return out.reshape(d, IN_WIDTH_PAD // IN_TN, IN_TN).transpose(1, 0, 2)


def _layout_w_uq(w):
    w = w.reshape(Q_LORA, N_H_B, D_NOPE + D_ROPE)
    nope = w[:, :, :D_NOPE].reshape(Q_LORA, N_H_B * D_NOPE)
    rope = w[:, :, D_NOPE:].reshape(Q_LORA, N_H_B * D_ROPE)
    w = jnp.concatenate([nope, rope], axis=1)
    return jnp.pad(w, ((0, W_CQKR - Q_LORA), (0, 0))).astype(BF16)


def _layout_w_ukv(w):
    w = w.reshape(KV_LORA, N_H_B, D_NOPE + D_V_B)
    kn = w[:, :, :D_NOPE].reshape(KV_LORA, N_H_B * D_NOPE)
    v = w[:, :, D_NOPE:].reshape(KV_LORA, N_H_B * D_V_B)
    return kn.astype(BF16), v.T.astype(BF16)


def kernel(x, c, positions, mod_w, mod_b, pre_mix_g, post_mix_g, pre_ffn_g, post_ffn_g, w_in, attn_sink, mla_q_norm_g, mla_w_uq, mla_kv_norm_g, mla_w_ukv, mlstm_gate_b, mlstm_head_g, w_out, ffn_w_gate, ffn_w_up, ffn_w_down):
    depth = mod_w.shape[0]
    bsz, s, d = x.shape
    mod = _modulation(c, mod_w, mod_b)
    pos3 = positions.reshape(bsz, 1, s)
    inv = 1.0 / (ROPE_THETA ** (jnp.arange(0, D_ROPE, 2, dtype=F32) / D_ROPE))
    inv_col = inv.reshape(D_ROPE // 2, 1)
    wg_bf, wu_bf, wd_bf = _to_bf16(ffn_w_gate), _to_bf16(ffn_w_up), _to_bf16(ffn_w_down)
    for l in range(depth):
        modl = mod[l].reshape(bsz, 1, 6 * d)
        proj, gates = _in_proj(x, pre_mix_g[l].reshape(1, d), modl, _layout_w_in(w_in[l]))
        ya = _window_attention(proj, attn_sink[l])
        qg = jnp.pad(mla_q_norm_g[l], (0, W_CQKR - Q_LORA)).reshape(1, W_CQKR)
        wk, wvt = _layout_w_ukv(mla_w_ukv[l])
        qb, kb, vtb = _mla_pre(proj, pos3, qg, _layout_w_uq(mla_w_uq[l]),
                               mla_kv_norm_g[l].reshape(1, KV_LORA), wk, wvt, inv_col)
        yb = _mla_attention(qb, kb, vtb)
        gate_b = jnp.pad(mlstm_gate_b[l], (0, LANES - 4 * N_H_C)).reshape(1, LANES)
        head_g = jnp.pad(mlstm_head_g[l].reshape(N_H_C, 1, DV_C), ((0, 0), (0, 0), (0, DV_PAD - DV_C)))
        yc = _mlstm(proj, gates, gate_b, head_g)
        wo = w_out[l]
        wa = _pair_heads_a(wo[:W_A], 0).astype(BF16)
        wb = wo[W_A:W_A + W_B].astype(BF16)
        wc = _pad_heads(wo[W_A + W_B:], N_H_C, DV_C, DV_PAD, 0).astype(BF16)
        x = _out_proj(ya, yb, yc, wa, wb, wc, post_mix_g[l].reshape(1, d), modl, x)
        x = _ffn(x, pre_ffn_g[l].reshape(1, d), modl, wg_bf, wu_bf, wd_bf, post_ffn_g[l].reshape(1, d), l)
    return x
```

```python
import functools
import math

import numpy as np
import jax
import jax.numpy as jnp
from jax import lax
from jax.experimental import pallas as pl
from jax.experimental.pallas import tpu as pltpu

F32 = jnp.float32
BF16 = jnp.bfloat16

EPS = 1e-6
N_Q_A, N_KV_A, DH_A, GROUP_A = 12, 4, 64, 3
WINDOW = 128
BLOCK_A = 128
N_H_B, Q_LORA, KV_LORA, D_NOPE, D_ROPE, D_V_B = 4, 448, 128, 128, 64, 128
ROPE_THETA = 10000.0
N_H_C, DQK_C, DV_C = 4, 96, 192
W_A, W_B, W_C = N_Q_A * DH_A, N_H_B * D_V_B, N_H_C * DV_C
W_KV_A = N_KV_A * DH_A
W_CQKR = Q_LORA + D_ROPE
W_UQ_B = N_H_B * (D_NOPE + D_ROPE)

LANES = 128
DQK_PAD = 128
DV_PAD = 256
ONES_COL = DV_C
D_QK_B = D_NOPE + LANES
W_QK_B = N_H_B * D_QK_B
VMEM_LIMIT = 56 * 1024 * 1024

W_C_PAD = N_H_C * DV_PAD
OFF_AK = W_A
OFF_AV = OFF_AK + W_KV_A
OFF_CKV = OFF_AV + W_KV_A
OFF_CG = OFF_CKV + KV_LORA
OFF_CQKR = OFF_CG + LANES
OFF_MQ = OFF_CQKR + W_CQKR
OFF_MK = OFF_MQ + N_H_C * DQK_PAD
OFF_MV = OFF_MK + N_H_C * DQK_PAD
OFF_MO = OFF_MV + W_C
IN_WIDTH_PAD = OFF_MO + W_C

CHUNK = 128


def _alibi_slopes(n):
    def pow2(m):
        start = 2.0 ** (-8.0 / m)
        return [start ** (i + 1) for i in range(m)]
    if math.log2(n).is_integer():
        s = pow2(n)
    else:
        p = 2 ** math.floor(math.log2(n))
        s = pow2(p) + pow2(2 * p)[0::2][: n - p]
    return [float(np.float32(v)) for v in s]


SLOPES_A = _alibi_slopes(N_Q_A)


def _cparams(sem, vmem_limit=VMEM_LIMIT):
    return pltpu.CompilerParams(dimension_semantics=sem, vmem_limit_bytes=vmem_limit)


def _dot(a, b):
    return jnp.dot(a, b, preferred_element_type=F32)


def _dot_nt(a, b):
    return lax.dot_general(a, b, (((1,), (1,)), ((), ())), preferred_element_type=F32)


def _mod_kernel(c_ref, w_ref, b_ref, o_ref):
    c = c_ref[...]
    s = c * jax.nn.sigmoid(c)
    o_ref[...] = _dot(s.astype(BF16), w_ref[...].astype(BF16)) + b_ref[...]


MOD_TN = 2048


def _modulation(c, mod_w, mod_b):
    depth, d, n = mod_w.shape
    bsz = c.shape[0]
    tn = MOD_TN
    return pl.pallas_call(
        _mod_kernel,
        grid=(depth, n // tn),
        in_specs=[
            pl.BlockSpec((bsz, d), lambda l, j: (0, 0)),
            pl.BlockSpec((None, d, tn), lambda l, j: (l, 0, j)),
            pl.BlockSpec((None, 1, tn), lambda l, j: (l, 0, j)),
        ],
        out_specs=pl.BlockSpec((None, bsz, tn), lambda l, j: (l, 0, j)),
        out_shape=jax.ShapeDtypeStruct((depth, bsz, n), F32),
        compiler_params=_cparams(("parallel", "parallel")),
        name="adaln_mod",
    )(c, mod_w, mod_b.reshape(depth, 1, n))


GEMM_TM = 1024
ROW_SLAB = 16
SUB_ROWS = 256


def _row_slabs(first_row, n_rows):
    return [slice(r, r + ROW_SLAB) for r in range(first_row, first_row + n_rows, ROW_SLAB)]


def _mod_norm_rows(x_ref, g_ref, sc_ref, sh_ref, h_ref, first_row, n_rows):
    gain = g_ref[...] * (1.0 + sc_ref[...])
    shift = sh_ref[...]
    for rows in _row_slabs(first_row, n_rows):
        x = x_ref[rows, :]
        ms = jnp.mean(x * x, axis=-1, keepdims=True)
        h_ref[rows, :] = (x * lax.rsqrt(ms + EPS) * gain + shift).astype(BF16)


def _norm_residual_rows(y_ref, x_ref, pg_ref, gate_ref, o_ref, first_row, n_rows):
    gain = gate_ref[...] * pg_ref[...]
    for rows in _row_slabs(first_row, n_rows):
        y = y_ref[rows, :]
        ms = jnp.mean(y * y, axis=-1, keepdims=True)
        o_ref[rows, :] = x_ref[rows, :] + y * lax.rsqrt(ms + EPS) * gain


IN_TN = 1536
GATE_TILE = OFF_CG // IN_TN
GATE_OFF = OFF_CG % IN_TN


def _in_kernel(x_ref, g_ref, sc_ref, sh_ref, w_ref, cs_ref, o_ref, gate_ref, h_scr):
    j = pl.program_id(2)
    tm = x_ref.shape[0]

    def project(rows):
        acc = _dot(h_scr[rows, :], w_ref[j])
        o_ref[rows, :] = (acc * cs_ref[...]).astype(BF16)
        return acc

    @pl.when(j == 0)
    def _():
        for r in range(0, tm, SUB_ROWS):
            rows = slice(r, r + SUB_ROWS)
            _mod_norm_rows(x_ref, g_ref, sc_ref, sh_ref, h_scr, r, SUB_ROWS)
            acc = project(rows)
            if GATE_TILE == 0:
                gate_ref[rows, :] = acc[:, GATE_OFF:GATE_OFF + LANES]

    @pl.when(j != 0)
    def _():
        acc = project(slice(0, tm))
        if GATE_TILE != 0:
            @pl.when(j == GATE_TILE)
            def _():
                gate_ref[...] = acc[:, GATE_OFF:GATE_OFF + LANES]


def _in_proj(x, g, modl, w):
    bsz, s, d = x.shape
    n = w.shape[0] * IN_TN
    tm = min(GEMM_TM, s)
    col_scale = jnp.where(jnp.arange(n) < W_A, WIN_QSCALE, 1.0).astype(F32).reshape(1, n)
    return pl.pallas_call(
        _in_kernel,
        grid=(bsz, s // tm, n // IN_TN),
        in_specs=[
            pl.BlockSpec((None, tm, d), lambda b, i, j: (b, i, 0)),
            pl.BlockSpec((1, d), lambda b, i, j: (0, 0)),
            pl.BlockSpec((None, 1, d), lambda b, i, j: (b, 0, 1)),
            pl.BlockSpec((None, 1, d), lambda b, i, j: (b, 0, 0)),
            pl.BlockSpec((n // IN_TN, d, IN_TN), lambda b, i, j: (0, 0, 0), pipeline_mode=pl.Buffered(1)),
            pl.BlockSpec((1, IN_TN), lambda b, i, j: (0, j)),
        ],
        out_specs=[
            pl.BlockSpec((None, tm, IN_TN), lambda b, i, j: (b, i, j)),
            pl.BlockSpec((None, tm, LANES), lambda b, i, j: (b, i, 0)),
        ],
        out_shape=[
            jax.ShapeDtypeStruct((bsz, s, n), BF16),
            jax.ShapeDtypeStruct((bsz, s, LANES), F32),
        ],
        scratch_shapes=[pltpu.VMEM((tm, d), BF16)],
        compiler_params=_cparams(("parallel", "parallel", "arbitrary")),
        name="in_proj",
    )(x, g, modl, modl, w, col_scale)


LOG2E = math.log2(math.e)
WIN_QSCALE = DH_A ** -0.5 * LOG2E


WIN_QB = 8


def _win_kernel(sink_ref, q_ref, kp_ref, kc_ref, kn_ref, vp_ref, vc_ref, vn_ref, o_ref, *, nb):
    n = pl.program_id(1)
    blk = BLOCK_A
    n_sub = q_ref.shape[0] // blk
    nk = 3 * blk
    qi = lax.broadcasted_iota(jnp.int32, (blk, nk), 0)
    kj = lax.broadcasted_iota(jnp.int32, (blk, nk), 1)
    dist = jnp.abs(qi - kj + blk)
    k_all = jnp.concatenate([kp_ref[...], kc_ref[...], kn_ref[...]], axis=0)
    v_all = jnp.concatenate([vp_ref[...], vc_ref[...], vn_ref[...]], axis=0)
    low_k = lax.broadcasted_iota(jnp.int32, (nk, LANES), 1) < DH_A
    low_q = lax.broadcasted_iota(jnp.int32, (blk, LANES), 1) < DH_A
    zero = jnp.zeros((nk, LANES), BF16)
    for sub in range(n_sub):
        g = n * n_sub + sub
        rows = slice(sub * blk, (sub + 1) * blk)
        in_seq = ((kj >= blk) | (g > 0)) & ((kj < 2 * blk) | (g < nb - 1))
        dist_masked = jnp.where((dist <= WINDOW) & in_seq, dist.astype(F32), jnp.inf)
        for p in range(2):
            kt = k_all[sub * blk:sub * blk + nk, p * LANES:(p + 1) * LANES]
            vt = v_all[sub * blk:sub * blk + nk, p * LANES:(p + 1) * LANES]
            k_big = jnp.concatenate([jnp.where(low_k, kt, zero), jnp.where(low_k, zero, kt)], axis=0)
            v_big = jnp.concatenate([jnp.where(low_k, vt, zero), jnp.where(low_k, zero, vt)], axis=0)
            q3 = jnp.concatenate([q_ref[rows, (p * GROUP_A + a) * LANES:(p * GROUP_A + a + 1) * LANES]
                                  for a in range(GROUP_A)], axis=0)
            s3 = _dot_nt(q3, k_big)
            p_rows, inv_rows = [], []
            for a in range(GROUP_A):
                p_halves, inv_halves = [], []
                for half in range(2):
                    head = (2 * p + half) * GROUP_A + a
                    s = (s3[a * blk:(a + 1) * blk, half * nk:(half + 1) * nk]
                         - (SLOPES_A[head] * LOG2E) * dist_masked)
                    sink = sink_ref[head] * LOG2E
                    m = jnp.maximum(jnp.max(s, axis=-1, keepdims=True), sink)
                    e = jnp.exp2(s - m)
                    l = jnp.sum(e, axis=-1, keepdims=True) + jnp.exp2(sink - m)
                    p_halves.append(e.astype(BF16))
                    inv_halves.append(1.0 / l)
                p_rows.append(jnp.concatenate(p_halves, axis=1))
                inv_rows.append(jnp.where(low_q, inv_halves[0], inv_halves[1]))
            o3 = _dot(jnp.concatenate(p_rows, axis=0), v_big)
            for a in range(GROUP_A):
                t = p * GROUP_A + a
                o_ref[rows, t * LANES:(t + 1) * LANES] = (o3[a * blk:(a + 1) * blk] * inv_rows[a]).astype(BF16)


def _window_attention(proj, sinks):
    bsz, s, _ = proj.shape
    blk = BLOCK_A
    nb = s // blk
    qb = math.gcd(nb, WIN_QB)
    tq = qb * blk
    kb, vb = OFF_AK // W_KV_A, OFF_AV // W_KV_A
    prev = lambda n: jnp.maximum(n * qb - 1, 0)
    nxt = lambda n: jnp.minimum((n + 1) * qb, nb - 1)
    return pl.pallas_call(
        functools.partial(_win_kernel, nb=nb),
        grid=(bsz, nb // qb),
        in_specs=[
            pl.BlockSpec(memory_space=pltpu.SMEM),
            pl.BlockSpec((None, tq, W_A), lambda b, n: (b, n, 0)),
            pl.BlockSpec((None, blk, W_KV_A), lambda b, n: (b, prev(n), kb)),
            pl.BlockSpec((None, tq, W_KV_A), lambda b, n: (b, n, kb)),
            pl.BlockSpec((None, blk, W_KV_A), lambda b, n: (b, nxt(n), kb)),
            pl.BlockSpec((None, blk, W_KV_A), lambda b, n: (b, prev(n), vb)),
            pl.BlockSpec((None, tq, W_KV_A), lambda b, n: (b, n, vb)),
            pl.BlockSpec((None, blk, W_KV_A), lambda b, n: (b, nxt(n), vb)),
        ],
        out_specs=pl.BlockSpec((None, tq, W_A), lambda b, n: (b, n, 0)),
        out_shape=jax.ShapeDtypeStruct((bsz, s, W_A), BF16),
        compiler_params=_cparams(("parallel", "parallel")),
        name="win_attn",
    )(sinks, proj, proj, proj, proj, proj, proj, proj)


MLA_TK = 512
MLA_QSCALE = (D_NOPE + D_ROPE) ** -0.5 * math.log2(math.e)


def _mla_pre_kernel(cqkr_ref, ckv_ref, pos_ref, qg_ref, wq_ref, kvg_ref, wk_ref, wvt_ref, inv_ref,
                    qt_ref, k_ref, vt_ref):
    tm = cqkr_ref.shape[0]
    t = cqkr_ref[...].astype(F32)
    lane = lax.broadcasted_iota(jnp.int32, (tm, 4 * LANES), 1)
    cq = jnp.where(lane < Q_LORA, t, 0.0)
    ms = jnp.sum(cq * cq, axis=-1, keepdims=True) * (1.0 / Q_LORA)
    qn = cq * lax.rsqrt(ms + EPS) * qg_ref[...]
    q = _dot(qn.astype(BF16), wq_ref[...]) * MLA_QSCALE
    c = ckv_ref[...].astype(F32)
    ms = jnp.mean(c * c, axis=-1, keepdims=True)
    kvn = (c * lax.rsqrt(ms + EPS) * kvg_ref[...]).astype(BF16)
    kn = _dot(kvn, wk_ref[...])
    vt_ref[...] = _dot_nt(wvt_ref[...], kvn).astype(BF16)

    ang_t = inv_ref[...] * pos_ref[...].astype(F32)
    cs_t, sn_t = jnp.cos(ang_t), jnp.sin(ang_t)
    cs = jnp.concatenate([cs_t] * 4, axis=0).T
    sgn_sn = jnp.concatenate([-sn_t, sn_t] * 2, axis=0).T
    l128 = lax.broadcasted_iota(jnp.int32, (tm, LANES), 1)
    first = (l128 % D_ROPE) < (D_ROPE // 2)

    def rope(x):
        swapped = jnp.where(first, pltpu.roll(x, LANES - D_ROPE // 2, 1), pltpu.roll(x, D_ROPE // 2, 1))
        return x * cs + swapped * sgn_sn

    kr = rope(t[:, 3 * LANES:4 * LANES])
    kr_hi = jnp.where(l128 >= D_ROPE, kr, 0.0)
    kr_lo = pltpu.roll(kr_hi, D_ROPE, 1)
    kr_tiles = (kr_lo.astype(BF16), kr_hi.astype(BF16))
    for pair in range(2):
        qr_t = rope(q[:, (4 + pair) * LANES:(5 + pair) * LANES]).T.astype(BF16)
        for half in range(2):
            h = 2 * pair + half
            qt_ref[(2 * h) * LANES:(2 * h + 1) * LANES, :] = q[:, h * LANES:(h + 1) * LANES].T.astype(BF16)
            qt_ref[(2 * h + 1) * LANES:(2 * h + 2) * LANES, :] = qr_t
            k_ref[:, (2 * h) * LANES:(2 * h + 1) * LANES] = kn[:, h * LANES:(h + 1) * LANES].astype(BF16)
            k_ref[:, (2 * h + 1) * LANES:(2 * h + 2) * LANES] = kr_tiles[half]


def _mla_pre(proj, pos3, qg, wq, kvg, wk, wvt, inv):
    bsz, s, _ = proj.shape
    tm = MLA_TK
    const = lambda shape: pl.BlockSpec(shape, lambda b, i: (0,) * len(shape))
    return pl.pallas_call(
        _mla_pre_kernel,
        grid=(bsz, s // tm),
        in_specs=[
            pl.BlockSpec((None, tm, W_CQKR), lambda b, i: (b, i, OFF_CQKR // W_CQKR)),
            pl.BlockSpec((None, tm, KV_LORA), lambda b, i: (b, i, OFF_CKV // KV_LORA)),
            pl.BlockSpec((None, 1, tm), lambda b, i: (b, 0, i)),
            const((1, W_CQKR)), const((W_CQKR, W_UQ_B)), const((1, KV_LORA)), const((KV_LORA, W_B)),
            const((W_B, KV_LORA)), const((D_ROPE // 2, 1)),
        ],
        out_specs=[
            pl.BlockSpec((None, None, W_QK_B, tm), lambda b, i: (b, i, 0, 0)),
            pl.BlockSpec((None, tm, W_QK_B), lambda b, i: (b, i, 0)),
            pl.BlockSpec((None, None, W_B, tm), lambda b, i: (b, i, 0, 0)),
        ],
        out_shape=[
            jax.ShapeDtypeStruct((bsz, s // tm, W_QK_B, tm), BF16),
            jax.ShapeDtypeStruct((bsz, s, W_QK_B), BF16),
            jax.ShapeDtypeStruct((bsz, s // tm, W_B, tm), BF16),
        ],
        compiler_params=_cparams(("parallel", "parallel")),
        name="mla_pre",
    )(proj, proj, pos3, qg, wq, kvg, wk, wvt, inv)


def _mla_attn_kernel(qt_ref, k_ref, vt_ref, o_ref, st_a, st_b):
    n_chunks, _, tk = vt_ref.shape
    nt, _, tq = qt_ref.shape

    def scores(t, st_scr):
        qt = qt_ref[t]
        m8 = jnp.full((8, tq), -jnp.inf, F32)
        for c in range(n_chunks):
            st = _dot(k_ref[c * tk:(c + 1) * tk, :], qt)
            st_scr[c * tk:(c + 1) * tk, :] = st
            m8 = jnp.maximum(m8, jnp.max(st.reshape(tk // 8, 8, tq), axis=0))
        return jnp.max(m8, axis=0, keepdims=True)

    def output(t, st_scr, m):
        l8 = jnp.zeros((8, tq), F32)
        acc = jnp.zeros((D_V_B, tq), F32)
        for c in range(n_chunks):
            p = jnp.exp2(st_scr[c * tk:(c + 1) * tk, :] - m)
            l8 = l8 + jnp.sum(p.reshape(tk // 8, 8, tq), axis=0)
            acc = acc + _dot(vt_ref[c], p.astype(BF16))
        l = jnp.sum(l8, axis=0, keepdims=True)
        o_ref[pl.ds(pl.multiple_of(t * tq, tq), tq), :] = (acc / l).T.astype(BF16)

    m_first = scores(0, st_a)
    if nt == 1:
        output(0, st_a, m_first)
        return

    def pair(u, m_a):
        t = 2 * u
        m_b = scores(t + 1, st_b)
        output(t, st_a, m_a)
        m_a = scores(t + 2, st_a)
        output(t + 1, st_b, m_b)
        return m_a

    m_a = lax.fori_loop(0, nt // 2 - 1, pair, m_first)
    m_b = scores(nt - 1, st_b)
    output(nt - 2, st_a, m_a)
    output(nt - 1, st_b, m_b)


def _mla_attention(qt, k, vt):
    bsz, s, _ = k.shape
    n_chunks, tk = vt.shape[1], vt.shape[3]
    nt, tq = qt.shape[1], qt.shape[3]
    assert nt == 1 or nt % 2 == 0
    return pl.pallas_call(
        _mla_attn_kernel,
        grid=(bsz, N_H_B),
        in_specs=[
            pl.BlockSpec((None, nt, D_QK_B, tq), lambda b, h: (b, 0, h, 0)),
            pl.BlockSpec((None, s, D_QK_B), lambda b, h: (b, 0, h)),
            pl.BlockSpec((None, n_chunks, D_V_B, tk), lambda b, h: (b, 0, h, 0)),
        ],
        out_specs=pl.BlockSpec((None, s, D_V_B), lambda b, h: (b, 0, h)),
        out_shape=jax.ShapeDtypeStruct((bsz, s, W_B), BF16),
        scratch_shapes=[pltpu.VMEM((s, tq), F32), pltpu.VMEM((s, tq), F32)],
        compiler_params=_cparams(("parallel", "parallel")),
        name="mla_attn",
    )(qt, k, vt)


def _log_sigmoid(x):
    return jnp.minimum(x, 0.0) - jnp.log1p(jnp.exp(-jnp.abs(x)))


def _split3(x):
    hi = x.astype(BF16).astype(F32)
    r1 = x - hi
    mid = r1.astype(BF16).astype(F32)
    return jnp.concatenate([hi, mid, r1 - mid], axis=-1)


def _mlstm_kernel(q_ref, k_ref, vf_ref, vh_ref, of_ref, oh_ref, gate_ref, gb_ref, hg_ref, y_ref,
                  hf_scr, hb_scr, tile_scr, kw_scr, rowq_scr, bp_scr, cf_scr, cb_scr, gt_scr, vext_scr):
    s_len = q_ref.shape[0]
    L = CHUNK
    nc = s_len // L
    head = pl.program_id(1)
    ii = lax.broadcasted_iota(jnp.int32, (L, L), 0)
    jj = lax.broadcasted_iota(jnp.int32, (L, L), 1)
    ones = jnp.ones((L, L), BF16)
    odd_head = (head % 2) == 1
    lane = lax.broadcasted_iota(jnp.int32, (L, LANES), 1)
    eye3 = jnp.concatenate([(ii == jj).astype(BF16)] * 3, axis=1)
    tri_ones = [jnp.concatenate([jnp.concatenate([m.astype(BF16), ones], axis=1)] * 3, axis=0)
                for m in (ii <= jj, ii >= jj)]
    masks = (jj <= ii, jj >= ii)
    h_scrs = (hf_scr, hb_scr)
    c_scrs = (cf_scr, cb_scr)

    def own_half(half_ref, r0):
        x = half_ref[pl.ds(r0, L), :].astype(F32)
        return jnp.where(odd_head, pltpu.roll(x, LANES // 2, 1), x)

    def load_v_ext(r0):
        tail = jnp.where(lane < DV_C - LANES, own_half(vh_ref, r0), jnp.where(lane == ONES_COL - LANES, 1.0, 0.0))
        return jnp.concatenate([vf_ref[pl.ds(r0, L), :], tail.astype(BF16)], axis=1)

    def transpose_gates(c, _):
        g = gate_ref[pl.ds(pl.multiple_of(c * L, L), L), :] + gb_ref[...]
        gt_scr[pl.ds(pl.multiple_of(c * 16, 16), 16), :] = g.T[0:16, :]
        return 0

    lax.fori_loop(0, nc, transpose_gates, 0, unroll=4)

    for d in range(2):
        li = gt_scr[pl.ds(d * N_H_C + head, nc, stride=16), :]
        lf = _log_sigmoid(gt_scr[pl.ds((2 + d) * N_H_C + head, nc, stride=16), :])
        bt = _dot(_split3(lf).astype(BF16), tri_ones[d])
        b, tot = bt[:, :L] * LOG2E, bt[:, L:] * LOG2E
        r = li * LOG2E - b
        m_loc = tot + jnp.broadcast_to(jnp.max(r, axis=-1, keepdims=True), r.shape)
        rowq_scr[d, 0] = r
        rowq_scr[d, 1] = jnp.exp2(tot + r - m_loc)
        rowq_scr[d, 2] = tot
        rowq_scr[d, 3] = m_loc
        bp_scr[d] = _split3(b + math.log2(DQK_C ** -0.5))

    def prep(c, _):
        r0 = pl.multiple_of(c * L, L)
        q, k = q_ref[pl.ds(r0, L), :], k_ref[pl.ds(r0, L), :]
        s = _dot_nt(q, k)
        kt = k.astype(F32).T
        p = []
        for d in range(2):
            u = jnp.where(masks[d], rowq_scr[d, 0, pl.ds(c, 1), :], -jnp.inf)
            ct = jnp.broadcast_to(jnp.max(u, axis=-1, keepdims=True), (L, L))
            p.append((s * jnp.exp2(u - ct)).astype(BF16))
            tile_scr[pl.ds(r0, L), (2 + d) * L:(3 + d) * L] = ct
            kw_scr[d, c] = (kt * rowq_scr[d, 1, pl.ds(c, 1), :]).astype(BF16)
        rhs = jnp.concatenate([jnp.broadcast_to(bp_scr[d, pl.ds(c, 1), :], (L, 3 * L)) for d in range(2)], axis=0)
        tile_scr[pl.ds(r0, L), 0:2 * L] = _dot_nt(eye3, rhs.astype(BF16))
        v_ext = load_v_ext(r0)
        vext_scr[pl.ds(r0, L), :] = v_ext
        intra = _dot(jnp.concatenate(p, axis=0), v_ext)
        hf_scr[pl.ds(r0, L), :] = intra[:L]
        hb_scr[pl.ds(r0, L), :] = intra[L:]
        return 0

    lax.fori_loop(0, nc, prep, 0, unroll=4)

    cf_scr[...] = jnp.zeros_like(cf_scr)
    cb_scr[...] = jnp.zeros_like(cb_scr)

    def one_dir(c, m_state, d):
        h_scr, c_scr = h_scrs[d], c_scrs[d]
        r0 = pl.multiple_of(c * L, L)
        tot, m_loc = rowq_scr[d, 2, pl.ds(c, 1), :], rowq_scr[d, 3, pl.ds(c, 1), :]
        bt = tile_scr[pl.ds(r0, L), d * L:(d + 1) * L]
        ct = tile_scr[pl.ds(r0, L), (2 + d) * L:(3 + d) * L]
        v_ext = vext_scr[pl.ds(r0, L), :]
        c_state = c_scr[...]
        qc = _dot(q_ref[pl.ds(r0, L), :], c_state.astype(BF16))
        mx = jnp.maximum(m_state, ct)
        iw = jnp.exp2(m_state - mx)
        xw = jnp.exp2(ct - mx)
        num = [iw * qc[:, j * L:(j + 1) * L] + xw * h_scr[pl.ds(r0, L), j * L:(j + 1) * L] for j in range(2)]
        den = jnp.broadcast_to(num[1][:, ONES_COL - L:ONES_COL - L + 1], (L, L))
        inv = 1.0 / jnp.maximum(jnp.abs(den), jnp.exp2(-(bt + mx)))
        h_scr[pl.ds(r0, L), 0:L] = num[0] * inv
        h_scr[pl.ds(r0, L), L:2 * L] = num[1] * inv
        c_loc = _dot(kw_scr[d, c], v_ext)
        m_new = jnp.maximum(tot + m_state, m_loc)
        sp, sl = jnp.exp2(tot + m_state - m_new), jnp.exp2(m_loc - m_new)
        c_scr[...] = jnp.concatenate(
            [sp * c_state[:, j * L:(j + 1) * L] + sl * c_loc[:, j * L:(j + 1) * L] for j in range(2)], axis=1)
        return m_new

    real = lax.broadcasted_iota(jnp.int32, (L, DV_PAD), 1) < DV_C

    def finish(c):
        r0 = pl.multiple_of(c * L, L)
        h = jnp.where(real, hf_scr[pl.ds(r0, L), :] + hb_scr[pl.ds(r0, L), :], 0.0)
        ms = jnp.sum(h * h, axis=-1, keepdims=True) * (1.0 / DV_C)
        hn = h * lax.rsqrt(ms + EPS) * hg_ref[...]
        o = jnp.concatenate([of_ref[pl.ds(r0, L), :].astype(F32), own_half(oh_ref, r0)], axis=1)
        y = jax.nn.sigmoid(o) * hn
        y_ref[pl.ds(r0, L), :] = y.astype(BF16)

    def scan(c, carry):
        return one_dir(c, carry[0], 0), one_dir(nc - 1 - c, carry[1], 1)

    def scan_and_finish(c, carry):
        carry = scan(c, carry)
        finish(c)
        finish(nc - 1 - c)
        return carry

    neg = jnp.full((1, LANES), -jnp.inf, F32)
    carry = lax.fori_loop(0, nc // 2, scan, (neg, neg), unroll=4)
    lax.fori_loop(nc // 2, nc, scan_and_finish, carry, unroll=4)


def _mlstm(proj, gates, gate_b, head_g):
    bsz, s, _ = proj.shape
    nc = s // CHUNK
    return pl.pallas_call(
        _mlstm_kernel,
        grid=(bsz, N_H_C),
        in_specs=[
            pl.BlockSpec((None, s, DQK_PAD), lambda b, h: (b, 0, OFF_MQ // DQK_PAD + h)),
            pl.BlockSpec((None, s, DQK_PAD), lambda b, h: (b, 0, OFF_MK // DQK_PAD + h)),
            pl.BlockSpec((None, s, LANES), lambda b, h: (b, 0, OFF_MV // LANES + 3 * (h // 2) + 2 * (h % 2))),
            pl.BlockSpec((None, s, LANES), lambda b, h: (b, 0, OFF_MV // LANES + 3 * (h // 2) + 1)),
            pl.BlockSpec((None, s, LANES), lambda b, h: (b, 0, OFF_MO // LANES + 3 * (h // 2) + 2 * (h % 2))),
            pl.BlockSpec((None, s, LANES), lambda b, h: (b, 0, OFF_MO // LANES + 3 * (h // 2) + 1)),
            pl.BlockSpec((None, s, LANES), lambda b, h: (b, 0, 0)),
            pl.BlockSpec((1, LANES), lambda b, h: (0, 0)),
            pl.BlockSpec((None, 1, DV_PAD), lambda b, h: (h, 0, 0)),
        ],
        out_specs=pl.BlockSpec((None, s, DV_PAD), lambda b, h: (b, 0, h)),
        out_shape=jax.ShapeDtypeStruct((bsz, s, W_C_PAD), BF16),
        scratch_shapes=[
            pltpu.VMEM((s, DV_PAD), F32), pltpu.VMEM((s, DV_PAD), F32),
            pltpu.VMEM((s, 4 * CHUNK), F32),
            pltpu.VMEM((2, nc, DQK_PAD, CHUNK), BF16),
            pltpu.VMEM((2, 4, nc, CHUNK), F32),
            pltpu.VMEM((2, nc, 3 * CHUNK), F32),
            pltpu.VMEM((DQK_PAD, DV_PAD), F32), pltpu.VMEM((DQK_PAD, DV_PAD), F32),
            pltpu.VMEM((nc * 16, CHUNK), F32),
            pltpu.VMEM((s, DV_PAD), BF16),
        ],
        compiler_params=_cparams(("parallel", "arbitrary")),
        name="mlstm",
    )(proj, proj, proj, proj, proj, proj, gates, gate_b, head_g)


def _out_kernel(ya_ref, yb_ref, yc_ref, wa_ref, wb_ref, wc_ref, pg_ref, gate_ref, x_ref, o_ref, y_scr):
    for r in range(0, o_ref.shape[0], SUB_ROWS):
        rows = slice(r, r + SUB_ROWS)
        y_scr[rows, :] = (_dot(ya_ref[rows, :], wa_ref[...]) + _dot(yb_ref[rows, :], wb_ref[...])
                          + _dot(yc_ref[rows, :], wc_ref[...]))
        _norm_residual_rows(y_scr, x_ref, pg_ref, gate_ref, o_ref, r, SUB_ROWS)


def _out_proj(ya, yb, yc, wa, wb, wc, pg, modl, x):
    bsz, s, d = x.shape
    tm = min(2 * SUB_ROWS, s)
    row = lambda w: pl.BlockSpec((None, tm, w), lambda b, i: (b, i, 0))
    const = lambda shape: pl.BlockSpec(shape, lambda b, i: (0,) * len(shape), pipeline_mode=pl.Buffered(1))
    return pl.pallas_call(
        _out_kernel,
        grid=(bsz, s // tm),
        in_specs=[
            row(W_A), row(W_B), row(W_C_PAD),
            const((W_A, d)), const((W_B, d)), const((W_C_PAD, d)), const((1, d)),
            pl.BlockSpec((None, 1, d), lambda b, i: (b, 0, 2)),
            row(d),
        ],
        out_specs=row(d),
        out_shape=jax.ShapeDtypeStruct((bsz, s, d), F32),
        scratch_shapes=[pltpu.VMEM((tm, d), F32)],
        compiler_params=_cparams(("parallel", "parallel")),
        name="out_proj",
    )(ya, yb, yc, wa, wb, wc, pg, modl, x)


FFN_TF = 512
FFN_VMEM_LIMIT = 62 * 1024 * 1024


def _ffn_kernel(x_ref, g_ref, sc_ref, sh_ref, wg_hbm, wu_hbm, wd_hbm, pg_ref, gate_ref, o_ref,
                h_scr, wg_buf, wu_buf, wd_buf, sem, *, layer, nf):
    tm = x_ref.shape[0]
    tf = wg_buf.shape[2]
    step = pl.program_id(0) * pl.num_programs(1) + pl.program_id(1)
    n_steps = pl.num_programs(0) * pl.num_programs(1)
    subs = [(r, slice(r, r + SUB_ROWS)) for r in range(0, tm, SUB_ROWS)]

    def slot_of(k):
        return (k + step) % 2

    def copies(k, slot):
        cols = pl.ds(pl.multiple_of(k * tf, tf), tf)
        return (pltpu.make_async_copy(wg_hbm.at[layer, :, cols], wg_buf.at[slot], sem.at[0, slot]),
                pltpu.make_async_copy(wu_hbm.at[layer, :, cols], wu_buf.at[slot], sem.at[1, slot]),
                pltpu.make_async_copy(wd_hbm.at[layer, cols, :], wd_buf.at[slot], sem.at[2, slot]))

    def fetch(k, slot):
        for cp in copies(k, slot):
            cp.start()

    def wait(k, slot):
        for cp in copies(k, slot):
            cp.wait()

    def partial_down(rows, slot):
        h = h_scr[rows, :]
        a = _dot(h, wg_buf[slot])
        u = _dot(h, wu_buf[slot])
        hid = (a * jax.nn.sigmoid(a)) * u
        return _dot(hid.astype(BF16), wd_buf[slot])

    @pl.when(step == 0)
    def _():
        fetch(0, slot_of(0))

    fetch(1, slot_of(1))
    wait(0, slot_of(0))
    for r, rows in subs:
        _mod_norm_rows(x_ref, g_ref, sc_ref, sh_ref, h_scr, r, SUB_ROWS)
        o_ref[rows, :] = partial_down(rows, slot_of(0))

    def middle(k):
        fetch(k + 1, slot_of(k + 1))
        wait(k, slot_of(k))
        o_ref[...] += partial_down(slice(0, tm), slot_of(k))

    def pair(u, carry):
        middle(1 + 2 * u)
        middle(2 + 2 * u)
        return carry

    lax.fori_loop(0, (nf - 3) // 2, pair, 0)
    middle(nf - 2)

    @pl.when(step < n_steps - 1)
    def _():
        fetch(0, slot_of(nf))
    wait(nf - 1, slot_of(nf - 1))
    for r, rows in subs:
        o_ref[rows, :] += partial_down(rows, slot_of(nf - 1))
        _norm_residual_rows(o_ref, x_ref, pg_ref, gate_ref, o_ref, r, SUB_ROWS)


def _ffn(x, g, modl, wg, wu, wd, pg, layer):
    bsz, s, d = x.shape
    f = wg.shape[2]
    nf = f // FFN_TF
    assert nf % 2 == 1 and nf >= 3
    tm = min(GEMM_TM, s)
    row = pl.BlockSpec((None, tm, d), lambda b, i: (b, i, 0))
    vec = pl.BlockSpec((1, d), lambda b, i: (0, 0))
    modv = lambda k: pl.BlockSpec((None, 1, d), lambda b, i: (b, 0, k))
    hbm = pl.BlockSpec(memory_space=pl.ANY)
    return pl.pallas_call(
        functools.partial(_ffn_kernel, layer=layer, nf=nf),
        grid=(bsz, s // tm),
        in_specs=[row, vec, modv(4), modv(3), hbm, hbm, hbm, vec, modv(5)],
        out_specs=row,
        out_shape=jax.ShapeDtypeStruct((bsz, s, d), F32),
        scratch_shapes=[
            pltpu.VMEM((tm, d), BF16),
            pltpu.VMEM((2, d, FFN_TF), BF16), pltpu.VMEM((2, d, FFN_TF), BF16), pltpu.VMEM((2, FFN_TF, d), BF16),
            pltpu.SemaphoreType.DMA((3, 2)),
        ],
        compiler_params=_cparams(("arbitrary", "arbitrary"), FFN_VMEM_LIMIT),
        name="ffn",
    )(x, g, modl, modl, wg, wu, wd, pg, modl)


CAST_BLOCK_BYTES = 4 * 1024 * 1024


def _cast_kernel(w_ref, o_ref):
    o_ref[...] = w_ref[...].astype(BF16)


def _to_bf16(w):
    depth, rows, cols = w.shape
    rb = rows
    while rb * cols * 4 > CAST_BLOCK_BYTES and rb % 16 == 0:
        rb //= 2
    spec = pl.BlockSpec((None, rb, cols), lambda l, i: (l, i, 0))
    return pl.pallas_call(
        _cast_kernel,
        grid=(depth, rows // rb),
        in_specs=[spec],
        out_specs=spec,
        out_shape=jax.ShapeDtypeStruct(w.shape, BF16),
        compiler_params=_cparams(("parallel", "parallel")),
        name="cast_bf16",
    )(w)


def _pair_heads_a(w, axis):
    shape = w.shape
    w = jnp.moveaxis(w, axis, -1)
    lead = w.shape[:-1]
    w = w.reshape(lead + (2, 2, GROUP_A, DH_A))
    w = jnp.swapaxes(w, -3, -2)
    w = w.reshape(lead + (W_A,))
    return jnp.moveaxis(w, -1, axis).reshape(shape)


def _pad_heads(w, n_heads, width, padded, axis):
    w = jnp.moveaxis(w, axis, -1)
    lead = w.shape[:-1]
    w = w.reshape(lead + (n_heads, width))
    w = jnp.pad(w, [(0, 0)] * len(lead) + [(0, 0), (0, padded - width)])
    w = w.reshape(lead + (n_heads * padded,))
    return jnp.moveaxis(w, -1, axis)


def _pair_heads_c(w):
    d = w.shape[0]
    w = w.reshape(d, N_H_C // 2, 2, DV_C)
    even, odd = w[:, :, 0], w[:, :, 1]
    out = jnp.concatenate([even[..., :LANES], even[..., LANES:], odd[..., LANES:], odd[..., :LANES]], axis=-1)
    return out.reshape(d, W_C)


def _layout_w_in(w):
    d = w.shape[0]
    sizes = (W_A, N_KV_A * DH_A, N_KV_A * DH_A, Q_LORA, KV_LORA, D_ROPE,
             N_H_C * DQK_C, N_H_C * DQK_C, W_C, 4 * N_H_C, W_C)
    pts = np.cumsum(sizes)[:-1].tolist()
    aq, ak, av, bcq, bckv, bkr, cq, ck, cv, cg, co = jnp.split(w, pts, axis=1)
    parts = [
        _pair_heads_a(aq, 1), ak, av, bckv,
        jnp.pad(cg, ((0, 0), (0, LANES - 4 * N_H_C))),
        bcq, bkr,
        _pad_heads(cq, N_H_C, DQK_C, DQK_PAD, 1), _pad_heads(ck, N_H_C, DQK_C, DQK_PAD, 1),
        _pair_heads_c(cv), _pair_heads_c(co),
    ]
    out = jnp.concatenate(parts, axis=1).astype(BF16)
    assert out.shape == (d, IN_WIDTH_PAD)
    return out.reshape(d, IN_WIDTH_PAD // IN_TN, IN_TN).transpose(1, 0, 2)


def _layout_w_uq(w):
    w = w.reshape(Q_LORA, N_H_B, D_NOPE + D_ROPE)
    nope = w[:, :, :D_NOPE].reshape(Q_LORA, N_H_B * D_NOPE)
    rope = w[:, :, D_NOPE:].reshape(Q_LORA, N_H_B * D_ROPE)
    w = jnp.concatenate([nope, rope], axis=1)
    return jnp.pad(w, ((0, W_CQKR - Q_LORA), (0, 0))).astype(BF16)


def _layout_w_ukv(w):
    w = w.reshape(KV_LORA, N_H_B, D_NOPE + D_V_B)
    kn = w[:, :, :D_NOPE].reshape(KV_LORA, N_H_B * D_NOPE)
    v = w[:, :, D_NOPE:].reshape(KV_LORA, N_H_B * D_V_B)
    return kn.astype(BF16), v.T.astype(BF16)


def kernel(x, c, positions, mod_w, mod_b, pre_mix_g, post_mix_g, pre_ffn_g, post_ffn_g, w_in, attn_sink, mla_q_norm_g, mla_w_uq, mla_kv_norm_g, mla_w_ukv, mlstm_gate_b, mlstm_head_g, w_out, ffn_w_gate, ffn_w_up, ffn_w_down):
    depth = mod_w.shape[0]
    bsz, s, d = x.shape
    mod = _modulation(c, mod_w, mod_b)
    pos3 = positions.reshape(bsz, 1, s)
    inv = 1.0 / (ROPE_THETA ** (jnp.arange(0, D_ROPE, 2, dtype=F32) / D_ROPE))
    inv_col = inv.reshape(D_ROPE // 2, 1)
    wg_bf, wu_bf, wd_bf = _to_bf16(ffn_w_gate), _to_bf16(ffn_w_up), _to_bf16(ffn_w_down)
    for l in range(depth):
        modl = mod[l].reshape(bsz, 1, 6 * d)
        proj, gates = _in_proj(x, pre_mix_g[l].reshape(1, d), modl, _layout_w_in(w_in[l]))
        ya = _window_attention(proj, attn_sink[l])
        qg = jnp.pad(mla_q_norm_g[l], (0, W_CQKR - Q_LORA)).reshape(1, W_CQKR)
        wk, wvt = _layout_w_ukv(mla_w_ukv[l])
        qb, kb, vtb = _mla_pre(proj, pos3, qg, _layout_w_uq(mla_w_uq[l]),
                               mla_kv_norm_g[l].reshape(1, KV_LORA), wk, wvt, inv_col)
        yb = _mla_attention(qb, kb, vtb)
        gate_b = jnp.pad(mlstm_gate_b[l], (0, LANES - 4 * N_H_C)).reshape(1, LANES)
        head_g = jnp.pad(mlstm_head_g[l].reshape(N_H_C, 1, DV_C), ((0, 0), (0, 0), (0, DV_PAD - DV_C)))
        yc = _mlstm(proj, gates, gate_b, head_g)
        wo = w_out[l]
        wa = _pair_heads_a(wo[:W_A], 0).astype(BF16)
        wb = wo[W_A:W_A + W_B].astype(BF16)
        wc = _pad_heads(wo[W_A + W_B:], N_H_C, DV_C, DV_PAD, 0).astype(BF16)
        x = _out_proj(ya, yb, yc, wa, wb, wc, post_mix_g[l].reshape(1, d), modl, x)
        x = _ffn(x, pre_ffn_g[l].reshape(1, d), modl, wg_bf, wu_bf, wd_bf, post_ffn_g[l].reshape(1, d), l)
    return x
```

```python
import functools
import math

import numpy as np
import jax
import jax.numpy as jnp
from jax import lax
from jax.experimental import pallas as pl
from jax.experimental.pallas import tpu as pltpu

F32 = jnp.float32
BF16 = jnp.bfloat16

EPS = 1e-6
N_Q_A, N_KV_A, DH_A, GROUP_A = 12, 4, 64, 3
WINDOW = 128
BLOCK_A = 128
N_H_B, Q_LORA, KV_LORA, D_NOPE, D_ROPE, D_V_B = 4, 448, 128, 128, 64, 128
ROPE_THETA = 10000.0
N_H_C, DQK_C, DV_C = 4, 96, 192
W_A, W_B, W_C = N_Q_A * DH_A, N_H_B * D_V_B, N_H_C * DV_C
W_KV_A = N_KV_A * DH_A
W_CQKR = Q_LORA + D_ROPE
W_UQ_B = N_H_B * (D_NOPE + D_ROPE)

LANES = 128
DQK_PAD = 128
DV_PAD = 256
ONES_COL = DV_C
D_QK_B = D_NOPE + LANES
W_QK_B = N_H_B * D_QK_B
VMEM_LIMIT = 56 * 1024 * 1024

OFF_AK = W_A
OFF_AV = OFF_AK + W_KV_A
OFF_CKV = OFF_AV + W_KV_A
OFF_CG = OFF_CKV + KV_LORA
OFF_CQKR = OFF_CG + LANES
OFF_MQ = OFF_CQKR + W_CQKR
OFF_MK = OFF_MQ + N_H_C * DQK_PAD
OFF_MV = OFF_MK + N_H_C * DQK_PAD
OFF_MO = OFF_MV + W_C
IN_WIDTH_PAD = OFF_MO + W_C

CHUNK = 128


def _alibi_slopes(n):
    def pow2(m):
        start = 2.0 ** (-8.0 / m)
        return [start ** (i + 1) for i in range(m)]
    if math.log2(n).is_integer():
        s = pow2(n)
    else:
        p = 2 ** math.floor(math.log2(n))
        s = pow2(p) + pow2(2 * p)[0::2][: n - p]
    return [float(np.float32(v)) for v in s]


SLOPES_A = _alibi_slopes(N_Q_A)


def _cparams(sem, vmem_limit=VMEM_LIMIT):
    return pltpu.CompilerParams(dimension_semantics=sem, vmem_limit_bytes=vmem_limit)


def _dot(a, b):
    return jnp.dot(a, b, preferred_element_type=F32)


def _dot_nt(a, b):
    return lax.dot_general(a, b, (((1,), (1,)), ((), ())), preferred_element_type=F32)


def _mod_kernel(c_ref, w_ref, b_ref, o_ref):
    c = c_ref[...]
    s = c * jax.nn.sigmoid(c)
    o_ref[...] = _dot(s.astype(BF16), w_ref[...].astype(BF16)) + b_ref[...]


MOD_TN = 2048


def _modulation(c, mod_w, mod_b):
    depth, d, n = mod_w.shape
    bsz = c.shape[0]
    tn = MOD_TN
    return pl.pallas_call(
        _mod_kernel,
        grid=(depth, n // tn),
        in_specs=[
            pl.BlockSpec((bsz, d), lambda l, j: (0, 0)),
            pl.BlockSpec((None, d, tn), lambda l, j: (l, 0, j)),
            pl.BlockSpec((None, 1, tn), lambda l, j: (l, 0, j)),
        ],
        out_specs=pl.BlockSpec((None, bsz, tn), lambda l, j: (l, 0, j)),
        out_shape=jax.ShapeDtypeStruct((depth, bsz, n), F32),
        compiler_params=_cparams(("parallel", "parallel")),
        name="adaln_mod",
    )(c, mod_w, mod_b.reshape(depth, 1, n))


GEMM_TM = 1024
ROW_SLAB = 16
SUB_ROWS = 256


def _row_slabs(first_row, n_rows):
    return [slice(r, r + ROW_SLAB) for r in range(first_row, first_row + n_rows, ROW_SLAB)]


def _mod_norm_rows(x_ref, g_ref, sc_ref, sh_ref, h_ref, first_row, n_rows):
    gain = g_ref[...] * (1.0 + sc_ref[...])
    shift = sh_ref[...]
    for rows in _row_slabs(first_row, n_rows):
        x = x_ref[rows, :]
        ms = jnp.mean(x * x, axis=-1, keepdims=True)
        h_ref[rows, :] = (x * lax.rsqrt(ms + EPS) * gain + shift).astype(BF16)


def _norm_residual_rows(y_ref, x_ref, pg_ref, gate_ref, o_ref, first_row, n_rows):
    gain = gate_ref[...] * pg_ref[...]
    for rows in _row_slabs(first_row, n_rows):
        y = y_ref[rows, :]
        ms = jnp.mean(y * y, axis=-1, keepdims=True)
        o_ref[rows, :] = x_ref[rows, :] + y * lax.rsqrt(ms + EPS) * gain


IN_TN = 1536
GATE_TILE = OFF_CG // IN_TN
GATE_OFF = OFF_CG % IN_TN


def _in_kernel(x_ref, g_ref, sc_ref, sh_ref, w_ref, cs_ref, o_ref, gate_ref, h_scr):
    j = pl.program_id(2)
    tm = x_ref.shape[0]

    def project(rows):
        acc = _dot(h_scr[rows, :], w_ref[j])
        o_ref[rows, :] = (acc * cs_ref[...]).astype(BF16)
        return acc

    @pl.when(j == 0)
    def _():
        for r in range(0, tm, SUB_ROWS):
            rows = slice(r, r + SUB_ROWS)
            _mod_norm_rows(x_ref, g_ref, sc_ref, sh_ref, h_scr, r, SUB_ROWS)
            acc = project(rows)
            if GATE_TILE == 0:
                gate_ref[rows, :] = acc[:, GATE_OFF:GATE_OFF + LANES]

    @pl.when(j != 0)
    def _():
        acc = project(slice(0, tm))
        if GATE_TILE != 0:
            @pl.when(j == GATE_TILE)
            def _():
                gate_ref[...] = acc[:, GATE_OFF:GATE_OFF + LANES]


def _in_proj(x, g, modl, w):
    bsz, s, d = x.shape
    n = w.shape[0] * IN_TN
    tm = min(GEMM_TM, s)
    col_scale = jnp.where(jnp.arange(n) < W_A, WIN_QSCALE, 1.0).astype(F32).reshape(1, n)
    return pl.pallas_call(
        _in_kernel,
        grid=(bsz, s // tm, n // IN_TN),
        in_specs=[
            pl.BlockSpec((None, tm, d), lambda b, i, j: (b, i, 0)),
            pl.BlockSpec((1, d), lambda b, i, j: (0, 0)),
            pl.BlockSpec((None, 1, d), lambda b, i, j: (b, 0, 1)),
            pl.BlockSpec((None, 1, d), lambda b, i, j: (b, 0, 0)),
            pl.BlockSpec((n // IN_TN, d, IN_TN), lambda b, i, j: (0, 0, 0), pipeline_mode=pl.Buffered(1)),
            pl.BlockSpec((1, IN_TN), lambda b, i, j: (0, j)),
        ],
        out_specs=[
            pl.BlockSpec((None, tm, IN_TN), lambda b, i, j: (b, i, j)),
            pl.BlockSpec((None, tm, LANES), lambda b, i, j: (b, i, 0)),
        ],
        out_shape=[
            jax.ShapeDtypeStruct((bsz, s, n), BF16),
            jax.ShapeDtypeStruct((bsz, s, LANES), F32),
        ],
        scratch_shapes=[pltpu.VMEM((tm, d), BF16)],
        compiler_params=_cparams(("parallel", "parallel", "arbitrary")),
        name="in_proj",
    )(x, g, modl, modl, w, col_scale)


LOG2E = math.log2(math.e)
WIN_QSCALE = DH_A ** -0.5 * LOG2E


WIN_QB = 8


def _win_kernel(sink_ref, q_ref, kp_ref, kc_ref, kn_ref, vp_ref, vc_ref, vn_ref, o_ref, *, nb):
    n = pl.program_id(1)
    blk = BLOCK_A
    n_sub = q_ref.shape[0] // blk
    nk = 3 * blk
    qi = lax.broadcasted_iota(jnp.int32, (blk, nk), 0)
    kj = lax.broadcasted_iota(jnp.int32, (blk, nk), 1)
    dist = jnp.abs(qi - kj + blk)
    k_all = jnp.concatenate([kp_ref[...], kc_ref[...], kn_ref[...]], axis=0)
    v_all = jnp.concatenate([vp_ref[...], vc_ref[...], vn_ref[...]], axis=0)
    low_k = lax.broadcasted_iota(jnp.int32, (nk, LANES), 1) < DH_A
    low_q = lax.broadcasted_iota(jnp.int32, (blk, LANES), 1) < DH_A
    zero = jnp.zeros((nk, LANES), BF16)
    for sub in range(n_sub):
        g = n * n_sub + sub
        rows = slice(sub * blk, (sub + 1) * blk)
        in_seq = ((kj >= blk) | (g > 0)) & ((kj < 2 * blk) | (g < nb - 1))
        dist_masked = jnp.where((dist <= WINDOW) & in_seq, dist.astype(F32), jnp.inf)
        for p in range(2):
            kt = k_all[sub * blk:sub * blk + nk, p * LANES:(p + 1) * LANES]
            vt = v_all[sub * blk:sub * blk + nk, p * LANES:(p + 1) * LANES]
            k_big = jnp.concatenate([jnp.where(low_k, kt, zero), jnp.where(low_k, zero, kt)], axis=0)
            v_big = jnp.concatenate([jnp.where(low_k, vt, zero), jnp.where(low_k, zero, vt)], axis=0)
            q3 = jnp.concatenate([q_ref[rows, (p * GROUP_A + a) * LANES:(p * GROUP_A + a + 1) * LANES]
                                  for a in range(GROUP_A)], axis=0)
            s3 = _dot_nt(q3, k_big)
            p_rows, inv_rows = [], []
            for a in range(GROUP_A):
                p_halves, inv_halves = [], []
                for half in range(2):
                    head = (2 * p + half) * GROUP_A + a
                    s = (s3[a * blk:(a + 1) * blk, half * nk:(half + 1) * nk]
                         - (SLOPES_A[head] * LOG2E) * dist_masked)
                    sink = sink_ref[head] * LOG2E
                    m = jnp.maximum(jnp.max(s, axis=-1, keepdims=True), sink)
                    e = jnp.exp2(s - m)
                    l = jnp.sum(e, axis=-1, keepdims=True) + jnp.exp2(sink - m)
                    p_halves.append(e.astype(BF16))
                    inv_halves.append(1.0 / l)
                p_rows.append(jnp.concatenate(p_halves, axis=1))
                inv_rows.append(jnp.where(low_q, inv_halves[0], inv_halves[1]))
            o3 = _dot(jnp.concatenate(p_rows, axis=0), v_big)
            for a in range(GROUP_A):
                t = p * GROUP_A + a
                o_ref[rows, t * LANES:(t + 1) * LANES] = (o3[a * blk:(a + 1) * blk] * inv_rows[a]).astype(BF16)


def _window_attention(proj, sinks):
    bsz, s, _ = proj.shape
    blk = BLOCK_A
    nb = s // blk
    qb = math.gcd(nb, WIN_QB)
    tq = qb * blk
    kb, vb = OFF_AK // W_KV_A, OFF_AV // W_KV_A
    prev = lambda n: jnp.maximum(n * qb - 1, 0)
    nxt = lambda n: jnp.minimum((n + 1) * qb, nb - 1)
    return pl.pallas_call(
        functools.partial(_win_kernel, nb=nb),
        grid=(bsz, nb // qb),
        in_specs=[
            pl.BlockSpec(memory_space=pltpu.SMEM),
            pl.BlockSpec((None, tq, W_A), lambda b, n: (b, n, 0)),
            pl.BlockSpec((None, blk, W_KV_A), lambda b, n: (b, prev(n), kb)),
            pl.BlockSpec((None, tq, W_KV_A), lambda b, n: (b, n, kb)),
            pl.BlockSpec((None, blk, W_KV_A), lambda b, n: (b, nxt(n), kb)),
            pl.BlockSpec((None, blk, W_KV_A), lambda b, n: (b, prev(n), vb)),
            pl.BlockSpec((None, tq, W_KV_A), lambda b, n: (b, n, vb)),
            pl.BlockSpec((None, blk, W_KV_A), lambda b, n: (b, nxt(n), vb)),
        ],
        out_specs=pl.BlockSpec((None, tq, W_A), lambda b, n: (b, n, 0)),
        out_shape=jax.ShapeDtypeStruct((bsz, s, W_A), BF16),
        compiler_params=_cparams(("parallel", "parallel")),
        name="win_attn",
    )(sinks, proj, proj, proj, proj, proj, proj, proj)


MLA_TK = 512
MLA_QSCALE = (D_NOPE + D_ROPE) ** -0.5 * math.log2(math.e)


def _mla_pre_kernel(cqkr_ref, ckv_ref, pos_ref, qg_ref, wq_ref, kvg_ref, wk_ref, wvt_ref, inv_ref,
                    qt_ref, k_ref, vt_ref):
    tm = cqkr_ref.shape[0]
    t = cqkr_ref[...].astype(F32)
    lane = lax.broadcasted_iota(jnp.int32, (tm, 4 * LANES), 1)
    cq = jnp.where(lane < Q_LORA, t, 0.0)
    ms = jnp.sum(cq * cq, axis=-1, keepdims=True) * (1.0 / Q_LORA)
    qn = cq * lax.rsqrt(ms + EPS) * qg_ref[...]
    q = _dot(qn.astype(BF16), wq_ref[...]) * MLA_QSCALE
    c = ckv_ref[...].astype(F32)
    ms = jnp.mean(c * c, axis=-1, keepdims=True)
    kvn = (c * lax.rsqrt(ms + EPS) * kvg_ref[...]).astype(BF16)
    kn = _dot(kvn, wk_ref[...])
    vt_ref[...] = _dot_nt(wvt_ref[...], kvn).astype(BF16)

    ang_t = inv_ref[...] * pos_ref[...].astype(F32)
    cs_t, sn_t = jnp.cos(ang_t), jnp.sin(ang_t)
    cs = jnp.concatenate([cs_t] * 4, axis=0).T
    sgn_sn = jnp.concatenate([-sn_t, sn_t] * 2, axis=0).T
    l128 = lax.broadcasted_iota(jnp.int32, (tm, LANES), 1)
    first = (l128 % D_ROPE) < (D_ROPE // 2)

    def rope(x):
        swapped = jnp.where(first, pltpu.roll(x, LANES - D_ROPE // 2, 1), pltpu.roll(x, D_ROPE // 2, 1))
        return x * cs + swapped * sgn_sn

    kr = rope(t[:, 3 * LANES:4 * LANES])
    kr_hi = jnp.where(l128 >= D_ROPE, kr, 0.0)
    kr_lo = pltpu.roll(kr_hi, D_ROPE, 1)
    kr_tiles = (kr_lo.astype(BF16), kr_hi.astype(BF16))
    for pair in range(2):
        qr_t = rope(q[:, (4 + pair) * LANES:(5 + pair) * LANES]).T.astype(BF16)
        for half in range(2):
            h = 2 * pair + half
            qt_ref[(2 * h) * LANES:(2 * h + 1) * LANES, :] = q[:, h * LANES:(h + 1) * LANES].T.astype(BF16)
            qt_ref[(2 * h + 1) * LANES:(2 * h + 2) * LANES, :] = qr_t
            k_ref[:, (2 * h) * LANES:(2 * h + 1) * LANES] = kn[:, h * LANES:(h + 1) * LANES].astype(BF16)
            k_ref[:, (2 * h + 1) * LANES:(2 * h + 2) * LANES] = kr_tiles[half]


def _mla_pre(proj, pos3, qg, wq, kvg, wk, wvt, inv):
    bsz, s, _ = proj.shape
    tm = MLA_TK
    const = lambda shape: pl.BlockSpec(shape, lambda b, i: (0,) * len(shape))
    return pl.pallas_call(
        _mla_pre_kernel,
        grid=(bsz, s // tm),
        in_specs=[
            pl.BlockSpec((None, tm, W_CQKR), lambda b, i: (b, i, OFF_CQKR // W_CQKR)),
            pl.BlockSpec((None, tm, KV_LORA), lambda b, i: (b, i, OFF_CKV // KV_LORA)),
            pl.BlockSpec((None, 1, tm), lambda b, i: (b, 0, i)),
            const((1, W_CQKR)), const((W_CQKR, W_UQ_B)), const((1, KV_LORA)), const((KV_LORA, W_B)),
            const((W_B, KV_LORA)), const((D_ROPE // 2, 1)),
        ],
        out_specs=[
            pl.BlockSpec((None, None, W_QK_B, tm), lambda b, i: (b, i, 0, 0)),
            pl.BlockSpec((None, tm, W_QK_B), lambda b, i: (b, i, 0)),
            pl.BlockSpec((None, None, W_B, tm), lambda b, i: (b, i, 0, 0)),
        ],
        out_shape=[
            jax.ShapeDtypeStruct((bsz, s // tm, W_QK_B, tm), BF16),
            jax.ShapeDtypeStruct((bsz, s, W_QK_B), BF16),
            jax.ShapeDtypeStruct((bsz, s // tm, W_B, tm), BF16),
        ],
        compiler_params=_cparams(("parallel", "parallel")),
        name="mla_pre",
    )(proj, proj, pos3, qg, wq, kvg, wk, wvt, inv)


def _mla_attn_kernel(qt_ref, k_ref, vt_ref, o_ref, st_a, st_b):
    n_chunks, _, tk = vt_ref.shape
    nt, _, tq = qt_ref.shape

    def scores(t, st_scr):
        qt = qt_ref[t]
        m8 = jnp.full((8, tq), -jnp.inf, F32)
        for c in range(n_chunks):
            st = _dot(k_ref[c * tk:(c + 1) * tk, :], qt)
            st_scr[c * tk:(c + 1) * tk, :] = st
            m8 = jnp.maximum(m8, jnp.max(st.reshape(tk // 8, 8, tq), axis=0))
        return jnp.max(m8, axis=0, keepdims=True)

    def output(t, st_scr, m):
        l8 = jnp.zeros((8, tq), F32)
        acc = jnp.zeros((D_V_B, tq), F32)
        for c in range(n_chunks):
            p = jnp.exp2(st_scr[c * tk:(c + 1) * tk, :] - m)
            l8 = l8 + jnp.sum(p.reshape(tk // 8, 8, tq), axis=0)
            acc = acc + _dot(vt_ref[c], p.astype(BF16))
        l = jnp.sum(l8, axis=0, keepdims=True)
        o_ref[pl.ds(pl.multiple_of(t * tq, tq), tq), :] = (acc / l).T.astype(BF16)

    m_first = scores(0, st_a)
    if nt == 1:
        output(0, st_a, m_first)
        return

    def pair(u, m_a):
        t = 2 * u
        m_b = scores(t + 1, st_b)
        output(t, st_a, m_a)
        m_a = scores(t + 2, st_a)
        output(t + 1, st_b, m_b)
        return m_a

    m_a = lax.fori_loop(0, nt // 2 - 1, pair, m_first)
    m_b = scores(nt - 1, st_b)
    output(nt - 2, st_a, m_a)
    output(nt - 1, st_b, m_b)


def _mla_attention(qt, k, vt):
    bsz, s, _ = k.shape
    n_chunks, tk = vt.shape[1], vt.shape[3]
    nt, tq = qt.shape[1], qt.shape[3]
    assert nt == 1 or nt % 2 == 0
    return pl.pallas_call(
        _mla_attn_kernel,
        grid=(bsz, N_H_B),
        in_specs=[
            pl.BlockSpec((None, nt, D_QK_B, tq), lambda b, h: (b, 0, h, 0)),
            pl.BlockSpec((None, s, D_QK_B), lambda b, h: (b, 0, h)),
            pl.BlockSpec((None, n_chunks, D_V_B, tk), lambda b, h: (b, 0, h, 0)),
        ],
        out_specs=pl.BlockSpec((None, s, D_V_B), lambda b, h: (b, 0, h)),
        out_shape=jax.ShapeDtypeStruct((bsz, s, W_B), BF16),
        scratch_shapes=[pltpu.VMEM((s, tq), F32), pltpu.VMEM((s, tq), F32)],
        compiler_params=_cparams(("parallel", "parallel")),
        name="mla_attn",
    )(qt, k, vt)


def _log_sigmoid(x):
    return jnp.minimum(x, 0.0) - jnp.log1p(jnp.exp(-jnp.abs(x)))


def _split3(x):
    hi = x.astype(BF16).astype(F32)
    r1 = x - hi
    mid = r1.astype(BF16).astype(F32)
    return jnp.concatenate([hi, mid, r1 - mid], axis=-1)


def _mlstm_kernel(q_ref, k_ref, vf_ref, vh_ref, of_ref, oh_ref, gate_ref, gb_ref, hg_ref, yf_ref, yt_ref,
                  hf_scr, hb_scr, tile_scr, kw_scr, rowq_scr, bp_scr, cf_scr, cb_scr, gt_scr, vext_scr):
    s_len = q_ref.shape[0]
    L = CHUNK
    nc = s_len // L
    head = pl.program_id(1)
    ii = lax.broadcasted_iota(jnp.int32, (L, L), 0)
    jj = lax.broadcasted_iota(jnp.int32, (L, L), 1)
    ones = jnp.ones((L, L), BF16)
    odd_head = (head % 2) == 1
    lane = lax.broadcasted_iota(jnp.int32, (L, LANES), 1)
    eye3 = jnp.concatenate([(ii == jj).astype(BF16)] * 3, axis=1)
    tri_ones = [jnp.concatenate([jnp.concatenate([m.astype(BF16), ones], axis=1)] * 3, axis=0)
                for m in (ii <= jj, ii >= jj)]
    masks = (jj <= ii, jj >= ii)
    h_scrs = (hf_scr, hb_scr)
    c_scrs = (cf_scr, cb_scr)

    def own_half(half_ref, r0):
        x = half_ref[pl.ds(r0, L), :].astype(F32)
        return jnp.where(odd_head, pltpu.roll(x, LANES // 2, 1), x)

    def load_v_ext(r0):
        tail = jnp.where(lane < DV_C - LANES, own_half(vh_ref, r0), jnp.where(lane == ONES_COL - LANES, 1.0, 0.0))
        return jnp.concatenate([vf_ref[pl.ds(r0, L), :], tail.astype(BF16)], axis=1)

    def transpose_gates(c, _):
        g = gate_ref[pl.ds(pl.multiple_of(c * L, L), L), :] + gb_ref[...]
        gt_scr[pl.ds(pl.multiple_of(c * 16, 16), 16), :] = g.T[0:16, :]
        return 0

    lax.fori_loop(0, nc, transpose_gates, 0, unroll=4)

    for d in range(2):
        li = gt_scr[pl.ds(d * N_H_C + head, nc, stride=16), :]
        lf = _log_sigmoid(gt_scr[pl.ds((2 + d) * N_H_C + head, nc, stride=16), :])
        bt = _dot(_split3(lf).astype(BF16), tri_ones[d])
        b, tot = bt[:, :L] * LOG2E, bt[:, L:] * LOG2E
        r = li * LOG2E - b
        m_loc = tot + jnp.broadcast_to(jnp.max(r, axis=-1, keepdims=True), r.shape)
        rowq_scr[d, 0] = r
        rowq_scr[d, 1] = jnp.exp2(tot + r - m_loc)
        rowq_scr[d, 2] = tot
        rowq_scr[d, 3] = m_loc
        bp_scr[d] = _split3(b + math.log2(DQK_C ** -0.5))

    def prep(c, _):
        r0 = pl.multiple_of(c * L, L)
        q, k = q_ref[pl.ds(r0, L), :], k_ref[pl.ds(r0, L), :]
        s = _dot_nt(q, k)
        kt = k.astype(F32).T
        p = []
        for d in range(2):
            u = jnp.where(masks[d], rowq_scr[d, 0, pl.ds(c, 1), :], -jnp.inf)
            ct = jnp.broadcast_to(jnp.max(u, axis=-1, keepdims=True), (L, L))
            p.append((s * jnp.exp2(u - ct)).astype(BF16))
            tile_scr[pl.ds(r0, L), (2 + d) * L:(3 + d) * L] = ct
            kw_scr[d, c] = (kt * rowq_scr[d, 1, pl.ds(c, 1), :]).astype(BF16)
        rhs = jnp.concatenate([jnp.broadcast_to(bp_scr[d, pl.ds(c, 1), :], (L, 3 * L)) for d in range(2)], axis=0)
        tile_scr[pl.ds(r0, L), 0:2 * L] = _dot_nt(eye3, rhs.astype(BF16))
        v_ext = load_v_ext(r0)
        vext_scr[pl.ds(r0, L), :] = v_ext
        intra = _dot(jnp.concatenate(p, axis=0), v_ext)
        hf_scr[pl.ds(r0, L), :] = intra[:L]
        hb_scr[pl.ds(r0, L), :] = intra[L:]
        return 0

    lax.fori_loop(0, nc, prep, 0, unroll=4)

    cf_scr[...] = jnp.zeros_like(cf_scr)

    @pl.when(jnp.logical_not(odd_head))
    def _():
        yt_ref[...] = jnp.zeros_like(yt_ref)
    cb_scr[...] = jnp.zeros_like(cb_scr)

    def one_dir(c, m_state, d):
        h_scr, c_scr = h_scrs[d], c_scrs[d]
        r0 = pl.multiple_of(c * L, L)
        tot, m_loc = rowq_scr[d, 2, pl.ds(c, 1), :], rowq_scr[d, 3, pl.ds(c, 1), :]
        bt = tile_scr[pl.ds(r0, L), d * L:(d + 1) * L]
        ct = tile_scr[pl.ds(r0, L), (2 + d) * L:(3 + d) * L]
        v_ext = vext_scr[pl.ds(r0, L), :]
        c_state = c_scr[...]
        qc = _dot(q_ref[pl.ds(r0, L), :], c_state.astype(BF16))
        mx = jnp.maximum(m_state, ct)
        iw = jnp.exp2(m_state - mx)
        xw = jnp.exp2(ct - mx)
        num = [iw * qc[:, j * L:(j + 1) * L] + xw * h_scr[pl.ds(r0, L), j * L:(j + 1) * L] for j in range(2)]
        den = jnp.broadcast_to(num[1][:, ONES_COL - L:ONES_COL - L + 1], (L, L))
        inv = 1.0 / jnp.maximum(jnp.abs(den), jnp.exp2(-(bt + mx)))
        h_scr[pl.ds(r0, L), 0:L] = num[0] * inv
        h_scr[pl.ds(r0, L), L:2 * L] = num[1] * inv
        c_loc = _dot(kw_scr[d, c], v_ext)
        m_new = jnp.maximum(tot + m_state, m_loc)
        sp, sl = jnp.exp2(tot + m_state - m_new), jnp.exp2(m_loc - m_new)
        c_scr[...] = jnp.concatenate(
            [sp * c_state[:, j * L:(j + 1) * L] + sl * c_loc[:, j * L:(j + 1) * L] for j in range(2)], axis=1)
        return m_new

    real = lax.broadcasted_iota(jnp.int32, (L, DV_PAD), 1) < DV_C

    def finish(c):
        r0 = pl.multiple_of(c * L, L)
        h = jnp.where(real, hf_scr[pl.ds(r0, L), :] + hb_scr[pl.ds(r0, L), :], 0.0)
        ms = jnp.sum(h * h, axis=-1, keepdims=True) * (1.0 / DV_C)
        hn = h * lax.rsqrt(ms + EPS) * hg_ref[...]
        o = jnp.concatenate([of_ref[pl.ds(r0, L), :].astype(F32), own_half(oh_ref, r0)], axis=1)
        y = jax.nn.sigmoid(o) * hn
        yf_ref[pl.ds(r0, L), :] = y[:, :LANES].astype(BF16)
        tail = y[:, LANES:]
        tail = jnp.where(odd_head, pltpu.roll(tail, LANES // 2, 1), tail)
        yt_ref[pl.ds(r0, L), :] = yt_ref[pl.ds(r0, L), :] + tail.astype(BF16)

    def scan(c, carry):
        return one_dir(c, carry[0], 0), one_dir(nc - 1 - c, carry[1], 1)

    def scan_and_finish(c, carry):
        carry = scan(c, carry)
        finish(c)
        finish(nc - 1 - c)
        return carry

    neg = jnp.full((1, LANES), -jnp.inf, F32)
    carry = lax.fori_loop(0, nc // 2, scan, (neg, neg), unroll=4)
    lax.fori_loop(nc // 2, nc, scan_and_finish, carry, unroll=4)


def _mlstm(proj, gates, gate_b, head_g):
    bsz, s, _ = proj.shape
    nc = s // CHUNK
    return pl.pallas_call(
        _mlstm_kernel,
        grid=(bsz, N_H_C),
        in_specs=[
            pl.BlockSpec((None, s, DQK_PAD), lambda b, h: (b, 0, OFF_MQ // DQK_PAD + h)),
            pl.BlockSpec((None, s, DQK_PAD), lambda b, h: (b, 0, OFF_MK // DQK_PAD + h)),
            pl.BlockSpec((None, s, LANES), lambda b, h: (b, 0, OFF_MV // LANES + 3 * (h // 2) + 2 * (h % 2))),
            pl.BlockSpec((None, s, LANES), lambda b, h: (b, 0, OFF_MV // LANES + 3 * (h // 2) + 1)),
            pl.BlockSpec((None, s, LANES), lambda b, h: (b, 0, OFF_MO // LANES + 3 * (h // 2) + 2 * (h % 2))),
            pl.BlockSpec((None, s, LANES), lambda b, h: (b, 0, OFF_MO // LANES + 3 * (h // 2) + 1)),
            pl.BlockSpec((None, s, LANES), lambda b, h: (b, 0, 0)),
            pl.BlockSpec((1, LANES), lambda b, h: (0, 0)),
            pl.BlockSpec((None, 1, DV_PAD), lambda b, h: (h, 0, 0)),
        ],
        out_specs=[
            pl.BlockSpec((None, s, LANES), lambda b, h: (b, 0, h)),
            pl.BlockSpec((None, s, LANES), lambda b, h: (b, 0, h // 2)),
        ],
        out_shape=[
            jax.ShapeDtypeStruct((bsz, s, N_H_C * LANES), BF16),
            jax.ShapeDtypeStruct((bsz, s, N_H_C // 2 * LANES), BF16),
        ],
        scratch_shapes=[
            pltpu.VMEM((s, DV_PAD), F32), pltpu.VMEM((s, DV_PAD), F32),
            pltpu.VMEM((s, 4 * CHUNK), F32),
            pltpu.VMEM((2, nc, DQK_PAD, CHUNK), BF16),
            pltpu.VMEM((2, 4, nc, CHUNK), F32),
            pltpu.VMEM((2, nc, 3 * CHUNK), F32),
            pltpu.VMEM((DQK_PAD, DV_PAD), F32), pltpu.VMEM((DQK_PAD, DV_PAD), F32),
            pltpu.VMEM((nc * 16, CHUNK), F32),
            pltpu.VMEM((s, DV_PAD), BF16),
        ],
        compiler_params=_cparams(("parallel", "arbitrary")),
        name="mlstm",
    )(proj, proj, proj, proj, proj, proj, gates, gate_b, head_g)


def _out_kernel(ya_ref, yb_ref, ycf_ref, yct_ref, wa_ref, wb_ref, wcf_ref, wct_ref, pg_ref, gate_ref, x_ref,
                o_ref, y_scr):
    for r in range(0, o_ref.shape[0], SUB_ROWS):
        rows = slice(r, r + SUB_ROWS)
        y_scr[rows, :] = (_dot(ya_ref[rows, :], wa_ref[...]) + _dot(yb_ref[rows, :], wb_ref[...])
                          + _dot(ycf_ref[rows, :], wcf_ref[...]) + _dot(yct_ref[rows, :], wct_ref[...]))
        _norm_residual_rows(y_scr, x_ref, pg_ref, gate_ref, o_ref, r, SUB_ROWS)


def _out_proj(ya, yb, ycf, yct, wa, wb, wcf, wct, pg, modl, x):
    bsz, s, d = x.shape
    w_cf, w_ct = ycf.shape[2], yct.shape[2]
    tm = min(2 * SUB_ROWS, s)
    row = lambda w: pl.BlockSpec((None, tm, w), lambda b, i: (b, i, 0))
    const = lambda shape: pl.BlockSpec(shape, lambda b, i: (0,) * len(shape), pipeline_mode=pl.Buffered(1))
    return pl.pallas_call(
        _out_kernel,
        grid=(bsz, s // tm),
        in_specs=[
            row(W_A), row(W_B), row(w_cf), row(w_ct),
            const((W_A, d)), const((W_B, d)), const((w_cf, d)), const((w_ct, d)), const((1, d)),
            pl.BlockSpec((None, 1, d), lambda b, i: (b, 0, 2)),
            row(d),
        ],
        out_specs=row(d),
        out_shape=jax.ShapeDtypeStruct((bsz, s, d), F32),
        scratch_shapes=[pltpu.VMEM((tm, d), F32)],
        compiler_params=_cparams(("parallel", "parallel")),
        name="out_proj",
    )(ya, yb, ycf, yct, wa, wb, wcf, wct, pg, modl, x)


FFN_TF = 512
FFN_VMEM_LIMIT = 62 * 1024 * 1024


def _ffn_kernel(x_ref, g_ref, sc_ref, sh_ref, wg_hbm, wu_hbm, wd_hbm, pg_ref, gate_ref, o_ref,
                h_scr, wg_buf, wu_buf, wd_buf, sem, *, layer, nf):
    tm = x_ref.shape[0]
    tf = wg_buf.shape[2]
    step = pl.program_id(0) * pl.num_programs(1) + pl.program_id(1)
    n_steps = pl.num_programs(0) * pl.num_programs(1)
    subs = [(r, slice(r, r + SUB_ROWS)) for r in range(0, tm, SUB_ROWS)]

    def slot_of(k):
        return (k + step) % 2

    def copies(k, slot):
        cols = pl.ds(pl.multiple_of(k * tf, tf), tf)
        return (pltpu.make_async_copy(wg_hbm.at[layer, :, cols], wg_buf.at[slot], sem.at[0, slot]),
                pltpu.make_async_copy(wu_hbm.at[layer, :, cols], wu_buf.at[slot], sem.at[1, slot]),
                pltpu.make_async_copy(wd_hbm.at[layer, cols, :], wd_buf.at[slot], sem.at[2, slot]))

    def fetch(k, slot):
        for cp in copies(k, slot):
            cp.start()

    def wait(k, slot):
        for cp in copies(k, slot):
            cp.wait()

    def partial_down(rows, slot):
        h = h_scr[rows, :]
        a = _dot(h, wg_buf[slot])
        u = _dot(h, wu_buf[slot])
        hid = (a * jax.nn.sigmoid(a)) * u
        return _dot(hid.astype(BF16), wd_buf[slot])

    @pl.when(step == 0)
    def _():
        fetch(0, slot_of(0))

    fetch(1, slot_of(1))
    wait(0, slot_of(0))
    for r, rows in subs:
        _mod_norm_rows(x_ref, g_ref, sc_ref, sh_ref, h_scr, r, SUB_ROWS)
        o_ref[rows, :] = partial_down(rows, slot_of(0))

    def middle(k):
        fetch(k + 1, slot_of(k + 1))
        wait(k, slot_of(k))
        o_ref[...] += partial_down(slice(0, tm), slot_of(k))

    def pair(u, carry):
        middle(1 + 2 * u)
        middle(2 + 2 * u)
        return carry

    lax.fori_loop(0, (nf - 3) // 2, pair, 0)
    middle(nf - 2)

    @pl.when(step < n_steps - 1)
    def _():
        fetch(0, slot_of(nf))
    wait(nf - 1, slot_of(nf - 1))
    for r, rows in subs:
        o_ref[rows, :] += partial_down(rows, slot_of(nf - 1))
        _norm_residual_rows(o_ref, x_ref, pg_ref, gate_ref, o_ref, r, SUB_ROWS)


def _ffn(x, g, modl, wg, wu, wd, pg, layer):
    bsz, s, d = x.shape
    f = wg.shape[2]
    nf = f // FFN_TF
    assert nf % 2 == 1 and nf >= 3
    tm = min(GEMM_TM, s)
    row = pl.BlockSpec((None, tm, d), lambda b, i: (b, i, 0))
    vec = pl.BlockSpec((1, d), lambda b, i: (0, 0))
    modv = lambda k: pl.BlockSpec((None, 1, d), lambda b, i: (b, 0, k))
    hbm = pl.BlockSpec(memory_space=pl.ANY)
    return pl.pallas_call(
        functools.partial(_ffn_kernel, layer=layer, nf=nf),
        grid=(bsz, s // tm),
        in_specs=[row, vec, modv(4), modv(3), hbm, hbm, hbm, vec, modv(5)],
        out_specs=row,
        out_shape=jax.ShapeDtypeStruct((bsz, s, d), F32),
        scratch_shapes=[
            pltpu.VMEM((tm, d), BF16),
            pltpu.VMEM((2, d, FFN_TF), BF16), pltpu.VMEM((2, d, FFN_TF), BF16), pltpu.VMEM((2, FFN_TF, d), BF16),
            pltpu.SemaphoreType.DMA((3, 2)),
        ],
        compiler_params=_cparams(("arbitrary", "arbitrary"), FFN_VMEM_LIMIT),
        name="ffn",
    )(x, g, modl, modl, wg, wu, wd, pg, modl)


CAST_BLOCK_BYTES = 4 * 1024 * 1024


def _cast_kernel(w_ref, o_ref):
    o_ref[...] = w_ref[...].astype(BF16)


def _to_bf16(w):
    depth, rows, cols = w.shape
    rb = rows
    while rb * cols * 4 > CAST_BLOCK_BYTES and rb % 16 == 0:
        rb //= 2
    spec = pl.BlockSpec((None, rb, cols), lambda l, i: (l, i, 0))
    return pl.pallas_call(
        _cast_kernel,
        grid=(depth, rows // rb),
        in_specs=[spec],
        out_specs=spec,
        out_shape=jax.ShapeDtypeStruct(w.shape, BF16),
        compiler_params=_cparams(("parallel", "parallel")),
        name="cast_bf16",
    )(w)


def _pair_heads_a(w, axis):
    shape = w.shape
    w = jnp.moveaxis(w, axis, -1)
    lead = w.shape[:-1]
    w = w.reshape(lead + (2, 2, GROUP_A, DH_A))
    w = jnp.swapaxes(w, -3, -2)
    w = w.reshape(lead + (W_A,))
    return jnp.moveaxis(w, -1, axis).reshape(shape)


def _pad_heads(w, n_heads, width, padded, axis):
    w = jnp.moveaxis(w, axis, -1)
    lead = w.shape[:-1]
    w = w.reshape(lead + (n_heads, width))
    w = jnp.pad(w, [(0, 0)] * len(lead) + [(0, 0), (0, padded - width)])
    w = w.reshape(lead + (n_heads * padded,))
    return jnp.moveaxis(w, -1, axis)


def _pair_heads_c(w):
    d = w.shape[0]
    w = w.reshape(d, N_H_C // 2, 2, DV_C)
    even, odd = w[:, :, 0], w[:, :, 1]
    out = jnp.concatenate([even[..., :LANES], even[..., LANES:], odd[..., LANES:], odd[..., :LANES]], axis=-1)
    return out.reshape(d, W_C)


def _layout_w_in(w):
    d = w.shape[0]
    sizes = (W_A, N_KV_A * DH_A, N_KV_A * DH_A, Q_LORA, KV_LORA, D_ROPE,
             N_H_C * DQK_C, N_H_C * DQK_C, W_C, 4 * N_H_C, W_C)
    pts = np.cumsum(sizes)[:-1].tolist()
    aq, ak, av, bcq, bckv, bkr, cq, ck, cv, cg, co = jnp.split(w, pts, axis=1)
    parts = [
        _pair_heads_a(aq, 1), ak, av, bckv,
        jnp.pad(cg, ((0, 0), (0, LANES - 4 * N_H_C))),
        bcq, bkr,
        _pad_heads(cq, N_H_C, DQK_C, DQK_PAD, 1), _pad_heads(ck, N_H_C, DQK_C, DQK_PAD, 1),
        _pair_heads_c(cv), _pair_heads_c(co),
    ]
    out = jnp.concatenate(parts, axis=1).astype(BF16)
    assert out.shape == (d, IN_WIDTH_PAD)
    return out.reshape(d, IN_WIDTH_PAD // IN_TN, IN_TN).transpose(1, 0, 2)


def _layout_w_uq(w):
    w = w.reshape(Q_LORA, N_H_B, D_NOPE + D_ROPE)
    nope = w[:, :, :D_NOPE].reshape(Q_LORA, N_H_B * D_NOPE)
    rope = w[:, :, D_NOPE:].reshape(Q_LORA, N_H_B * D_ROPE)
    w = jnp.concatenate([nope, rope], axis=1)
    return jnp.pad(w, ((0, W_CQKR - Q_LORA), (0, 0))).astype(BF16)


def _layout_w_ukv(w):
    w = w.reshape(KV_LORA, N_H_B, D_NOPE + D_V_B)
    kn = w[:, :, :D_NOPE].reshape(KV_LORA, N_H_B * D_NOPE)
    v = w[:, :, D_NOPE:].reshape(KV_LORA, N_H_B * D_V_B)
    return kn.astype(BF16), v.T.astype(BF16)


def kernel(x, c, positions, mod_w, mod_b, pre_mix_g, post_mix_g, pre_ffn_g, post_ffn_g, w_in, attn_sink, mla_q_norm_g, mla_w_uq, mla_kv_norm_g, mla_w_ukv, mlstm_gate_b, mlstm_head_g, w_out, ffn_w_gate, ffn_w_up, ffn_w_down):
    depth = mod_w.shape[0]
    bsz, s, d = x.shape
    mod = _modulation(c, mod_w, mod_b)
    pos3 = positions.reshape(bsz, 1, s)
    inv = 1.0 / (ROPE_THETA ** (jnp.arange(0, D_ROPE, 2, dtype=F32) / D_ROPE))
    inv_col = inv.reshape(D_ROPE // 2, 1)
    wg_bf, wu_bf, wd_bf = _to_bf16(ffn_w_gate), _to_bf16(ffn_w_up), _to_bf16(ffn_w_down)
    for l in range(depth):
        modl = mod[l].reshape(bsz, 1, 6 * d)
        proj, gates = _in_proj(x, pre_mix_g[l].reshape(1, d), modl, _layout_w_in(w_in[l]))
        ya = _window_attention(proj, attn_sink[l])
        qg = jnp.pad(mla_q_norm_g[l], (0, W_CQKR - Q_LORA)).reshape(1, W_CQKR)
        wk, wvt = _layout_w_ukv(mla_w_ukv[l])
        qb, kb, vtb = _mla_pre(proj, pos3, qg, _layout_w_uq(mla_w_uq[l]),
                               mla_kv_norm_g[l].reshape(1, KV_LORA), wk, wvt, inv_col)
        yb = _mla_attention(qb, kb, vtb)
        gate_b = jnp.pad(mlstm_gate_b[l], (0, LANES - 4 * N_H_C)).reshape(1, LANES)
        head_g = jnp.pad(mlstm_head_g[l].reshape(N_H_C, 1, DV_C), ((0, 0), (0, 0), (0, DV_PAD - DV_C)))
        ycf, yct = _mlstm(proj, gates, gate_b, head_g)
        wo = w_out[l]
        wa = _pair_heads_a(wo[:W_A], 0).astype(BF16)
        wb = wo[W_A:W_A + W_B].astype(BF16)
        wc = wo[W_A + W_B:].reshape(N_H_C, DV_C, d)
        wcf = wc[:, :LANES].reshape(N_H_C * LANES, d).astype(BF16)
        wct = wc[:, LANES:].reshape(N_H_C * (DV_C - LANES), d).astype(BF16)
        x = _out_proj(ya, yb, ycf, yct, wa, wb, wcf, wct, post_mix_g[l].reshape(1, d), modl, x)
        x = _ffn(x, pre_ffn_g[l].reshape(1, d), modl, wg_bf, wu_bf, wd_bf, post_ffn_g[l].reshape(1, d), l)
    return x
```

```python
import functools
import math

import numpy as np
import jax
import jax.numpy as jnp
from jax import lax
from jax.experimental import pallas as pl
from jax.experimental.pallas import tpu as pltpu

F32 = jnp.float32
BF16 = jnp.bfloat16

EPS = 1e-6
N_Q_A, N_KV_A, DH_A, GROUP_A = 12, 4, 64, 3
WINDOW = 128
BLOCK_A = 128
N_H_B, Q_LORA, KV_LORA, D_NOPE, D_ROPE, D_V_B = 4, 448, 128, 128, 64, 128
ROPE_THETA = 10000.0
N_H_C, DQK_C, DV_C = 4, 96, 192
W_A, W_B, W_C = N_Q_A * DH_A, N_H_B * D_V_B, N_H_C * DV_C
W_KV_A = N_KV_A * DH_A
W_CQKR = Q_LORA + D_ROPE
W_UQ_B = N_H_B * (D_NOPE + D_ROPE)

LANES = 128
DQK_PAD = 128
DV_PAD = 256
ONES_COL = DV_C
D_QK_B = D_NOPE + LANES
W_QK_B = N_H_B * D_QK_B
VMEM_LIMIT = 56 * 1024 * 1024

OFF_AK = W_A
OFF_AV = OFF_AK + W_KV_A
OFF_CKV = OFF_AV + W_KV_A
OFF_CG = OFF_CKV + KV_LORA
OFF_CQKR = OFF_CG + LANES
OFF_MQ = OFF_CQKR + W_CQKR
OFF_MK = OFF_MQ + N_H_C * DQK_PAD
OFF_MV = OFF_MK + N_H_C * DQK_PAD
OFF_MO = OFF_MV + W_C
IN_WIDTH_PAD = OFF_MO + W_C

CHUNK = 128


def _alibi_slopes(n):
    def pow2(m):
        start = 2.0 ** (-8.0 / m)
        return [start ** (i + 1) for i in range(m)]
    if math.log2(n).is_integer():
        s = pow2(n)
    else:
        p = 2 ** math.floor(math.log2(n))
        s = pow2(p) + pow2(2 * p)[0::2][: n - p]
    return [float(np.float32(v)) for v in s]


SLOPES_A = _alibi_slopes(N_Q_A)


def _cparams(sem, vmem_limit=VMEM_LIMIT):
    return pltpu.CompilerParams(dimension_semantics=sem, vmem_limit_bytes=vmem_limit)


def _dot(a, b):
    return jnp.dot(a, b, preferred_element_type=F32)


def _dot_nt(a, b):
    return lax.dot_general(a, b, (((1,), (1,)), ((), ())), preferred_element_type=F32)


def _mod_kernel(c_ref, w_ref, b_ref, o_ref):
    c = c_ref[...]
    s = c * jax.nn.sigmoid(c)
    o_ref[...] = _dot(s.astype(BF16), w_ref[...].astype(BF16)) + b_ref[...]


MOD_TN = 2048


def _modulation(c, mod_w, mod_b):
    depth, d, n = mod_w.shape
    bsz = c.shape[0]
    tn = MOD_TN
    return pl.pallas_call(
        _mod_kernel,
        grid=(depth, n // tn),
        in_specs=[
            pl.BlockSpec((bsz, d), lambda l, j: (0, 0)),
            pl.BlockSpec((None, d, tn), lambda l, j: (l, 0, j)),
            pl.BlockSpec((None, 1, tn), lambda l, j: (l, 0, j)),
        ],
        out_specs=pl.BlockSpec((None, bsz, tn), lambda l, j: (l, 0, j)),
        out_shape=jax.ShapeDtypeStruct((depth, bsz, n), F32),
        compiler_params=_cparams(("parallel", "parallel")),
        name="adaln_mod",
    )(c, mod_w, mod_b.reshape(depth, 1, n))


GEMM_TM = 1024
ROW_SLAB = 16
SUB_ROWS = 256


def _row_slabs(first_row, n_rows):
    return [slice(r, r + ROW_SLAB) for r in range(first_row, first_row + n_rows, ROW_SLAB)]


def _mod_norm_rows(x_ref, g_ref, sc_ref, sh_ref, h_ref, first_row, n_rows):
    gain = g_ref[...] * (1.0 + sc_ref[...])
    shift = sh_ref[...]
    for rows in _row_slabs(first_row, n_rows):
        x = x_ref[rows, :]
        ms = jnp.mean(x * x, axis=-1, keepdims=True)
        h_ref[rows, :] = (x * lax.rsqrt(ms + EPS) * gain + shift).astype(BF16)


def _norm_residual_rows(y_ref, x_ref, pg_ref, gate_ref, o_ref, first_row, n_rows):
    gain = gate_ref[...] * pg_ref[...]
    for rows in _row_slabs(first_row, n_rows):
        y = y_ref[rows, :]
        ms = jnp.mean(y * y, axis=-1, keepdims=True)
        o_ref[rows, :] = x_ref[rows, :] + y * lax.rsqrt(ms + EPS) * gain


IN_TN = 1536
GATE_TILE = OFF_CG // IN_TN
GATE_OFF = OFF_CG % IN_TN


def _in_kernel(x_ref, g_ref, sc_ref, sh_ref, w_ref, cs_ref, o_ref, gate_ref, h_scr):
    j = pl.program_id(2)
    tm = x_ref.shape[0]

    def project(rows):
        acc = _dot(h_scr[rows, :], w_ref[j])
        o_ref[rows, :] = (acc * cs_ref[...]).astype(BF16)
        return acc

    @pl.when(j == 0)
    def _():
        for r in range(0, tm, SUB_ROWS):
            rows = slice(r, r + SUB_ROWS)
            _mod_norm_rows(x_ref, g_ref, sc_ref, sh_ref, h_scr, r, SUB_ROWS)
            acc = project(rows)
            if GATE_TILE == 0:
                gate_ref[rows, :] = acc[:, GATE_OFF:GATE_OFF + LANES]

    @pl.when(j != 0)
    def _():
        acc = project(slice(0, tm))
        if GATE_TILE != 0:
            @pl.when(j == GATE_TILE)
            def _():
                gate_ref[...] = acc[:, GATE_OFF:GATE_OFF + LANES]


def _in_proj(x, g, modl, w):
    bsz, s, d = x.shape
    n = w.shape[0] * IN_TN
    tm = min(GEMM_TM, s)
    col_scale = jnp.where(jnp.arange(n) < W_A, WIN_QSCALE, 1.0).astype(F32).reshape(1, n)
    return pl.pallas_call(
        _in_kernel,
        grid=(bsz, s // tm, n // IN_TN),
        in_specs=[
            pl.BlockSpec((None, tm, d), lambda b, i, j: (b, i, 0)),
            pl.BlockSpec((1, d), lambda b, i, j: (0, 0)),
            pl.BlockSpec((None, 1, d), lambda b, i, j: (b, 0, 1)),
            pl.BlockSpec((None, 1, d), lambda b, i, j: (b, 0, 0)),
            pl.BlockSpec((n // IN_TN, d, IN_TN), lambda b, i, j: (0, 0, 0), pipeline_mode=pl.Buffered(1)),
            pl.BlockSpec((1, IN_TN), lambda b, i, j: (0, j)),
        ],
        out_specs=[
            pl.BlockSpec((None, tm, IN_TN), lambda b, i, j: (b, i, j)),
            pl.BlockSpec((None, tm, LANES), lambda b, i, j: (b, i, 0)),
        ],
        out_shape=[
            jax.ShapeDtypeStruct((bsz, s, n), BF16),
            jax.ShapeDtypeStruct((bsz, s, LANES), F32),
        ],
        scratch_shapes=[pltpu.VMEM((tm, d), BF16)],
        compiler_params=_cparams(("parallel", "parallel", "arbitrary")),
        name="in_proj",
    )(x, g, modl, modl, w, col_scale)


LOG2E = math.log2(math.e)
WIN_QSCALE = DH_A ** -0.5 * LOG2E


WIN_QB = 8


def _win_kernel(sink_ref, q_ref, kp_ref, kc_ref, kn_ref, vp_ref, vc_ref, vn_ref, o_ref, *, nb):
    n = pl.program_id(1)
    blk = BLOCK_A
    n_sub = q_ref.shape[0] // blk
    nk = 3 * blk
    qi = lax.broadcasted_iota(jnp.int32, (blk, nk), 0)
    kj = lax.broadcasted_iota(jnp.int32, (blk, nk), 1)
    dist = jnp.abs(qi - kj + blk)
    k_all = jnp.concatenate([kp_ref[...], kc_ref[...], kn_ref[...]], axis=0)
    v_all = jnp.concatenate([vp_ref[...], vc_ref[...], vn_ref[...]], axis=0)
    low_k = lax.broadcasted_iota(jnp.int32, (nk, LANES), 1) < DH_A
    low_q = lax.broadcasted_iota(jnp.int32, (blk, LANES), 1) < DH_A
    zero = jnp.zeros((nk, LANES), BF16)
    for sub in range(n_sub):
        g = n * n_sub + sub
        rows = slice(sub * blk, (sub + 1) * blk)
        in_seq = ((kj >= blk) | (g > 0)) & ((kj < 2 * blk) | (g < nb - 1))
        dist_masked = jnp.where((dist <= WINDOW) & in_seq, dist.astype(F32), jnp.inf)
        for p in range(2):
            kt = k_all[sub * blk:sub * blk + nk, p * LANES:(p + 1) * LANES]
            vt = v_all[sub * blk:sub * blk + nk, p * LANES:(p + 1) * LANES]
            k_big = jnp.concatenate([jnp.where(low_k, kt, zero), jnp.where(low_k, zero, kt)], axis=0)
            v_big = jnp.concatenate([jnp.where(low_k, vt, zero), jnp.where(low_k, zero, vt)], axis=0)
            q3 = jnp.concatenate([q_ref[rows, (p * GROUP_A + a) * LANES:(p * GROUP_A + a + 1) * LANES]
                                  for a in range(GROUP_A)], axis=0)
            s3 = _dot_nt(q3, k_big)
            p_rows, inv_rows = [], []
            for a in range(GROUP_A):
                p_halves, inv_halves = [], []
                for half in range(2):
                    head = (2 * p + half) * GROUP_A + a
                    s = (s3[a * blk:(a + 1) * blk, half * nk:(half + 1) * nk]
                         - (SLOPES_A[head] * LOG2E) * dist_masked)
                    sink = sink_ref[head] * LOG2E
                    m = jnp.maximum(jnp.max(s, axis=-1, keepdims=True), sink)
                    e = jnp.exp2(s - m)
                    l = jnp.sum(e, axis=-1, keepdims=True) + jnp.exp2(sink - m)
                    p_halves.append(e.astype(BF16))
                    inv_halves.append(1.0 / l)
                p_rows.append(jnp.concatenate(p_halves, axis=1))
                inv_rows.append(jnp.where(low_q, inv_halves[0], inv_halves[1]))
            o3 = _dot(jnp.concatenate(p_rows, axis=0), v_big)
            for a in range(GROUP_A):
                t = p * GROUP_A + a
                o_ref[rows, t * LANES:(t + 1) * LANES] = (o3[a * blk:(a + 1) * blk] * inv_rows[a]).astype(BF16)


def _window_attention(proj, sinks):
    bsz, s, _ = proj.shape
    blk = BLOCK_A
    nb = s // blk
    qb = math.gcd(nb, WIN_QB)
    tq = qb * blk
    kb, vb = OFF_AK // W_KV_A, OFF_AV // W_KV_A
    prev = lambda n: jnp.maximum(n * qb - 1, 0)
    nxt = lambda n: jnp.minimum((n + 1) * qb, nb - 1)
    return pl.pallas_call(
        functools.partial(_win_kernel, nb=nb),
        grid=(bsz, nb // qb),
        in_specs=[
            pl.BlockSpec(memory_space=pltpu.SMEM),
            pl.BlockSpec((None, tq, W_A), lambda b, n: (b, n, 0)),
            pl.BlockSpec((None, blk, W_KV_A), lambda b, n: (b, prev(n), kb)),
            pl.BlockSpec((None, tq, W_KV_A), lambda b, n: (b, n, kb)),
            pl.BlockSpec((None, blk, W_KV_A), lambda b, n: (b, nxt(n), kb)),
            pl.BlockSpec((None, blk, W_KV_A), lambda b, n: (b, prev(n), vb)),
            pl.BlockSpec((None, tq, W_KV_A), lambda b, n: (b, n, vb)),
            pl.BlockSpec((None, blk, W_KV_A), lambda b, n: (b, nxt(n), vb)),
        ],
        out_specs=pl.BlockSpec((None, tq, W_A), lambda b, n: (b, n, 0)),
        out_shape=jax.ShapeDtypeStruct((bsz, s, W_A), BF16),
        compiler_params=_cparams(("parallel", "parallel")),
        name="win_attn",
    )(sinks, proj, proj, proj, proj, proj, proj, proj)


MLA_TK = 512
MLA_QSCALE = (D_NOPE + D_ROPE) ** -0.5 * math.log2(math.e)


def _mla_pre_kernel(cqkr_ref, ckv_ref, pos_ref, qg_ref, wq_ref, kvg_ref, wk_ref, wvt_ref, inv_ref,
                    qt_ref, k_ref, vt_ref):
    tm = cqkr_ref.shape[0]
    t = cqkr_ref[...].astype(F32)
    lane = lax.broadcasted_iota(jnp.int32, (tm, 4 * LANES), 1)
    cq = jnp.where(lane < Q_LORA, t, 0.0)
    ms = jnp.sum(cq * cq, axis=-1, keepdims=True) * (1.0 / Q_LORA)
    qn = cq * lax.rsqrt(ms + EPS) * qg_ref[...]
    q_t = _dot_nt(wq_ref[...], qn.astype(BF16)) * MLA_QSCALE
    c = ckv_ref[...].astype(F32)
    ms = jnp.mean(c * c, axis=-1, keepdims=True)
    kvn = (c * lax.rsqrt(ms + EPS) * kvg_ref[...]).astype(BF16)
    kn = _dot(kvn, wk_ref[...])
    vt_ref[...] = _dot_nt(wvt_ref[...], kvn).astype(BF16)

    ang_t = inv_ref[...] * pos_ref[...].astype(F32)
    cs_t, sn_t = jnp.cos(ang_t), jnp.sin(ang_t)
    cs = jnp.concatenate([cs_t] * 4, axis=0).T
    sgn_sn = jnp.concatenate([-sn_t, sn_t] * 2, axis=0).T
    half = D_ROPE // 2

    def rope_t(x):
        x1, x2 = x[:half], x[half:]
        return jnp.concatenate([x1 * cs_t - x2 * sn_t, x1 * sn_t + x2 * cs_t], axis=0)
    l128 = lax.broadcasted_iota(jnp.int32, (tm, LANES), 1)
    first = (l128 % D_ROPE) < (D_ROPE // 2)

    def rope(x):
        swapped = jnp.where(first, pltpu.roll(x, LANES - D_ROPE // 2, 1), pltpu.roll(x, D_ROPE // 2, 1))
        return x * cs + swapped * sgn_sn

    kr = rope(t[:, 3 * LANES:4 * LANES])
    kr_hi = jnp.where(l128 >= D_ROPE, kr, 0.0)
    kr_lo = pltpu.roll(kr_hi, D_ROPE, 1)
    kr_tiles = (kr_lo.astype(BF16), kr_hi.astype(BF16))
    rope0 = N_H_B * D_NOPE
    for pair in range(2):
        qr_t = jnp.concatenate([rope_t(q_t[rope0 + h * D_ROPE:rope0 + (h + 1) * D_ROPE])
                                for h in (2 * pair, 2 * pair + 1)], axis=0).astype(BF16)
        for e in range(2):
            h = 2 * pair + e
            qt_ref[(2 * h) * LANES:(2 * h + 1) * LANES, :] = q_t[h * D_NOPE:(h + 1) * D_NOPE].astype(BF16)
            qt_ref[(2 * h + 1) * LANES:(2 * h + 2) * LANES, :] = qr_t
            k_ref[:, (2 * h) * LANES:(2 * h + 1) * LANES] = kn[:, h * LANES:(h + 1) * LANES].astype(BF16)
            k_ref[:, (2 * h + 1) * LANES:(2 * h + 2) * LANES] = kr_tiles[e]


def _mla_pre(proj, pos3, qg, wq, kvg, wk, wvt, inv):
    bsz, s, _ = proj.shape
    tm = MLA_TK
    const = lambda shape: pl.BlockSpec(shape, lambda b, i: (0,) * len(shape))
    return pl.pallas_call(
        _mla_pre_kernel,
        grid=(bsz, s // tm),
        in_specs=[
            pl.BlockSpec((None, tm, W_CQKR), lambda b, i: (b, i, OFF_CQKR // W_CQKR)),
            pl.BlockSpec((None, tm, KV_LORA), lambda b, i: (b, i, OFF_CKV // KV_LORA)),
            pl.BlockSpec((None, 1, tm), lambda b, i: (b, 0, i)),
            const((1, W_CQKR)), const((W_UQ_B, W_CQKR)), const((1, KV_LORA)), const((KV_LORA, W_B)),
            const((W_B, KV_LORA)), const((D_ROPE // 2, 1)),
        ],
        out_specs=[
            pl.BlockSpec((None, None, W_QK_B, tm), lambda b, i: (b, i, 0, 0)),
            pl.BlockSpec((None, tm, W_QK_B), lambda b, i: (b, i, 0)),
            pl.BlockSpec((None, None, W_B, tm), lambda b, i: (b, i, 0, 0)),
        ],
        out_shape=[
            jax.ShapeDtypeStruct((bsz, s // tm, W_QK_B, tm), BF16),
            jax.ShapeDtypeStruct((bsz, s, W_QK_B), BF16),
            jax.ShapeDtypeStruct((bsz, s // tm, W_B, tm), BF16),
        ],
        compiler_params=_cparams(("parallel", "parallel")),
        name="mla_pre",
    )(proj, proj, pos3, qg, wq, kvg, wk, wvt, inv)


def _mla_attn_kernel(qt_ref, k_ref, vt_ref, o_ref, st_a, st_b):
    n_chunks, _, tk = vt_ref.shape
    nt, _, tq = qt_ref.shape

    def scores(t, st_scr):
        qt = qt_ref[t]
        m8 = jnp.full((8, tq), -jnp.inf, F32)
        for c in range(n_chunks):
            st = _dot(k_ref[c * tk:(c + 1) * tk, :], qt)
            st_scr[c * tk:(c + 1) * tk, :] = st
            m8 = jnp.maximum(m8, jnp.max(st.reshape(tk // 8, 8, tq), axis=0))
        return jnp.max(m8, axis=0, keepdims=True)

    def output(t, st_scr, m):
        l8 = jnp.zeros((8, tq), F32)
        acc = jnp.zeros((D_V_B, tq), F32)
        for c in range(n_chunks):
            p = jnp.exp2(st_scr[c * tk:(c + 1) * tk, :] - m)
            l8 = l8 + jnp.sum(p.reshape(tk // 8, 8, tq), axis=0)
            acc = acc + _dot(vt_ref[c], p.astype(BF16))
        l = jnp.sum(l8, axis=0, keepdims=True)
        o_ref[pl.ds(pl.multiple_of(t * tq, tq), tq), :] = (acc / l).T.astype(BF16)

    m_first = scores(0, st_a)
    if nt == 1:
        output(0, st_a, m_first)
        return

    def pair(u, m_a):
        t = 2 * u
        m_b = scores(t + 1, st_b)
        output(t, st_a, m_a)
        m_a = scores(t + 2, st_a)
        output(t + 1, st_b, m_b)
        return m_a

    m_a = lax.fori_loop(0, nt // 2 - 1, pair, m_first)
    m_b = scores(nt - 1, st_b)
    output(nt - 2, st_a, m_a)
    output(nt - 1, st_b, m_b)


def _mla_attention(qt, k, vt):
    bsz, s, _ = k.shape
    n_chunks, tk = vt.shape[1], vt.shape[3]
    nt, tq = qt.shape[1], qt.shape[3]
    assert nt == 1 or nt % 2 == 0
    return pl.pallas_call(
        _mla_attn_kernel,
        grid=(bsz, N_H_B),
        in_specs=[
            pl.BlockSpec((None, nt, D_QK_B, tq), lambda b, h: (b, 0, h, 0)),
            pl.BlockSpec((None, s, D_QK_B), lambda b, h: (b, 0, h)),
            pl.BlockSpec((None, n_chunks, D_V_B, tk), lambda b, h: (b, 0, h, 0)),
        ],
        out_specs=pl.BlockSpec((None, s, D_V_B), lambda b, h: (b, 0, h)),
        out_shape=jax.ShapeDtypeStruct((bsz, s, W_B), BF16),
        scratch_shapes=[pltpu.VMEM((s, tq), F32), pltpu.VMEM((s, tq), F32)],
        compiler_params=_cparams(("parallel", "parallel")),
        name="mla_attn",
    )(qt, k, vt)


def _log_sigmoid(x):
    return jnp.minimum(x, 0.0) - jnp.log1p(jnp.exp(-jnp.abs(x)))


def _split3(x):
    hi = x.astype(BF16).astype(F32)
    r1 = x - hi
    mid = r1.astype(BF16).astype(F32)
    return jnp.concatenate([hi, mid, r1 - mid], axis=-1)


def _mlstm_kernel(q_ref, k_ref, vf_ref, vh_ref, of_ref, oh_ref, gate_ref, gb_ref, hg_ref, yf_ref, yt_ref,
                  hf_scr, hb_scr, tile_scr, kw_scr, rowq_scr, bp_scr, cf_scr, cb_scr, gt_scr, vext_scr):
    s_len = q_ref.shape[0]
    L = CHUNK
    nc = s_len // L
    head = pl.program_id(1)
    ii = lax.broadcasted_iota(jnp.int32, (L, L), 0)
    jj = lax.broadcasted_iota(jnp.int32, (L, L), 1)
    ones = jnp.ones((L, L), BF16)
    odd_head = (head % 2) == 1
    lane = lax.broadcasted_iota(jnp.int32, (L, LANES), 1)
    eye3 = jnp.concatenate([(ii == jj).astype(BF16)] * 3, axis=1)
    tri_ones = [jnp.concatenate([jnp.concatenate([m.astype(BF16), ones], axis=1)] * 3, axis=0)
                for m in (ii <= jj, ii >= jj)]
    masks = (jj <= ii, jj >= ii)
    h_scrs = (hf_scr, hb_scr)
    c_scrs = (cf_scr, cb_scr)

    def own_half(half_ref, r0):
        x = half_ref[pl.ds(r0, L), :].astype(F32)
        return jnp.where(odd_head, pltpu.roll(x, LANES // 2, 1), x)

    def load_v_ext(r0):
        tail = jnp.where(lane < DV_C - LANES, own_half(vh_ref, r0), jnp.where(lane == ONES_COL - LANES, 1.0, 0.0))
        return jnp.concatenate([vf_ref[pl.ds(r0, L), :], tail.astype(BF16)], axis=1)

    def transpose_gates(c, _):
        g = gate_ref[pl.ds(pl.multiple_of(c * L, L), L), :] + gb_ref[...]
        gt_scr[pl.ds(pl.multiple_of(c * 16, 16), 16), :] = g.T[0:16, :]
        return 0

    lax.fori_loop(0, nc, transpose_gates, 0, unroll=4)

    for d in range(2):
        li = gt_scr[pl.ds(d * N_H_C + head, nc, stride=16), :]
        lf = _log_sigmoid(gt_scr[pl.ds((2 + d) * N_H_C + head, nc, stride=16), :])
        bt = _dot(_split3(lf).astype(BF16), tri_ones[d])
        b, tot = bt[:, :L] * LOG2E, bt[:, L:] * LOG2E
        r = li * LOG2E - b
        m_loc = tot + jnp.broadcast_to(jnp.max(r, axis=-1, keepdims=True), r.shape)
        rowq_scr[d, 0] = r
        rowq_scr[d, 1] = jnp.exp2(tot + r - m_loc)
        rowq_scr[d, 2] = tot
        rowq_scr[d, 3] = m_loc
        bp_scr[d] = _split3(b + math.log2(DQK_C ** -0.5))

    def prep(c, _):
        r0 = pl.multiple_of(c * L, L)
        q, k = q_ref[pl.ds(r0, L), :], k_ref[pl.ds(r0, L), :]
        s = _dot_nt(q, k)
        kt = k.astype(F32).T
        p = []
        for d in range(2):
            u = jnp.where(masks[d], rowq_scr[d, 0, pl.ds(c, 1), :], -jnp.inf)
            ct = jnp.broadcast_to(jnp.max(u, axis=-1, keepdims=True), (L, L))
            p.append((s * jnp.exp2(u - ct)).astype(BF16))
            tile_scr[pl.ds(r0, L), (2 + d) * L:(3 + d) * L] = ct
            kw_scr[d, c] = (kt * rowq_scr[d, 1, pl.ds(c, 1), :]).astype(BF16)
        rhs = jnp.concatenate([jnp.broadcast_to(bp_scr[d, pl.ds(c, 1), :], (L, 3 * L)) for d in range(2)], axis=0)
        tile_scr[pl.ds(r0, L), 0:2 * L] = _dot_nt(eye3, rhs.astype(BF16))
        v_ext = load_v_ext(r0)
        vext_scr[pl.ds(r0, L), :] = v_ext
        intra = _dot(jnp.concatenate(p, axis=0), v_ext)
        hf_scr[pl.ds(r0, L), :] = intra[:L]
        hb_scr[pl.ds(r0, L), :] = intra[L:]
        return 0

    lax.fori_loop(0, nc, prep, 0, unroll=4)

    cf_scr[...] = jnp.zeros_like(cf_scr)

    @pl.when(jnp.logical_not(odd_head))
    def _():
        yt_ref[...] = jnp.zeros_like(yt_ref)
    cb_scr[...] = jnp.zeros_like(cb_scr)

    def one_dir(c, m_state, d):
        h_scr, c_scr = h_scrs[d], c_scrs[d]
        r0 = pl.multiple_of(c * L, L)
        tot, m_loc = rowq_scr[d, 2, pl.ds(c, 1), :], rowq_scr[d, 3, pl.ds(c, 1), :]
        bt = tile_scr[pl.ds(r0, L), d * L:(d + 1) * L]
        ct = tile_scr[pl.ds(r0, L), (2 + d) * L:(3 + d) * L]
        v_ext = vext_scr[pl.ds(r0, L), :]
        c_state = c_scr[...]
        qc = _dot(q_ref[pl.ds(r0, L), :], c_state.astype(BF16))
        mx = jnp.maximum(m_state, ct)
        iw = jnp.exp2(m_state - mx)
        xw = jnp.exp2(ct - mx)
        num = [iw * qc[:, j * L:(j + 1) * L] + xw * h_scr[pl.ds(r0, L), j * L:(j + 1) * L] for j in range(2)]
        den = jnp.broadcast_to(num[1][:, ONES_COL - L:ONES_COL - L + 1], (L, L))
        inv = 1.0 / jnp.maximum(jnp.abs(den), jnp.exp2(-(bt + mx)))
        h_scr[pl.ds(r0, L), 0:L] = num[0] * inv
        h_scr[pl.ds(r0, L), L:2 * L] = num[1] * inv
        c_loc = _dot(kw_scr[d, c], v_ext)
        m_new = jnp.maximum(tot + m_state, m_loc)
        sp, sl = jnp.exp2(tot + m_state - m_new), jnp.exp2(m_loc - m_new)
        c_scr[...] = jnp.concatenate(
            [sp * c_state[:, j * L:(j + 1) * L] + sl * c_loc[:, j * L:(j + 1) * L] for j in range(2)], axis=1)
        return m_new

    real = lax.broadcasted_iota(jnp.int32, (L, DV_PAD), 1) < DV_C

    def finish(c):
        r0 = pl.multiple_of(c * L, L)
        h = jnp.where(real, hf_scr[pl.ds(r0, L), :] + hb_scr[pl.ds(r0, L), :], 0.0)
        ms = jnp.sum(h * h, axis=-1, keepdims=True) * (1.0 / DV_C)
        hn = h * lax.rsqrt(ms + EPS) * hg_ref[...]
        o = jnp.concatenate([of_ref[pl.ds(r0, L), :].astype(F32), own_half(oh_ref, r0)], axis=1)
        y = jax.nn.sigmoid(o) * hn
        yf_ref[pl.ds(r0, L), :] = y[:, :LANES].astype(BF16)
        tail = y[:, LANES:]
        tail = jnp.where(odd_head, pltpu.roll(tail, LANES // 2, 1), tail)
        yt_ref[pl.ds(r0, L), :] = yt_ref[pl.ds(r0, L), :] + tail.astype(BF16)

    def scan(c, carry):
        return one_dir(c, carry[0], 0), one_dir(nc - 1 - c, carry[1], 1)

    def scan_and_finish(c, carry):
        carry = scan(c, carry)
        finish(c)
        finish(nc - 1 - c)
        return carry

    neg = jnp.full((1, LANES), -jnp.inf, F32)
    carry = lax.fori_loop(0, nc // 2, scan, (neg, neg), unroll=4)
    lax.fori_loop(nc // 2, nc, scan_and_finish, carry, unroll=4)


def _mlstm(proj, gates, gate_b, head_g):
    bsz, s, _ = proj.shape
    nc = s // CHUNK
    return pl.pallas_call(
        _mlstm_kernel,
        grid=(bsz, N_H_C),
        in_specs=[
            pl.BlockSpec((None, s, DQK_PAD), lambda b, h: (b, 0, OFF_MQ // DQK_PAD + h)),
            pl.BlockSpec((None, s, DQK_PAD), lambda b, h: (b, 0, OFF_MK // DQK_PAD + h)),
            pl.BlockSpec((None, s, LANES), lambda b, h: (b, 0, OFF_MV // LANES + 3 * (h // 2) + 2 * (h % 2))),
            pl.BlockSpec((None, s, LANES), lambda b, h: (b, 0, OFF_MV // LANES + 3 * (h // 2) + 1)),
            pl.BlockSpec((None, s, LANES), lambda b, h: (b, 0, OFF_MO // LANES + 3 * (h // 2) + 2 * (h % 2))),
            pl.BlockSpec((None, s, LANES), lambda b, h: (b, 0, OFF_MO // LANES + 3 * (h // 2) + 1)),
            pl.BlockSpec((None, s, LANES), lambda b, h: (b, 0, 0)),
            pl.BlockSpec((1, LANES), lambda b, h: (0, 0)),
            pl.BlockSpec((None, 1, DV_PAD), lambda b, h: (h, 0, 0)),
        ],
        out_specs=[
            pl.BlockSpec((None, s, LANES), lambda b, h: (b, 0, h)),
            pl.BlockSpec((None, s, LANES), lambda b, h: (b, 0, h // 2)),
        ],
        out_shape=[
            jax.ShapeDtypeStruct((bsz, s, N_H_C * LANES), BF16),
            jax.ShapeDtypeStruct((bsz, s, N_H_C // 2 * LANES), BF16),
        ],
        scratch_shapes=[
            pltpu.VMEM((s, DV_PAD), F32), pltpu.VMEM((s, DV_PAD), F32),
            pltpu.VMEM((s, 4 * CHUNK), F32),
            pltpu.VMEM((2, nc, DQK_PAD, CHUNK), BF16),
            pltpu.VMEM((2, 4, nc, CHUNK), F32),
            pltpu.VMEM((2, nc, 3 * CHUNK), F32),
            pltpu.VMEM((DQK_PAD, DV_PAD), F32), pltpu.VMEM((DQK_PAD, DV_PAD), F32),
            pltpu.VMEM((nc * 16, CHUNK), F32),
            pltpu.VMEM((s, DV_PAD), BF16),
        ],
        compiler_params=_cparams(("parallel", "arbitrary")),
        name="mlstm",
    )(proj, proj, proj, proj, proj, proj, gates, gate_b, head_g)


def _out_kernel(ya_ref, yb_ref, ycf_ref, yct_ref, wa_ref, wb_ref, wcf_ref, wct_ref, pg_ref, gate_ref, x_ref,
                o_ref, y_scr):
    for r in range(0, o_ref.shape[0], SUB_ROWS):
        rows = slice(r, r + SUB_ROWS)
        y_scr[rows, :] = (_dot(ya_ref[rows, :], wa_ref[...]) + _dot(yb_ref[rows, :], wb_ref[...])
                          + _dot(ycf_ref[rows, :], wcf_ref[...]) + _dot(yct_ref[rows, :], wct_ref[...]))
        _norm_residual_rows(y_scr, x_ref, pg_ref, gate_ref, o_ref, r, SUB_ROWS)


def _out_proj(ya, yb, ycf, yct, wa, wb, wcf, wct, pg, modl, x):
    bsz, s, d = x.shape
    w_cf, w_ct = ycf.shape[2], yct.shape[2]
    tm = min(2 * SUB_ROWS, s)
    row = lambda w: pl.BlockSpec((None, tm, w), lambda b, i: (b, i, 0))
    const = lambda shape: pl.BlockSpec(shape, lambda b, i: (0,) * len(shape), pipeline_mode=pl.Buffered(1))
    return pl.pallas_call(
        _out_kernel,
        grid=(bsz, s // tm),
        in_specs=[
            row(W_A), row(W_B), row(w_cf), row(w_ct),
            const((W_A, d)), const((W_B, d)), const((w_cf, d)), const((w_ct, d)), const((1, d)),
            pl.BlockSpec((None, 1, d), lambda b, i: (b, 0, 2)),
            row(d),
        ],
        out_specs=row(d),
        out_shape=jax.ShapeDtypeStruct((bsz, s, d), F32),
        scratch_shapes=[pltpu.VMEM((tm, d), F32)],
        compiler_params=_cparams(("parallel", "parallel")),
        name="out_proj",
    )(ya, yb, ycf, yct, wa, wb, wcf, wct, pg, modl, x)


FFN_TF = 512
FFN_VMEM_LIMIT = 62 * 1024 * 1024


def _ffn_kernel(x_ref, g_ref, sc_ref, sh_ref, wg_hbm, wu_hbm, wd_hbm, pg_ref, gate_ref, o_ref,
                h_scr, wg_buf, wu_buf, wd_buf, sem, *, layer, nf):
    tm = x_ref.shape[0]
    tf = wg_buf.shape[2]
    step = pl.program_id(0) * pl.num_programs(1) + pl.program_id(1)
    n_steps = pl.num_programs(0) * pl.num_programs(1)
    subs = [(r, slice(r, r + SUB_ROWS)) for r in range(0, tm, SUB_ROWS)]

    def slot_of(k):
        return (k + step) % 2

    def copies(k, slot):
        cols = pl.ds(pl.multiple_of(k * tf, tf), tf)
        return (pltpu.make_async_copy(wg_hbm.at[layer, :, cols], wg_buf.at[slot], sem.at[0, slot]),
                pltpu.make_async_copy(wu_hbm.at[layer, :, cols], wu_buf.at[slot], sem.at[1, slot]),
                pltpu.make_async_copy(wd_hbm.at[layer, cols, :], wd_buf.at[slot], sem.at[2, slot]))

    def fetch(k, slot):
        for cp in copies(k, slot):
            cp.start()

    def wait(k, slot):
        for cp in copies(k, slot):
            cp.wait()

    def partial_down(rows, slot):
        h = h_scr[rows, :]
        a = _dot(h, wg_buf[slot])
        u = _dot(h, wu_buf[slot])
        hid = (a * jax.nn.sigmoid(a)) * u
        return _dot(hid.astype(BF16), wd_buf[slot])

    @pl.when(step == 0)
    def _():
        fetch(0, slot_of(0))

    fetch(1, slot_of(1))
    wait(0, slot_of(0))
    for r, rows in subs:
        _mod_norm_rows(x_ref, g_ref, sc_ref, sh_ref, h_scr, r, SUB_ROWS)
        o_ref[rows, :] = partial_down(rows, slot_of(0))

    def middle(k):
        fetch(k + 1, slot_of(k + 1))
        wait(k, slot_of(k))
        o_ref[...] += partial_down(slice(0, tm), slot_of(k))

    def pair(u, carry):
        middle(1 + 2 * u)
        middle(2 + 2 * u)
        return carry

    lax.fori_loop(0, (nf - 3) // 2, pair, 0)
    middle(nf - 2)

    @pl.when(step < n_steps - 1)
    def _():
        fetch(0, slot_of(nf))
    wait(nf - 1, slot_of(nf - 1))
    for r, rows in subs:
        o_ref[rows, :] += partial_down(rows, slot_of(nf - 1))
        _norm_residual_rows(o_ref, x_ref, pg_ref, gate_ref, o_ref, r, SUB_ROWS)


def _ffn(x, g, modl, wg, wu, wd, pg, layer):
    bsz, s, d = x.shape
    f = wg.shape[2]
    nf = f // FFN_TF
    assert nf % 2 == 1 and nf >= 3
    tm = min(GEMM_TM, s)
    row = pl.BlockSpec((None, tm, d), lambda b, i: (b, i, 0))
    vec = pl.BlockSpec((1, d), lambda b, i: (0, 0))
    modv = lambda k: pl.BlockSpec((None, 1, d), lambda b, i: (b, 0, k))
    hbm = pl.BlockSpec(memory_space=pl.ANY)
    return pl.pallas_call(
        functools.partial(_ffn_kernel, layer=layer, nf=nf),
        grid=(bsz, s // tm),
        in_specs=[row, vec, modv(4), modv(3), hbm, hbm, hbm, vec, modv(5)],
        out_specs=row,
        out_shape=jax.ShapeDtypeStruct((bsz, s, d), F32),
        scratch_shapes=[
            pltpu.VMEM((tm, d), BF16),
            pltpu.VMEM((2, d, FFN_TF), BF16), pltpu.VMEM((2, d, FFN_TF), BF16), pltpu.VMEM((2, FFN_TF, d), BF16),
            pltpu.SemaphoreType.DMA((3, 2)),
        ],
        compiler_params=_cparams(("arbitrary", "arbitrary"), FFN_VMEM_LIMIT),
        name="ffn",
    )(x, g, modl, modl, wg, wu, wd, pg, modl)


CAST_BLOCK_BYTES = 4 * 1024 * 1024


def _cast_kernel(w_ref, o_ref):
    o_ref[...] = w_ref[...].astype(BF16)


def _to_bf16(w):
    depth, rows, cols = w.shape
    rb = rows
    while rb * cols * 4 > CAST_BLOCK_BYTES and rb % 16 == 0:
        rb //= 2
    spec = pl.BlockSpec((None, rb, cols), lambda l, i: (l, i, 0))
    return pl.pallas_call(
        _cast_kernel,
        grid=(depth, rows // rb),
        in_specs=[spec],
        out_specs=spec,
        out_shape=jax.ShapeDtypeStruct(w.shape, BF16),
        compiler_params=_cparams(("parallel", "parallel")),
        name="cast_bf16",
    )(w)


def _pair_heads_a(w, axis):
    shape = w.shape
    w = jnp.moveaxis(w, axis, -1)
    lead = w.shape[:-1]
    w = w.reshape(lead + (2, 2, GROUP_A, DH_A))
    w = jnp.swapaxes(w, -3, -2)
    w = w.reshape(lead + (W_A,))
    return jnp.moveaxis(w, -1, axis).reshape(shape)


def _pad_heads(w, n_heads, width, padded, axis):
    w = jnp.moveaxis(w, axis, -1)
    lead = w.shape[:-1]
    w = w.reshape(lead + (n_heads, width))
    w = jnp.pad(w, [(0, 0)] * len(lead) + [(0, 0), (0, padded - width)])
    w = w.reshape(lead + (n_heads * padded,))
    return jnp.moveaxis(w, -1, axis)


def _pair_heads_c(w):
    d = w.shape[0]
    w = w.reshape(d, N_H_C // 2, 2, DV_C)
    even, odd = w[:, :, 0], w[:, :, 1]
    out = jnp.concatenate([even[..., :LANES], even[..., LANES:], odd[..., LANES:], odd[..., :LANES]], axis=-1)
    return out.reshape(d, W_C)


def _layout_w_in(w):
    d = w.shape[0]
    sizes = (W_A, N_KV_A * DH_A, N_KV_A * DH_A, Q_LORA, KV_LORA, D_ROPE,
             N_H_C * DQK_C, N_H_C * DQK_C, W_C, 4 * N_H_C, W_C)
    pts = np.cumsum(sizes)[:-1].tolist()
    aq, ak, av, bcq, bckv, bkr, cq, ck, cv, cg, co = jnp.split(w, pts, axis=1)
    parts = [
        _pair_heads_a(aq, 1), ak, av, bckv,
        jnp.pad(cg, ((0, 0), (0, LANES - 4 * N_H_C))),
        bcq, bkr,
        _pad_heads(cq, N_H_C, DQK_C, DQK_PAD, 1), _pad_heads(ck, N_H_C, DQK_C, DQK_PAD, 1),
        _pair_heads_c(cv), _pair_heads_c(co),
    ]
    out = jnp.concatenate(parts, axis=1).astype(BF16)
    assert out.shape == (d, IN_WIDTH_PAD)
    return out.reshape(d, IN_WIDTH_PAD // IN_TN, IN_TN).transpose(1, 0, 2)


def _layout_w_uq(w):
    w = w.reshape(Q_LORA, N_H_B, D_NOPE + D_ROPE)
    nope = w[:, :, :D_NOPE].reshape(Q_LORA, N_H_B * D_NOPE)
    rope = w[:, :, D_NOPE:].reshape(Q_LORA, N_H_B * D_ROPE)
    w = jnp.concatenate([nope, rope], axis=1)
    return jnp.pad(w, ((0, W_CQKR - Q_LORA), (0, 0))).T.astype(BF16)


def _layout_w_ukv(w):
    w = w.reshape(KV_LORA, N_H_B, D_NOPE + D_V_B)
    kn = w[:, :, :D_NOPE].reshape(KV_LORA, N_H_B * D_NOPE)
    v = w[:, :, D_NOPE:].reshape(KV_LORA, N_H_B * D_V_B)
    return kn.astype(BF16), v.T.astype(BF16)


def kernel(x, c, positions, mod_w, mod_b, pre_mix_g, post_mix_g, pre_ffn_g, post_ffn_g, w_in, attn_sink, mla_q_norm_g, mla_w_uq, mla_kv_norm_g, mla_w_ukv, mlstm_gate_b, mlstm_head_g, w_out, ffn_w_gate, ffn_w_up, ffn_w_down):
    depth = mod_w.shape[0]
    bsz, s, d = x.shape
    mod = _modulation(c, mod_w, mod_b)
    pos3 = positions.reshape(bsz, 1, s)
    inv = 1.0 / (ROPE_THETA ** (jnp.arange(0, D_ROPE, 2, dtype=F32) / D_ROPE))
    inv_col = inv.reshape(D_ROPE // 2, 1)
    wg_bf, wu_bf, wd_bf = _to_bf16(ffn_w_gate), _to_bf16(ffn_w_up), _to_bf16(ffn_w_down)
    for l in range(depth):
        modl = mod[l].reshape(bsz, 1, 6 * d)
        proj, gates = _in_proj(x, pre_mix_g[l].reshape(1, d), modl, _layout_w_in(w_in[l]))
        ya = _window_attention(proj, attn_sink[l])
        qg = jnp.pad(mla_q_norm_g[l], (0, W_CQKR - Q_LORA)).reshape(1, W_CQKR)
        wk, wvt = _layout_w_ukv(mla_w_ukv[l])
        qb, kb, vtb = _mla_pre(proj, pos3, qg, _layout_w_uq(mla_w_uq[l]),
                               mla_kv_norm_g[l].reshape(1, KV_LORA), wk, wvt, inv_col)
        yb = _mla_attention(qb, kb, vtb)
        gate_b = jnp.pad(mlstm_gate_b[l], (0, LANES - 4 * N_H_C)).reshape(1, LANES)
        head_g = jnp.pad(mlstm_head_g[l].reshape(N_H_C, 1, DV_C), ((0, 0), (0, 0), (0, DV_PAD - DV_C)))
        ycf, yct = _mlstm(proj, gates, gate_b, head_g)
        wo = w_out[l]
        wa = _pair_heads_a(wo[:W_A], 0).astype(BF16)
        wb = wo[W_A:W_A + W_B].astype(BF16)
        wc = wo[W_A + W_B:].reshape(N_H_C, DV_C, d)
        wcf = wc[:, :LANES].reshape(N_H_C * LANES, d).astype(BF16)
        wct = wc[:, LANES:].reshape(N_H_C * (DV_C - LANES), d).astype(BF16)
        x = _out_proj(ya, yb, ycf, yct, wa, wb, wcf, wct, post_mix_g[l].reshape(1, d), modl, x)
        x = _ffn(x, pre_ffn_g[l].reshape(1, d), modl, wg_bf, wu_bf, wd_bf, post_ffn_g[l].reshape(1, d), l)
    return x
```

```python
import functools
import math

import numpy as np
import jax
import jax.numpy as jnp
from jax import lax
from jax.experimental import pallas as pl
from jax.experimental.pallas import tpu as pltpu

F32 = jnp.float32
BF16 = jnp.bfloat16

EPS = 1e-6
N_Q_A, N_KV_A, DH_A, GROUP_A = 12, 4, 64, 3
WINDOW = 128
BLOCK_A = 128
N_H_B, Q_LORA, KV_LORA, D_NOPE, D_ROPE, D_V_B = 4, 448, 128, 128, 64, 128
ROPE_THETA = 10000.0
N_H_C, DQK_C, DV_C = 4, 96, 192
W_A, W_B, W_C = N_Q_A * DH_A, N_H_B * D_V_B, N_H_C * DV_C
W_KV_A = N_KV_A * DH_A
W_CQKR = Q_LORA + D_ROPE
W_UQ_B = N_H_B * (D_NOPE + D_ROPE)

LANES = 128
DQK_PAD = 128
DV_PAD = 256
ONES_COL = DV_C
D_QK_B = D_NOPE + LANES
W_QK_B = N_H_B * D_QK_B
VMEM_LIMIT = 56 * 1024 * 1024

OFF_AK = W_A
OFF_AV = OFF_AK + W_KV_A
OFF_CKV = OFF_AV + W_KV_A
OFF_CG = OFF_CKV + KV_LORA
OFF_CQKR = OFF_CG + LANES
OFF_MQ = OFF_CQKR + W_CQKR
OFF_MK = OFF_MQ + N_H_C * DQK_PAD
OFF_MV = OFF_MK + N_H_C * DQK_PAD
OFF_MO = OFF_MV + W_C
IN_WIDTH_PAD = OFF_MO + W_C

CHUNK = 128


def _alibi_slopes(n):
    def pow2(m):
        start = 2.0 ** (-8.0 / m)
        return [start ** (i + 1) for i in range(m)]
    if math.log2(n).is_integer():
        s = pow2(n)
    else:
        p = 2 ** math.floor(math.log2(n))
        s = pow2(p) + pow2(2 * p)[0::2][: n - p]
    return [float(np.float32(v)) for v in s]


SLOPES_A = _alibi_slopes(N_Q_A)


def _cparams(sem, vmem_limit=VMEM_LIMIT):
    return pltpu.CompilerParams(dimension_semantics=sem, vmem_limit_bytes=vmem_limit)


def _dot(a, b):
    return jnp.dot(a, b, preferred_element_type=F32)


def _dot_nt(a, b):
    return lax.dot_general(a, b, (((1,), (1,)), ((), ())), preferred_element_type=F32)


def _mod_kernel(c_ref, w_ref, b_ref, o_ref):
    c = c_ref[...]
    s = c * jax.nn.sigmoid(c)
    o_ref[...] = _dot(s.astype(BF16), w_ref[...].astype(BF16)) + b_ref[...]


MOD_TN = 2048


def _modulation(c, mod_w, mod_b):
    depth, d, n = mod_w.shape
    bsz = c.shape[0]
    tn = MOD_TN
    return pl.pallas_call(
        _mod_kernel,
        grid=(depth, n // tn),
        in_specs=[
            pl.BlockSpec((bsz, d), lambda l, j: (0, 0)),
            pl.BlockSpec((None, d, tn), lambda l, j: (l, 0, j)),
            pl.BlockSpec((None, 1, tn), lambda l, j: (l, 0, j)),
        ],
        out_specs=pl.BlockSpec((None, bsz, tn), lambda l, j: (l, 0, j)),
        out_shape=jax.ShapeDtypeStruct((depth, bsz, n), F32),
        compiler_params=_cparams(("parallel", "parallel")),
        name="adaln_mod",
    )(c, mod_w, mod_b.reshape(depth, 1, n))


GEMM_TM = 1024
ROW_SLAB = 16
SUB_ROWS = 256


def _row_slabs(first_row, n_rows):
    return [slice(r, r + ROW_SLAB) for r in range(first_row, first_row + n_rows, ROW_SLAB)]


def _mod_norm_rows(x_ref, g_ref, sc_ref, sh_ref, h_ref, first_row, n_rows):
    gain = g_ref[...] * (1.0 + sc_ref[...])
    shift = sh_ref[...]
    for rows in _row_slabs(first_row, n_rows):
        x = x_ref[rows, :]
        ms = jnp.mean(x * x, axis=-1, keepdims=True)
        h_ref[rows, :] = (x * lax.rsqrt(ms + EPS) * gain + shift).astype(BF16)


def _norm_residual_rows(y_ref, x_ref, pg_ref, gate_ref, o_ref, first_row, n_rows):
    gain = gate_ref[...] * pg_ref[...]
    for rows in _row_slabs(first_row, n_rows):
        y = y_ref[rows, :]
        ms = jnp.mean(y * y, axis=-1, keepdims=True)
        o_ref[rows, :] = x_ref[rows, :] + y * lax.rsqrt(ms + EPS) * gain


IN_TN = 1536
GATE_TILE = OFF_CG // IN_TN
GATE_OFF = OFF_CG % IN_TN
IN_VMEM_LIMIT = 62 * 1024 * 1024


def _in_kernel(x_ref, g_ref, sc_ref, sh_ref, w_ref, cs_ref, wg_ref, wu_ref, wd_ref,
               o_ref, gate_ref, wg_out, wu_out, wd_out, h_scr):
    j = pl.program_id(2)
    tm = x_ref.shape[0]

    def project(rows):
        acc = _dot(h_scr[rows, :], w_ref[j])
        o_ref[rows, :] = (acc * cs_ref[...]).astype(BF16)
        return acc

    @pl.when(j == 0)
    def _():
        for r in range(0, tm, SUB_ROWS):
            rows = slice(r, r + SUB_ROWS)
            _mod_norm_rows(x_ref, g_ref, sc_ref, sh_ref, h_scr, r, SUB_ROWS)
            acc = project(rows)
            if GATE_TILE == 0:
                gate_ref[rows, :] = acc[:, GATE_OFF:GATE_OFF + LANES]
        for src, dst in ((wg_ref, wg_out), (wu_ref, wu_out), (wd_ref, wd_out)):
            dst[...] = src[...].astype(BF16)

    @pl.when(j != 0)
    def _():
        acc = project(slice(0, tm))
        if GATE_TILE != 0:
            @pl.when(j == GATE_TILE)
            def _():
                gate_ref[...] = acc[:, GATE_OFF:GATE_OFF + LANES]


def _in_proj(x, g, modl, w, ffn_w, layer):
    bsz, s, d = x.shape
    n = w.shape[0] * IN_TN
    tm = min(GEMM_TM, s)
    n_i = s // tm
    n_tiles = bsz * n_i
    bf16_rows = 16
    for a in ffn_w:
        assert a.shape[1] % (n_tiles * bf16_rows) == 0, (a.shape, n_tiles)
    share = lambda a: pl.BlockSpec((None, a.shape[1] // n_tiles, a.shape[2]), lambda b, i, j: (layer, b * n_i + i, 0))
    share_out = lambda a: pl.BlockSpec((a.shape[1] // n_tiles, a.shape[2]), lambda b, i, j: (b * n_i + i, 0))
    col_scale = jnp.where(jnp.arange(n) < W_A, WIN_QSCALE, 1.0).astype(F32).reshape(1, n)
    return pl.pallas_call(
        _in_kernel,
        grid=(bsz, s // tm, n // IN_TN),
        in_specs=[
            pl.BlockSpec((None, tm, d), lambda b, i, j: (b, i, 0)),
            pl.BlockSpec((1, d), lambda b, i, j: (0, 0)),
            pl.BlockSpec((None, 1, d), lambda b, i, j: (b, 0, 1)),
            pl.BlockSpec((None, 1, d), lambda b, i, j: (b, 0, 0)),
            pl.BlockSpec((n // IN_TN, d, IN_TN), lambda b, i, j: (0, 0, 0), pipeline_mode=pl.Buffered(1)),
            pl.BlockSpec((1, IN_TN), lambda b, i, j: (0, j)),
        ] + [share(a) for a in ffn_w],
        out_specs=[
            pl.BlockSpec((None, tm, IN_TN), lambda b, i, j: (b, i, j)),
            pl.BlockSpec((None, tm, LANES), lambda b, i, j: (b, i, 0)),
        ] + [share_out(a) for a in ffn_w],
        out_shape=[
            jax.ShapeDtypeStruct((bsz, s, n), BF16),
            jax.ShapeDtypeStruct((bsz, s, LANES), F32),
        ] + [jax.ShapeDtypeStruct(a.shape[1:], BF16) for a in ffn_w],
        scratch_shapes=[pltpu.VMEM((tm, d), BF16)],
        compiler_params=_cparams(("parallel", "parallel", "arbitrary"), IN_VMEM_LIMIT),
        name="in_proj",
    )(x, g, modl, modl, w, col_scale, *ffn_w)


LOG2E = math.log2(math.e)
WIN_QSCALE = DH_A ** -0.5 * LOG2E


WIN_QB = 8


def _win_kernel(sink_ref, q_ref, kp_ref, kc_ref, kn_ref, vp_ref, vc_ref, vn_ref, o_ref, *, nb):
    n = pl.program_id(1)
    blk = BLOCK_A
    n_sub = q_ref.shape[0] // blk
    nk = 3 * blk
    qi = lax.broadcasted_iota(jnp.int32, (blk, nk), 0)
    kj = lax.broadcasted_iota(jnp.int32, (blk, nk), 1)
    dist = jnp.abs(qi - kj + blk)
    k_all = jnp.concatenate([kp_ref[...], kc_ref[...], kn_ref[...]], axis=0)
    v_all = jnp.concatenate([vp_ref[...], vc_ref[...], vn_ref[...]], axis=0)
    low_k = lax.broadcasted_iota(jnp.int32, (nk, LANES), 1) < DH_A
    low_q = lax.broadcasted_iota(jnp.int32, (blk, LANES), 1) < DH_A
    zero = jnp.zeros((nk, LANES), BF16)
    for sub in range(n_sub):
        g = n * n_sub + sub
        rows = slice(sub * blk, (sub + 1) * blk)
        in_seq = ((kj >= blk) | (g > 0)) & ((kj < 2 * blk) | (g < nb - 1))
        dist_masked = jnp.where((dist <= WINDOW) & in_seq, dist.astype(F32), jnp.inf)
        for p in range(2):
            kt = k_all[sub * blk:sub * blk + nk, p * LANES:(p + 1) * LANES]
            vt = v_all[sub * blk:sub * blk + nk, p * LANES:(p + 1) * LANES]
            k_big = jnp.concatenate([jnp.where(low_k, kt, zero), jnp.where(low_k, zero, kt)], axis=0)
            v_big = jnp.concatenate([jnp.where(low_k, vt, zero), jnp.where(low_k, zero, vt)], axis=0)
            q3 = jnp.concatenate([q_ref[rows, (p * GROUP_A + a) * LANES:(p * GROUP_A + a + 1) * LANES]
                                  for a in range(GROUP_A)], axis=0)
            s3 = _dot_nt(q3, k_big)
            p_rows, inv_rows = [], []
            for a in range(GROUP_A):
                p_halves, inv_halves = [], []
                for half in range(2):
                    head = (2 * p + half) * GROUP_A + a
                    s = (s3[a * blk:(a + 1) * blk, half * nk:(half + 1) * nk]
                         - (SLOPES_A[head] * LOG2E) * dist_masked)
                    sink = sink_ref[head] * LOG2E
                    m = jnp.maximum(jnp.max(s, axis=-1, keepdims=True), sink)
                    e = jnp.exp2(s - m)
                    l = jnp.sum(e, axis=-1, keepdims=True) + jnp.exp2(sink - m)
                    p_halves.append(e.astype(BF16))
                    inv_halves.append(1.0 / l)
                p_rows.append(jnp.concatenate(p_halves, axis=1))
                inv_rows.append(jnp.where(low_q, inv_halves[0], inv_halves[1]))
            o3 = _dot(jnp.concatenate(p_rows, axis=0), v_big)
            for a in range(GROUP_A):
                t = p * GROUP_A + a
                o_ref[rows, t * LANES:(t + 1) * LANES] = (o3[a * blk:(a + 1) * blk] * inv_rows[a]).astype(BF16)


def _window_attention(proj, sinks):
    bsz, s, _ = proj.shape
    blk = BLOCK_A
    nb = s // blk
    qb = math.gcd(nb, WIN_QB)
    tq = qb * blk
    kb, vb = OFF_AK // W_KV_A, OFF_AV // W_KV_A
    prev = lambda n: jnp.maximum(n * qb - 1, 0)
    nxt = lambda n: jnp.minimum((n + 1) * qb, nb - 1)
    return pl.pallas_call(
        functools.partial(_win_kernel, nb=nb),
        grid=(bsz, nb // qb),
        in_specs=[
            pl.BlockSpec(memory_space=pltpu.SMEM),
            pl.BlockSpec((None, tq, W_A), lambda b, n: (b, n, 0)),
            pl.BlockSpec((None, blk, W_KV_A), lambda b, n: (b, prev(n), kb)),
            pl.BlockSpec((None, tq, W_KV_A), lambda b, n: (b, n, kb)),
            pl.BlockSpec((None, blk, W_KV_A), lambda b, n: (b, nxt(n), kb)),
            pl.BlockSpec((None, blk, W_KV_A), lambda b, n: (b, prev(n), vb)),
            pl.BlockSpec((None, tq, W_KV_A), lambda b, n: (b, n, vb)),
            pl.BlockSpec((None, blk, W_KV_A), lambda b, n: (b, nxt(n), vb)),
        ],
        out_specs=pl.BlockSpec((None, tq, W_A), lambda b, n: (b, n, 0)),
        out_shape=jax.ShapeDtypeStruct((bsz, s, W_A), BF16),
        compiler_params=_cparams(("parallel", "parallel")),
        name="win_attn",
    )(sinks, proj, proj, proj, proj, proj, proj, proj)


MLA_TK = 512
MLA_QSCALE = (D_NOPE + D_ROPE) ** -0.5 * math.log2(math.e)


def _mla_pre_kernel(cqkr_ref, ckv_ref, pos_ref, qg_ref, wq_ref, kvg_ref, wk_ref, wvt_ref, inv_ref,
                    qt_ref, k_ref, vt_ref):
    tm = cqkr_ref.shape[0]
    t = cqkr_ref[...].astype(F32)
    lane = lax.broadcasted_iota(jnp.int32, (tm, 4 * LANES), 1)
    cq = jnp.where(lane < Q_LORA, t, 0.0)
    ms = jnp.sum(cq * cq, axis=-1, keepdims=True) * (1.0 / Q_LORA)
    qn = cq * lax.rsqrt(ms + EPS) * qg_ref[...]
    q_t = _dot_nt(wq_ref[...], qn.astype(BF16)) * MLA_QSCALE
    c = ckv_ref[...].astype(F32)
    ms = jnp.mean(c * c, axis=-1, keepdims=True)
    kvn = (c * lax.rsqrt(ms + EPS) * kvg_ref[...]).astype(BF16)
    kn = _dot(kvn, wk_ref[...])
    vt_ref[...] = _dot_nt(wvt_ref[...], kvn).astype(BF16)

    ang_t = inv_ref[...] * pos_ref[...].astype(F32)
    cs_t, sn_t = jnp.cos(ang_t), jnp.sin(ang_t)
    cs = jnp.concatenate([cs_t] * 4, axis=0).T
    sgn_sn = jnp.concatenate([-sn_t, sn_t] * 2, axis=0).T
    half = D_ROPE // 2

    def rope_t(x):
        x1, x2 = x[:half], x[half:]
        return jnp.concatenate([x1 * cs_t - x2 * sn_t, x1 * sn_t + x2 * cs_t], axis=0)
    l128 = lax.broadcasted_iota(jnp.int32, (tm, LANES), 1)
    first = (l128 % D_ROPE) < (D_ROPE // 2)

    def rope(x):
        swapped = jnp.where(first, pltpu.roll(x, LANES - D_ROPE // 2, 1), pltpu.roll(x, D_ROPE // 2, 1))
        return x * cs + swapped * sgn_sn

    kr = rope(t[:, 3 * LANES:4 * LANES])
    kr_hi = jnp.where(l128 >= D_ROPE, kr, 0.0)
    kr_lo = pltpu.roll(kr_hi, D_ROPE, 1)
    kr_tiles = (kr_lo.astype(BF16), kr_hi.astype(BF16))
    rope0 = N_H_B * D_NOPE
    for pair in range(2):
        qr_t = jnp.concatenate([rope_t(q_t[rope0 + h * D_ROPE:rope0 + (h + 1) * D_ROPE])
                                for h in (2 * pair, 2 * pair + 1)], axis=0).astype(BF16)
        for e in range(2):
            h = 2 * pair + e
            qt_ref[(2 * h) * LANES:(2 * h + 1) * LANES, :] = q_t[h * D_NOPE:(h + 1) * D_NOPE].astype(BF16)
            qt_ref[(2 * h + 1) * LANES:(2 * h + 2) * LANES, :] = qr_t
            k_ref[:, (2 * h) * LANES:(2 * h + 1) * LANES] = kn[:, h * LANES:(h + 1) * LANES].astype(BF16)
            k_ref[:, (2 * h + 1) * LANES:(2 * h + 2) * LANES] = kr_tiles[e]


def _mla_pre(proj, pos3, qg, wq, kvg, wk, wvt, inv):
    bsz, s, _ = proj.shape
    tm = MLA_TK
    const = lambda shape: pl.BlockSpec(shape, lambda b, i: (0,) * len(shape))
    return pl.pallas_call(
        _mla_pre_kernel,
        grid=(bsz, s // tm),
        in_specs=[
            pl.BlockSpec((None, tm, W_CQKR), lambda b, i: (b, i, OFF_CQKR // W_CQKR)),
            pl.BlockSpec((None, tm, KV_LORA), lambda b, i: (b, i, OFF_CKV // KV_LORA)),
            pl.BlockSpec((None, 1, tm), lambda b, i: (b, 0, i)),
            const((1, W_CQKR)), const((W_UQ_B, W_CQKR)), const((1, KV_LORA)), const((KV_LORA, W_B)),
            const((W_B, KV_LORA)), const((D_ROPE // 2, 1)),
        ],
        out_specs=[
            pl.BlockSpec((None, None, W_QK_B, tm), lambda b, i: (b, i, 0, 0)),
            pl.BlockSpec((None, tm, W_QK_B), lambda b, i: (b, i, 0)),
            pl.BlockSpec((None, None, W_B, tm), lambda b, i: (b, i, 0, 0)),
        ],
        out_shape=[
            jax.ShapeDtypeStruct((bsz, s // tm, W_QK_B, tm), BF16),
            jax.ShapeDtypeStruct((bsz, s, W_QK_B), BF16),
            jax.ShapeDtypeStruct((bsz, s // tm, W_B, tm), BF16),
        ],
        compiler_params=_cparams(("parallel", "parallel")),
        name="mla_pre",
    )(proj, proj, pos3, qg, wq, kvg, wk, wvt, inv)


def _mla_attn_kernel(qt_ref, k_ref, vt_ref, o_ref, st_a, st_b):
    n_chunks, _, tk = vt_ref.shape
    nt, _, tq = qt_ref.shape

    def scores(t, st_scr):
        qt = qt_ref[t]
        m8 = jnp.full((8, tq), -jnp.inf, F32)
        for c in range(n_chunks):
            st = _dot(k_ref[c * tk:(c + 1) * tk, :], qt)
            st_scr[c * tk:(c + 1) * tk, :] = st
            m8 = jnp.maximum(m8, jnp.max(st.reshape(tk // 8, 8, tq), axis=0))
        return jnp.max(m8, axis=0, keepdims=True)

    def output(t, st_scr, m):
        l8 = jnp.zeros((8, tq), F32)
        acc = jnp.zeros((D_V_B, tq), F32)
        for c in range(n_chunks):
            p = jnp.exp2(st_scr[c * tk:(c + 1) * tk, :] - m)
            l8 = l8 + jnp.sum(p.reshape(tk // 8, 8, tq), axis=0)
            acc = acc + _dot(vt_ref[c], p.astype(BF16))
        l = jnp.sum(l8, axis=0, keepdims=True)
        o_ref[pl.ds(pl.multiple_of(t * tq, tq), tq), :] = (acc / l).T.astype(BF16)

    m_first = scores(0, st_a)
    if nt == 1:
        output(0, st_a, m_first)
        return

    def pair(u, m_a):
        t = 2 * u
        m_b = scores(t + 1, st_b)
        output(t, st_a, m_a)
        m_a = scores(t + 2, st_a)
        output(t + 1, st_b, m_b)
        return m_a

    m_a = lax.fori_loop(0, nt // 2 - 1, pair, m_first)
    m_b = scores(nt - 1, st_b)
    output(nt - 2, st_a, m_a)
    output(nt - 1, st_b, m_b)


def _mla_attention(qt, k, vt):
    bsz, s, _ = k.shape
    n_chunks, tk = vt.shape[1], vt.shape[3]
    nt, tq = qt.shape[1], qt.shape[3]
    assert nt == 1 or nt % 2 == 0
    return pl.pallas_call(
        _mla_attn_kernel,
        grid=(bsz, N_H_B),
        in_specs=[
            pl.BlockSpec((None, nt, D_QK_B, tq), lambda b, h: (b, 0, h, 0)),
            pl.BlockSpec((None, s, D_QK_B), lambda b, h: (b, 0, h)),
            pl.BlockSpec((None, n_chunks, D_V_B, tk), lambda b, h: (b, 0, h, 0)),
        ],
        out_specs=pl.BlockSpec((None, s, D_V_B), lambda b, h: (b, 0, h)),
        out_shape=jax.ShapeDtypeStruct((bsz, s, W_B), BF16),
        scratch_shapes=[pltpu.VMEM((s, tq), F32), pltpu.VMEM((s, tq), F32)],
        compiler_params=_cparams(("parallel", "parallel")),
        name="mla_attn",
    )(qt, k, vt)


def _log_sigmoid(x):
    return jnp.minimum(x, 0.0) - jnp.log1p(jnp.exp(-jnp.abs(x)))


def _split3(x):
    hi = x.astype(BF16).astype(F32)
    r1 = x - hi
    mid = r1.astype(BF16).astype(F32)
    return jnp.concatenate([hi, mid, r1 - mid], axis=-1)


def _mlstm_kernel(q_ref, k_ref, vf_ref, vh_ref, of_ref, oh_ref, gate_ref, gb_ref, hg_ref, yf_ref, yt_ref,
                  hf_scr, hb_scr, tile_scr, kw_scr, rowq_scr, bp_scr, cf_scr, cb_scr, gt_scr, vext_scr):
    s_len = q_ref.shape[0]
    L = CHUNK
    nc = s_len // L
    head = pl.program_id(1)
    ii = lax.broadcasted_iota(jnp.int32, (L, L), 0)
    jj = lax.broadcasted_iota(jnp.int32, (L, L), 1)
    ones = jnp.ones((L, L), BF16)
    odd_head = (head % 2) == 1
    lane = lax.broadcasted_iota(jnp.int32, (L, LANES), 1)
    eye3 = jnp.concatenate([(ii == jj).astype(BF16)] * 3, axis=1)
    tri_ones = [jnp.concatenate([jnp.concatenate([m.astype(BF16), ones], axis=1)] * 3, axis=0)
                for m in (ii <= jj, ii >= jj)]
    masks = (jj <= ii, jj >= ii)
    h_scrs = (hf_scr, hb_scr)
    c_scrs = (cf_scr, cb_scr)

    def own_half(half_ref, r0):
        x = half_ref[pl.ds(r0, L), :].astype(F32)
        return jnp.where(odd_head, pltpu.roll(x, LANES // 2, 1), x)

    def load_v_ext(r0):
        tail = jnp.where(lane < DV_C - LANES, own_half(vh_ref, r0), jnp.where(lane == ONES_COL - LANES, 1.0, 0.0))
        return jnp.concatenate([vf_ref[pl.ds(r0, L), :], tail.astype(BF16)], axis=1)

    def transpose_gates(c, _):
        g = gate_ref[pl.ds(pl.multiple_of(c * L, L), L), :] + gb_ref[...]
        gt_scr[pl.ds(pl.multiple_of(c * 16, 16), 16), :] = g.T[0:16, :]
        return 0

    lax.fori_loop(0, nc, transpose_gates, 0, unroll=4)

    for d in range(2):
        li = gt_scr[pl.ds(d * N_H_C + head, nc, stride=16), :]
        lf = _log_sigmoid(gt_scr[pl.ds((2 + d) * N_H_C + head, nc, stride=16), :])
        bt = _dot(_split3(lf).astype(BF16), tri_ones[d])
        b, tot = bt[:, :L] * LOG2E, bt[:, L:] * LOG2E
        r = li * LOG2E - b
        m_loc = tot + jnp.broadcast_to(jnp.max(r, axis=-1, keepdims=True), r.shape)
        rowq_scr[d, 0] = r
        rowq_scr[d, 1] = jnp.exp2(tot + r - m_loc)
        rowq_scr[d, 2] = tot
        rowq_scr[d, 3] = m_loc
        bp_scr[d] = _split3(b + math.log2(DQK_C ** -0.5))

    def prep(c, _):
        r0 = pl.multiple_of(c * L, L)
        q, k = q_ref[pl.ds(r0, L), :], k_ref[pl.ds(r0, L), :]
        s = _dot_nt(q, k)
        kt = k.astype(F32).T
        p = []
        for d in range(2):
            u = jnp.where(masks[d], rowq_scr[d, 0, pl.ds(c, 1), :], -jnp.inf)
            ct = jnp.broadcast_to(jnp.max(u, axis=-1, keepdims=True), (L, L))
            p.append((s * jnp.exp2(u - ct)).astype(BF16))
            tile_scr[pl.ds(r0, L), (2 + d) * L:(3 + d) * L] = ct
            kw_scr[d, c] = (kt * rowq_scr[d, 1, pl.ds(c, 1), :]).astype(BF16)
        rhs = jnp.concatenate([jnp.broadcast_to(bp_scr[d, pl.ds(c, 1), :], (L, 3 * L)) for d in range(2)], axis=0)
        tile_scr[pl.ds(r0, L), 0:2 * L] = _dot_nt(eye3, rhs.astype(BF16))
        v_ext = load_v_ext(r0)
        vext_scr[pl.ds(r0, L), :] = v_ext
        intra = _dot(jnp.concatenate(p, axis=0), v_ext)
        hf_scr[pl.ds(r0, L), :] = intra[:L]
        hb_scr[pl.ds(r0, L), :] = intra[L:]
        return 0

    lax.fori_loop(0, nc, prep, 0, unroll=4)

    cf_scr[...] = jnp.zeros_like(cf_scr)

    @pl.when(jnp.logical_not(odd_head))
    def _():
        yt_ref[...] = jnp.zeros_like(yt_ref)
    cb_scr[...] = jnp.zeros_like(cb_scr)

    def one_dir(c, m_state, d):
        h_scr, c_scr = h_scrs[d], c_scrs[d]
        r0 = pl.multiple_of(c * L, L)
        tot, m_loc = rowq_scr[d, 2, pl.ds(c, 1), :], rowq_scr[d, 3, pl.ds(c, 1), :]
        bt = tile_scr[pl.ds(r0, L), d * L:(d + 1) * L]
        ct = tile_scr[pl.ds(r0, L), (2 + d) * L:(3 + d) * L]
        v_ext = vext_scr[pl.ds(r0, L), :]
        c_state = c_scr[...]
        qc = _dot(q_ref[pl.ds(r0, L), :], c_state.astype(BF16))
        mx = jnp.maximum(m_state, ct)
        iw = jnp.exp2(m_state - mx)
        xw = jnp.exp2(ct - mx)
        num = [iw * qc[:, j * L:(j + 1) * L] + xw * h_scr[pl.ds(r0, L), j * L:(j + 1) * L] for j in range(2)]
        den = jnp.broadcast_to(num[1][:, ONES_COL - L:ONES_COL - L + 1], (L, L))
        inv = 1.0 / jnp.maximum(jnp.abs(den), jnp.exp2(-(bt + mx)))
        h_scr[pl.ds(r0, L), 0:L] = num[0] * inv
        h_scr[pl.ds(r0, L), L:2 * L] = num[1] * inv
        c_loc = _dot(kw_scr[d, c], v_ext)
        m_new = jnp.maximum(tot + m_state, m_loc)
        sp, sl = jnp.exp2(tot + m_state - m_new), jnp.exp2(m_loc - m_new)
        c_scr[...] = jnp.concatenate(
            [sp * c_state[:, j * L:(j + 1) * L] + sl * c_loc[:, j * L:(j + 1) * L] for j in range(2)], axis=1)
        return m_new

    real = lax.broadcasted_iota(jnp.int32, (L, DV_PAD), 1) < DV_C

    def finish(c):
        r0 = pl.multiple_of(c * L, L)
        h = jnp.where(real, hf_scr[pl.ds(r0, L), :] + hb_scr[pl.ds(r0, L), :], 0.0)
        ms = jnp.sum(h * h, axis=-1, keepdims=True) * (1.0 / DV_C)
        hn = h * lax.rsqrt(ms + EPS) * hg_ref[...]
        o = jnp.concatenate([of_ref[pl.ds(r0, L), :].astype(F32), own_half(oh_ref, r0)], axis=1)
        y = jax.nn.sigmoid(o) * hn
        yf_ref[pl.ds(r0, L), :] = y[:, :LANES].astype(BF16)
        tail = y[:, LANES:]
        tail = jnp.where(odd_head, pltpu.roll(tail, LANES // 2, 1), tail)
        yt_ref[pl.ds(r0, L), :] = yt_ref[pl.ds(r0, L), :] + tail.astype(BF16)

    def scan(c, carry):
        return one_dir(c, carry[0], 0), one_dir(nc - 1 - c, carry[1], 1)

    def scan_and_finish(c, carry):
        carry = scan(c, carry)
        finish(c)
        finish(nc - 1 - c)
        return carry

    neg = jnp.full((1, LANES), -jnp.inf, F32)
    carry = lax.fori_loop(0, nc // 2, scan, (neg, neg), unroll=4)
    lax.fori_loop(nc // 2, nc, scan_and_finish, carry, unroll=4)


def _mlstm(proj, gates, gate_b, head_g):
    bsz, s, _ = proj.shape
    nc = s // CHUNK
    return pl.pallas_call(
        _mlstm_kernel,
        grid=(bsz, N_H_C),
        in_specs=[
            pl.BlockSpec((None, s, DQK_PAD), lambda b, h: (b, 0, OFF_MQ // DQK_PAD + h)),
            pl.BlockSpec((None, s, DQK_PAD), lambda b, h: (b, 0, OFF_MK // DQK_PAD + h)),
            pl.BlockSpec((None, s, LANES), lambda b, h: (b, 0, OFF_MV // LANES + 3 * (h // 2) + 2 * (h % 2))),
            pl.BlockSpec((None, s, LANES), lambda b, h: (b, 0, OFF_MV // LANES + 3 * (h // 2) + 1)),
            pl.BlockSpec((None, s, LANES), lambda b, h: (b, 0, OFF_MO // LANES + 3 * (h // 2) + 2 * (h % 2))),
            pl.BlockSpec((None, s, LANES), lambda b, h: (b, 0, OFF_MO // LANES + 3 * (h // 2) + 1)),
            pl.BlockSpec((None, s, LANES), lambda b, h: (b, 0, 0)),
            pl.BlockSpec((1, LANES), lambda b, h: (0, 0)),
            pl.BlockSpec((None, 1, DV_PAD), lambda b, h: (h, 0, 0)),
        ],
        out_specs=[
            pl.BlockSpec((None, s, LANES), lambda b, h: (b, 0, h)),
            pl.BlockSpec((None, s, LANES), lambda b, h: (b, 0, h // 2)),
        ],
        out_shape=[
            jax.ShapeDtypeStruct((bsz, s, N_H_C * LANES), BF16),
            jax.ShapeDtypeStruct((bsz, s, N_H_C // 2 * LANES), BF16),
        ],
        scratch_shapes=[
            pltpu.VMEM((s, DV_PAD), F32), pltpu.VMEM((s, DV_PAD), F32),
            pltpu.VMEM((s, 4 * CHUNK), F32),
            pltpu.VMEM((2, nc, DQK_PAD, CHUNK), BF16),
            pltpu.VMEM((2, 4, nc, CHUNK), F32),
            pltpu.VMEM((2, nc, 3 * CHUNK), F32),
            pltpu.VMEM((DQK_PAD, DV_PAD), F32), pltpu.VMEM((DQK_PAD, DV_PAD), F32),
            pltpu.VMEM((nc * 16, CHUNK), F32),
            pltpu.VMEM((s, DV_PAD), BF16),
        ],
        compiler_params=_cparams(("parallel", "arbitrary")),
        name="mlstm",
    )(proj, proj, proj, proj, proj, proj, gates, gate_b, head_g)


def _out_kernel(ya_ref, yb_ref, ycf_ref, yct_ref, wa_ref, wb_ref, wcf_ref, wct_ref, pg_ref, gate_ref, x_ref,
                o_ref, y_scr):
    for r in range(0, o_ref.shape[0], SUB_ROWS):
        rows = slice(r, r + SUB_ROWS)
        y_scr[rows, :] = (_dot(ya_ref[rows, :], wa_ref[...]) + _dot(yb_ref[rows, :], wb_ref[...])
                          + _dot(ycf_ref[rows, :], wcf_ref[...]) + _dot(yct_ref[rows, :], wct_ref[...]))
        _norm_residual_rows(y_scr, x_ref, pg_ref, gate_ref, o_ref, r, SUB_ROWS)


def _out_proj(ya, yb, ycf, yct, wa, wb, wcf, wct, pg, modl, x):
    bsz, s, d = x.shape
    w_cf, w_ct = ycf.shape[2], yct.shape[2]
    tm = min(2 * SUB_ROWS, s)
    row = lambda w: pl.BlockSpec((None, tm, w), lambda b, i: (b, i, 0))
    const = lambda shape: pl.BlockSpec(shape, lambda b, i: (0,) * len(shape), pipeline_mode=pl.Buffered(1))
    return pl.pallas_call(
        _out_kernel,
        grid=(bsz, s // tm),
        in_specs=[
            row(W_A), row(W_B), row(w_cf), row(w_ct),
            const((W_A, d)), const((W_B, d)), const((w_cf, d)), const((w_ct, d)), const((1, d)),
            pl.BlockSpec((None, 1, d), lambda b, i: (b, 0, 2)),
            row(d),
        ],
        out_specs=row(d),
        out_shape=jax.ShapeDtypeStruct((bsz, s, d), F32),
        scratch_shapes=[pltpu.VMEM((tm, d), F32)],
        compiler_params=_cparams(("parallel", "parallel")),
        name="out_proj",
    )(ya, yb, ycf, yct, wa, wb, wcf, wct, pg, modl, x)


FFN_TF = 512
FFN_VMEM_LIMIT = 62 * 1024 * 1024


def _ffn_kernel(x_ref, g_ref, sc_ref, sh_ref, wg_hbm, wu_hbm, wd_hbm, pg_ref, gate_ref, o_ref,
                h_scr, wg_buf, wu_buf, wd_buf, sem, *, nf):
    tm = x_ref.shape[0]
    tf = wg_buf.shape[2]
    step = pl.program_id(0) * pl.num_programs(1) + pl.program_id(1)
    n_steps = pl.num_programs(0) * pl.num_programs(1)
    subs = [(r, slice(r, r + SUB_ROWS)) for r in range(0, tm, SUB_ROWS)]

    def slot_of(k):
        return (k + step) % 2

    def copies(k, slot):
        cols = pl.ds(pl.multiple_of(k * tf, tf), tf)
        return (pltpu.make_async_copy(wg_hbm.at[:, cols], wg_buf.at[slot], sem.at[0, slot]),
                pltpu.make_async_copy(wu_hbm.at[:, cols], wu_buf.at[slot], sem.at[1, slot]),
                pltpu.make_async_copy(wd_hbm.at[cols, :], wd_buf.at[slot], sem.at[2, slot]))

    def fetch(k, slot):
        for cp in copies(k, slot):
            cp.start()

    def wait(k, slot):
        for cp in copies(k, slot):
            cp.wait()

    def partial_down(rows, slot):
        h = h_scr[rows, :]
        a = _dot(h, wg_buf[slot])
        u = _dot(h, wu_buf[slot])
        hid = (a * jax.nn.sigmoid(a)) * u
        return _dot(hid.astype(BF16), wd_buf[slot])

    @pl.when(step == 0)
    def _():
        fetch(0, slot_of(0))

    fetch(1, slot_of(1))
    wait(0, slot_of(0))
    for r, rows in subs:
        _mod_norm_rows(x_ref, g_ref, sc_ref, sh_ref, h_scr, r, SUB_ROWS)
        o_ref[rows, :] = partial_down(rows, slot_of(0))

    def middle(k):
        fetch(k + 1, slot_of(k + 1))
        wait(k, slot_of(k))
        o_ref[...] += partial_down(slice(0, tm), slot_of(k))

    def pair(u, carry):
        middle(1 + 2 * u)
        middle(2 + 2 * u)
        return carry

    lax.fori_loop(0, (nf - 3) // 2, pair, 0)
    middle(nf - 2)

    @pl.when(step < n_steps - 1)
    def _():
        fetch(0, slot_of(nf))
    wait(nf - 1, slot_of(nf - 1))
    for r, rows in subs:
        o_ref[rows, :] += partial_down(rows, slot_of(nf - 1))
        _norm_residual_rows(o_ref, x_ref, pg_ref, gate_ref, o_ref, r, SUB_ROWS)


def _ffn(x, g, modl, wg, wu, wd, pg):
    bsz, s, d = x.shape
    f = wg.shape[1]
    nf = f // FFN_TF
    assert nf % 2 == 1 and nf >= 3
    tm = min(GEMM_TM, s)
    row = pl.BlockSpec((None, tm, d), lambda b, i: (b, i, 0))
    vec = pl.BlockSpec((1, d), lambda b, i: (0, 0))
    modv = lambda k: pl.BlockSpec((None, 1, d), lambda b, i: (b, 0, k))
    hbm = pl.BlockSpec(memory_space=pl.ANY)
    return pl.pallas_call(
        functools.partial(_ffn_kernel, nf=nf),
        grid=(bsz, s // tm),
        in_specs=[row, vec, modv(4), modv(3), hbm, hbm, hbm, vec, modv(5)],
        out_specs=row,
        out_shape=jax.ShapeDtypeStruct((bsz, s, d), F32),
        scratch_shapes=[
            pltpu.VMEM((tm, d), BF16),
            pltpu.VMEM((2, d, FFN_TF), BF16), pltpu.VMEM((2, d, FFN_TF), BF16), pltpu.VMEM((2, FFN_TF, d), BF16),
            pltpu.SemaphoreType.DMA((3, 2)),
        ],
        compiler_params=_cparams(("arbitrary", "arbitrary"), FFN_VMEM_LIMIT),
        name="ffn",
    )(x, g, modl, modl, wg, wu, wd, pg, modl)


def _pair_heads_a(w, axis):
    shape = w.shape
    w = jnp.moveaxis(w, axis, -1)
    lead = w.shape[:-1]
    w = w.reshape(lead + (2, 2, GROUP_A, DH_A))
    w = jnp.swapaxes(w, -3, -2)
    w = w.reshape(lead + (W_A,))
    return jnp.moveaxis(w, -1, axis).reshape(shape)


def _pad_heads(w, n_heads, width, padded, axis):
    w = jnp.moveaxis(w, axis, -1)
    lead = w.shape[:-1]
    w = w.reshape(lead + (n_heads, width))
    w = jnp.pad(w, [(0, 0)] * len(lead) + [(0, 0), (0, padded - width)])
    w = w.reshape(lead + (n_heads * padded,))
    return jnp.moveaxis(w, -1, axis)


def _pair_heads_c(w):
    d = w.shape[0]
    w = w.reshape(d, N_H_C // 2, 2, DV_C)
    even, odd = w[:, :, 0], w[:, :, 1]
    out = jnp.concatenate([even[..., :LANES], even[..., LANES:], odd[..., LANES:], odd[..., :LANES]], axis=-1)
    return out.reshape(d, W_C)


def _layout_w_in(w):
    d = w.shape[0]
    sizes = (W_A, N_KV_A * DH_A, N_KV_A * DH_A, Q_LORA, KV_LORA, D_ROPE,
             N_H_C * DQK_C, N_H_C * DQK_C, W_C, 4 * N_H_C, W_C)
    pts = np.cumsum(sizes)[:-1].tolist()
    aq, ak, av, bcq, bckv, bkr, cq, ck, cv, cg, co = jnp.split(w, pts, axis=1)
    parts = [
        _pair_heads_a(aq, 1), ak, av, bckv,
        jnp.pad(cg, ((0, 0), (0, LANES - 4 * N_H_C))),
        bcq, bkr,
        _pad_heads(cq, N_H_C, DQK_C, DQK_PAD, 1), _pad_heads(ck, N_H_C, DQK_C, DQK_PAD, 1),
        _pair_heads_c(cv), _pair_heads_c(co),
    ]
    out = jnp.concatenate(parts, axis=1).astype(BF16)
    assert out.shape == (d, IN_WIDTH_PAD)
    return out.reshape(d, IN_WIDTH_PAD // IN_TN, IN_TN).transpose(1, 0, 2)


def _layout_w_uq(w):
    w = w.reshape(Q_LORA, N_H_B, D_NOPE + D_ROPE)
    nope = w[:, :, :D_NOPE].reshape(Q_LORA, N_H_B * D_NOPE)
    rope = w[:, :, D_NOPE:].reshape(Q_LORA, N_H_B * D_ROPE)
    w = jnp.concatenate([nope, rope], axis=1)
    return jnp.pad(w, ((0, W_CQKR - Q_LORA), (0, 0))).T.astype(BF16)


def _layout_w_ukv(w):
    w = w.reshape(KV_LORA, N_H_B, D_NOPE + D_V_B)
    kn = w[:, :, :D_NOPE].reshape(KV_LORA, N_H_B * D_NOPE)
    v = w[:, :, D_NOPE:].reshape(KV_LORA, N_H_B * D_V_B)
    return kn.astype(BF16), v.T.astype(BF16)


def kernel(x, c, positions, mod_w, mod_b, pre_mix_g, post_mix_g, pre_ffn_g, post_ffn_g, w_in, attn_sink, mla_q_norm_g, mla_w_uq, mla_kv_norm_g, mla_w_ukv, mlstm_gate_b, mlstm_head_g, w_out, ffn_w_gate, ffn_w_up, ffn_w_down):
    depth = mod_w.shape[0]
    bsz, s, d = x.shape
    mod = _modulation(c, mod_w, mod_b)
    pos3 = positions.reshape(bsz, 1, s)
    inv = 1.0 / (ROPE_THETA ** (jnp.arange(0, D_ROPE, 2, dtype=F32) / D_ROPE))
    inv_col = inv.reshape(D_ROPE // 2, 1)
    for l in range(depth):
        modl = mod[l].reshape(bsz, 1, 6 * d)
        proj, gates, wg_bf, wu_bf, wd_bf = _in_proj(x, pre_mix_g[l].reshape(1, d), modl, _layout_w_in(w_in[l]),
                                                    (ffn_w_gate, ffn_w_up, ffn_w_down), l)
        ya = _window_attention(proj, attn_sink[l])
        qg = jnp.pad(mla_q_norm_g[l], (0, W_CQKR - Q_LORA)).reshape(1, W_CQKR)
        wk, wvt = _layout_w_ukv(mla_w_ukv[l])
        qb, kb, vtb = _mla_pre(proj, pos3, qg, _layout_w_uq(mla_w_uq[l]),
                               mla_kv_norm_g[l].reshape(1, KV_LORA), wk, wvt, inv_col)
        yb = _mla_attention(qb, kb, vtb)
        gate_b = jnp.pad(mlstm_gate_b[l], (0, LANES - 4 * N_H_C)).reshape(1, LANES)
        head_g = jnp.pad(mlstm_head_g[l].reshape(N_H_C, 1, DV_C), ((0, 0), (0, 0), (0, DV_PAD - DV_C)))
        ycf, yct = _mlstm(proj, gates, gate_b, head_g)
        wo = w_out[l]
        wa = _pair_heads_a(wo[:W_A], 0).astype(BF16)
        wb = wo[W_A:W_A + W_B].astype(BF16)
        wc = wo[W_A + W_B:].reshape(N_H_C, DV_C, d)
        wcf = wc[:, :LANES].reshape(N_H_C * LANES, d).astype(BF16)
        wct = wc[:, LANES:].reshape(N_H_C * (DV_C - LANES), d).astype(BF16)
        x = _out_proj(ya, yb, ycf, yct, wa, wb, wcf, wct, post_mix_g[l].reshape(1, d), modl, x)
        x = _ffn(x, pre_ffn_g[l].reshape(1, d), modl, wg_bf, wu_bf, wd_bf, post_ffn_g[l].reshape(1, d))
    return x
```
